```python
import math
import jax
import jax.numpy as jnp
from jax import lax
import numpy as np

D_MODEL = 2048
BATCH = 1
SEQ = 16384
DEPTH = 4

GRID_W = 64
CTX_LEN = 256
N_EVEN = (DEPTH + 1) // 2
N_ODD = DEPTH // 2

A_HEADS = 8
A_QK_DIM = 64
A_V_DIM = 2 * A_QK_DIM
A_QKV = A_HEADS * 2 * A_QK_DIM
B_HEADS = 8
B_HEAD_DIM = 128
B_QKV = B_HEADS * B_HEAD_DIM
WIN_R = 8
WIN_C = 16
S5_CH = 512
S5_GROUP = 16
S5_GROUPS = S5_CH // S5_GROUP
S5_STATE = 64
RET_HEADS = 12
RET_HEAD_DIM = 128
RET_WIDTH = RET_HEADS * RET_HEAD_DIM
RET_CHUNK = 128
EVEN_IN = 3 * A_QKV + 3 * B_QKV
EVEN_OUT = A_HEADS * A_V_DIM + B_QKV
ODD_IN = S5_CH + 4 * RET_WIDTH
ODD_OUT = S5_CH + RET_WIDTH
D_FF = -(-8 * D_MODEL // (3 * 256)) * 256

Q_BLOCK = 128
ROPE_BASE = 10000.0
EPS = 1e-6
NEG_INF = -1e30
F32 = jnp.float32

kernel_name = "hybrid_diffattn_natten_s5_retention_dit"


def rms_norm(x, w=None):
    xf = x.astype(F32)
    y = xf * lax.rsqrt(jnp.mean(xf * xf, axis=-1, keepdims=True) + EPS)
    if w is not None:
        y = y * w.astype(F32)
    return y.astype(x.dtype)


def modulate(h, shift, scale):
    return h * (1.0 + scale) + shift


def split_cols(t, sizes):
    outs, start = [], 0
    for s in sizes:
        outs.append(t[..., start:start + s])
        start += s
    return outs


def swiglu(h, w13, w2):
    a, b = jnp.split(h @ w13, 2, axis=-1)
    return (jax.nn.silu(a) * b) @ w2


def axial_rope_tables(n_tokens, dim):
    n_freq = dim // 4
    freq = ROPE_BASE ** (-jnp.arange(n_freq, dtype=F32) / n_freq)
    t = jnp.arange(n_tokens)
    row = (t // GRID_W).astype(F32)
    col = (t % GRID_W).astype(F32)
    ang = jnp.stack([row[:, None] * freq, col[:, None] * freq], axis=1)
    ang = jnp.broadcast_to(ang[:, :, None, :], (n_tokens, 2, 2, n_freq)).reshape(n_tokens, dim)
    return jnp.cos(ang), jnp.sin(ang)


def apply_axial_rope(x, cos, sin):
    shp = x.shape
    xr = x.reshape(shp[:-1] + (2, 2, shp[-1] // 4))
    rot = jnp.stack([-xr[..., 1, :], xr[..., 0, :]], axis=-2).reshape(shp)
    return (x * cos + rot * sin).astype(x.dtype)


def diff_attention(q1, q2, k1, k2, v, lam):
    b, h, nq, dqk = q1.shape
    nb = nq // Q_BLOCK
    scale = dqk ** -0.5

    def to_blocks(q):
        return jnp.moveaxis(q.reshape(b, h, nb, Q_BLOCK, dqk), 2, 0)

    def one_block(qs):
        qa, qb = qs
        s1 = jnp.einsum('bhqd,bhkd->bhqk', qa, k1).astype(F32) * scale
        s2 = jnp.einsum('bhqd,bhkd->bhqk', qb, k2).astype(F32) * scale
        p = jax.nn.softmax(s1, axis=-1) - lam * jax.nn.softmax(s2, axis=-1)
        return jnp.einsum('bhqk,bhkd->bhqd', p.astype(v.dtype), v)

    o = lax.map(one_block, (to_blocks(q1), to_blocks(q2)))
    return jnp.moveaxis(o, 0, 2).reshape(b, h, nq, v.shape[-1])


def softmax_attention(q, k, v):
    s = jnp.einsum('bhqd,bhkd->bhqk', q, k).astype(F32) * (q.shape[-1] ** -0.5)
    return jnp.einsum('bhqk,bhkd->bhqd', jax.nn.softmax(s, axis=-1).astype(v.dtype), v)


def neighborhood_attention(q, k, v, k_ctx, v_ctx, rpb):
    bsz, nh, n, dh = q.shape
    rows = n // GRID_W
    wr = min(WIN_R, rows)
    scale = dh ** -0.5
    qg = q.reshape(bsz, nh, rows, GRID_W, dh)
    kg = k.reshape(bsz, nh, rows, GRID_W, dh)
    vg = v.reshape(bsz, nh, rows, GRID_W, dh)
    r = jnp.arange(rows)
    r0 = jnp.clip(r - wr // 2, 0, rows - wr)
    ridx = r0[:, None] + jnp.arange(wr)[None, :]
    kw = kg[:, :, ridx].reshape(bsz, nh, rows, wr * GRID_W, dh)
    vw = vg[:, :, ridx].reshape(bsz, nh, rows, wr * GRID_W, dh)
    j = jnp.arange(GRID_W)
    c0 = jnp.clip(j - WIN_C // 2, 0, GRID_W - WIN_C)
    col_ok = (j[None, :] >= c0[:, None]) & (j[None, :] < c0[:, None] + WIN_C)
    col_ok = jnp.broadcast_to(col_ok[:, None, :], (GRID_W, wr, GRID_W)).reshape(GRID_W, wr * GRID_W)
    drow = ridx - r[:, None] + (WIN_R - 1)
    dcol = jnp.clip(j[None, :] - j[:, None] + (WIN_C - 1), 0, 2 * WIN_C - 2)
    bias = rpb.astype(F32)[:, drow][:, :, :, dcol]
    bias = bias.transpose(0, 1, 3, 2, 4).reshape(nh, rows, GRID_W, wr * GRID_W)
    s_loc = jnp.einsum('bhrqd,bhrkd->bhrqk', qg, kw).astype(F32) * scale + bias
    s_loc = jnp.where(col_ok, s_loc, NEG_INF)
    s_ctx = jnp.einsum('bhrqd,bhkd->bhrqk', qg, k_ctx).astype(F32) * scale
    p = jax.nn.softmax(jnp.concatenate([s_loc, s_ctx], axis=-1), axis=-1).astype(v.dtype)
    n_loc = wr * GRID_W
    o = (jnp.einsum('bhrqk,bhrkd->bhrqd', p[..., :n_loc], vw)
         + jnp.einsum('bhrqk,bhkd->bhrqd', p[..., n_loc:], v_ctx))
    return o.reshape(bsz, nh, n, dh)


def even_mixer(h, hc, w_in, w_out, lq1, lk1, lq2, lk2, subln_w, rpb, lambda_init,
               rope_cos, rope_sin, compute_ctx):
    sizes = [A_QKV] * 3 + [B_QKV] * 3
    qa, ka, va, qb, kb, vb = split_cols(h @ w_in, sizes)
    qac, kac, vac, qbc, kbc, vbc = split_cols(hc @ w_in, sizes)

    def a_qk(t):
        return t.reshape(t.shape[0], t.shape[1], A_HEADS, 2, A_QK_DIM).transpose(0, 2, 3, 1, 4)

    def heads(t, nh):
        return t.reshape(t.shape[0], t.shape[1], nh, -1).transpose(0, 2, 1, 3)

    def merge(o):
        return o.transpose(0, 2, 1, 3).reshape(o.shape[0], o.shape[2], -1)

    def a_out(o):
        return merge(rms_norm(o, subln_w) * (1.0 - lambda_init))

    lam = (jnp.exp(jnp.sum(lq1.astype(F32) * lk1.astype(F32)))
           - jnp.exp(jnp.sum(lq2.astype(F32) * lk2.astype(F32))) + lambda_init)
    q_a = apply_axial_rope(a_qk(qa), rope_cos, rope_sin)
    k_a = apply_axial_rope(a_qk(ka), rope_cos, rope_sin)
    q_ac, k_ac, v_ac = a_qk(qac), a_qk(kac), heads(vac, A_HEADS)
    k1 = jnp.concatenate([k_ac[:, :, 0], k_a[:, :, 0]], axis=2)
    k2 = jnp.concatenate([k_ac[:, :, 1], k_a[:, :, 1]], axis=2)
    v_all = jnp.concatenate([v_ac, heads(va, A_HEADS)], axis=2)
    o_a = a_out(diff_attention(q_a[:, :, 0], q_a[:, :, 1], k1, k2, v_all, lam))
    k_bc, v_bc = heads(kbc, B_HEADS), heads(vbc, B_HEADS)
    o_b = merge(neighborhood_attention(heads(qb, B_HEADS), heads(kb, B_HEADS), heads(vb, B_HEADS),
                                       k_bc, v_bc, rpb))
    y = jnp.concatenate([o_a, o_b], axis=-1).astype(h.dtype) @ w_out
    if not compute_ctx:
        return y, None
    o_ac = a_out(diff_attention(q_ac[:, :, 0], q_ac[:, :, 1], k_ac[:, :, 0], k_ac[:, :, 1], v_ac, lam))
    o_bc = merge(softmax_attention(heads(qbc, B_HEADS), k_bc, v_bc))
    yc = jnp.concatenate([o_ac, o_bc], axis=-1).astype(hc.dtype) @ w_out
    return y, yc


def s5_discretize(lam_re, lam_im, b_re, b_im, log_step):
    lr, li = lam_re.astype(F32), lam_im.astype(F32)
    dt = jnp.exp(log_step.astype(F32))[:, None]
    mag = jnp.exp(lr * dt)
    ar, ai = mag * jnp.cos(li * dt), mag * jnp.sin(li * dt)
    den = lr * lr + li * li
    nr, ni = ar - 1.0, ai
    fr = (nr * lr + ni * li) / den
    fi = (ni * lr - nr * li) / den
    br_, bi_ = b_re.astype(F32), b_im.astype(F32)
    bbr = fr[..., None] * br_ - fi[..., None] * bi_
    bbi = fr[..., None] * bi_ + fi[..., None] * br_
    return ar, ai, bbr, bbi


def _linear_recurrence_op(e1, e2):
    a1r, a1i, b1r, b1i = e1
    a2r, a2i, b2r, b2i = e2
    return (a1r * a2r - a1i * a2i, a1r * a2i + a1i * a2r,
            a2r * b1r - a2i * b1i + b2r, a2r * b1i + a2i * b1r + b2i)


def s5_scan(u, ar, ai, bbr, bbi, h0, reverse):
    xr = jnp.einsum('bngc,gpc->bngp', u, bbr)
    xi = jnp.einsum('bngc,gpc->bngp', u, bbi)
    if h0 is not None:
        h0r, h0i = h0
        pos = -1 if reverse else 0
        xr = xr.at[:, pos].add(ar * h0r - ai * h0i)
        xi = xi.at[:, pos].add(ar * h0i + ai * h0r)
    a_r = jnp.broadcast_to(ar, xr.shape)
    a_i = jnp.broadcast_to(ai, xr.shape)
    _, _, hr, hi = lax.associative_scan(_linear_recurrence_op, (a_r, a_i, xr, xi), axis=1, reverse=reverse)
    return hr, hi


def s5_readout(hr, hi, c_re, c_im):
    return jnp.einsum('bngp,gcp->bngc', hr, c_re) - jnp.einsum('bngp,gcp->bngc', hi, c_im)


def s5_glu(y, w_glu):
    y = jax.nn.gelu(y.reshape(y.shape[0], y.shape[1], S5_CH))
    return y * jax.nn.sigmoid(y @ w_glu.astype(F32))


def s5_mixer(u, uc, lam_re, lam_im, b_re, b_im, c_re, c_im, log_step, d_skip, w_glu, compute_ctx):
    def groups(t):
        return t.astype(F32).reshape(t.shape[0], t.shape[1], S5_GROUPS, S5_GROUP)

    ug, ucg = groups(u), groups(uc)
    d = d_skip.astype(F32).reshape(S5_GROUPS, S5_GROUP)
    y = ug * d
    yc = ucg * d if compute_ctx else None
    for direction, reverse in ((0, False), (1, True)):
        ar, ai, bbr, bbi = s5_discretize(lam_re[direction], lam_im[direction], b_re[direction],
                                         b_im[direction], log_step[direction])
        cr, ci = c_re[direction].astype(F32), c_im[direction].astype(F32)
        hcr, hci = s5_scan(ucg, ar, ai, bbr, bbi, None, reverse)
        end = 0 if reverse else -1
        hr, hi = s5_scan(ug, ar, ai, bbr, bbi, (hcr[:, end], hci[:, end]), reverse)
        y = y + s5_readout(hr, hi, cr, ci)
        if compute_ctx:
            yc = yc + s5_readout(hcr, hci, cr, ci)
    out = s5_glu(y, w_glu).astype(u.dtype)
    if not compute_ctx:
        return out, None
    return out, s5_glu(yc, w_glu).astype(uc.dtype)


def retention_chunkwise(q, k, v, log_g, s0):
    b, h, n, d = q.shape
    nc = n // RET_CHUNK
    idx = jnp.arange(RET_CHUNK, dtype=F32)
    diff = idx[:, None] - idx[None, :]
    intra = jnp.where(diff >= 0, jnp.exp(log_g[:, None, None] * jnp.maximum(diff, 0.0)), 0.0)
    q_dec = jnp.exp(log_g[:, None] * (idx + 1.0))[:, :, None]
    k_dec = jnp.exp(log_g[:, None] * (RET_CHUNK - 1.0 - idx))[:, :, None]
    c_dec = jnp.exp(log_g * RET_CHUNK)[:, None, None]

    def chunks(t):
        return jnp.moveaxis(t.reshape(b, h, nc, RET_CHUNK, t.shape[-1]), 2, 0)

    def step(s, inp):
        qb, kb, vb = inp
        att = jnp.einsum('bhid,bhjd->bhij', qb, kb) * intra
        o = jnp.einsum('bhij,bhjd->bhid', att, vb) + jnp.einsum('bhid,bhde->bhie', qb * q_dec, s)
        s = s * c_dec + jnp.einsum('bhjd,bhje->bhde', kb * k_dec, vb)
        return s, o

    s, o = lax.scan(step, s0, (chunks(q), chunks(k), chunks(v)))
    return jnp.moveaxis(o, 0, 2).reshape(b, h, n, v.shape[-1]), s


def retention_final_state(k, v, log_g):
    n = k.shape[2]
    w = jnp.exp(log_g[:, None] * (n - 1.0 - jnp.arange(n, dtype=F32)))
    return jnp.einsum('bhnd,bhne->bhde', k * w[..., None], v)


def retention_mixer(q, k, v, g, qc, kc, vc, gc, decay_logit, compute_ctx):
    def heads(t):
        return t.astype(F32).reshape(t.shape[0], t.shape[1], RET_HEADS, RET_HEAD_DIM).transpose(0, 2, 1, 3)

    def gated(o, gate):
        bb, nh, n, d = o.shape
        o = rms_norm(o).transpose(0, 2, 1, 3).reshape(bb, n, nh * d)
        return (o * jax.nn.silu(gate.astype(F32))).astype(gate.dtype)

    scale = RET_HEAD_DIM ** -0.5
    log_g = jax.nn.log_sigmoid(decay_logit.astype(F32))
    q_, k_, v_ = heads(q), heads(k) * scale, heads(v)
    kc_, vc_ = heads(kc) * scale, heads(vc)
    qc_ = heads(qc) if compute_ctx else None
    s_zero = jnp.zeros((q.shape[0], RET_HEADS, RET_HEAD_DIM, RET_HEAD_DIM), F32)
    o = None
    oc = None
    for direction, rev in ((0, False), (1, True)):
        lg = log_g[direction]
        fl = (lambda t: jnp.flip(t, axis=2)) if rev else (lambda t: t)
        if compute_ctx:
            oc_dir, s_ctx = retention_chunkwise(fl(qc_), fl(kc_), fl(vc_), lg, s_zero)
            oc = fl(oc_dir) if oc is None else oc + fl(oc_dir)
        else:
            s_ctx = retention_final_state(fl(kc_), fl(vc_), lg)
        o_dir, _ = retention_chunkwise(fl(q_), fl(k_), fl(v_), lg, s_ctx)
        o = fl(o_dir) if o is None else o + fl(o_dir)
    y = gated(o, g)
    if not compute_ctx:
        return y, None
    return y, gated(oc, gc)


def odd_mixer(h, hc, w_in, w_out, lam_re, lam_im, b_re, b_im, c_re, c_im, log_step, d_skip, w_glu,
              decay_logit, compute_ctx):
    sizes = [S5_CH] + [RET_WIDTH] * 4
    u, q, k, v, g = split_cols(h @ w_in, sizes)
    uc, qc, kc, vc, gc = split_cols(hc @ w_in, sizes)
    y_c, yc_c = s5_mixer(u, uc, lam_re, lam_im, b_re, b_im, c_re, c_im, log_step, d_skip, w_glu, compute_ctx)
    y_d, yc_d = retention_mixer(q, k, v, g, qc, kc, vc, gc, decay_logit, compute_ctx)
    y = jnp.concatenate([y_c, y_d], axis=-1) @ w_out
    if not compute_ctx:
        return y, None
    return y, jnp.concatenate([yc_c, yc_d], axis=-1) @ w_out


def setup_inputs(seed: int = 0) -> dict:
    key = jax.random.key(seed)
    ks = iter(jax.random.split(key, 40))
    D = D_MODEL

    def nrm(shape, scale):
        return jax.random.normal(next(ks), shape, F32) * scale

    x = nrm((BATCH, SEQ, D), 1.0)
    c = nrm((BATCH, D), 1.0)
    ctx = nrm((BATCH, CTX_LEN, D), 1.0)
    c_ctx = nrm((D,), 1.0)
    ada_w = nrm((DEPTH, D, 6 * D), 0.5 * D ** -0.5)
    ada_b = nrm((DEPTH, 6 * D), 0.02)
    norm1_w = 1.0 + nrm((DEPTH, D), 0.02)
    norm2_w = 1.0 + nrm((DEPTH, D), 0.02)
    ffn_w13 = nrm((DEPTH, D, 2 * D_FF), D ** -0.5)
    ffn_w2 = nrm((DEPTH, D_FF, D), D_FF ** -0.5)
    e_w_in = nrm((N_EVEN, D, EVEN_IN), D ** -0.5)
    e_w_out = nrm((N_EVEN, EVEN_OUT, D), EVEN_OUT ** -0.5)
    diff_lq1 = nrm((N_EVEN, A_QK_DIM), 0.1)
    diff_lk1 = nrm((N_EVEN, A_QK_DIM), 0.1)
    diff_lq2 = nrm((N_EVEN, A_QK_DIM), 0.1)
    diff_lk2 = nrm((N_EVEN, A_QK_DIM), 0.1)
    diff_subln_w = 1.0 + nrm((N_EVEN, A_V_DIM), 0.02)
    na_rpb = nrm((N_EVEN, B_HEADS, 2 * WIN_R - 1, 2 * WIN_C - 1), 0.1)
    o_w_in = nrm((N_ODD, D, ODD_IN), D ** -0.5)
    o_w_out = nrm((N_ODD, ODD_OUT, D), ODD_OUT ** -0.5)
    ssm_shape = (N_ODD, 2, S5_GROUPS, S5_STATE)
    s5_lam_re = -0.5 + nrm(ssm_shape, 0.01)
    s5_lam_im = math.pi * jnp.arange(S5_STATE, dtype=F32) + nrm(ssm_shape, 0.01)
    s5_b_re = nrm(ssm_shape + (S5_GROUP,), (2.0 * S5_GROUP) ** -0.5)
    s5_b_im = nrm(ssm_shape + (S5_GROUP,), (2.0 * S5_GROUP) ** -0.5)
    s5_c_re = nrm((N_ODD, 2, S5_GROUPS, S5_GROUP, S5_STATE), (2.0 * S5_STATE) ** -0.5)
    s5_c_im = nrm((N_ODD, 2, S5_GROUPS, S5_GROUP, S5_STATE), (2.0 * S5_STATE) ** -0.5)
    s5_log_step = jax.random.uniform(next(ks), (N_ODD, 2, S5_GROUPS), F32,
                                     minval=math.log(1e-3), maxval=math.log(1e-1))
    s5_d = nrm((N_ODD, S5_CH), 1.0)
    s5_w_glu = nrm((N_ODD, S5_CH, S5_CH), S5_CH ** -0.5)
    h_idx = jnp.arange(RET_HEADS, dtype=F32)
    gamma = 1.0 - 2.0 ** (-5.0 - h_idx)
    gamma_logit = jnp.log(gamma) + (5.0 + h_idx) * math.log(2.0)
    ret_decay_logit = gamma_logit + nrm((N_ODD, 2, RET_HEADS), 0.05)
    final_norm_w = 1.0 + nrm((D,), 0.02)
    return {"x": x, "c": c, "ctx": ctx, "c_ctx": c_ctx, "ada_w": ada_w, "ada_b": ada_b,
            "norm1_w": norm1_w, "norm2_w": norm2_w, "ffn_w13": ffn_w13, "ffn_w2": ffn_w2,
            "e_w_in": e_w_in, "e_w_out": e_w_out, "diff_lq1": diff_lq1, "diff_lk1": diff_lk1,
            "diff_lq2": diff_lq2, "diff_lk2": diff_lk2, "diff_subln_w": diff_subln_w, "na_rpb": na_rpb,
            "o_w_in": o_w_in, "o_w_out": o_w_out, "s5_lam_re": s5_lam_re, "s5_lam_im": s5_lam_im,
            "s5_b_re": s5_b_re, "s5_b_im": s5_b_im, "s5_c_re": s5_c_re, "s5_c_im": s5_c_im,
            "s5_log_step": s5_log_step, "s5_d": s5_d, "s5_w_glu": s5_w_glu,
            "ret_decay_logit": ret_decay_logit, "final_norm_w": final_norm_w}


def reference(x, c, ctx, c_ctx, ada_w, ada_b, norm1_w, norm2_w, ffn_w13, ffn_w2,
              e_w_in, e_w_out, diff_lq1, diff_lk1, diff_lq2, diff_lk2, diff_subln_w, na_rpb,
              o_w_in, o_w_out, s5_lam_re, s5_lam_im, s5_b_re, s5_b_im, s5_c_re, s5_c_im,
              s5_log_step, s5_d, s5_w_glu, ret_decay_logit, final_norm_w):
    n_lat = x.shape[1]
    rope_cos, rope_sin = axial_rope_tables(n_lat, A_QK_DIM)
    sc = jax.nn.silu(c)
    scc = jax.nn.silu(c_ctx)
    xc = ctx
    for i in range(DEPTH):
        last = i == DEPTH - 1
        compute_ctx = not last
        mod = jnp.split(sc @ ada_w[i] + ada_b[i], 6, axis=-1)
        n_mc = 6 if compute_ctx else 2
        modc = jnp.split(scc @ ada_w[i][:, :n_mc * D_MODEL] + ada_b[i][:n_mc * D_MODEL], n_mc, axis=-1)
        h = modulate(rms_norm(x, norm1_w[i]), mod[0][:, None], mod[1][:, None])
        hc = modulate(rms_norm(xc, norm1_w[i]), modc[0], modc[1])
        if i % 2 == 0:
            j = i // 2
            lambda_init = 0.8 - 0.6 * math.exp(-0.3 * i)
            y, yc = even_mixer(h, hc, e_w_in[j], e_w_out[j], diff_lq1[j], diff_lk1[j], diff_lq2[j], diff_lk2[j],
                               diff_subln_w[j], na_rpb[j], lambda_init, rope_cos, rope_sin, compute_ctx)
        else:
            j = i // 2
            y, yc = odd_mixer(h, hc, o_w_in[j], o_w_out[j], s5_lam_re[j], s5_lam_im[j], s5_b_re[j], s5_b_im[j],
                              s5_c_re[j], s5_c_im[j], s5_log_step[j], s5_d[j], s5_w_glu[j],
                              ret_decay_logit[j], compute_ctx)
        x = x + mod[2][:, None] * y
        h = modulate(rms_norm(x, norm2_w[i]), mod[3][:, None], mod[4][:, None])
        x = x + mod[5][:, None] * swiglu(h, ffn_w13[i], ffn_w2[i])
        if compute_ctx:
            xc = xc + modc[2] * yc
            hc = modulate(rms_norm(xc, norm2_w[i]), modc[3], modc[4])
            xc = xc + modc[5] * swiglu(hc, ffn_w13[i], ffn_w2[i])
    return rms_norm(x, final_norm_w)
```

```python
import functools
import math

import jax
import jax.numpy as jnp
from jax import lax
from jax.experimental import pallas as pl
from jax.experimental.pallas import tpu as pltpu

F32 = jnp.float32
BF16 = jnp.bfloat16

GRID_W = 64
A_HEADS = 8
A_QK_DIM = 64
HEAD_W = 128
B_HEADS = 8
WIN_R = 8
WIN_C = 16
NA_ROWS = 4
S5_CH = 512
S5_GROUP = 16
S5_GROUPS = 32
S5_STATE = 64
S5_CHUNK = 16
S5_BLOCK = 16
RET_HEADS = 12
RET_CHUNK = 128
ROPE_BASE = 10000.0
EPS = 1e-6
NEG_INF = -1e30
VMEM_LIMIT = 56 * 1024 * 1024
N_MOD = 8


def _cparams(sem):
    return pltpu.CompilerParams(dimension_semantics=sem, vmem_limit_bytes=VMEM_LIMIT)


def _row_tile(t, candidates):
    for c in candidates:
        if t % c == 0:
            return c
    raise ValueError(f"no row tile for {t}")


def _pick_mod(mods_ref, idx, is_ctx):
    return jnp.where(is_ctx, mods_ref[1, idx:idx + 1, :], mods_ref[0, idx:idx + 1, :])


def _is_ctx_rows(tm, n_lat, axis):
    row = pl.program_id(axis) * tm + lax.broadcasted_iota(jnp.int32, (tm, 1), 0)
    return row >= n_lat


def _mods_kernel(s_ref, w_ref, b_ref, o_ref):
    s = s_ref[...]
    s = s * jax.nn.sigmoid(s)
    o_ref[...] = jnp.dot(s, w_ref[...], preferred_element_type=F32,
                         precision=lax.Precision.HIGHEST) + b_ref[...]


def _mods(c, c_ctx, ada_w, ada_b):
    depth, d, w6 = ada_w.shape
    s = jnp.zeros((8, d), F32).at[0].set(c[0]).at[1].set(c_ctx)
    tn = 1024
    out = pl.pallas_call(
        _mods_kernel,
        grid=(depth, w6 // tn),
        in_specs=[pl.BlockSpec((8, d), lambda l, j: (0, 0)),
                  pl.BlockSpec((None, d, tn), lambda l, j: (l, 0, j)),
                  pl.BlockSpec((None, 1, tn), lambda l, j: (l, 0, j))],
        out_specs=pl.BlockSpec((None, 8, tn), lambda l, j: (l, 0, j)),
        out_shape=jax.ShapeDtypeStruct((depth, 8, w6), F32),
        compiler_params=_cparams(("arbitrary", "arbitrary")),
        name="ada_mods",
    )(s, ada_w, ada_b.reshape(depth, 1, w6))
    m = out[:, :2].reshape(depth, 2, 6, d)
    return jnp.pad(m, ((0, 0), (0, 0), (0, N_MOD - 6), (0, 0)))


def _normmod_kernel(x_ref, w_ref, mods_ref, o_ref, *, n_lat, tm, shift_idx, scale_idx):
    x = x_ref[...]
    y = x * lax.rsqrt(jnp.mean(x * x, axis=-1, keepdims=True) + EPS) * w_ref[...]
    is_ctx = _is_ctx_rows(tm, n_lat, 0)
    shift = _pick_mod(mods_ref, shift_idx, is_ctx)
    scale = _pick_mod(mods_ref, scale_idx, is_ctx)
    o_ref[...] = (y * (1.0 + scale) + shift).astype(o_ref.dtype)


def _normmod(x, w, mods, n_lat, shift_idx, scale_idx):
    t, d = x.shape
    tm = _row_tile(t, (640, 256, 128))
    return pl.pallas_call(
        functools.partial(_normmod_kernel, n_lat=n_lat, tm=tm, shift_idx=shift_idx, scale_idx=scale_idx),
        grid=(t // tm,),
        in_specs=[pl.BlockSpec((tm, d), lambda i: (i, 0)),
                  pl.BlockSpec((1, d), lambda i: (0, 0)),
                  pl.BlockSpec((2, N_MOD, d), lambda i: (0, 0, 0))],
        out_specs=pl.BlockSpec((tm, d), lambda i: (i, 0)),
        out_shape=jax.ShapeDtypeStruct((t, d), BF16),
        compiler_params=_cparams(("arbitrary",)),
        name="norm_modulate",
    )(x, w.reshape(1, d), mods)


def _final_norm_kernel(x_ref, w_ref, o_ref):
    x = x_ref[...]
    o_ref[...] = x * lax.rsqrt(jnp.mean(x * x, axis=-1, keepdims=True) + EPS) * w_ref[...]


def _final_norm(x, w, n_lat):
    d = x.shape[1]
    tm = _row_tile(n_lat, (512, 256, 128))
    return pl.pallas_call(
        _final_norm_kernel,
        grid=(n_lat // tm,),
        in_specs=[pl.BlockSpec((tm, d), lambda i: (i, 0)),
                  pl.BlockSpec((1, d), lambda i: (0, 0))],
        out_specs=pl.BlockSpec((tm, d), lambda i: (i, 0)),
        out_shape=jax.ShapeDtypeStruct((n_lat, d), F32),
        compiler_params=_cparams(("arbitrary",)),
        name="final_norm",
    )(x, w.reshape(1, d))


def _rope_store(acc, cos, sin, o_ref, scale):
    first_half = (lax.broadcasted_iota(jnp.int32, (1, HEAD_W), 1) % 32) < 16
    for c in range(acc.shape[1] // HEAD_W):
        x = acc[:, c * HEAD_W:(c + 1) * HEAD_W]
        rot = jnp.where(first_half, -pltpu.roll(x, HEAD_W - 16, 1), pltpu.roll(x, 16, 1))
        o_ref[:, c * HEAD_W:(c + 1) * HEAD_W] = ((x * cos + rot * sin) * scale).astype(o_ref.dtype)


def _inproj_even_kernel(h_ref, w_ref, cos_ref, sin_ref, o_ref, *, a_scale, b_scale):
    j = pl.program_id(0)
    acc = jnp.dot(h_ref[...], w_ref[...], preferred_element_type=F32)

    @pl.when(j == 0)
    def _():
        _rope_store(acc, cos_ref[...], sin_ref[...], o_ref, a_scale)

    @pl.when(j == 1)
    def _():
        _rope_store(acc, cos_ref[...], sin_ref[...], o_ref, 1.0)

    @pl.when(j == 3)
    def _():
        o_ref[...] = (acc * b_scale).astype(o_ref.dtype)

    @pl.when((j == 2) | (j > 3))
    def _():
        o_ref[...] = acc.astype(o_ref.dtype)


def _inproj_even(h, w, cos, sin):
    t, d = h.shape
    n = w.shape[1]
    tn = n // 6
    tm = _row_tile(t, (1280, 640, 256, 128))
    return pl.pallas_call(
        functools.partial(_inproj_even_kernel, a_scale=A_QK_DIM ** -0.5, b_scale=HEAD_W ** -0.5),
        grid=(6, t // tm),
        in_specs=[pl.BlockSpec((tm, d), lambda j, i: (i, 0)),
                  pl.BlockSpec((d, tn), lambda j, i: (0, j)),
                  pl.BlockSpec((tm, HEAD_W), lambda j, i: (i, 0)),
                  pl.BlockSpec((tm, HEAD_W), lambda j, i: (i, 0))],
        out_specs=pl.BlockSpec((tm, tn), lambda j, i: (i, j)),
        out_shape=jax.ShapeDtypeStruct((t, n), BF16),
        compiler_params=_cparams(("arbitrary", "arbitrary")),
        name="inproj_even",
    )(h, w, cos, sin)


def _matmul_kernel(h_ref, w_ref, o_ref):
    o_ref[...] = jnp.dot(h_ref[...], w_ref[...], preferred_element_type=F32).astype(o_ref.dtype)


def _matmul(h, w, tn, name):
    t, d = h.shape
    n = w.shape[1]
    tm = _row_tile(t, (1280, 640, 256, 128))
    return pl.pallas_call(
        _matmul_kernel,
        grid=(n // tn, t // tm),
        in_specs=[pl.BlockSpec((tm, d), lambda j, i: (i, 0)),
                  pl.BlockSpec((d, tn), lambda j, i: (0, j))],
        out_specs=pl.BlockSpec((tm, tn), lambda j, i: (i, j)),
        out_shape=jax.ShapeDtypeStruct((t, n), BF16),
        compiler_params=_cparams(("arbitrary", "arbitrary")),
        name=name,
    )(h, w)


def _ffn_up_kernel(h_ref, w1_ref, w3_ref, o_ref):
    h = h_ref[...]
    a = jnp.dot(h, w1_ref[...], preferred_element_type=F32)
    b = jnp.dot(h, w3_ref[...], preferred_element_type=F32)
    o_ref[...] = (a * jax.nn.sigmoid(a) * b).astype(o_ref.dtype)


def _ffn_up(h, w13):
    t, d = h.shape
    d_ff = w13.shape[1] // 2
    tn = 512
    nj = d_ff // tn
    tm = _row_tile(t, (1280, 640, 256, 128))
    return pl.pallas_call(
        _ffn_up_kernel,
        grid=(nj, t // tm),
        in_specs=[pl.BlockSpec((tm, d), lambda j, i: (i, 0)),
                  pl.BlockSpec((d, tn), lambda j, i: (0, j)),
                  pl.BlockSpec((d, tn), lambda j, i: (0, j + nj))],
        out_specs=pl.BlockSpec((tm, tn), lambda j, i: (i, j)),
        out_shape=jax.ShapeDtypeStruct((t, d_ff), BF16),
        compiler_params=_cparams(("arbitrary", "arbitrary")),
        name="ffn_up",
    )(h, w13, w13)


def _gated_residual_kernel(*refs, n_a, n_lat, tm, gate_idx):
    a_refs = refs[:n_a]
    w_ref, x_ref, mods_ref, o_ref = refs[n_a:]
    k0 = 0
    y = None
    for a_ref in a_refs:
        kk = a_ref.shape[1]
        part = jnp.dot(a_ref[...], w_ref[k0:k0 + kk, :], preferred_element_type=F32)
        y = part if y is None else y + part
        k0 += kk
    gate = _pick_mod(mods_ref, gate_idx, _is_ctx_rows(tm, n_lat, 1))
    o_ref[...] = x_ref[...] + gate * y


def _gated_residual(a_list, w, x, mods, n_lat, gate_idx, name):
    t, d = x.shape
    k = w.shape[0]
    tn = 512
    tm = _row_tile(t, (640, 256, 128))
    n_a = len(a_list)
    in_specs = [pl.BlockSpec((tm, a.shape[1]), lambda j, i: (i, 0)) for a in a_list]
    in_specs += [pl.BlockSpec((k, tn), lambda j, i: (0, j)),
                 pl.BlockSpec((tm, tn), lambda j, i: (i, j)),
                 pl.BlockSpec((2, N_MOD, tn), lambda j, i: (0, 0, j))]
    return pl.pallas_call(
        functools.partial(_gated_residual_kernel, n_a=n_a, n_lat=n_lat, tm=tm, gate_idx=gate_idx),
        grid=(d // tn, t // tm),
        in_specs=in_specs,
        out_specs=pl.BlockSpec((tm, tn), lambda j, i: (i, j)),
        out_shape=jax.ShapeDtypeStruct((t, d), F32),
        input_output_aliases={n_a + 1: 0},
        compiler_params=_cparams(("arbitrary", "arbitrary")),
        name=name,
    )(*a_list, w, x, mods)


def _softmax_block(qs, k, v):
    s = lax.dot_general(qs, k, (((1,), (1,)), ((), ())), preferred_element_type=F32)
    m = jnp.max(s, axis=-1, keepdims=True)
    p = jnp.exp(s - m)
    l = jnp.sum(p, axis=-1, keepdims=True)
    acc = jnp.dot(p.astype(v.dtype), v, preferred_element_type=F32)
    return m, l, acc


def _diff_attn_kernel(*refs, tq, lambda_init, use_lat):
    if use_lat:
        (lq1, lk1, lq2, lk2, subw, q_ref, kc_ref, vc_ref, k_ref, v_ref, o_ref,
         qs_scr, m_scr, l_scr, acc_scr) = refs
    else:
        (lq1, lk1, lq2, lk2, subw, q_ref, kc_ref, vc_ref, o_ref,
         qs_scr, m_scr, l_scr, acc_scr) = refs
    ki = pl.program_id(2)

    @pl.when(ki == 0)
    def _():
        q = q_ref[...]
        comp1 = lax.broadcasted_iota(jnp.int32, (1, HEAD_W), 1) < A_QK_DIM
        zero = jnp.zeros_like(q)
        qs_scr[0:tq, :] = jnp.where(comp1, q, zero)
        qs_scr[tq:2 * tq, :] = jnp.where(comp1, zero, q)
        m, l, acc = _softmax_block(qs_scr[...], kc_ref[...], vc_ref[...])
        m_scr[...] = m
        l_scr[...] = l
        acc_scr[...] = acc

    if use_lat:
        m_prev = m_scr[...]
        s = lax.dot_general(qs_scr[...], k_ref[...], (((1,), (1,)), ((), ())), preferred_element_type=F32)
        m_new = jnp.maximum(m_prev, jnp.max(s, axis=-1, keepdims=True))
        alpha = jnp.exp(m_prev - m_new)
        p = jnp.exp(s - m_new)
        l_scr[...] = alpha * l_scr[...] + jnp.sum(p, axis=-1, keepdims=True)
        acc_scr[...] = alpha * acc_scr[...] + jnp.dot(p.astype(BF16), v_ref[...], preferred_element_type=F32)
        m_scr[...] = m_new

    @pl.when(ki == pl.num_programs(2) - 1)
    def _():
        lam = (jnp.exp(jnp.sum(lq1[...] * lk1[...], axis=-1, keepdims=True))
               - jnp.exp(jnp.sum(lq2[...] * lk2[...], axis=-1, keepdims=True)) + lambda_init)
        o1 = acc_scr[0:tq, :] / l_scr[0:tq, :]
        o2 = acc_scr[tq:2 * tq, :] / l_scr[tq:2 * tq, :]
        o = o1 - lam * o2
        y = o * lax.rsqrt(jnp.mean(o * o, axis=-1, keepdims=True) + EPS) * subw[...]
        o_ref[...] = (y * (1.0 - lambda_init)).astype(o_ref.dtype)


def _diff_attn(qkv, lq1, lk1, lq2, lk2, subw, n_lat, n_ctx, lambda_init, prev=None):
    t = qkv.shape[0]
    use_lat = prev is None
    ctx_blk = n_lat // n_ctx
    small = [lq1.reshape(1, -1), lk1.reshape(1, -1), lq2.reshape(1, -1), lk2.reshape(1, -1), subw.reshape(1, -1)]
    small_specs = [pl.BlockSpec(a.shape, lambda h, qi, ki: (0, 0)) for a in small]
    kc_spec = pl.BlockSpec((n_ctx, HEAD_W), lambda h, qi, ki: (ctx_blk, A_HEADS + h))
    vc_spec = pl.BlockSpec((n_ctx, HEAD_W), lambda h, qi, ki: (ctx_blk, 2 * A_HEADS + h))
    if use_lat:
        tq = _row_tile(n_lat, (512, 256, 128))
        tk = _row_tile(n_lat, (1024, 512, 256, 128))
        grid = (A_HEADS, n_lat // tq, n_lat // tk)
        in_specs = small_specs + [
            pl.BlockSpec((tq, HEAD_W), lambda h, qi, ki: (qi, h)), kc_spec, vc_spec,
            pl.BlockSpec((tk, HEAD_W), lambda h, qi, ki: (ki, A_HEADS + h)),
            pl.BlockSpec((tk, HEAD_W), lambda h, qi, ki: (ki, 2 * A_HEADS + h))]
        args = small + [qkv, qkv, qkv, qkv, qkv]
        out_spec = pl.BlockSpec((tq, HEAD_W), lambda h, qi, ki: (qi, h))
        aliases = {}
    else:
        tq = n_ctx
        grid = (A_HEADS, 1, 1)
        in_specs = small_specs + [
            pl.BlockSpec((tq, HEAD_W), lambda h, qi, ki: (ctx_blk, h)), kc_spec, vc_spec,
            pl.BlockSpec(memory_space=pl.ANY)]
        args = small + [qkv, qkv, qkv, prev]
        out_spec = pl.BlockSpec((tq, HEAD_W), lambda h, qi, ki: (ctx_blk, h))
        aliases = {len(args) - 1: 0}

    kern = functools.partial(_diff_attn_kernel, tq=tq, lambda_init=lambda_init, use_lat=use_lat)
    if not use_lat:
        inner = kern

        def kern(*refs):
            n_in = len(args)
            inner(*refs[:n_in - 1], *refs[n_in:])

    return pl.pallas_call(
        kern,
        grid=grid,
        in_specs=in_specs,
        out_specs=out_spec,
        out_shape=jax.ShapeDtypeStruct((t, A_HEADS * HEAD_W), BF16),
        scratch_shapes=[pltpu.VMEM((2 * tq, HEAD_W), BF16),
                        pltpu.VMEM((2 * tq, 1), F32),
                        pltpu.VMEM((2 * tq, 1), F32),
                        pltpu.VMEM((2 * tq, HEAD_W), F32)],
        input_output_aliases=aliases,
        compiler_params=_cparams(("arbitrary", "arbitrary", "arbitrary")),
        name="diff_attn" if use_lat else "diff_attn_ctx",
    )(*args)


def _na_bias_tables(rpb, rows):
    nblk = rows // NA_ROWS
    wr = min(WIN_R, rows)
    rq = jnp.arange(NA_ROWS)[:, None, None, None, None]
    jq = jnp.arange(GRID_W)[None, :, None, None, None]
    slot = jnp.arange(3)[None, None, :, None, None]
    rk = jnp.arange(NA_ROWS)[None, None, None, :, None]
    jk = jnp.arange(GRID_W)[None, None, None, None, :]
    tabs = []
    for b, dup in ((0, 0), (1, -1), (nblk - 1, 2)):
        r = NA_ROWS * b + rq
        r0 = jnp.clip(r - wr // 2, 0, rows - wr)
        rkey = NA_ROWS * (b - 1 + slot) + rk
        ok_row = (rkey >= r0) & (rkey < r0 + wr) & (slot != dup)
        c0 = jnp.clip(jq - WIN_C // 2, 0, GRID_W - WIN_C)
        ok_col = (jk >= c0) & (jk < c0 + WIN_C)
        drow = jnp.clip(rkey - r + (WIN_R - 1), 0, 2 * WIN_R - 2)
        dcol = jnp.clip(jk - jq + (WIN_C - 1), 0, 2 * WIN_C - 2)
        shape = (NA_ROWS, GRID_W, 3, NA_ROWS, GRID_W)
        bias = rpb[:, jnp.broadcast_to(drow, shape), jnp.broadcast_to(dcol, shape)]
        tab = jnp.where(jnp.broadcast_to(ok_row & ok_col, shape)[None], bias, NEG_INF)
        tabs.append(tab.reshape(rpb.shape[0], NA_ROWS * GRID_W, 3 * NA_ROWS * GRID_W))
    return jnp.stack(tabs, axis=1).astype(F32)


def _natten_kernel(tab_ref, q_ref, k0_ref, k1_ref, k2_ref, v0_ref, v1_ref, v2_ref, kc_ref, vc_ref, o_ref):
    q = q_ref[...]
    nt = (((1,), (1,)), ((), ()))
    blk = q.shape[0]
    s_loc = [lax.dot_general(q, k_ref[...], nt, preferred_element_type=F32)
             + tab_ref[:, i * blk:(i + 1) * blk] for i, k_ref in enumerate((k0_ref, k1_ref, k2_ref))]
    s_ctx = lax.dot_general(q, kc_ref[...], nt, preferred_element_type=F32)
    m = jnp.max(s_ctx, axis=-1, keepdims=True)
    for s in s_loc:
        m = jnp.maximum(m, jnp.max(s, axis=-1, keepdims=True))
    p = jnp.exp(s_ctx - m)
    l = jnp.sum(p, axis=-1, keepdims=True)
    acc = jnp.dot(p.astype(BF16), vc_ref[...], preferred_element_type=F32)
    for s, v_ref in zip(s_loc, (v0_ref, v1_ref, v2_ref)):
        p = jnp.exp(s - m)
        l = l + jnp.sum(p, axis=-1, keepdims=True)
        acc = acc + jnp.dot(p.astype(BF16), v_ref[...], preferred_element_type=F32)
    o_ref[...] = (acc / l).astype(o_ref.dtype)


def _natten(qkv, tabs, n_lat, n_ctx):
    t = qkv.shape[0]
    blk = NA_ROWS * GRID_W
    nblk = n_lat // blk
    assert nblk >= 3 and blk == n_ctx
    qo, ko, vo = 3 * A_HEADS, 3 * A_HEADS + B_HEADS, 3 * A_HEADS + 2 * B_HEADS
    ctx_blk = n_lat // n_ctx

    def kv_spec(off, shift):
        return pl.BlockSpec((blk, HEAD_W), lambda h, b: (jnp.clip(b + shift, 0, nblk - 1), off + h))

    return pl.pallas_call(
        _natten_kernel,
        grid=(B_HEADS, nblk),
        in_specs=[pl.BlockSpec((None, None, blk, 3 * blk),
                               lambda h, b: (h, jnp.where(b == 0, 0, jnp.where(b == nblk - 1, 2, 1)), 0, 0)),
                  pl.BlockSpec((blk, HEAD_W), lambda h, b: (b, qo + h)),
                  kv_spec(ko, -1), kv_spec(ko, 0), kv_spec(ko, 1),
                  kv_spec(vo, -1), kv_spec(vo, 0), kv_spec(vo, 1),
                  pl.BlockSpec((n_ctx, HEAD_W), lambda h, b: (ctx_blk, ko + h)),
                  pl.BlockSpec((n_ctx, HEAD_W), lambda h, b: (ctx_blk, vo + h))],
        out_specs=pl.BlockSpec((blk, HEAD_W), lambda h, b: (b, h)),
        out_shape=jax.ShapeDtypeStruct((t, B_HEADS * HEAD_W), BF16),
        compiler_params=_cparams(("arbitrary", "arbitrary")),
        name="natten",
    )(tabs, qkv, qkv, qkv, qkv, qkv, qkv, qkv, qkv, qkv)


def _ctx_attn_kernel(q_ref, k_ref, v_ref, prev_ref, o_ref):
    del prev_ref
    _, l, acc = _softmax_block(q_ref[...], k_ref[...], v_ref[...])
    o_ref[...] = (acc / l).astype(o_ref.dtype)


def _ctx_attn(qkv, prev, n_lat, n_ctx):
    qo, ko, vo = 3 * A_HEADS, 3 * A_HEADS + B_HEADS, 3 * A_HEADS + 2 * B_HEADS
    ctx_blk = n_lat // n_ctx
    return pl.pallas_call(
        _ctx_attn_kernel,
        grid=(B_HEADS,),
        in_specs=[pl.BlockSpec((n_ctx, HEAD_W), lambda h: (ctx_blk, qo + h)),
                  pl.BlockSpec((n_ctx, HEAD_W), lambda h: (ctx_blk, ko + h)),
                  pl.BlockSpec((n_ctx, HEAD_W), lambda h: (ctx_blk, vo + h)),
                  pl.BlockSpec(memory_space=pl.ANY)],
        out_specs=pl.BlockSpec((n_ctx, HEAD_W), lambda h: (ctx_blk, h)),
        out_shape=jax.ShapeDtypeStruct(prev.shape, prev.dtype),
        input_output_aliases={3: 0},
        compiler_params=_cparams(("arbitrary",)),
        name="ctx_attn",
    )(qkv, qkv, qkv, prev)


def _s5_tables(lam_re, lam_im, b_re, b_im, c_re, c_im, log_step, d_skip):
    hp = lax.Precision.HIGHEST
    ln = S5_CHUNK
    g, p, c = S5_GROUPS, S5_STATE, S5_GROUP
    lr, li = lam_re.astype(F32), lam_im.astype(F32)
    dt = jnp.exp(log_step.astype(F32))[:, :, None]
    mag = jnp.exp(lr * dt)
    ar, ai = mag * jnp.cos(li * dt), mag * jnp.sin(li * dt)
    den = lr * lr + li * li
    nr, ni = ar - 1.0, ai
    fr = (nr * lr + ni * li) / den
    fi = (ni * lr - nr * li) / den
    br_, bi_ = b_re.astype(F32), b_im.astype(F32)
    bbr = fr[..., None] * br_ - fi[..., None] * bi_
    bbi = fr[..., None] * bi_ + fi[..., None] * br_
    lag = jnp.arange(ln + 1, dtype=F32)[:, None, None, None]
    magl = jnp.exp(lr * dt * lag)
    pr, pi_ = magl * jnp.cos(li * dt * lag), magl * jnp.sin(li * dt * lag)
    wr = pr[..., None] * bbr - pi_[..., None] * bbi
    wi = pr[..., None] * bbi + pi_[..., None] * bbr
    cr, ci = c_re.astype(F32), c_im.astype(F32)
    kern = (jnp.einsum('dgcp,ldgpe->ldgce', cr, wr[:ln], precision=hp)
            - jnp.einsum('dgcp,ldgpe->ldgce', ci, wi[:ln], precision=hp))
    s_idx = jnp.arange(ln)[:, None]
    t_idx = jnp.arange(ln)[None, :]
    lag_f = t_idx - s_idx
    kf = kern[jnp.clip(lag_f, 0, ln - 1), 0] * (lag_f >= 0)[:, :, None, None, None]
    kb = kern[jnp.clip(-lag_f, 0, ln - 1), 1] * (lag_f <= 0)[:, :, None, None, None]
    m_tab = (kf + kb).transpose(2, 0, 4, 1, 3).reshape(g, ln * c, ln * c)
    wf_r, wf_i = wr[:ln, 0][::-1], wi[:ln, 0][::-1]
    wb_r, wb_i = wr[:ln, 1], wi[:ln, 1]

    def inj(w):
        return w.transpose(1, 0, 3, 2).reshape(g, ln * c, p)

    b_tab = jnp.concatenate([inj(wf_r), inj(wf_i), inj(wb_r), inj(wb_i)], axis=-1)
    pf_r, pf_i = pr[1:ln + 1, 0], pi_[1:ln + 1, 0]
    pb_r, pb_i = pr[1:ln + 1, 1][::-1], pi_[1:ln + 1, 1][::-1]

    def rd(pw_r, pw_i, cre, cim):
        car = cre[None] * pw_r[:, :, None, :] - cim[None] * pw_i[:, :, None, :]
        cai = cre[None] * pw_i[:, :, None, :] + cim[None] * pw_r[:, :, None, :]
        to_rows = lambda a: a.transpose(1, 3, 0, 2).reshape(g, p, ln * c)
        return to_rows(car), to_rows(-cai)

    c_tab = jnp.concatenate(rd(pf_r, pf_i, cr[0], ci[0]) + rd(pb_r, pb_i, cr[1], ci[1]), axis=1)
    a_chunk = jnp.stack([jnp.stack([pr[ln, 0], pi_[ln, 0]]), jnp.stack([pr[ln, 1], pi_[ln, 1]])])
    d_tab = jnp.tile(d_skip.astype(F32).reshape(g, 1, c), (1, ln, 1)).reshape(g, 1, ln * c)
    return m_tab.astype(BF16), b_tab.astype(BF16), c_tab.astype(BF16), a_chunk, d_tab


def _s5_inject_kernel(u_ref, b_ref, o_ref):
    o_ref[...] = jnp.dot(u_ref[...], b_ref[...], preferred_element_type=F32)


def _s5_scan_kernel(a_ref, e_ref, o_ref, s_scr, *, nb):
    d = pl.program_id(0)

    @pl.when(pl.program_id(1) == 0)
    def _():
        s_scr[...] = jnp.zeros_like(s_scr)

    ar, ai = a_ref[0], a_ref[1]

    def body(i, carry):
        sr, si = carry
        c = jnp.where(d == 0, i, nb - 1 - i)
        o_ref[0, c] = sr
        o_ref[1, c] = si
        return ar * sr - ai * si + e_ref[0, c], ar * si + ai * sr + e_ref[1, c]

    sr, si = lax.fori_loop(0, nb, body, (s_scr[0], s_scr[1]))
    s_scr[0] = sr
    s_scr[1] = si


def _gelu_tanh(x):
    return 0.5 * x * (1.0 + jnp.tanh(math.sqrt(2.0 / math.pi) * (x + 0.044715 * (x * x * x))))


def _s5_readout_kernel(u_ref, m_ref, s_ref, c_ref, d_ref, o_ref):
    u = u_ref[...]
    y = jnp.dot(u, m_ref[...], preferred_element_type=F32) + u.astype(F32) * d_ref[...]
    for k in range(4):
        y = y + jnp.dot(s_ref[k].astype(BF16), c_ref[k * S5_STATE:(k + 1) * S5_STATE, :],
                        preferred_element_type=F32)
    o_ref[...] = _gelu_tanh(y)


def _s5_glu_kernel(z_ref, w_ref, o_ref):
    z = z_ref[...]
    o_ref[...] = (z * jax.nn.sigmoid(jnp.dot(z.astype(BF16), w_ref[...], preferred_element_type=F32))
                  ).astype(o_ref.dtype)


def _s5_mixer(u, tables, w_glu, n_lat, n_ctx):
    m_tab, b_tab, c_tab, a_chunk, d_tab = tables
    t = u.shape[0]
    g, c, ln, p = S5_GROUPS, S5_GROUP, S5_CHUNK, S5_STATE
    nc = t // ln
    w = ln * c
    ug = u.reshape(nc, ln, g, c).transpose(2, 0, 1, 3).reshape(g, nc, w)

    e = pl.pallas_call(
        _s5_inject_kernel,
        grid=(g,),
        in_specs=[pl.BlockSpec((None, nc, w), lambda i: (i, 0, 0)),
                  pl.BlockSpec((None, w, 4 * p), lambda i: (i, 0, 0))],
        out_specs=pl.BlockSpec((None, nc, 4 * p), lambda i: (i, 0, 0)),
        out_shape=jax.ShapeDtypeStruct((g, nc, 4 * p), F32),
        compiler_params=_cparams(("arbitrary",)),
        name="s5_inject",
    )(ug, b_tab)
    e = e.reshape(g, nc, 2, 2, p).transpose(2, 3, 1, 0, 4)

    nb = S5_BLOCK
    assert (n_lat // ln) % nb == 0 and n_ctx // ln == nb
    lat_blocks = n_lat // ln // nb

    def blk(d, s):
        return jnp.where(s == 0, lat_blocks, jnp.where(d == 0, s - 1, lat_blocks - s))

    s_in = pl.pallas_call(
        functools.partial(_s5_scan_kernel, nb=nb),
        grid=(2, lat_blocks + 1),
        in_specs=[pl.BlockSpec((None, 2, g, p), lambda d, s: (d, 0, 0, 0)),
                  pl.BlockSpec((None, 2, nb, g, p), lambda d, s: (d, 0, blk(d, s), 0, 0))],
        out_specs=pl.BlockSpec((None, 2, nb, g, p), lambda d, s: (d, 0, blk(d, s), 0, 0)),
        out_shape=jax.ShapeDtypeStruct((2, 2, nc, g, p), F32),
        scratch_shapes=[pltpu.VMEM((2, g, p), F32)],
        compiler_params=_cparams(("arbitrary", "arbitrary")),
        name="s5_scan",
    )(a_chunk, e)
    s_in = s_in.transpose(3, 0, 1, 2, 4).reshape(g, 4, nc, p)

    z = pl.pallas_call(
        _s5_readout_kernel,
        grid=(g,),
        in_specs=[pl.BlockSpec((None, nc, w), lambda i: (i, 0, 0)),
                  pl.BlockSpec((None, w, w), lambda i: (i, 0, 0)),
                  pl.BlockSpec((None, 4, nc, p), lambda i: (i, 0, 0, 0)),
                  pl.BlockSpec((None, 4 * p, w), lambda i: (i, 0, 0)),
                  pl.BlockSpec((None, 1, w), lambda i: (i, 0, 0))],
        out_specs=pl.BlockSpec((None, nc, w), lambda i: (i, 0, 0)),
        out_shape=jax.ShapeDtypeStruct((g, nc, w), F32),
        compiler_params=_cparams(("arbitrary",)),
        name="s5_readout",
    )(ug, m_tab, s_in, c_tab, d_tab)
    z = z.reshape(g, nc, ln, c).transpose(1, 2, 0, 3).reshape(t, g * c)

    tm = _row_tile(t, (1280, 640, 256, 128))
    return pl.pallas_call(
        _s5_glu_kernel,
        grid=(t // tm,),
        in_specs=[pl.BlockSpec((tm, g * c), lambda i: (i, 0)),
                  pl.BlockSpec((g * c, g * c), lambda i: (0, 0))],
        out_specs=pl.BlockSpec((tm, g * c), lambda i: (i, 0)),
        out_shape=jax.ShapeDtypeStruct((t, g * c), BF16),
        compiler_params=_cparams(("arbitrary",)),
        name="s5_glu",
    )(z, w_glu)


def _ret_tables(decay_logit):
    scale = HEAD_W ** -0.5
    lg = jax.nn.log_sigmoid(decay_logit.astype(F32))
    lf, lb = lg[0][:, None, None], lg[1][:, None, None]
    i = jnp.arange(RET_CHUNK, dtype=F32)[None, :, None]
    j = jnp.arange(RET_CHUNK, dtype=F32)[None, None, :]
    diff = i - j
    intra = (jnp.where(diff >= 0, jnp.exp(lf * jnp.maximum(diff, 0.0)), 0.0)
             + jnp.where(diff <= 0, jnp.exp(lb * jnp.maximum(-diff, 0.0)), 0.0)) * scale
    ones = jnp.ones((1, 1, HEAD_W), F32)
    q_f = jnp.exp(lf * (i + 1.0)) * ones
    k_f = jnp.exp(lf * (RET_CHUNK - 1.0 - i)) * scale * ones
    c_f = jnp.exp(lf * RET_CHUNK) * jnp.ones((1, RET_CHUNK, HEAD_W), F32)
    q_b = jnp.exp(lb * (RET_CHUNK - i)) * ones
    k_b = jnp.exp(lb * i) * scale * ones
    c_b = jnp.exp(lb * RET_CHUNK) * jnp.ones((1, RET_CHUNK, HEAD_W), F32)
    return intra, jnp.stack([q_f, k_f, c_f]), jnp.stack([q_b, k_b, c_b])


def _ret_state_step(q, k, v, dec_ref, s_scr, h):
    s = s_scr[h]
    qd = (q.astype(F32) * dec_ref[0, h]).astype(BF16)
    kd = (k.astype(F32) * dec_ref[1, h]).astype(BF16)
    o = jnp.dot(qd, s.astype(BF16), preferred_element_type=F32)
    s_scr[h] = s * dec_ref[2, h] + lax.dot_general(kd, v, (((0,), (0,)), ((), ())),
                                                    preferred_element_type=F32)
    return o


def _ret_fwd_kernel(intra_ref, dec_ref, q_ref, k_ref, v_ref, o_ref, s_scr):
    @pl.when(pl.program_id(0) == 0)
    def _():
        s_scr[...] = jnp.zeros_like(s_scr)

    for h in range(RET_HEADS):
        sl = slice(h * HEAD_W, (h + 1) * HEAD_W)
        q, k, v = q_ref[:, sl], k_ref[:, sl], v_ref[:, sl]
        att = lax.dot_general(q, k, (((1,), (1,)), ((), ())), preferred_element_type=F32) * intra_ref[h]
        o = jnp.dot(att.astype(BF16), v, preferred_element_type=F32)
        o_ref[:, sl] = o + _ret_state_step(q, k, v, dec_ref, s_scr, h)


def _ret_bwd_kernel(dec_ref, q_ref, k_ref, v_ref, g_ref, o1_ref, o_ref, s_scr):
    @pl.when(pl.program_id(0) == 0)
    def _():
        s_scr[...] = jnp.zeros_like(s_scr)

    for h in range(RET_HEADS):
        sl = slice(h * HEAD_W, (h + 1) * HEAD_W)
        q, k, v = q_ref[:, sl], k_ref[:, sl], v_ref[:, sl]
        o = o1_ref[:, sl] + _ret_state_step(q, k, v, dec_ref, s_scr, h)
        y = o * lax.rsqrt(jnp.mean(o * o, axis=-1, keepdims=True) + EPS)
        gate = g_ref[:, sl].astype(F32)
        o_ref[:, sl] = (y * (gate * jax.nn.sigmoid(gate))).astype(o_ref.dtype)


def _retention(proj, tables, n_lat, n_ctx):
    intra, dec_f, dec_b = tables
    t = proj.shape[0]
    wd = RET_HEADS * HEAD_W
    ck = RET_CHUNK
    n_lat_c, n_ctx_c = n_lat // ck, n_ctx // ck
    steps = n_lat_c + n_ctx_c

    def fwd_blk(s):
        return jnp.where(s < n_ctx_c, n_lat_c + s, s - n_ctx_c)

    def bwd_blk(s):
        return jnp.where(s < n_ctx_c, n_lat_c + n_ctx_c - 1 - s, n_lat_c - 1 - (s - n_ctx_c))

    def col_spec(col, order):
        return pl.BlockSpec((ck, wd), lambda s: (order(s), col))

    tab_spec = pl.BlockSpec((3, RET_HEADS, ck, HEAD_W), lambda s: (0, 0, 0, 0))
    o1 = pl.pallas_call(
        _ret_fwd_kernel,
        grid=(steps,),
        in_specs=[pl.BlockSpec((RET_HEADS, ck, ck), lambda s: (0, 0, 0)), tab_spec,
                  col_spec(0, fwd_blk), col_spec(1, fwd_blk), col_spec(2, fwd_blk)],
        out_specs=col_spec(0, fwd_blk),
        out_shape=jax.ShapeDtypeStruct((t, wd), F32),
        scratch_shapes=[pltpu.VMEM((RET_HEADS, HEAD_W, HEAD_W), F32)],
        compiler_params=_cparams(("arbitrary",)),
        name="retention_fwd",
    )(intra, dec_f, proj, proj, proj)
    return pl.pallas_call(
        _ret_bwd_kernel,
        grid=(steps,),
        in_specs=[tab_spec, col_spec(0, bwd_blk), col_spec(1, bwd_blk), col_spec(2, bwd_blk),
                  col_spec(3, bwd_blk), col_spec(0, bwd_blk)],
        out_specs=col_spec(0, bwd_blk),
        out_shape=jax.ShapeDtypeStruct((t, wd), BF16),
        scratch_shapes=[pltpu.VMEM((RET_HEADS, HEAD_W, HEAD_W), F32)],
        compiler_params=_cparams(("arbitrary",)),
        name="retention_bwd",
    )(dec_b, proj, proj, proj, proj, o1)


def _rope_tables(n_lat, n_ctx):
    n_freq = A_QK_DIM // 4
    freq = ROPE_BASE ** (-jnp.arange(n_freq, dtype=F32) / n_freq)
    tok = jnp.arange(n_lat)
    row = (tok // GRID_W).astype(F32)
    col = (tok % GRID_W).astype(F32)
    ang = jnp.stack([row[:, None] * freq, col[:, None] * freq], axis=1)
    ang = jnp.broadcast_to(ang[:, :, None, :], (n_lat, 2, 2, n_freq)).reshape(n_lat, A_QK_DIM)
    reps = HEAD_W // A_QK_DIM
    cos = jnp.concatenate([jnp.tile(jnp.cos(ang), (1, reps)), jnp.ones((n_ctx, HEAD_W), F32)], axis=0)
    sin = jnp.concatenate([jnp.tile(jnp.sin(ang), (1, reps)), jnp.zeros((n_ctx, HEAD_W), F32)], axis=0)
    return cos, sin


def kernel(x, c, ctx, c_ctx, ada_w, ada_b, norm1_w, norm2_w, ffn_w13, ffn_w2, e_w_in, e_w_out, diff_lq1, diff_lk1, diff_lq2, diff_lk2, diff_subln_w, na_rpb, o_w_in, o_w_out, s5_lam_re, s5_lam_im, s5_b_re, s5_b_im, s5_c_re, s5_c_im, s5_log_step, s5_d, s5_w_glu, ret_decay_logit, final_norm_w):
    bsz, n_lat, d = x.shape
    n_ctx = ctx.shape[1]
    depth = ada_w.shape[0]
    assert bsz == 1
    xs = jnp.concatenate([x[0], ctx[0]], axis=0)
    mods_all = _mods(c, c_ctx, ada_w, ada_b)
    cos, sin = _rope_tables(n_lat, n_ctx)
    ret_w = RET_HEADS * HEAD_W

    for i in range(depth):
        compute_ctx = i != depth - 1
        mods = mods_all[i]
        j = i // 2
        h = _normmod(xs, norm1_w[i], mods, n_lat, 0, 1)
        if i % 2 == 0:
            lambda_init = 0.8 - 0.6 * math.exp(-0.3 * i)
            qkv = _inproj_even(h, e_w_in[j].astype(BF16), cos, sin)
            lam_args = (diff_lq1[j], diff_lk1[j], diff_lq2[j], diff_lk2[j], diff_subln_w[j])
            o_a = _diff_attn(qkv, *lam_args, n_lat, n_ctx, lambda_init)
            o_b = _natten(qkv, _na_bias_tables(na_rpb[j], n_lat // GRID_W), n_lat, n_ctx)
            if compute_ctx:
                o_a = _diff_attn(qkv, *lam_args, n_lat, n_ctx, lambda_init, prev=o_a)
                o_b = _ctx_attn(qkv, o_b, n_lat, n_ctx)
            xs = _gated_residual([o_a, o_b], e_w_out[j].astype(BF16), xs, mods, n_lat, 2, "outproj_even")
        else:
            w_in = o_w_in[j]
            w_in = jnp.concatenate([w_in[:, S5_CH:], w_in[:, :S5_CH]], axis=1).astype(BF16)
            proj = _matmul(h, w_in, 512, "inproj_odd")
            s5_tabs = _s5_tables(s5_lam_re[j], s5_lam_im[j], s5_b_re[j], s5_b_im[j], s5_c_re[j], s5_c_im[j],
                                 s5_log_step[j], s5_d[j])
            y_c = _s5_mixer(proj[:, 4 * ret_w:], s5_tabs, s5_w_glu[j].astype(BF16), n_lat, n_ctx)
            y_d = _retention(proj, _ret_tables(ret_decay_logit[j]), n_lat, n_ctx)
            xs = _gated_residual([y_c, y_d], o_w_out[j].astype(BF16), xs, mods, n_lat, 2, "outproj_odd")
        h = _normmod(xs, norm2_w[i], mods, n_lat, 3, 4)
        a = _ffn_up(h, ffn_w13[i].astype(BF16))
        xs = _gated_residual([a], ffn_w2[i].astype(BF16), xs, mods, n_lat, 5, "ffn_down")
    return _final_norm(xs, final_norm_w, n_lat)[None]
```

```python
import functools
import math

import jax
import jax.numpy as jnp
from jax import lax
from jax.experimental import pallas as pl
from jax.experimental.pallas import tpu as pltpu

F32 = jnp.float32
BF16 = jnp.bfloat16

GRID_W = 64
A_HEADS = 8
A_QK_DIM = 64
HEAD_W = 128
B_HEADS = 8
WIN_R = 8
WIN_C = 16
NA_ROWS = 4
S5_CH = 512
S5_GROUP = 16
S5_GROUPS = 32
S5_STATE = 64
S5_CHUNK = 16
S5_BLOCK = 16
RET_HEADS = 12
RET_CHUNK = 128
ROPE_BASE = 10000.0
EPS = 1e-6
NEG_INF = -1e30
VMEM_LIMIT = 56 * 1024 * 1024
N_MOD = 8


def _cparams(sem):
    return pltpu.CompilerParams(dimension_semantics=sem, vmem_limit_bytes=VMEM_LIMIT)


def _row_tile(t, candidates):
    for c in candidates:
        if t % c == 0:
            return c
    raise ValueError(f"no row tile for {t}")


def _pick_mod(mods_ref, idx, is_ctx):
    return jnp.where(is_ctx, mods_ref[1, idx:idx + 1, :], mods_ref[0, idx:idx + 1, :])


def _is_ctx_rows(tm, n_lat, axis):
    row = pl.program_id(axis) * tm + lax.broadcasted_iota(jnp.int32, (tm, 1), 0)
    return row >= n_lat


def _mods_kernel(s_ref, w_ref, b_ref, o_ref):
    s = s_ref[...]
    s = s * jax.nn.sigmoid(s)
    o_ref[...] = jnp.dot(s, w_ref[...], preferred_element_type=F32,
                         precision=lax.Precision.HIGHEST) + b_ref[...]


def _mods(c, c_ctx, ada_w, ada_b):
    depth, d, w6 = ada_w.shape
    s = jnp.zeros((8, d), F32).at[0].set(c[0]).at[1].set(c_ctx)
    tn = 1024
    out = pl.pallas_call(
        _mods_kernel,
        grid=(depth, w6 // tn),
        in_specs=[pl.BlockSpec((8, d), lambda l, j: (0, 0)),
                  pl.BlockSpec((None, d, tn), lambda l, j: (l, 0, j)),
                  pl.BlockSpec((None, 1, tn), lambda l, j: (l, 0, j))],
        out_specs=pl.BlockSpec((None, 8, tn), lambda l, j: (l, 0, j)),
        out_shape=jax.ShapeDtypeStruct((depth, 8, w6), F32),
        compiler_params=_cparams(("arbitrary", "arbitrary")),
        name="ada_mods",
    )(s, ada_w, ada_b.reshape(depth, 1, w6))
    m = out[:, :2].reshape(depth, 2, 6, d)
    return jnp.pad(m, ((0, 0), (0, 0), (0, N_MOD - 6), (0, 0)))


def _normmod_kernel(x_ref, w_ref, mods_ref, o_ref, *, n_lat, tm, shift_idx, scale_idx):
    x = x_ref[...]
    y = x * lax.rsqrt(jnp.mean(x * x, axis=-1, keepdims=True) + EPS) * w_ref[...]
    is_ctx = _is_ctx_rows(tm, n_lat, 0)
    shift = _pick_mod(mods_ref, shift_idx, is_ctx)
    scale = _pick_mod(mods_ref, scale_idx, is_ctx)
    o_ref[...] = (y * (1.0 + scale) + shift).astype(o_ref.dtype)


def _normmod(x, w, mods, n_lat, shift_idx, scale_idx):
    t, d = x.shape
    tm = _row_tile(t, (640, 256, 128))
    return pl.pallas_call(
        functools.partial(_normmod_kernel, n_lat=n_lat, tm=tm, shift_idx=shift_idx, scale_idx=scale_idx),
        grid=(t // tm,),
        in_specs=[pl.BlockSpec((tm, d), lambda i: (i, 0)),
                  pl.BlockSpec((1, d), lambda i: (0, 0)),
                  pl.BlockSpec((2, N_MOD, d), lambda i: (0, 0, 0))],
        out_specs=pl.BlockSpec((tm, d), lambda i: (i, 0)),
        out_shape=jax.ShapeDtypeStruct((t, d), BF16),
        compiler_params=_cparams(("arbitrary",)),
        name="norm_modulate",
    )(x, w.reshape(1, d), mods)


def _final_norm_kernel(x_ref, w_ref, o_ref):
    x = x_ref[...]
    o_ref[...] = x * lax.rsqrt(jnp.mean(x * x, axis=-1, keepdims=True) + EPS) * w_ref[...]


def _final_norm(x, w, n_lat):
    d = x.shape[1]
    tm = _row_tile(n_lat, (512, 256, 128))
    return pl.pallas_call(
        _final_norm_kernel,
        grid=(n_lat // tm,),
        in_specs=[pl.BlockSpec((tm, d), lambda i: (i, 0)),
                  pl.BlockSpec((1, d), lambda i: (0, 0))],
        out_specs=pl.BlockSpec((tm, d), lambda i: (i, 0)),
        out_shape=jax.ShapeDtypeStruct((n_lat, d), F32),
        compiler_params=_cparams(("arbitrary",)),
        name="final_norm",
    )(x, w.reshape(1, d))


def _rope_store(acc, cos, sin, o_ref, scale):
    first_half = (lax.broadcasted_iota(jnp.int32, (1, HEAD_W), 1) % 32) < 16
    for c in range(acc.shape[1] // HEAD_W):
        x = acc[:, c * HEAD_W:(c + 1) * HEAD_W]
        rot = jnp.where(first_half, -pltpu.roll(x, HEAD_W - 16, 1), pltpu.roll(x, 16, 1))
        o_ref[:, c * HEAD_W:(c + 1) * HEAD_W] = ((x * cos + rot * sin) * scale).astype(o_ref.dtype)


def _inproj_even_kernel(h_ref, w_ref, cos_ref, sin_ref, o_ref, *, a_scale, b_scale):
    j = pl.program_id(0)
    acc = jnp.dot(h_ref[...], w_ref[...], preferred_element_type=F32)

    @pl.when(j == 0)
    def _():
        _rope_store(acc, cos_ref[...], sin_ref[...], o_ref, a_scale)

    @pl.when(j == 1)
    def _():
        _rope_store(acc, cos_ref[...], sin_ref[...], o_ref, 1.0)

    @pl.when(j == 3)
    def _():
        o_ref[...] = (acc * b_scale).astype(o_ref.dtype)

    @pl.when((j == 2) | (j > 3))
    def _():
        o_ref[...] = acc.astype(o_ref.dtype)


def _inproj_even(h, w, cos, sin):
    t, d = h.shape
    n = w.shape[1]
    tn = n // 6
    tm = _row_tile(t, (1280, 640, 256, 128))
    return pl.pallas_call(
        functools.partial(_inproj_even_kernel, a_scale=A_QK_DIM ** -0.5 * math.log2(math.e),
                          b_scale=HEAD_W ** -0.5),
        grid=(6, t // tm),
        in_specs=[pl.BlockSpec((tm, d), lambda j, i: (i, 0)),
                  pl.BlockSpec((d, tn), lambda j, i: (0, j)),
                  pl.BlockSpec((tm, HEAD_W), lambda j, i: (i, 0)),
                  pl.BlockSpec((tm, HEAD_W), lambda j, i: (i, 0))],
        out_specs=pl.BlockSpec((tm, tn), lambda j, i: (i, j)),
        out_shape=jax.ShapeDtypeStruct((t, n), BF16),
        compiler_params=_cparams(("arbitrary", "arbitrary")),
        name="inproj_even",
    )(h, w, cos, sin)


def _matmul_kernel(h_ref, w_ref, o_ref):
    o_ref[...] = jnp.dot(h_ref[...], w_ref[...], preferred_element_type=F32).astype(o_ref.dtype)


def _matmul(h, w, tn, name):
    t, d = h.shape
    n = w.shape[1]
    tm = _row_tile(t, (1280, 640, 256, 128))
    return pl.pallas_call(
        _matmul_kernel,
        grid=(n // tn, t // tm),
        in_specs=[pl.BlockSpec((tm, d), lambda j, i: (i, 0)),
                  pl.BlockSpec((d, tn), lambda j, i: (0, j))],
        out_specs=pl.BlockSpec((tm, tn), lambda j, i: (i, j)),
        out_shape=jax.ShapeDtypeStruct((t, n), BF16),
        compiler_params=_cparams(("arbitrary", "arbitrary")),
        name=name,
    )(h, w)


def _ffn_up_kernel(h_ref, w1_ref, w3_ref, o_ref):
    h = h_ref[...]
    a = jnp.dot(h, w1_ref[...], preferred_element_type=F32)
    b = jnp.dot(h, w3_ref[...], preferred_element_type=F32)
    o_ref[...] = (a * jax.nn.sigmoid(a) * b).astype(o_ref.dtype)


def _ffn_up(h, w13):
    t, d = h.shape
    d_ff = w13.shape[1] // 2
    tn = 512
    nj = d_ff // tn
    tm = _row_tile(t, (1280, 640, 256, 128))
    return pl.pallas_call(
        _ffn_up_kernel,
        grid=(nj, t // tm),
        in_specs=[pl.BlockSpec((tm, d), lambda j, i: (i, 0)),
                  pl.BlockSpec((d, tn), lambda j, i: (0, j)),
                  pl.BlockSpec((d, tn), lambda j, i: (0, j + nj))],
        out_specs=pl.BlockSpec((tm, tn), lambda j, i: (i, j)),
        out_shape=jax.ShapeDtypeStruct((t, d_ff), BF16),
        compiler_params=_cparams(("arbitrary", "arbitrary")),
        name="ffn_up",
    )(h, w13, w13)


def _gated_residual_kernel(*refs, n_a, n_lat, tm, gate_idx):
    a_refs = refs[:n_a]
    w_ref, x_ref, mods_ref, o_ref = refs[n_a:]
    k0 = 0
    y = None
    for a_ref in a_refs:
        kk = a_ref.shape[1]
        part = jnp.dot(a_ref[...], w_ref[k0:k0 + kk, :], preferred_element_type=F32)
        y = part if y is None else y + part
        k0 += kk
    gate = _pick_mod(mods_ref, gate_idx, _is_ctx_rows(tm, n_lat, 1))
    o_ref[...] = x_ref[...] + gate * y


def _gated_residual(a_list, w, x, mods, n_lat, gate_idx, name):
    t, d = x.shape
    k = w.shape[0]
    tn = 512
    tm = _row_tile(t, (640, 256, 128))
    n_a = len(a_list)
    in_specs = [pl.BlockSpec((tm, a.shape[1]), lambda j, i: (i, 0)) for a in a_list]
    in_specs += [pl.BlockSpec((k, tn), lambda j, i: (0, j)),
                 pl.BlockSpec((tm, tn), lambda j, i: (i, j)),
                 pl.BlockSpec((2, N_MOD, tn), lambda j, i: (0, 0, j))]
    return pl.pallas_call(
        functools.partial(_gated_residual_kernel, n_a=n_a, n_lat=n_lat, tm=tm, gate_idx=gate_idx),
        grid=(d // tn, t // tm),
        in_specs=in_specs,
        out_specs=pl.BlockSpec((tm, tn), lambda j, i: (i, j)),
        out_shape=jax.ShapeDtypeStruct((t, d), F32),
        input_output_aliases={n_a + 1: 0},
        compiler_params=_cparams(("arbitrary", "arbitrary")),
        name=name,
    )(*a_list, w, x, mods)


def _softmax_block(qs, k, v):
    s = lax.dot_general(qs, k, (((1,), (1,)), ((), ())), preferred_element_type=F32)
    m = jnp.max(s, axis=-1, keepdims=True)
    p = jnp.exp(s - m)
    l = jnp.sum(p, axis=-1, keepdims=True)
    acc = jnp.dot(p.astype(v.dtype), v, preferred_element_type=F32)
    return m, l, acc


DIFF_ROW_CHUNK = 256


def _diff_attn_kernel(*refs, tq, tk, n_chunks, lambda_init):
    if n_chunks:
        (lq1, lk1, lq2, lk2, subw, q_ref, kc_ref, vc_ref, k_ref, v_ref, o_ref,
         qs_scr, sa_scr, sb_scr, m_scr, l_scr, acc_scr) = refs
    else:
        (lq1, lk1, lq2, lk2, subw, q_ref, kc_ref, vc_ref, o_ref, qs_scr, m_scr, l_scr, acc_scr) = refs
    nt = (((1,), (1,)), ((), ()))
    q = q_ref[...]
    comp1 = lax.broadcasted_iota(jnp.int32, (1, HEAD_W), 1) < A_QK_DIM
    zero = jnp.zeros_like(q)
    qs_scr[0:tq, :] = jnp.where(comp1, q, zero)
    qs_scr[tq:2 * tq, :] = jnp.where(comp1, zero, q)

    s = lax.dot_general(qs_scr[...], kc_ref[...], nt, preferred_element_type=F32)
    m = jnp.max(s, axis=-1, keepdims=True)
    p = jnp.exp2(s - m)
    m_scr[...] = m
    l_scr[...] = jnp.sum(p, axis=-1, keepdims=True)
    acc_scr[...] = jnp.dot(p.astype(BF16), vc_ref[...], preferred_element_type=F32)

    def qk(j, s_buf):
        off = pl.multiple_of(j * tk, tk)
        s_buf[...] = lax.dot_general(qs_scr[...], k_ref[pl.ds(off, tk), :], nt, preferred_element_type=F32)

    def soft_pv(j, s_buf):
        off = pl.multiple_of(j * tk, tk)
        v = v_ref[pl.ds(off, tk), :]
        rc = min(DIFF_ROW_CHUNK, 2 * tq)
        for r in range(2 * tq // rc):
            rows = slice(r * rc, (r + 1) * rc)
            sc = s_buf[rows, :]
            m_prev = m_scr[rows, :]
            m_new = jnp.maximum(m_prev, jnp.max(sc, axis=-1, keepdims=True))
            alpha = jnp.exp2(m_prev - m_new)
            pc = jnp.exp2(sc - m_new)
            l_scr[rows, :] = alpha * l_scr[rows, :] + jnp.sum(pc, axis=-1, keepdims=True)
            acc_scr[rows, :] = alpha * acc_scr[rows, :] + jnp.dot(pc.astype(BF16), v, preferred_element_type=F32)
            m_scr[rows, :] = m_new

    if n_chunks == 1:
        qk(0, sa_scr)
        soft_pv(0, sa_scr)
    elif n_chunks > 1:
        qk(0, sa_scr)

        def pair(j2, carry):
            qk(2 * j2 + 1, sb_scr)
            soft_pv(2 * j2, sa_scr)
            qk(2 * j2 + 2, sa_scr)
            soft_pv(2 * j2 + 1, sb_scr)
            return carry

        lax.fori_loop(0, n_chunks // 2 - 1, pair, 0)
        qk(n_chunks - 1, sb_scr)
        soft_pv(n_chunks - 2, sa_scr)
        soft_pv(n_chunks - 1, sb_scr)

    lam = (jnp.exp(jnp.sum(lq1[...] * lk1[...], axis=-1, keepdims=True))
           - jnp.exp(jnp.sum(lq2[...] * lk2[...], axis=-1, keepdims=True)) + lambda_init)
    o1 = acc_scr[0:tq, :] / l_scr[0:tq, :]
    o2 = acc_scr[tq:2 * tq, :] / l_scr[tq:2 * tq, :]
    o = o1 - lam * o2
    y = o * lax.rsqrt(jnp.mean(o * o, axis=-1, keepdims=True) + EPS) * subw[...]
    o_ref[...] = (y * (1.0 - lambda_init)).astype(o_ref.dtype)


def _diff_attn(qkv, lq1, lk1, lq2, lk2, subw, n_lat, n_ctx, lambda_init, prev=None):
    t = qkv.shape[0]
    use_lat = prev is None
    ctx_blk = n_lat // n_ctx
    small = [lq1.reshape(1, -1), lk1.reshape(1, -1), lq2.reshape(1, -1), lk2.reshape(1, -1), subw.reshape(1, -1)]
    small_specs = [pl.BlockSpec(a.shape, lambda h, qi: (0, 0)) for a in small]
    kc_spec = pl.BlockSpec((n_ctx, HEAD_W), lambda h, qi: (ctx_blk, A_HEADS + h))
    vc_spec = pl.BlockSpec((n_ctx, HEAD_W), lambda h, qi: (ctx_blk, 2 * A_HEADS + h))
    if use_lat:
        tq = _row_tile(n_lat, (512, 256, 128))
        tk = _row_tile(n_lat, (1024, 512, 256, 128))
        n_chunks = n_lat // tk
        assert n_chunks == 1 or n_chunks % 2 == 0
        grid = (A_HEADS, n_lat // tq)
        in_specs = small_specs + [
            pl.BlockSpec((tq, HEAD_W), lambda h, qi: (qi, h)), kc_spec, vc_spec,
            pl.BlockSpec((n_lat, HEAD_W), lambda h, qi: (0, A_HEADS + h)),
            pl.BlockSpec((n_lat, HEAD_W), lambda h, qi: (0, 2 * A_HEADS + h))]
        args = small + [qkv, qkv, qkv, qkv, qkv]
        out_spec = pl.BlockSpec((tq, HEAD_W), lambda h, qi: (qi, h))
        aliases = {}
        score_scratch = [pltpu.VMEM((2 * tq, tk), F32), pltpu.VMEM((2 * tq, tk), F32)]
    else:
        tq, tk, n_chunks = n_ctx, 0, 0
        grid = (A_HEADS, 1)
        in_specs = small_specs + [
            pl.BlockSpec((tq, HEAD_W), lambda h, qi: (ctx_blk, h)), kc_spec, vc_spec,
            pl.BlockSpec(memory_space=pl.ANY)]
        args = small + [qkv, qkv, qkv, prev]
        out_spec = pl.BlockSpec((tq, HEAD_W), lambda h, qi: (ctx_blk, h))
        aliases = {len(args) - 1: 0}
        score_scratch = []

    kern = functools.partial(_diff_attn_kernel, tq=tq, tk=tk, n_chunks=n_chunks, lambda_init=lambda_init)
    if not use_lat:
        inner = kern

        def kern(*refs):
            n_in = len(args)
            inner(*refs[:n_in - 1], *refs[n_in:])

    return pl.pallas_call(
        kern,
        grid=grid,
        in_specs=in_specs,
        out_specs=out_spec,
        out_shape=jax.ShapeDtypeStruct((t, A_HEADS * HEAD_W), BF16),
        scratch_shapes=[pltpu.VMEM((2 * tq, HEAD_W), BF16)] + score_scratch + [
            pltpu.VMEM((2 * tq, 1), F32),
            pltpu.VMEM((2 * tq, 1), F32),
            pltpu.VMEM((2 * tq, HEAD_W), F32)],
        input_output_aliases=aliases,
        compiler_params=_cparams(("arbitrary", "arbitrary")),
        name="diff_attn" if use_lat else "diff_attn_ctx",
    )(*args)


def _na_bias_tables(rpb, rows):
    nblk = rows // NA_ROWS
    wr = min(WIN_R, rows)
    nh = rpb.shape[0]
    rq = jnp.arange(NA_ROWS)[:, None, None]
    slot = jnp.arange(3)[None, :, None]
    rk = jnp.arange(NA_ROWS)[None, None, :]
    row_sel, row_ok = [], []
    for b, dup in ((0, 0), (1, -1), (nblk - 1, 2)):
        r = NA_ROWS * b + rq
        r0 = jnp.clip(r - wr // 2, 0, rows - wr)
        rkey = NA_ROWS * (b - 1 + slot) + rk
        ok = (rkey >= r0) & (rkey < r0 + wr) & (slot != dup)
        drow = jnp.broadcast_to(rkey - r + (WIN_R - 1), ok.shape)
        row_sel.append((drow[..., None] == jnp.arange(2 * WIN_R - 1)) & ok[..., None])
        row_ok.append(ok)
    row_sel = jnp.stack(row_sel).astype(F32)
    row_ok = jnp.stack(row_ok)
    jq = jnp.arange(GRID_W)[:, None]
    jk = jnp.arange(GRID_W)[None, :]
    c0 = jnp.clip(jq - WIN_C // 2, 0, GRID_W - WIN_C)
    col_ok = (jk >= c0) & (jk < c0 + WIN_C)
    dcol = jnp.clip(jk - jq + (WIN_C - 1), 0, 2 * WIN_C - 2)
    col_sel = (dcol[..., None] == jnp.arange(2 * WIN_C - 1)).astype(F32)
    bias = jnp.einsum('vasbr,hrc,qkc->hvaqsbk', row_sel, rpb.astype(F32), col_sel,
                      precision=lax.Precision.HIGHEST)
    valid = row_ok[None, :, :, None, :, :, None] & col_ok[None, None, None, :, None, None, :]
    tab = jnp.where(valid, bias, NEG_INF)
    return tab.reshape(nh, 3, NA_ROWS * GRID_W, 3 * NA_ROWS * GRID_W)


def _natten_kernel(tab_ref, q_ref, k0_ref, k1_ref, k2_ref, v0_ref, v1_ref, v2_ref, kc_ref, vc_ref, o_ref):
    q = q_ref[...]
    nt = (((1,), (1,)), ((), ()))
    blk = q.shape[0]
    s_loc = [lax.dot_general(q, k_ref[...], nt, preferred_element_type=F32)
             + tab_ref[:, i * blk:(i + 1) * blk] for i, k_ref in enumerate((k0_ref, k1_ref, k2_ref))]
    s_ctx = lax.dot_general(q, kc_ref[...], nt, preferred_element_type=F32)
    m = jnp.max(s_ctx, axis=-1, keepdims=True)
    for s in s_loc:
        m = jnp.maximum(m, jnp.max(s, axis=-1, keepdims=True))
    p = jnp.exp(s_ctx - m)
    l = jnp.sum(p, axis=-1, keepdims=True)
    acc = jnp.dot(p.astype(BF16), vc_ref[...], preferred_element_type=F32)
    for s, v_ref in zip(s_loc, (v0_ref, v1_ref, v2_ref)):
        p = jnp.exp(s - m)
        l = l + jnp.sum(p, axis=-1, keepdims=True)
        acc = acc + jnp.dot(p.astype(BF16), v_ref[...], preferred_element_type=F32)
    o_ref[...] = (acc / l).astype(o_ref.dtype)


def _natten(qkv, tabs, n_lat, n_ctx):
    t = qkv.shape[0]
    blk = NA_ROWS * GRID_W
    nblk = n_lat // blk
    assert nblk >= 3 and blk == n_ctx
    qo, ko, vo = 3 * A_HEADS, 3 * A_HEADS + B_HEADS, 3 * A_HEADS + 2 * B_HEADS
    ctx_blk = n_lat // n_ctx

    def kv_spec(off, shift):
        return pl.BlockSpec((blk, HEAD_W), lambda h, b: (jnp.clip(b + shift, 0, nblk - 1), off + h))

    return pl.pallas_call(
        _natten_kernel,
        grid=(B_HEADS, nblk),
        in_specs=[pl.BlockSpec((None, None, blk, 3 * blk),
                               lambda h, b: (h, jnp.where(b == 0, 0, jnp.where(b == nblk - 1, 2, 1)), 0, 0)),
                  pl.BlockSpec((blk, HEAD_W), lambda h, b: (b, qo + h)),
                  kv_spec(ko, -1), kv_spec(ko, 0), kv_spec(ko, 1),
                  kv_spec(vo, -1), kv_spec(vo, 0), kv_spec(vo, 1),
                  pl.BlockSpec((n_ctx, HEAD_W), lambda h, b: (ctx_blk, ko + h)),
                  pl.BlockSpec((n_ctx, HEAD_W), lambda h, b: (ctx_blk, vo + h))],
        out_specs=pl.BlockSpec((blk, HEAD_W), lambda h, b: (b, h)),
        out_shape=jax.ShapeDtypeStruct((t, B_HEADS * HEAD_W), BF16),
        compiler_params=_cparams(("arbitrary", "arbitrary")),
        name="natten",
    )(tabs, qkv, qkv, qkv, qkv, qkv, qkv, qkv, qkv, qkv)


def _ctx_attn_kernel(q_ref, k_ref, v_ref, prev_ref, o_ref):
    del prev_ref
    _, l, acc = _softmax_block(q_ref[...], k_ref[...], v_ref[...])
    o_ref[...] = (acc / l).astype(o_ref.dtype)


def _ctx_attn(qkv, prev, n_lat, n_ctx):
    qo, ko, vo = 3 * A_HEADS, 3 * A_HEADS + B_HEADS, 3 * A_HEADS + 2 * B_HEADS
    ctx_blk = n_lat // n_ctx
    return pl.pallas_call(
        _ctx_attn_kernel,
        grid=(B_HEADS,),
        in_specs=[pl.BlockSpec((n_ctx, HEAD_W), lambda h: (ctx_blk, qo + h)),
                  pl.BlockSpec((n_ctx, HEAD_W), lambda h: (ctx_blk, ko + h)),
                  pl.BlockSpec((n_ctx, HEAD_W), lambda h: (ctx_blk, vo + h)),
                  pl.BlockSpec(memory_space=pl.ANY)],
        out_specs=pl.BlockSpec((n_ctx, HEAD_W), lambda h: (ctx_blk, h)),
        out_shape=jax.ShapeDtypeStruct(prev.shape, prev.dtype),
        input_output_aliases={3: 0},
        compiler_params=_cparams(("arbitrary",)),
        name="ctx_attn",
    )(qkv, qkv, qkv, prev)


def _s5_tables(lam_re, lam_im, b_re, b_im, c_re, c_im, log_step, d_skip):
    hp = lax.Precision.HIGHEST
    ln = S5_CHUNK
    g, p, c = S5_GROUPS, S5_STATE, S5_GROUP
    lr, li = lam_re.astype(F32), lam_im.astype(F32)
    dt = jnp.exp(log_step.astype(F32))[:, :, None]
    mag = jnp.exp(lr * dt)
    ar, ai = mag * jnp.cos(li * dt), mag * jnp.sin(li * dt)
    den = lr * lr + li * li
    nr, ni = ar - 1.0, ai
    fr = (nr * lr + ni * li) / den
    fi = (ni * lr - nr * li) / den
    br_, bi_ = b_re.astype(F32), b_im.astype(F32)
    bbr = fr[..., None] * br_ - fi[..., None] * bi_
    bbi = fr[..., None] * bi_ + fi[..., None] * br_
    lag = jnp.arange(ln + 1, dtype=F32)[:, None, None, None]
    magl = jnp.exp(lr * dt * lag)
    pr, pi_ = magl * jnp.cos(li * dt * lag), magl * jnp.sin(li * dt * lag)
    wr = pr[..., None] * bbr - pi_[..., None] * bbi
    wi = pr[..., None] * bbi + pi_[..., None] * bbr
    cr, ci = c_re.astype(F32), c_im.astype(F32)
    kern = (jnp.einsum('dgcp,ldgpe->ldgce', cr, wr[:ln], precision=hp)
            - jnp.einsum('dgcp,ldgpe->ldgce', ci, wi[:ln], precision=hp))
    s_idx = jnp.arange(ln)[:, None]
    t_idx = jnp.arange(ln)[None, :]
    lag_f = t_idx - s_idx
    kf = kern[jnp.clip(lag_f, 0, ln - 1), 0] * (lag_f >= 0)[:, :, None, None, None]
    kb = kern[jnp.clip(-lag_f, 0, ln - 1), 1] * (lag_f <= 0)[:, :, None, None, None]
    m_tab = (kf + kb).transpose(2, 0, 4, 1, 3).reshape(g, ln * c, ln * c)
    wf_r, wf_i = wr[:ln, 0][::-1], wi[:ln, 0][::-1]
    wb_r, wb_i = wr[:ln, 1], wi[:ln, 1]

    def inj(w):
        return w.transpose(1, 0, 3, 2).reshape(g, ln * c, p)

    b_tab = jnp.concatenate([inj(wf_r), inj(wf_i), inj(wb_r), inj(wb_i)], axis=-1)
    pf_r, pf_i = pr[1:ln + 1, 0], pi_[1:ln + 1, 0]
    pb_r, pb_i = pr[1:ln + 1, 1][::-1], pi_[1:ln + 1, 1][::-1]

    def rd(pw_r, pw_i, cre, cim):
        car = cre[None] * pw_r[:, :, None, :] - cim[None] * pw_i[:, :, None, :]
        cai = cre[None] * pw_i[:, :, None, :] + cim[None] * pw_r[:, :, None, :]
        to_rows = lambda a: a.transpose(1, 3, 0, 2).reshape(g, p, ln * c)
        return to_rows(car), to_rows(-cai)

    c_tab = jnp.concatenate(rd(pf_r, pf_i, cr[0], ci[0]) + rd(pb_r, pb_i, cr[1], ci[1]), axis=1)
    a_chunk = jnp.stack([jnp.stack([pr[ln, 0], pi_[ln, 0]]), jnp.stack([pr[ln, 1], pi_[ln, 1]])])
    d_tab = jnp.tile(d_skip.astype(F32).reshape(g, 1, c), (1, ln, 1)).reshape(g, 1, ln * c)
    return m_tab.astype(BF16), b_tab.astype(BF16), c_tab.astype(BF16), a_chunk, d_tab


def _s5_inject_kernel(u_ref, b_ref, o_ref):
    o_ref[...] = jnp.dot(u_ref[...], b_ref[...], preferred_element_type=F32)


def _s5_scan_kernel(a_ref, e_ref, o_ref, s_scr, *, nb):
    d = pl.program_id(0)

    @pl.when(pl.program_id(1) == 0)
    def _():
        s_scr[...] = jnp.zeros_like(s_scr)

    ar, ai = a_ref[0], a_ref[1]

    def body(i, carry):
        sr, si = carry
        c = jnp.where(d == 0, i, nb - 1 - i)
        o_ref[0, c] = sr
        o_ref[1, c] = si
        return ar * sr - ai * si + e_ref[0, c], ar * si + ai * sr + e_ref[1, c]

    sr, si = lax.fori_loop(0, nb, body, (s_scr[0], s_scr[1]))
    s_scr[0] = sr
    s_scr[1] = si


def _gelu_tanh(x):
    return 0.5 * x * (1.0 + jnp.tanh(math.sqrt(2.0 / math.pi) * (x + 0.044715 * (x * x * x))))


def _s5_readout_kernel(u_ref, m_ref, s_ref, c_ref, d_ref, o_ref):
    u = u_ref[...]
    y = jnp.dot(u, m_ref[...], preferred_element_type=F32) + u.astype(F32) * d_ref[...]
    for k in range(4):
        y = y + jnp.dot(s_ref[k].astype(BF16), c_ref[k * S5_STATE:(k + 1) * S5_STATE, :],
                        preferred_element_type=F32)
    o_ref[...] = _gelu_tanh(y)


def _s5_glu_kernel(z_ref, w_ref, o_ref):
    z = z_ref[...]
    o_ref[...] = (z * jax.nn.sigmoid(jnp.dot(z.astype(BF16), w_ref[...], preferred_element_type=F32))
                  ).astype(o_ref.dtype)


def _s5_mixer(u, tables, w_glu, n_lat, n_ctx):
    m_tab, b_tab, c_tab, a_chunk, d_tab = tables
    t = u.shape[0]
    g, c, ln, p = S5_GROUPS, S5_GROUP, S5_CHUNK, S5_STATE
    nc = t // ln
    w = ln * c
    ug = u.reshape(nc, ln, g, c).transpose(2, 0, 1, 3).reshape(g, nc, w)

    e = pl.pallas_call(
        _s5_inject_kernel,
        grid=(g,),
        in_specs=[pl.BlockSpec((None, nc, w), lambda i: (i, 0, 0)),
                  pl.BlockSpec((None, w, 4 * p), lambda i: (i, 0, 0))],
        out_specs=pl.BlockSpec((None, nc, 4 * p), lambda i: (i, 0, 0)),
        out_shape=jax.ShapeDtypeStruct((g, nc, 4 * p), F32),
        compiler_params=_cparams(("arbitrary",)),
        name="s5_inject",
    )(ug, b_tab)
    e = e.reshape(g, nc, 2, 2, p).transpose(2, 3, 1, 0, 4)

    nb = S5_BLOCK
    assert (n_lat // ln) % nb == 0 and n_ctx // ln == nb
    lat_blocks = n_lat // ln // nb

    def blk(d, s):
        return jnp.where(s == 0, lat_blocks, jnp.where(d == 0, s - 1, lat_blocks - s))

    s_in = pl.pallas_call(
        functools.partial(_s5_scan_kernel, nb=nb),
        grid=(2, lat_blocks + 1),
        in_specs=[pl.BlockSpec((None, 2, g, p), lambda d, s: (d, 0, 0, 0)),
                  pl.BlockSpec((None, 2, nb, g, p), lambda d, s: (d, 0, blk(d, s), 0, 0))],
        out_specs=pl.BlockSpec((None, 2, nb, g, p), lambda d, s: (d, 0, blk(d, s), 0, 0)),
        out_shape=jax.ShapeDtypeStruct((2, 2, nc, g, p), F32),
        scratch_shapes=[pltpu.VMEM((2, g, p), F32)],
        compiler_params=_cparams(("arbitrary", "arbitrary")),
        name="s5_scan",
    )(a_chunk, e)
    s_in = s_in.transpose(3, 0, 1, 2, 4).reshape(g, 4, nc, p)

    z = pl.pallas_call(
        _s5_readout_kernel,
        grid=(g,),
        in_specs=[pl.BlockSpec((None, nc, w), lambda i: (i, 0, 0)),
                  pl.BlockSpec((None, w, w), lambda i: (i, 0, 0)),
                  pl.BlockSpec((None, 4, nc, p), lambda i: (i, 0, 0, 0)),
                  pl.BlockSpec((None, 4 * p, w), lambda i: (i, 0, 0)),
                  pl.BlockSpec((None, 1, w), lambda i: (i, 0, 0))],
        out_specs=pl.BlockSpec((None, nc, w), lambda i: (i, 0, 0)),
        out_shape=jax.ShapeDtypeStruct((g, nc, w), F32),
        compiler_params=_cparams(("arbitrary",)),
        name="s5_readout",
    )(ug, m_tab, s_in, c_tab, d_tab)
    z = z.reshape(g, nc, ln, c).transpose(1, 2, 0, 3).reshape(t, g * c)

    tm = _row_tile(t, (1280, 640, 256, 128))
    return pl.pallas_call(
        _s5_glu_kernel,
        grid=(t // tm,),
        in_specs=[pl.BlockSpec((tm, g * c), lambda i: (i, 0)),
                  pl.BlockSpec((g * c, g * c), lambda i: (0, 0))],
        out_specs=pl.BlockSpec((tm, g * c), lambda i: (i, 0)),
        out_shape=jax.ShapeDtypeStruct((t, g * c), BF16),
        compiler_params=_cparams(("arbitrary",)),
        name="s5_glu",
    )(z, w_glu)


def _ret_tables(decay_logit):
    scale = HEAD_W ** -0.5
    lg = jax.nn.log_sigmoid(decay_logit.astype(F32))
    lf, lb = lg[0][:, None, None], lg[1][:, None, None]
    i = jnp.arange(RET_CHUNK, dtype=F32)[None, :, None]
    j = jnp.arange(RET_CHUNK, dtype=F32)[None, None, :]
    diff = i - j
    intra = (jnp.where(diff >= 0, jnp.exp(lf * jnp.maximum(diff, 0.0)), 0.0)
             + jnp.where(diff <= 0, jnp.exp(lb * jnp.maximum(-diff, 0.0)), 0.0)) * scale
    ones = jnp.ones((1, 1, HEAD_W), F32)
    q_f = jnp.exp(lf * (i + 1.0)) * ones
    k_f = jnp.exp(lf * (RET_CHUNK - 1.0 - i)) * scale * ones
    c_f = jnp.exp(lf * RET_CHUNK) * jnp.ones((1, RET_CHUNK, HEAD_W), F32)
    q_b = jnp.exp(lb * (RET_CHUNK - i)) * ones
    k_b = jnp.exp(lb * i) * scale * ones
    c_b = jnp.exp(lb * RET_CHUNK) * jnp.ones((1, RET_CHUNK, HEAD_W), F32)
    return intra, jnp.stack([q_f, k_f, c_f]), jnp.stack([q_b, k_b, c_b])


def _ret_state_step(q, k, v, dec_ref, s_scr, h):
    s = s_scr[h]
    qd = (q.astype(F32) * dec_ref[0, h]).astype(BF16)
    kd = (k.astype(F32) * dec_ref[1, h]).astype(BF16)
    o = jnp.dot(qd, s.astype(BF16), preferred_element_type=F32)
    s_scr[h] = s * dec_ref[2, h] + lax.dot_general(kd, v, (((0,), (0,)), ((), ())),
                                                    preferred_element_type=F32)
    return o


def _ret_fwd_kernel(intra_ref, dec_ref, q_ref, k_ref, v_ref, o_ref, s_scr):
    @pl.when(pl.program_id(0) == 0)
    def _():
        s_scr[...] = jnp.zeros_like(s_scr)

    for h in range(RET_HEADS):
        sl = slice(h * HEAD_W, (h + 1) * HEAD_W)
        q, k, v = q_ref[:, sl], k_ref[:, sl], v_ref[:, sl]
        att = lax.dot_general(q, k, (((1,), (1,)), ((), ())), preferred_element_type=F32) * intra_ref[h]
        o = jnp.dot(att.astype(BF16), v, preferred_element_type=F32)
        o_ref[:, sl] = o + _ret_state_step(q, k, v, dec_ref, s_scr, h)


def _ret_bwd_kernel(dec_ref, q_ref, k_ref, v_ref, g_ref, o1_ref, o_ref, s_scr):
    @pl.when(pl.program_id(0) == 0)
    def _():
        s_scr[...] = jnp.zeros_like(s_scr)

    for h in range(RET_HEADS):
        sl = slice(h * HEAD_W, (h + 1) * HEAD_W)
        q, k, v = q_ref[:, sl], k_ref[:, sl], v_ref[:, sl]
        o = o1_ref[:, sl] + _ret_state_step(q, k, v, dec_ref, s_scr, h)
        y = o * lax.rsqrt(jnp.mean(o * o, axis=-1, keepdims=True) + EPS)
        gate = g_ref[:, sl].astype(F32)
        o_ref[:, sl] = (y * (gate * jax.nn.sigmoid(gate))).astype(o_ref.dtype)


def _retention(proj, tables, n_lat, n_ctx):
    intra, dec_f, dec_b = tables
    t = proj.shape[0]
    wd = RET_HEADS * HEAD_W
    ck = RET_CHUNK
    n_lat_c, n_ctx_c = n_lat // ck, n_ctx // ck
    steps = n_lat_c + n_ctx_c

    def fwd_blk(s):
        return jnp.where(s < n_ctx_c, n_lat_c + s, s - n_ctx_c)

    def bwd_blk(s):
        return jnp.where(s < n_ctx_c, n_lat_c + n_ctx_c - 1 - s, n_lat_c - 1 - (s - n_ctx_c))

    def col_spec(col, order):
        return pl.BlockSpec((ck, wd), lambda s: (order(s), col))

    tab_spec = pl.BlockSpec((3, RET_HEADS, ck, HEAD_W), lambda s: (0, 0, 0, 0))
    o1 = pl.pallas_call(
        _ret_fwd_kernel,
        grid=(steps,),
        in_specs=[pl.BlockSpec((RET_HEADS, ck, ck), lambda s: (0, 0, 0)), tab_spec,
                  col_spec(0, fwd_blk), col_spec(1, fwd_blk), col_spec(2, fwd_blk)],
        out_specs=col_spec(0, fwd_blk),
        out_shape=jax.ShapeDtypeStruct((t, wd), F32),
        scratch_shapes=[pltpu.VMEM((RET_HEADS, HEAD_W, HEAD_W), F32)],
        compiler_params=_cparams(("arbitrary",)),
        name="retention_fwd",
    )(intra, dec_f, proj, proj, proj)
    return pl.pallas_call(
        _ret_bwd_kernel,
        grid=(steps,),
        in_specs=[tab_spec, col_spec(0, bwd_blk), col_spec(1, bwd_blk), col_spec(2, bwd_blk),
                  col_spec(3, bwd_blk), col_spec(0, bwd_blk)],
        out_specs=col_spec(0, bwd_blk),
        out_shape=jax.ShapeDtypeStruct((t, wd), BF16),
        scratch_shapes=[pltpu.VMEM((RET_HEADS, HEAD_W, HEAD_W), F32)],
        compiler_params=_cparams(("arbitrary",)),
        name="retention_bwd",
    )(dec_b, proj, proj, proj, proj, o1)


def _rope_tables(n_lat, n_ctx):
    n_freq = A_QK_DIM // 4
    freq = ROPE_BASE ** (-jnp.arange(n_freq, dtype=F32) / n_freq)
    tok = jnp.arange(n_lat)
    row = (tok // GRID_W).astype(F32)
    col = (tok % GRID_W).astype(F32)
    ang = jnp.stack([row[:, None] * freq, col[:, None] * freq], axis=1)
    ang = jnp.broadcast_to(ang[:, :, None, :], (n_lat, 2, 2, n_freq)).reshape(n_lat, A_QK_DIM)
    reps = HEAD_W // A_QK_DIM
    cos = jnp.concatenate([jnp.tile(jnp.cos(ang), (1, reps)), jnp.ones((n_ctx, HEAD_W), F32)], axis=0)
    sin = jnp.concatenate([jnp.tile(jnp.sin(ang), (1, reps)), jnp.zeros((n_ctx, HEAD_W), F32)], axis=0)
    return cos, sin


def kernel(x, c, ctx, c_ctx, ada_w, ada_b, norm1_w, norm2_w, ffn_w13, ffn_w2, e_w_in, e_w_out, diff_lq1, diff_lk1, diff_lq2, diff_lk2, diff_subln_w, na_rpb, o_w_in, o_w_out, s5_lam_re, s5_lam_im, s5_b_re, s5_b_im, s5_c_re, s5_c_im, s5_log_step, s5_d, s5_w_glu, ret_decay_logit, final_norm_w):
    bsz, n_lat, d = x.shape
    n_ctx = ctx.shape[1]
    depth = ada_w.shape[0]
    assert bsz == 1
    xs = jnp.concatenate([x[0], ctx[0]], axis=0)
    mods_all = _mods(c, c_ctx, ada_w, ada_b)
    cos, sin = _rope_tables(n_lat, n_ctx)
    ret_w = RET_HEADS * HEAD_W

    for i in range(depth):
        compute_ctx = i != depth - 1
        mods = mods_all[i]
        j = i // 2
        h = _normmod(xs, norm1_w[i], mods, n_lat, 0, 1)
        if i % 2 == 0:
            lambda_init = 0.8 - 0.6 * math.exp(-0.3 * i)
            qkv = _inproj_even(h, e_w_in[j].astype(BF16), cos, sin)
            lam_args = (diff_lq1[j], diff_lk1[j], diff_lq2[j], diff_lk2[j], diff_subln_w[j])
            o_a = _diff_attn(qkv, *lam_args, n_lat, n_ctx, lambda_init)
            o_b = _natten(qkv, _na_bias_tables(na_rpb[j], n_lat // GRID_W), n_lat, n_ctx)
            if compute_ctx:
                o_a = _diff_attn(qkv, *lam_args, n_lat, n_ctx, lambda_init, prev=o_a)
                o_b = _ctx_attn(qkv, o_b, n_lat, n_ctx)
            xs = _gated_residual([o_a, o_b], e_w_out[j].astype(BF16), xs, mods, n_lat, 2, "outproj_even")
        else:
            w_in = o_w_in[j]
            w_in = jnp.concatenate([w_in[:, S5_CH:], w_in[:, :S5_CH]], axis=1).astype(BF16)
            proj = _matmul(h, w_in, 512, "inproj_odd")
            s5_tabs = _s5_tables(s5_lam_re[j], s5_lam_im[j], s5_b_re[j], s5_b_im[j], s5_c_re[j], s5_c_im[j],
                                 s5_log_step[j], s5_d[j])
            y_c = _s5_mixer(proj[:, 4 * ret_w:], s5_tabs, s5_w_glu[j].astype(BF16), n_lat, n_ctx)
            y_d = _retention(proj, _ret_tables(ret_decay_logit[j]), n_lat, n_ctx)
            xs = _gated_residual([y_c, y_d], o_w_out[j].astype(BF16), xs, mods, n_lat, 2, "outproj_odd")
        h = _normmod(xs, norm2_w[i], mods, n_lat, 3, 4)
        a = _ffn_up(h, ffn_w13[i].astype(BF16))
        xs = _gated_residual([a], ffn_w2[i].astype(BF16), xs, mods, n_lat, 5, "ffn_down")
    return _final_norm(xs, final_norm_w, n_lat)[None]
```

```python
import functools
import math

import jax
import jax.numpy as jnp
from jax import lax
from jax.experimental import pallas as pl
from jax.experimental.pallas import tpu as pltpu

F32 = jnp.float32
BF16 = jnp.bfloat16

GRID_W = 64
A_HEADS = 8
A_QK_DIM = 64
HEAD_W = 128
B_HEADS = 8
WIN_R = 8
WIN_C = 16
NA_ROWS = 4
S5_CH = 512
S5_GROUP = 16
S5_GROUPS = 32
S5_STATE = 64
S5_CHUNK = 16
S5_BLOCK = 16
RET_HEADS = 12
RET_CHUNK = 128
ROPE_BASE = 10000.0
EPS = 1e-6
NEG_INF = -1e30
VMEM_LIMIT = 56 * 1024 * 1024
N_MOD = 8


def _cparams(sem):
    return pltpu.CompilerParams(dimension_semantics=sem, vmem_limit_bytes=VMEM_LIMIT)


def _row_tile(t, candidates):
    for c in candidates:
        if t % c == 0:
            return c
    raise ValueError(f"no row tile for {t}")


def _pick_mod(mods_ref, idx, is_ctx):
    return jnp.where(is_ctx, mods_ref[1, idx:idx + 1, :], mods_ref[0, idx:idx + 1, :])


def _is_ctx_rows(tm, n_lat, axis):
    row = pl.program_id(axis) * tm + lax.broadcasted_iota(jnp.int32, (tm, 1), 0)
    return row >= n_lat


def _mods_kernel(s_ref, w_ref, b_ref, o_ref):
    s = s_ref[...]
    s = s * jax.nn.sigmoid(s)
    o_ref[...] = jnp.dot(s, w_ref[...], preferred_element_type=F32,
                         precision=lax.Precision.HIGHEST) + b_ref[...]


def _mods(c, c_ctx, ada_w, ada_b):
    depth, d, w6 = ada_w.shape
    s = jnp.zeros((8, d), F32).at[0].set(c[0]).at[1].set(c_ctx)
    tn = 1024
    out = pl.pallas_call(
        _mods_kernel,
        grid=(depth, w6 // tn),
        in_specs=[pl.BlockSpec((8, d), lambda l, j: (0, 0)),
                  pl.BlockSpec((None, d, tn), lambda l, j: (l, 0, j)),
                  pl.BlockSpec((None, 1, tn), lambda l, j: (l, 0, j))],
        out_specs=pl.BlockSpec((None, 8, tn), lambda l, j: (l, 0, j)),
        out_shape=jax.ShapeDtypeStruct((depth, 8, w6), F32),
        compiler_params=_cparams(("arbitrary", "arbitrary")),
        name="ada_mods",
    )(s, ada_w, ada_b.reshape(depth, 1, w6))
    m = out[:, :2].reshape(depth, 2, 6, d)
    return jnp.pad(m, ((0, 0), (0, 0), (0, N_MOD - 6), (0, 0)))


def _normmod_kernel(x_ref, w_ref, mods_ref, o_ref, *, n_lat, tm, shift_idx, scale_idx):
    x = x_ref[...]
    y = x * lax.rsqrt(jnp.mean(x * x, axis=-1, keepdims=True) + EPS) * w_ref[...]
    is_ctx = _is_ctx_rows(tm, n_lat, 0)
    shift = _pick_mod(mods_ref, shift_idx, is_ctx)
    scale = _pick_mod(mods_ref, scale_idx, is_ctx)
    o_ref[...] = (y * (1.0 + scale) + shift).astype(o_ref.dtype)


def _normmod(x, w, mods, n_lat, shift_idx, scale_idx):
    t, d = x.shape
    tm = _row_tile(t, (640, 256, 128))
    return pl.pallas_call(
        functools.partial(_normmod_kernel, n_lat=n_lat, tm=tm, shift_idx=shift_idx, scale_idx=scale_idx),
        grid=(t // tm,),
        in_specs=[pl.BlockSpec((tm, d), lambda i: (i, 0)),
                  pl.BlockSpec((1, d), lambda i: (0, 0)),
                  pl.BlockSpec((2, N_MOD, d), lambda i: (0, 0, 0))],
        out_specs=pl.BlockSpec((tm, d), lambda i: (i, 0)),
        out_shape=jax.ShapeDtypeStruct((t, d), BF16),
        compiler_params=_cparams(("arbitrary",)),
        name="norm_modulate",
    )(x, w.reshape(1, d), mods)


def _final_norm_kernel(x_ref, w_ref, o_ref):
    x = x_ref[...]
    o_ref[...] = x * lax.rsqrt(jnp.mean(x * x, axis=-1, keepdims=True) + EPS) * w_ref[...]


def _final_norm(x, w, n_lat):
    d = x.shape[1]
    tm = _row_tile(n_lat, (512, 256, 128))
    return pl.pallas_call(
        _final_norm_kernel,
        grid=(n_lat // tm,),
        in_specs=[pl.BlockSpec((tm, d), lambda i: (i, 0)),
                  pl.BlockSpec((1, d), lambda i: (0, 0))],
        out_specs=pl.BlockSpec((tm, d), lambda i: (i, 0)),
        out_shape=jax.ShapeDtypeStruct((n_lat, d), F32),
        compiler_params=_cparams(("arbitrary",)),
        name="final_norm",
    )(x, w.reshape(1, d))


def _rope_store(acc, cos, sin, o_ref, scale):
    first_half = (lax.broadcasted_iota(jnp.int32, (1, HEAD_W), 1) % 32) < 16
    for c in range(acc.shape[1] // HEAD_W):
        x = acc[:, c * HEAD_W:(c + 1) * HEAD_W]
        rot = jnp.where(first_half, -pltpu.roll(x, HEAD_W - 16, 1), pltpu.roll(x, 16, 1))
        o_ref[:, c * HEAD_W:(c + 1) * HEAD_W] = ((x * cos + rot * sin) * scale).astype(o_ref.dtype)


def _resident_bf16(w_ref, w_scr):
    @pl.when(pl.program_id(1) == 0)
    def _():
        w_scr[...] = w_ref[...].astype(BF16)

    return w_scr


def _inproj_even_kernel(h_ref, w_ref, cos_ref, sin_ref, o_ref, w_scr, *, a_scale, b_scale):
    j = pl.program_id(0)
    acc = jnp.dot(h_ref[...], _resident_bf16(w_ref, w_scr)[...], preferred_element_type=F32)

    @pl.when(j == 0)
    def _():
        _rope_store(acc, cos_ref[...], sin_ref[...], o_ref, a_scale)

    @pl.when(j == 1)
    def _():
        _rope_store(acc, cos_ref[...], sin_ref[...], o_ref, 1.0)

    @pl.when(j == 3)
    def _():
        o_ref[...] = (acc * b_scale).astype(o_ref.dtype)

    @pl.when((j == 2) | (j > 3))
    def _():
        o_ref[...] = acc.astype(o_ref.dtype)


def _inproj_even(h, w, layer, cos, sin):
    t, d = h.shape
    n = w.shape[2]
    tn = n // 6
    tm = _row_tile(t, (1280, 640, 256, 128))
    return pl.pallas_call(
        functools.partial(_inproj_even_kernel, a_scale=A_QK_DIM ** -0.5 * math.log2(math.e),
                          b_scale=HEAD_W ** -0.5),
        grid=(6, t // tm),
        in_specs=[pl.BlockSpec((tm, d), lambda j, i: (i, 0)),
                  pl.BlockSpec((None, d, tn), lambda j, i: (layer, 0, j)),
                  pl.BlockSpec((tm, HEAD_W), lambda j, i: (i, 0)),
                  pl.BlockSpec((tm, HEAD_W), lambda j, i: (i, 0))],
        out_specs=pl.BlockSpec((tm, tn), lambda j, i: (i, j)),
        out_shape=jax.ShapeDtypeStruct((t, n), BF16),
        scratch_shapes=[pltpu.VMEM((d, tn), BF16)],
        compiler_params=_cparams(("arbitrary", "arbitrary")),
        name="inproj_even",
    )(h, w, cos, sin)


def _matmul_kernel(h_ref, w_ref, o_ref):
    o_ref[...] = jnp.dot(h_ref[...], w_ref[...], preferred_element_type=F32).astype(o_ref.dtype)


def _matmul(h, w, tn, name):
    t, d = h.shape
    n = w.shape[1]
    tm = _row_tile(t, (1280, 640, 256, 128))
    return pl.pallas_call(
        _matmul_kernel,
        grid=(n // tn, t // tm),
        in_specs=[pl.BlockSpec((tm, d), lambda j, i: (i, 0)),
                  pl.BlockSpec((d, tn), lambda j, i: (0, j))],
        out_specs=pl.BlockSpec((tm, tn), lambda j, i: (i, j)),
        out_shape=jax.ShapeDtypeStruct((t, n), BF16),
        compiler_params=_cparams(("arbitrary", "arbitrary")),
        name=name,
    )(h, w)


def _ffn_up_kernel(h_ref, w1_ref, w3_ref, o_ref, w1_scr, w3_scr):
    h = h_ref[...]
    a = jnp.dot(h, _resident_bf16(w1_ref, w1_scr)[...], preferred_element_type=F32)
    b = jnp.dot(h, _resident_bf16(w3_ref, w3_scr)[...], preferred_element_type=F32)
    o_ref[...] = (a * jax.nn.sigmoid(a) * b).astype(o_ref.dtype)


def _ffn_up(h, w13, layer):
    t, d = h.shape
    d_ff = w13.shape[2] // 2
    tn = 512
    nj = d_ff // tn
    tm = _row_tile(t, (1280, 640, 256, 128))
    return pl.pallas_call(
        _ffn_up_kernel,
        grid=(nj, t // tm),
        in_specs=[pl.BlockSpec((tm, d), lambda j, i: (i, 0)),
                  pl.BlockSpec((None, d, tn), lambda j, i: (layer, 0, j)),
                  pl.BlockSpec((None, d, tn), lambda j, i: (layer, 0, j + nj))],
        out_specs=pl.BlockSpec((tm, tn), lambda j, i: (i, j)),
        out_shape=jax.ShapeDtypeStruct((t, d_ff), BF16),
        scratch_shapes=[pltpu.VMEM((d, tn), BF16), pltpu.VMEM((d, tn), BF16)],
        compiler_params=_cparams(("arbitrary", "arbitrary")),
        name="ffn_up",
    )(h, w13, w13)


def _gated_residual_kernel(*refs, n_a, n_lat, tm, gate_idx):
    a_refs = refs[:n_a]
    w_ref, x_ref, mods_ref, o_ref = refs[n_a:n_a + 4]
    if w_ref.dtype != BF16:
        w_ref = _resident_bf16(w_ref, refs[n_a + 4])
    k0 = 0
    y = None
    for a_ref in a_refs:
        kk = a_ref.shape[1]
        part = jnp.dot(a_ref[...], w_ref[k0:k0 + kk, :], preferred_element_type=F32)
        y = part if y is None else y + part
        k0 += kk
    gate = _pick_mod(mods_ref, gate_idx, _is_ctx_rows(tm, n_lat, 1))
    o_ref[...] = x_ref[...] + gate * y


def _gated_residual(a_list, w, x, mods, n_lat, gate_idx, name, layer=None):
    t, d = x.shape
    k = w.shape[-2]
    tn = 512
    tm = _row_tile(t, (640, 256, 128))
    n_a = len(a_list)
    in_specs = [pl.BlockSpec((tm, a.shape[1]), lambda j, i: (i, 0)) for a in a_list]
    w_spec = (pl.BlockSpec((k, tn), lambda j, i: (0, j)) if layer is None
              else pl.BlockSpec((None, k, tn), lambda j, i: (layer, 0, j)))
    in_specs += [w_spec,
                 pl.BlockSpec((tm, tn), lambda j, i: (i, j)),
                 pl.BlockSpec((2, N_MOD, tn), lambda j, i: (0, 0, j))]
    return pl.pallas_call(
        functools.partial(_gated_residual_kernel, n_a=n_a, n_lat=n_lat, tm=tm, gate_idx=gate_idx),
        grid=(d // tn, t // tm),
        in_specs=in_specs,
        out_specs=pl.BlockSpec((tm, tn), lambda j, i: (i, j)),
        out_shape=jax.ShapeDtypeStruct((t, d), F32),
        scratch_shapes=[] if w.dtype == BF16 else [pltpu.VMEM((k, tn), BF16)],
        input_output_aliases={n_a + 1: 0},
        compiler_params=_cparams(("arbitrary", "arbitrary")),
        name=name,
    )(*a_list, w, x, mods)


def _softmax_block(qs, k, v):
    s = lax.dot_general(qs, k, (((1,), (1,)), ((), ())), preferred_element_type=F32)
    m = jnp.max(s, axis=-1, keepdims=True)
    p = jnp.exp(s - m)
    l = jnp.sum(p, axis=-1, keepdims=True)
    acc = jnp.dot(p.astype(v.dtype), v, preferred_element_type=F32)
    return m, l, acc


DIFF_PV_ROWS = 256
DIFF_SOFTMAX_ROWS = 64


def _diff_attn_kernel(*refs, tq, tk, n_ctx, n_chunks, lambda_init):
    if n_chunks > 1:
        (lq1, lk1, lq2, lk2, subw, q_ref, kc_ref, vc_ref, k_ref, v_ref, o_ref,
         qs_scr, sa_scr, mca_scr, p_scr, m_scr, l_scr, alpha_scr, acc_scr, sb_scr, mcb_scr) = refs
        buf_b = (sb_scr, mcb_scr)
    elif n_chunks == 1:
        (lq1, lk1, lq2, lk2, subw, q_ref, kc_ref, vc_ref, k_ref, v_ref, o_ref,
         qs_scr, sa_scr, mca_scr, p_scr, m_scr, l_scr, alpha_scr, acc_scr) = refs
    else:
        (lq1, lk1, lq2, lk2, subw, q_ref, kc_ref, vc_ref, o_ref,
         qs_scr, sa_scr, mca_scr, p_scr, m_scr, l_scr, alpha_scr, acc_scr) = refs
    buf_a = (sa_scr, mca_scr)
    nt = (((1,), (1,)), ((), ()))
    rows_all = 2 * tq
    pv_rows = min(DIFF_PV_ROWS, rows_all)
    sm_rows = min(DIFF_SOFTMAX_ROWS, pv_rows)
    q = q_ref[...]
    comp1 = lax.broadcasted_iota(jnp.int32, (1, HEAD_W), 1) < A_QK_DIM
    zero = jnp.zeros_like(q)
    qs_scr[0:tq, :] = jnp.where(comp1, q, zero)
    qs_scr[tq:2 * tq, :] = jnp.where(comp1, zero, q)

    def scores(buf, rows, keys):
        s_buf, mc_buf = buf
        s = lax.dot_general(qs_scr[rows, :], keys, nt, preferred_element_type=F32)
        width = s.shape[1]
        s_buf[rows, 0:width] = s
        mp = s[:, 0:HEAD_W]
        for c in range(1, width // HEAD_W):
            mp = jnp.maximum(mp, s[:, c * HEAD_W:(c + 1) * HEAD_W])
        mc_buf[rows, :] = jnp.broadcast_to(jnp.max(mp, axis=-1, keepdims=True), mp.shape)

    def softmax_rows(buf, rows, n_tiles, first):
        s_buf, mc_buf = buf
        m_cur = mc_buf[rows, :]
        if first:
            m_new = m_cur
        else:
            m_prev = m_scr[rows, :]
            m_new = jnp.maximum(m_prev, m_cur)
        lp = None
        for c in range(n_tiles):
            pc = jnp.exp2(s_buf[rows, c * HEAD_W:(c + 1) * HEAD_W] - m_new)
            p_scr[rows, c * HEAD_W:(c + 1) * HEAD_W] = pc.astype(BF16)
            lp = pc if lp is None else lp + pc
        l_cur = jnp.sum(lp, axis=-1, keepdims=True)
        if first:
            l_scr[rows, :] = jnp.broadcast_to(l_cur, m_new.shape)
        else:
            alpha = jnp.exp2(m_prev - m_new)
            alpha_scr[rows, :] = alpha
            l_scr[rows, :] = alpha * l_scr[rows, :] + l_cur
        m_scr[rows, :] = m_new

    def softmax_pv(buf, width, load_v, first, after_group=None):
        for r in range(rows_all // pv_rows):
            for r2 in range(pv_rows // sm_rows):
                r0 = r * pv_rows + r2 * sm_rows
                softmax_rows(buf, slice(r0, r0 + sm_rows), width // HEAD_W, first)
            rows = slice(r * pv_rows, (r + 1) * pv_rows)
            pv = jnp.dot(p_scr[rows, 0:width], load_v(), preferred_element_type=F32)
            acc_scr[rows, :] = pv if first else alpha_scr[rows, :] * acc_scr[rows, :] + pv
            if after_group is not None:
                after_group(rows)

    def all_groups(fn):
        for r in range(rows_all // pv_rows):
            fn(slice(r * pv_rows, (r + 1) * pv_rows))

    all_groups(lambda rows: scores(buf_a, rows, kc_ref[...]))
    softmax_pv(buf_a, n_ctx, lambda: vc_ref[...], True)

    def qk_rows(j, buf, rows):
        off = pl.multiple_of(j * tk, tk)
        scores(buf, rows, k_ref[pl.ds(off, tk), :])

    def soft_pv(j, buf, j_next=None, buf_next=None):
        off = pl.multiple_of(j * tk, tk)
        after = None if j_next is None else (lambda rows: qk_rows(j_next, buf_next, rows))
        softmax_pv(buf, tk, lambda: v_ref[pl.ds(off, tk), :], False, after)

    if n_chunks == 1:
        all_groups(lambda rows: qk_rows(0, buf_a, rows))
        soft_pv(0, buf_a)
    elif n_chunks > 1:
        all_groups(lambda rows: qk_rows(0, buf_a, rows))

        def pair(j2, carry):
            soft_pv(2 * j2, buf_a, 2 * j2 + 1, buf_b)
            soft_pv(2 * j2 + 1, buf_b, 2 * j2 + 2, buf_a)
            return carry

        lax.fori_loop(0, n_chunks // 2 - 1, pair, 0)
        soft_pv(n_chunks - 2, buf_a, n_chunks - 1, buf_b)
        soft_pv(n_chunks - 1, buf_b)

    lam = (jnp.exp(jnp.sum(lq1[...] * lk1[...], axis=-1, keepdims=True))
           - jnp.exp(jnp.sum(lq2[...] * lk2[...], axis=-1, keepdims=True)) + lambda_init)
    o1 = acc_scr[0:tq, :] / l_scr[0:tq, :]
    o2 = acc_scr[tq:2 * tq, :] / l_scr[tq:2 * tq, :]
    o = o1 - lam * o2
    y = o * lax.rsqrt(jnp.mean(o * o, axis=-1, keepdims=True) + EPS) * subw[...]
    o_ref[...] = (y * (1.0 - lambda_init)).astype(o_ref.dtype)


def _diff_attn(qkv, lq1, lk1, lq2, lk2, subw, n_lat, n_ctx, lambda_init, prev=None):
    t = qkv.shape[0]
    use_lat = prev is None
    ctx_blk = n_lat // n_ctx
    small = [lq1.reshape(1, -1), lk1.reshape(1, -1), lq2.reshape(1, -1), lk2.reshape(1, -1), subw.reshape(1, -1)]
    small_specs = [pl.BlockSpec(a.shape, lambda h, qi: (0, 0)) for a in small]
    kc_spec = pl.BlockSpec((n_ctx, HEAD_W), lambda h, qi: (ctx_blk, A_HEADS + h))
    vc_spec = pl.BlockSpec((n_ctx, HEAD_W), lambda h, qi: (ctx_blk, 2 * A_HEADS + h))
    if use_lat:
        tq = _row_tile(n_lat, (512, 256, 128))
        tk = _row_tile(n_lat, (1024, 512, 256, 128))
        n_chunks = n_lat // tk
        assert n_chunks == 1 or n_chunks % 2 == 0
        grid = (A_HEADS, n_lat // tq)
        in_specs = small_specs + [
            pl.BlockSpec((tq, HEAD_W), lambda h, qi: (qi, h)), kc_spec, vc_spec,
            pl.BlockSpec((n_lat, HEAD_W), lambda h, qi: (0, A_HEADS + h)),
            pl.BlockSpec((n_lat, HEAD_W), lambda h, qi: (0, 2 * A_HEADS + h))]
        args = small + [qkv, qkv, qkv, qkv, qkv]
        out_spec = pl.BlockSpec((tq, HEAD_W), lambda h, qi: (qi, h))
        aliases = {}
    else:
        tq, tk, n_chunks = n_ctx, 0, 0
        grid = (A_HEADS, 1)
        in_specs = small_specs + [
            pl.BlockSpec((tq, HEAD_W), lambda h, qi: (ctx_blk, h)), kc_spec, vc_spec,
            pl.BlockSpec(memory_space=pl.ANY)]
        args = small + [qkv, qkv, qkv, prev]
        out_spec = pl.BlockSpec((tq, HEAD_W), lambda h, qi: (ctx_blk, h))
        aliases = {len(args) - 1: 0}

    sw = max(tk, n_ctx)
    kern = functools.partial(_diff_attn_kernel, tq=tq, tk=tk, n_ctx=n_ctx, n_chunks=n_chunks,
                             lambda_init=lambda_init)
    if not use_lat:
        inner = kern

        def kern(*refs):
            n_in = len(args)
            inner(*refs[:n_in - 1], *refs[n_in:])

    return pl.pallas_call(
        kern,
        grid=grid,
        in_specs=in_specs,
        out_specs=out_spec,
        out_shape=jax.ShapeDtypeStruct((t, A_HEADS * HEAD_W), BF16),
        scratch_shapes=[pltpu.VMEM((2 * tq, HEAD_W), BF16),
                        pltpu.VMEM((2 * tq, sw), F32),
                        pltpu.VMEM((2 * tq, HEAD_W), F32),
                        pltpu.VMEM((2 * tq, sw), BF16),
                        pltpu.VMEM((2 * tq, HEAD_W), F32),
                        pltpu.VMEM((2 * tq, HEAD_W), F32),
                        pltpu.VMEM((2 * tq, HEAD_W), F32),
                        pltpu.VMEM((2 * tq, HEAD_W), F32)]
        + ([pltpu.VMEM((2 * tq, sw), F32),
            pltpu.VMEM((2 * tq, HEAD_W), F32)] if n_chunks > 1 else []),
        input_output_aliases=aliases,
        compiler_params=_cparams(("arbitrary", "arbitrary")),
        name="diff_attn" if use_lat else "diff_attn_ctx",
    )(*args)


def _na_bias_tables(rpb, rows):
    nblk = rows // NA_ROWS
    wr = min(WIN_R, rows)
    nh = rpb.shape[0]
    rq = jnp.arange(NA_ROWS)[:, None, None]
    slot = jnp.arange(3)[None, :, None]
    rk = jnp.arange(NA_ROWS)[None, None, :]
    row_sel, row_ok = [], []
    for b, dup in ((0, 0), (1, -1), (nblk - 1, 2)):
        r = NA_ROWS * b + rq
        r0 = jnp.clip(r - wr // 2, 0, rows - wr)
        rkey = NA_ROWS * (b - 1 + slot) + rk
        ok = (rkey >= r0) & (rkey < r0 + wr) & (slot != dup)
        drow = jnp.broadcast_to(rkey - r + (WIN_R - 1), ok.shape)
        row_sel.append((drow[..., None] == jnp.arange(2 * WIN_R - 1)) & ok[..., None])
        row_ok.append(ok)
    row_sel = jnp.stack(row_sel).astype(F32)
    row_ok = jnp.stack(row_ok)
    jq = jnp.arange(GRID_W)[:, None]
    jk = jnp.arange(GRID_W)[None, :]
    c0 = jnp.clip(jq - WIN_C // 2, 0, GRID_W - WIN_C)
    col_ok = (jk >= c0) & (jk < c0 + WIN_C)
    dcol = jnp.clip(jk - jq + (WIN_C - 1), 0, 2 * WIN_C - 2)
    col_sel = (dcol[..., None] == jnp.arange(2 * WIN_C - 1)).astype(F32)
    bias = jnp.einsum('vasbr,hrc,qkc->hvaqsbk', row_sel, rpb.astype(F32), col_sel,
                      precision=lax.Precision.HIGHEST)
    valid = row_ok[None, :, :, None, :, :, None] & col_ok[None, None, None, :, None, None, :]
    tab = jnp.where(valid, bias, NEG_INF)
    return tab.reshape(nh, 3, NA_ROWS * GRID_W, 3 * NA_ROWS * GRID_W)


def _natten_kernel(tab_ref, q_ref, k0_ref, k1_ref, k2_ref, v0_ref, v1_ref, v2_ref, kc_ref, vc_ref, o_ref):
    q = q_ref[...]
    nt = (((1,), (1,)), ((), ()))
    blk = q.shape[0]
    s_loc = [lax.dot_general(q, k_ref[...], nt, preferred_element_type=F32)
             + tab_ref[:, i * blk:(i + 1) * blk] for i, k_ref in enumerate((k0_ref, k1_ref, k2_ref))]
    s_ctx = lax.dot_general(q, kc_ref[...], nt, preferred_element_type=F32)
    m = jnp.max(s_ctx, axis=-1, keepdims=True)
    for s in s_loc:
        m = jnp.maximum(m, jnp.max(s, axis=-1, keepdims=True))
    p = jnp.exp(s_ctx - m)
    l = jnp.sum(p, axis=-1, keepdims=True)
    acc = jnp.dot(p.astype(BF16), vc_ref[...], preferred_element_type=F32)
    for s, v_ref in zip(s_loc, (v0_ref, v1_ref, v2_ref)):
        p = jnp.exp(s - m)
        l = l + jnp.sum(p, axis=-1, keepdims=True)
        acc = acc + jnp.dot(p.astype(BF16), v_ref[...], preferred_element_type=F32)
    o_ref[...] = (acc / l).astype(o_ref.dtype)


def _natten(qkv, tabs, n_lat, n_ctx):
    t = qkv.shape[0]
    blk = NA_ROWS * GRID_W
    nblk = n_lat // blk
    assert nblk >= 3 and blk == n_ctx
    qo, ko, vo = 3 * A_HEADS, 3 * A_HEADS + B_HEADS, 3 * A_HEADS + 2 * B_HEADS
    ctx_blk = n_lat // n_ctx

    def kv_spec(off, shift):
        return pl.BlockSpec((blk, HEAD_W), lambda h, b: (jnp.clip(b + shift, 0, nblk - 1), off + h))

    return pl.pallas_call(
        _natten_kernel,
        grid=(B_HEADS, nblk),
        in_specs=[pl.BlockSpec((None, None, blk, 3 * blk),
                               lambda h, b: (h, jnp.where(b == 0, 0, jnp.where(b == nblk - 1, 2, 1)), 0, 0)),
                  pl.BlockSpec((blk, HEAD_W), lambda h, b: (b, qo + h)),
                  kv_spec(ko, -1), kv_spec(ko, 0), kv_spec(ko, 1),
                  kv_spec(vo, -1), kv_spec(vo, 0), kv_spec(vo, 1),
                  pl.BlockSpec((n_ctx, HEAD_W), lambda h, b: (ctx_blk, ko + h)),
                  pl.BlockSpec((n_ctx, HEAD_W), lambda h, b: (ctx_blk, vo + h))],
        out_specs=pl.BlockSpec((blk, HEAD_W), lambda h, b: (b, h)),
        out_shape=jax.ShapeDtypeStruct((t, B_HEADS * HEAD_W), BF16),
        compiler_params=_cparams(("arbitrary", "arbitrary")),
        name="natten",
    )(tabs, qkv, qkv, qkv, qkv, qkv, qkv, qkv, qkv, qkv)


def _ctx_attn_kernel(q_ref, k_ref, v_ref, prev_ref, o_ref):
    del prev_ref
    _, l, acc = _softmax_block(q_ref[...], k_ref[...], v_ref[...])
    o_ref[...] = (acc / l).astype(o_ref.dtype)


def _ctx_attn(qkv, prev, n_lat, n_ctx):
    qo, ko, vo = 3 * A_HEADS, 3 * A_HEADS + B_HEADS, 3 * A_HEADS + 2 * B_HEADS
    ctx_blk = n_lat // n_ctx
    return pl.pallas_call(
        _ctx_attn_kernel,
        grid=(B_HEADS,),
        in_specs=[pl.BlockSpec((n_ctx, HEAD_W), lambda h: (ctx_blk, qo + h)),
                  pl.BlockSpec((n_ctx, HEAD_W), lambda h: (ctx_blk, ko + h)),
                  pl.BlockSpec((n_ctx, HEAD_W), lambda h: (ctx_blk, vo + h)),
                  pl.BlockSpec(memory_space=pl.ANY)],
        out_specs=pl.BlockSpec((n_ctx, HEAD_W), lambda h: (ctx_blk, h)),
        out_shape=jax.ShapeDtypeStruct(prev.shape, prev.dtype),
        input_output_aliases={3: 0},
        compiler_params=_cparams(("arbitrary",)),
        name="ctx_attn",
    )(qkv, qkv, qkv, prev)


def _s5_tables(lam_re, lam_im, b_re, b_im, c_re, c_im, log_step, d_skip):
    hp = lax.Precision.HIGHEST
    ln = S5_CHUNK
    g, p, c = S5_GROUPS, S5_STATE, S5_GROUP
    lr, li = lam_re.astype(F32), lam_im.astype(F32)
    dt = jnp.exp(log_step.astype(F32))[:, :, None]
    mag = jnp.exp(lr * dt)
    ar, ai = mag * jnp.cos(li * dt), mag * jnp.sin(li * dt)
    den = lr * lr + li * li
    nr, ni = ar - 1.0, ai
    fr = (nr * lr + ni * li) / den
    fi = (ni * lr - nr * li) / den
    br_, bi_ = b_re.astype(F32), b_im.astype(F32)
    bbr = fr[..., None] * br_ - fi[..., None] * bi_
    bbi = fr[..., None] * bi_ + fi[..., None] * br_
    lag = jnp.arange(ln + 1, dtype=F32)[:, None, None, None]
    magl = jnp.exp(lr * dt * lag)
    pr, pi_ = magl * jnp.cos(li * dt * lag), magl * jnp.sin(li * dt * lag)
    wr = pr[..., None] * bbr - pi_[..., None] * bbi
    wi = pr[..., None] * bbi + pi_[..., None] * bbr
    cr, ci = c_re.astype(F32), c_im.astype(F32)
    kern = (jnp.einsum('dgcp,ldgpe->ldgce', cr, wr[:ln], precision=hp)
            - jnp.einsum('dgcp,ldgpe->ldgce', ci, wi[:ln], precision=hp))
    s_idx = jnp.arange(ln)[:, None]
    t_idx = jnp.arange(ln)[None, :]
    lag_f = t_idx - s_idx
    kf = kern[jnp.clip(lag_f, 0, ln - 1), 0] * (lag_f >= 0)[:, :, None, None, None]
    kb = kern[jnp.clip(-lag_f, 0, ln - 1), 1] * (lag_f <= 0)[:, :, None, None, None]
    m_tab = (kf + kb).transpose(2, 0, 4, 1, 3).reshape(g, ln * c, ln * c)
    wf_r, wf_i = wr[:ln, 0][::-1], wi[:ln, 0][::-1]
    wb_r, wb_i = wr[:ln, 1], wi[:ln, 1]

    def inj(w):
        return w.transpose(1, 0, 3, 2).reshape(g, ln * c, p)

    b_tab = jnp.concatenate([inj(wf_r), inj(wf_i), inj(wb_r), inj(wb_i)], axis=-1)
    pf_r, pf_i = pr[1:ln + 1, 0], pi_[1:ln + 1, 0]
    pb_r, pb_i = pr[1:ln + 1, 1][::-1], pi_[1:ln + 1, 1][::-1]

    def rd(pw_r, pw_i, cre, cim):
        car = cre[None] * pw_r[:, :, None, :] - cim[None] * pw_i[:, :, None, :]
        cai = cre[None] * pw_i[:, :, None, :] + cim[None] * pw_r[:, :, None, :]
        to_rows = lambda a: a.transpose(1, 3, 0, 2).reshape(g, p, ln * c)
        return to_rows(car), to_rows(-cai)

    c_tab = jnp.concatenate(rd(pf_r, pf_i, cr[0], ci[0]) + rd(pb_r, pb_i, cr[1], ci[1]), axis=1)
    a_chunk = jnp.stack([jnp.stack([pr[ln, 0], pi_[ln, 0]]), jnp.stack([pr[ln, 1], pi_[ln, 1]])])
    d_tab = jnp.tile(d_skip.astype(F32).reshape(g, 1, c), (1, ln, 1)).reshape(g, 1, ln * c)
    return m_tab.astype(BF16), b_tab.astype(BF16), c_tab.astype(BF16), a_chunk, d_tab


def _s5_inject_kernel(u_ref, b_ref, o_ref):
    o_ref[...] = jnp.dot(u_ref[...], b_ref[...], preferred_element_type=F32)


def _s5_scan_kernel(a_ref, e_ref, o_ref, s_scr, *, nb):
    d = pl.program_id(0)

    @pl.when(pl.program_id(1) == 0)
    def _():
        s_scr[...] = jnp.zeros_like(s_scr)

    ar, ai = a_ref[0], a_ref[1]

    def body(i, carry):
        sr, si = carry
        c = jnp.where(d == 0, i, nb - 1 - i)
        o_ref[0, c] = sr
        o_ref[1, c] = si
        return ar * sr - ai * si + e_ref[0, c], ar * si + ai * sr + e_ref[1, c]

    sr, si = lax.fori_loop(0, nb, body, (s_scr[0], s_scr[1]))
    s_scr[0] = sr
    s_scr[1] = si


def _gelu_tanh(x):
    return 0.5 * x * (1.0 + jnp.tanh(math.sqrt(2.0 / math.pi) * (x + 0.044715 * (x * x * x))))


def _s5_readout_kernel(u_ref, m_ref, s_ref, c_ref, d_ref, o_ref):
    u = u_ref[...]
    y = jnp.dot(u, m_ref[...], preferred_element_type=F32) + u.astype(F32) * d_ref[...]
    for k in range(4):
        y = y + jnp.dot(s_ref[k].astype(BF16), c_ref[k * S5_STATE:(k + 1) * S5_STATE, :],
                        preferred_element_type=F32)
    o_ref[...] = _gelu_tanh(y)


def _s5_glu_kernel(z_ref, w_ref, o_ref):
    z = z_ref[...]
    o_ref[...] = (z * jax.nn.sigmoid(jnp.dot(z.astype(BF16), w_ref[...], preferred_element_type=F32))
                  ).astype(o_ref.dtype)


def _s5_mixer(u, tables, w_glu, n_lat, n_ctx):
    m_tab, b_tab, c_tab, a_chunk, d_tab = tables
    t = u.shape[0]
    g, c, ln, p = S5_GROUPS, S5_GROUP, S5_CHUNK, S5_STATE
    nc = t // ln
    w = ln * c
    ug = u.reshape(nc, ln, g, c).transpose(2, 0, 1, 3).reshape(g, nc, w)

    e = pl.pallas_call(
        _s5_inject_kernel,
        grid=(g,),
        in_specs=[pl.BlockSpec((None, nc, w), lambda i: (i, 0, 0)),
                  pl.BlockSpec((None, w, 4 * p), lambda i: (i, 0, 0))],
        out_specs=pl.BlockSpec((None, nc, 4 * p), lambda i: (i, 0, 0)),
        out_shape=jax.ShapeDtypeStruct((g, nc, 4 * p), F32),
        compiler_params=_cparams(("arbitrary",)),
        name="s5_inject",
    )(ug, b_tab)
    e = e.reshape(g, nc, 2, 2, p).transpose(2, 3, 1, 0, 4)

    nb = S5_BLOCK
    assert (n_lat // ln) % nb == 0 and n_ctx // ln == nb
    lat_blocks = n_lat // ln // nb

    def blk(d, s):
        return jnp.where(s == 0, lat_blocks, jnp.where(d == 0, s - 1, lat_blocks - s))

    s_in = pl.pallas_call(
        functools.partial(_s5_scan_kernel, nb=nb),
        grid=(2, lat_blocks + 1),
        in_specs=[pl.BlockSpec((None, 2, g, p), lambda d, s: (d, 0, 0, 0)),
                  pl.BlockSpec((None, 2, nb, g, p), lambda d, s: (d, 0, blk(d, s), 0, 0))],
        out_specs=pl.BlockSpec((None, 2, nb, g, p), lambda d, s: (d, 0, blk(d, s), 0, 0)),
        out_shape=jax.ShapeDtypeStruct((2, 2, nc, g, p), F32),
        scratch_shapes=[pltpu.VMEM((2, g, p), F32)],
        compiler_params=_cparams(("arbitrary", "arbitrary")),
        name="s5_scan",
    )(a_chunk, e)
    s_in = s_in.transpose(3, 0, 1, 2, 4).reshape(g, 4, nc, p)

    z = pl.pallas_call(
        _s5_readout_kernel,
        grid=(g,),
        in_specs=[pl.BlockSpec((None, nc, w), lambda i: (i, 0, 0)),
                  pl.BlockSpec((None, w, w), lambda i: (i, 0, 0)),
                  pl.BlockSpec((None, 4, nc, p), lambda i: (i, 0, 0, 0)),
                  pl.BlockSpec((None, 4 * p, w), lambda i: (i, 0, 0)),
                  pl.BlockSpec((None, 1, w), lambda i: (i, 0, 0))],
        out_specs=pl.BlockSpec((None, nc, w), lambda i: (i, 0, 0)),
        out_shape=jax.ShapeDtypeStruct((g, nc, w), F32),
        compiler_params=_cparams(("arbitrary",)),
        name="s5_readout",
    )(ug, m_tab, s_in, c_tab, d_tab)
    z = z.reshape(g, nc, ln, c).transpose(1, 2, 0, 3).reshape(t, g * c)

    tm = _row_tile(t, (1280, 640, 256, 128))
    return pl.pallas_call(
        _s5_glu_kernel,
        grid=(t // tm,),
        in_specs=[pl.BlockSpec((tm, g * c), lambda i: (i, 0)),
                  pl.BlockSpec((g * c, g * c), lambda i: (0, 0))],
        out_specs=pl.BlockSpec((tm, g * c), lambda i: (i, 0)),
        out_shape=jax.ShapeDtypeStruct((t, g * c), BF16),
        compiler_params=_cparams(("arbitrary",)),
        name="s5_glu",
    )(z, w_glu)


def _ret_tables(decay_logit):
    scale = HEAD_W ** -0.5
    lg = jax.nn.log_sigmoid(decay_logit.astype(F32))
    lf, lb = lg[0][:, None, None], lg[1][:, None, None]
    i = jnp.arange(RET_CHUNK, dtype=F32)[None, :, None]
    j = jnp.arange(RET_CHUNK, dtype=F32)[None, None, :]
    diff = i - j
    intra = (jnp.where(diff >= 0, jnp.exp(lf * jnp.maximum(diff, 0.0)), 0.0)
             + jnp.where(diff <= 0, jnp.exp(lb * jnp.maximum(-diff, 0.0)), 0.0)) * scale
    ones = jnp.ones((1, 1, HEAD_W), F32)
    q_f = jnp.exp(lf * (i + 1.0)) * ones
    k_f = jnp.exp(lf * (RET_CHUNK - 1.0 - i)) * scale * ones
    c_f = jnp.exp(lf * RET_CHUNK) * jnp.ones((1, RET_CHUNK, HEAD_W), F32)
    q_b = jnp.exp(lb * (RET_CHUNK - i)) * ones
    k_b = jnp.exp(lb * i) * scale * ones
    c_b = jnp.exp(lb * RET_CHUNK) * jnp.ones((1, RET_CHUNK, HEAD_W), F32)
    return intra, jnp.stack([q_f, k_f, c_f]), jnp.stack([q_b, k_b, c_b])


def _ret_state_step(q, k, v, dec_ref, s_scr, h):
    s = s_scr[h]
    qd = (q.astype(F32) * dec_ref[0, h]).astype(BF16)
    kd = (k.astype(F32) * dec_ref[1, h]).astype(BF16)
    o = jnp.dot(qd, s.astype(BF16), preferred_element_type=F32)
    s_scr[h] = s * dec_ref[2, h] + lax.dot_general(kd, v, (((0,), (0,)), ((), ())),
                                                    preferred_element_type=F32)
    return o


def _ret_fwd_kernel(intra_ref, dec_ref, q_ref, k_ref, v_ref, o_ref, s_scr):
    @pl.when(pl.program_id(0) == 0)
    def _():
        s_scr[...] = jnp.zeros_like(s_scr)

    for h in range(RET_HEADS):
        sl = slice(h * HEAD_W, (h + 1) * HEAD_W)
        q, k, v = q_ref[:, sl], k_ref[:, sl], v_ref[:, sl]
        att = lax.dot_general(q, k, (((1,), (1,)), ((), ())), preferred_element_type=F32) * intra_ref[h]
        o = jnp.dot(att.astype(BF16), v, preferred_element_type=F32)
        o_ref[:, sl] = o + _ret_state_step(q, k, v, dec_ref, s_scr, h)


def _ret_bwd_kernel(dec_ref, q_ref, k_ref, v_ref, g_ref, o1_ref, o_ref, s_scr):
    @pl.when(pl.program_id(0) == 0)
    def _():
        s_scr[...] = jnp.zeros_like(s_scr)

    for h in range(RET_HEADS):
        sl = slice(h * HEAD_W, (h + 1) * HEAD_W)
        q, k, v = q_ref[:, sl], k_ref[:, sl], v_ref[:, sl]
        o = o1_ref[:, sl] + _ret_state_step(q, k, v, dec_ref, s_scr, h)
        y = o * lax.rsqrt(jnp.mean(o * o, axis=-1, keepdims=True) + EPS)
        gate = g_ref[:, sl].astype(F32)
        o_ref[:, sl] = (y * (gate * jax.nn.sigmoid(gate))).astype(o_ref.dtype)


def _retention(proj, tables, n_lat, n_ctx):
    intra, dec_f, dec_b = tables
    t = proj.shape[0]
    wd = RET_HEADS * HEAD_W
    ck = RET_CHUNK
    n_lat_c, n_ctx_c = n_lat // ck, n_ctx // ck
    steps = n_lat_c + n_ctx_c

    def fwd_blk(s):
        return jnp.where(s < n_ctx_c, n_lat_c + s, s - n_ctx_c)

    def bwd_blk(s):
        return jnp.where(s < n_ctx_c, n_lat_c + n_ctx_c - 1 - s, n_lat_c - 1 - (s - n_ctx_c))

    def col_spec(col, order):
        return pl.BlockSpec((ck, wd), lambda s: (order(s), col))

    tab_spec = pl.BlockSpec((3, RET_HEADS, ck, HEAD_W), lambda s: (0, 0, 0, 0))
    o1 = pl.pallas_call(
        _ret_fwd_kernel,
        grid=(steps,),
        in_specs=[pl.BlockSpec((RET_HEADS, ck, ck), lambda s: (0, 0, 0)), tab_spec,
                  col_spec(0, fwd_blk), col_spec(1, fwd_blk), col_spec(2, fwd_blk)],
        out_specs=col_spec(0, fwd_blk),
        out_shape=jax.ShapeDtypeStruct((t, wd), F32),
        scratch_shapes=[pltpu.VMEM((RET_HEADS, HEAD_W, HEAD_W), F32)],
        compiler_params=_cparams(("arbitrary",)),
        name="retention_fwd",
    )(intra, dec_f, proj, proj, proj)
    return pl.pallas_call(
        _ret_bwd_kernel,
        grid=(steps,),
        in_specs=[tab_spec, col_spec(0, bwd_blk), col_spec(1, bwd_blk), col_spec(2, bwd_blk),
                  col_spec(3, bwd_blk), col_spec(0, bwd_blk)],
        out_specs=col_spec(0, bwd_blk),
        out_shape=jax.ShapeDtypeStruct((t, wd), BF16),
        scratch_shapes=[pltpu.VMEM((RET_HEADS, HEAD_W, HEAD_W), F32)],
        compiler_params=_cparams(("arbitrary",)),
        name="retention_bwd",
    )(dec_b, proj, proj, proj, proj, o1)


def _rope_tables(n_lat, n_ctx):
    n_freq = A_QK_DIM // 4
    freq = ROPE_BASE ** (-jnp.arange(n_freq, dtype=F32) / n_freq)
    tok = jnp.arange(n_lat)
    row = (tok // GRID_W).astype(F32)
    col = (tok % GRID_W).astype(F32)
    ang = jnp.stack([row[:, None] * freq, col[:, None] * freq], axis=1)
    ang = jnp.broadcast_to(ang[:, :, None, :], (n_lat, 2, 2, n_freq)).reshape(n_lat, A_QK_DIM)
    reps = HEAD_W // A_QK_DIM
    cos = jnp.concatenate([jnp.tile(jnp.cos(ang), (1, reps)), jnp.ones((n_ctx, HEAD_W), F32)], axis=0)
    sin = jnp.concatenate([jnp.tile(jnp.sin(ang), (1, reps)), jnp.zeros((n_ctx, HEAD_W), F32)], axis=0)
    return cos, sin


def kernel(x, c, ctx, c_ctx, ada_w, ada_b, norm1_w, norm2_w, ffn_w13, ffn_w2, e_w_in, e_w_out, diff_lq1, diff_lk1, diff_lq2, diff_lk2, diff_subln_w, na_rpb, o_w_in, o_w_out, s5_lam_re, s5_lam_im, s5_b_re, s5_b_im, s5_c_re, s5_c_im, s5_log_step, s5_d, s5_w_glu, ret_decay_logit, final_norm_w):
    bsz, n_lat, d = x.shape
    n_ctx = ctx.shape[1]
    depth = ada_w.shape[0]
    assert bsz == 1
    xs = jnp.concatenate([x[0], ctx[0]], axis=0)
    mods_all = _mods(c, c_ctx, ada_w, ada_b)
    cos, sin = _rope_tables(n_lat, n_ctx)
    ret_w = RET_HEADS * HEAD_W

    for i in range(depth):
        compute_ctx = i != depth - 1
        mods = mods_all[i]
        j = i // 2
        h = _normmod(xs, norm1_w[i], mods, n_lat, 0, 1)
        if i % 2 == 0:
            lambda_init = 0.8 - 0.6 * math.exp(-0.3 * i)
            qkv = _inproj_even(h, e_w_in, j, cos, sin)
            lam_args = (diff_lq1[j], diff_lk1[j], diff_lq2[j], diff_lk2[j], diff_subln_w[j])
            o_a = _diff_attn(qkv, *lam_args, n_lat, n_ctx, lambda_init)
            o_b = _natten(qkv, _na_bias_tables(na_rpb[j], n_lat // GRID_W), n_lat, n_ctx)
            if compute_ctx:
                o_a = _diff_attn(qkv, *lam_args, n_lat, n_ctx, lambda_init, prev=o_a)
                o_b = _ctx_attn(qkv, o_b, n_lat, n_ctx)
            xs = _gated_residual([o_a, o_b], e_w_out, xs, mods, n_lat, 2, "outproj_even", layer=j)
        else:
            w_in = o_w_in[j]
            w_in = jnp.concatenate([w_in[:, S5_CH:], w_in[:, :S5_CH]], axis=1).astype(BF16)
            proj = _matmul(h, w_in, 512, "inproj_odd")
            s5_tabs = _s5_tables(s5_lam_re[j], s5_lam_im[j], s5_b_re[j], s5_b_im[j], s5_c_re[j], s5_c_im[j],
                                 s5_log_step[j], s5_d[j])
            y_c = _s5_mixer(proj[:, 4 * ret_w:], s5_tabs, s5_w_glu[j].astype(BF16), n_lat, n_ctx)
            y_d = _retention(proj, _ret_tables(ret_decay_logit[j]), n_lat, n_ctx)
            xs = _gated_residual([y_c, y_d], o_w_out, xs, mods, n_lat, 2, "outproj_odd", layer=j)
        h = _normmod(xs, norm2_w[i], mods, n_lat, 3, 4)
        a = _ffn_up(h, ffn_w13, i)
        xs = _gated_residual([a], ffn_w2[i].astype(BF16), xs, mods, n_lat, 5, "ffn_down")
    return _final_norm(xs, final_norm_w, n_lat)[None]
```

```python
import functools
import math

import jax
import jax.numpy as jnp
from jax import lax
from jax.experimental import pallas as pl
from jax.experimental.pallas import tpu as pltpu

F32 = jnp.float32
BF16 = jnp.bfloat16

GRID_W = 64
A_HEADS = 8
A_QK_DIM = 64
HEAD_W = 128
B_HEADS = 8
WIN_R = 8
WIN_C = 16
NA_ROWS = 4
S5_CH = 512
S5_GROUP = 16
S5_GROUPS = 32
S5_STATE = 64
S5_CHUNK = 16
S5_BLOCK = 16
RET_HEADS = 12
RET_CHUNK = 128
ROPE_BASE = 10000.0
EPS = 1e-6
NEG_INF = -1e30
VMEM_LIMIT = 56 * 1024 * 1024
N_MOD = 8


def _cparams(sem):
    return pltpu.CompilerParams(dimension_semantics=sem, vmem_limit_bytes=VMEM_LIMIT)


def _row_tile(t, candidates):
    for c in candidates:
        if t % c == 0:
            return c
    raise ValueError(f"no row tile for {t}")


def _pick_mod(mods_ref, idx, is_ctx):
    return jnp.where(is_ctx, mods_ref[1, idx:idx + 1, :], mods_ref[0, idx:idx + 1, :])


def _is_ctx_rows(tm, n_lat, axis):
    row = pl.program_id(axis) * tm + lax.broadcasted_iota(jnp.int32, (tm, 1), 0)
    return row >= n_lat


def _mods_kernel(s_ref, w_ref, b_ref, o_ref):
    s = s_ref[...]
    s = s * jax.nn.sigmoid(s)
    o_ref[...] = jnp.dot(s, w_ref[...], preferred_element_type=F32,
                         precision=lax.Precision.HIGHEST) + b_ref[...]


def _mods(c, c_ctx, ada_w, ada_b):
    depth, d, w6 = ada_w.shape
    s = jnp.zeros((8, d), F32).at[0].set(c[0]).at[1].set(c_ctx)
    tn = 1024
    out = pl.pallas_call(
        _mods_kernel,
        grid=(depth, w6 // tn),
        in_specs=[pl.BlockSpec((8, d), lambda l, j: (0, 0)),
                  pl.BlockSpec((None, d, tn), lambda l, j: (l, 0, j)),
                  pl.BlockSpec((None, 1, tn), lambda l, j: (l, 0, j))],
        out_specs=pl.BlockSpec((None, 8, tn), lambda l, j: (l, 0, j)),
        out_shape=jax.ShapeDtypeStruct((depth, 8, w6), F32),
        compiler_params=_cparams(("arbitrary", "arbitrary")),
        name="ada_mods",
    )(s, ada_w, ada_b.reshape(depth, 1, w6))
    m = out[:, :2].reshape(depth, 2, 6, d)
    return jnp.pad(m, ((0, 0), (0, 0), (0, N_MOD - 6), (0, 0)))


def _normmod_kernel(x_ref, w_ref, mods_ref, o_ref, *, n_lat, tm, shift_idx, scale_idx):
    x = x_ref[...]
    y = x * lax.rsqrt(jnp.mean(x * x, axis=-1, keepdims=True) + EPS) * w_ref[...]
    is_ctx = _is_ctx_rows(tm, n_lat, 0)
    shift = _pick_mod(mods_ref, shift_idx, is_ctx)
    scale = _pick_mod(mods_ref, scale_idx, is_ctx)
    o_ref[...] = (y * (1.0 + scale) + shift).astype(o_ref.dtype)


def _normmod(x, w, mods, n_lat, shift_idx, scale_idx):
    t, d = x.shape
    tm = _row_tile(t, (640, 256, 128))
    return pl.pallas_call(
        functools.partial(_normmod_kernel, n_lat=n_lat, tm=tm, shift_idx=shift_idx, scale_idx=scale_idx),
        grid=(t // tm,),
        in_specs=[pl.BlockSpec((tm, d), lambda i: (i, 0)),
                  pl.BlockSpec((1, d), lambda i: (0, 0)),
                  pl.BlockSpec((2, N_MOD, d), lambda i: (0, 0, 0))],
        out_specs=pl.BlockSpec((tm, d), lambda i: (i, 0)),
        out_shape=jax.ShapeDtypeStruct((t, d), BF16),
        compiler_params=_cparams(("arbitrary",)),
        name="norm_modulate",
    )(x, w.reshape(1, d), mods)


def _final_norm_kernel(x_ref, w_ref, o_ref):
    x = x_ref[...]
    o_ref[...] = x * lax.rsqrt(jnp.mean(x * x, axis=-1, keepdims=True) + EPS) * w_ref[...]


def _final_norm(x, w, n_lat):
    d = x.shape[1]
    tm = _row_tile(n_lat, (512, 256, 128))
    return pl.pallas_call(
        _final_norm_kernel,
        grid=(n_lat // tm,),
        in_specs=[pl.BlockSpec((tm, d), lambda i: (i, 0)),
                  pl.BlockSpec((1, d), lambda i: (0, 0))],
        out_specs=pl.BlockSpec((tm, d), lambda i: (i, 0)),
        out_shape=jax.ShapeDtypeStruct((n_lat, d), F32),
        compiler_params=_cparams(("arbitrary",)),
        name="final_norm",
    )(x, w.reshape(1, d))


def _rope_store(acc, cos, sin, o_ref, scale):
    first_half = (lax.broadcasted_iota(jnp.int32, (1, HEAD_W), 1) % 32) < 16
    for c in range(acc.shape[1] // HEAD_W):
        x = acc[:, c * HEAD_W:(c + 1) * HEAD_W]
        rot = jnp.where(first_half, -pltpu.roll(x, HEAD_W - 16, 1), pltpu.roll(x, 16, 1))
        o_ref[:, c * HEAD_W:(c + 1) * HEAD_W] = ((x * cos + rot * sin) * scale).astype(o_ref.dtype)


def _resident_bf16(w_ref, w_scr):
    @pl.when(pl.program_id(1) == 0)
    def _():
        w_scr[...] = w_ref[...].astype(BF16)

    return w_scr


def _inproj_even_kernel(h_ref, w_ref, cos_ref, sin_ref, o_ref, w_scr, *, a_scale, b_scale):
    j = pl.program_id(0)
    acc = jnp.dot(h_ref[...], _resident_bf16(w_ref, w_scr)[...], preferred_element_type=F32)

    @pl.when(j == 0)
    def _():
        _rope_store(acc, cos_ref[...], sin_ref[...], o_ref, a_scale)

    @pl.when(j == 1)
    def _():
        _rope_store(acc, cos_ref[...], sin_ref[...], o_ref, 1.0)

    @pl.when(j == 3)
    def _():
        o_ref[...] = (acc * b_scale).astype(o_ref.dtype)

    @pl.when((j == 2) | (j > 3))
    def _():
        o_ref[...] = acc.astype(o_ref.dtype)


def _inproj_even(h, w, layer, cos, sin):
    t, d = h.shape
    n = w.shape[2]
    tn = n // 6
    tm = _row_tile(t, (1280, 640, 256, 128))
    return pl.pallas_call(
        functools.partial(_inproj_even_kernel, a_scale=A_QK_DIM ** -0.5 * math.log2(math.e),
                          b_scale=HEAD_W ** -0.5),
        grid=(6, t // tm),
        in_specs=[pl.BlockSpec((tm, d), lambda j, i: (i, 0)),
                  pl.BlockSpec((None, d, tn), lambda j, i: (layer, 0, j)),
                  pl.BlockSpec((tm, HEAD_W), lambda j, i: (i, 0)),
                  pl.BlockSpec((tm, HEAD_W), lambda j, i: (i, 0))],
        out_specs=pl.BlockSpec((tm, tn), lambda j, i: (i, j)),
        out_shape=jax.ShapeDtypeStruct((t, n), BF16),
        scratch_shapes=[pltpu.VMEM((d, tn), BF16)],
        compiler_params=_cparams(("arbitrary", "arbitrary")),
        name="inproj_even",
    )(h, w, cos, sin)


def _matmul_kernel(h_ref, w_ref, o_ref):
    o_ref[...] = jnp.dot(h_ref[...], w_ref[...], preferred_element_type=F32).astype(o_ref.dtype)


def _matmul(h, w, tn, name):
    t, d = h.shape
    n = w.shape[1]
    tm = _row_tile(t, (1280, 640, 256, 128))
    return pl.pallas_call(
        _matmul_kernel,
        grid=(n // tn, t // tm),
        in_specs=[pl.BlockSpec((tm, d), lambda j, i: (i, 0)),
                  pl.BlockSpec((d, tn), lambda j, i: (0, j))],
        out_specs=pl.BlockSpec((tm, tn), lambda j, i: (i, j)),
        out_shape=jax.ShapeDtypeStruct((t, n), BF16),
        compiler_params=_cparams(("arbitrary", "arbitrary")),
        name=name,
    )(h, w)


def _ffn_up_kernel(h_ref, w1_ref, w3_ref, o_ref, w1_scr, w3_scr):
    h = h_ref[...]
    a = jnp.dot(h, _resident_bf16(w1_ref, w1_scr)[...], preferred_element_type=F32)
    b = jnp.dot(h, _resident_bf16(w3_ref, w3_scr)[...], preferred_element_type=F32)
    o_ref[...] = (a * jax.nn.sigmoid(a) * b).astype(o_ref.dtype)


def _ffn_up(h, w13, layer):
    t, d = h.shape
    d_ff = w13.shape[2] // 2
    tn = 512
    nj = d_ff // tn
    tm = _row_tile(t, (1280, 640, 256, 128))
    return pl.pallas_call(
        _ffn_up_kernel,
        grid=(nj, t // tm),
        in_specs=[pl.BlockSpec((tm, d), lambda j, i: (i, 0)),
                  pl.BlockSpec((None, d, tn), lambda j, i: (layer, 0, j)),
                  pl.BlockSpec((None, d, tn), lambda j, i: (layer, 0, j + nj))],
        out_specs=pl.BlockSpec((tm, tn), lambda j, i: (i, j)),
        out_shape=jax.ShapeDtypeStruct((t, d_ff), BF16),
        scratch_shapes=[pltpu.VMEM((d, tn), BF16), pltpu.VMEM((d, tn), BF16)],
        compiler_params=_cparams(("arbitrary", "arbitrary")),
        name="ffn_up",
    )(h, w13, w13)


def _gated_residual_kernel(*refs, n_a, n_lat, tm, gate_idx):
    a_refs = refs[:n_a]
    w_ref, x_ref, mods_ref, o_ref = refs[n_a:n_a + 4]
    if w_ref.dtype != BF16:
        w_ref = _resident_bf16(w_ref, refs[n_a + 4])
    k0 = 0
    y = None
    for a_ref in a_refs:
        kk = a_ref.shape[1]
        part = jnp.dot(a_ref[...], w_ref[k0:k0 + kk, :], preferred_element_type=F32)
        y = part if y is None else y + part
        k0 += kk
    gate = _pick_mod(mods_ref, gate_idx, _is_ctx_rows(tm, n_lat, 1))
    o_ref[...] = x_ref[...] + gate * y


def _gated_residual(a_list, w, x, mods, n_lat, gate_idx, name, layer=None):
    t, d = x.shape
    k = w.shape[-2]
    tn = 512
    tm = _row_tile(t, (640, 256, 128))
    n_a = len(a_list)
    in_specs = [pl.BlockSpec((tm, a.shape[1]), lambda j, i: (i, 0)) for a in a_list]
    w_spec = (pl.BlockSpec((k, tn), lambda j, i: (0, j)) if layer is None
              else pl.BlockSpec((None, k, tn), lambda j, i: (layer, 0, j)))
    in_specs += [w_spec,
                 pl.BlockSpec((tm, tn), lambda j, i: (i, j)),
                 pl.BlockSpec((2, N_MOD, tn), lambda j, i: (0, 0, j))]
    return pl.pallas_call(
        functools.partial(_gated_residual_kernel, n_a=n_a, n_lat=n_lat, tm=tm, gate_idx=gate_idx),
        grid=(d // tn, t // tm),
        in_specs=in_specs,
        out_specs=pl.BlockSpec((tm, tn), lambda j, i: (i, j)),
        out_shape=jax.ShapeDtypeStruct((t, d), F32),
        scratch_shapes=[] if w.dtype == BF16 else [pltpu.VMEM((k, tn), BF16)],
        input_output_aliases={n_a + 1: 0},
        compiler_params=_cparams(("arbitrary", "arbitrary")),
        name=name,
    )(*a_list, w, x, mods)


def _softmax_block(qs, k, v):
    s = lax.dot_general(qs, k, (((1,), (1,)), ((), ())), preferred_element_type=F32)
    m = jnp.max(s, axis=-1, keepdims=True)
    p = jnp.exp(s - m)
    l = jnp.sum(p, axis=-1, keepdims=True)
    acc = jnp.dot(p.astype(v.dtype), v, preferred_element_type=F32)
    return m, l, acc


DIFF_PV_ROWS = 256
DIFF_SOFTMAX_ROWS = 64


def _diff_attn_kernel(*refs, tq, tk, n_ctx, n_chunks, lambda_init, pv_rows_max):
    if n_chunks > 1:
        (lq1, lk1, lq2, lk2, subw, q_ref, kc_ref, vc_ref, k_ref, v_ref, o_ref,
         qs_scr, sa_scr, mca_scr, p_scr, m_scr, l_scr, alpha_scr, acc_scr, sb_scr, mcb_scr) = refs
        buf_b = (sb_scr, mcb_scr)
    elif n_chunks == 1:
        (lq1, lk1, lq2, lk2, subw, q_ref, kc_ref, vc_ref, k_ref, v_ref, o_ref,
         qs_scr, sa_scr, mca_scr, p_scr, m_scr, l_scr, alpha_scr, acc_scr) = refs
    else:
        (lq1, lk1, lq2, lk2, subw, q_ref, kc_ref, vc_ref, o_ref,
         qs_scr, sa_scr, mca_scr, p_scr, m_scr, l_scr, alpha_scr, acc_scr) = refs
    buf_a = (sa_scr, mca_scr)
    nt = (((1,), (1,)), ((), ()))
    rows_all = 2 * tq
    pv_rows = min(pv_rows_max, rows_all)
    sm_rows = min(DIFF_SOFTMAX_ROWS, pv_rows)
    q = q_ref[...]
    comp1 = lax.broadcasted_iota(jnp.int32, (1, HEAD_W), 1) < A_QK_DIM
    zero = jnp.zeros_like(q)
    qs_scr[0:tq, :] = jnp.where(comp1, q, zero)
    qs_scr[tq:2 * tq, :] = jnp.where(comp1, zero, q)

    def scores(buf, rows, keys):
        s_buf, mc_buf = buf
        s = lax.dot_general(qs_scr[rows, :], keys, nt, preferred_element_type=F32)
        width = s.shape[1]
        s_buf[rows, 0:width] = s
        mp = s[:, 0:HEAD_W]
        for c in range(1, width // HEAD_W):
            mp = jnp.maximum(mp, s[:, c * HEAD_W:(c + 1) * HEAD_W])
        mc_buf[rows, :] = jnp.broadcast_to(jnp.max(mp, axis=-1, keepdims=True), mp.shape)

    def softmax_rows(buf, rows, n_tiles, first):
        s_buf, mc_buf = buf
        m_cur = mc_buf[rows, :]
        if first:
            m_new = m_cur
        else:
            m_prev = m_scr[rows, :]
            m_new = jnp.maximum(m_prev, m_cur)
        lp = None
        for c in range(n_tiles):
            pc = jnp.exp2(s_buf[rows, c * HEAD_W:(c + 1) * HEAD_W] - m_new)
            p_scr[rows, c * HEAD_W:(c + 1) * HEAD_W] = pc.astype(BF16)
            lp = pc if lp is None else lp + pc
        l_cur = jnp.sum(lp, axis=-1, keepdims=True)
        if first:
            l_scr[rows, :] = jnp.broadcast_to(l_cur, m_new.shape)
        else:
            alpha = jnp.exp2(m_prev - m_new)
            alpha_scr[rows, :] = alpha
            l_scr[rows, :] = alpha * l_scr[rows, :] + l_cur
        m_scr[rows, :] = m_new

    def softmax_pv(buf, width, load_v, first, after_group=None):
        for r in range(rows_all // pv_rows):
            for r2 in range(pv_rows // sm_rows):
                r0 = r * pv_rows + r2 * sm_rows
                softmax_rows(buf, slice(r0, r0 + sm_rows), width // HEAD_W, first)
            rows = slice(r * pv_rows, (r + 1) * pv_rows)
            pv = jnp.dot(p_scr[rows, 0:width], load_v(), preferred_element_type=F32)
            acc_scr[rows, :] = pv if first else alpha_scr[rows, :] * acc_scr[rows, :] + pv
            if after_group is not None:
                after_group(rows)

    def all_groups(fn):
        for r in range(rows_all // pv_rows):
            fn(slice(r * pv_rows, (r + 1) * pv_rows))

    all_groups(lambda rows: scores(buf_a, rows, kc_ref[...]))
    softmax_pv(buf_a, n_ctx, lambda: vc_ref[...], True)

    def qk_rows(j, buf, rows):
        off = pl.multiple_of(j * tk, tk)
        scores(buf, rows, k_ref[pl.ds(off, tk), :])

    def soft_pv(j, buf, j_next=None, buf_next=None):
        off = pl.multiple_of(j * tk, tk)
        after = None if j_next is None else (lambda rows: qk_rows(j_next, buf_next, rows))
        softmax_pv(buf, tk, lambda: v_ref[pl.ds(off, tk), :], False, after)

    if n_chunks == 1:
        all_groups(lambda rows: qk_rows(0, buf_a, rows))
        soft_pv(0, buf_a)
    elif n_chunks > 1:
        all_groups(lambda rows: qk_rows(0, buf_a, rows))

        def pair(j2, carry):
            soft_pv(2 * j2, buf_a, 2 * j2 + 1, buf_b)
            soft_pv(2 * j2 + 1, buf_b, 2 * j2 + 2, buf_a)
            return carry

        lax.fori_loop(0, n_chunks // 2 - 1, pair, 0)
        soft_pv(n_chunks - 2, buf_a, n_chunks - 1, buf_b)
        soft_pv(n_chunks - 1, buf_b)

    lam = (jnp.exp(jnp.sum(lq1[...] * lk1[...], axis=-1, keepdims=True))
           - jnp.exp(jnp.sum(lq2[...] * lk2[...], axis=-1, keepdims=True)) + lambda_init)
    o1 = acc_scr[0:tq, :] / l_scr[0:tq, :]
    o2 = acc_scr[tq:2 * tq, :] / l_scr[tq:2 * tq, :]
    o = o1 - lam * o2
    y = o * lax.rsqrt(jnp.mean(o * o, axis=-1, keepdims=True) + EPS) * subw[...]
    o_ref[...] = (y * (1.0 - lambda_init)).astype(o_ref.dtype)


def _diff_attn(qkv, lq1, lk1, lq2, lk2, subw, n_lat, n_ctx, lambda_init, prev=None, tq_max=512,
               pv_rows=DIFF_PV_ROWS):
    t = qkv.shape[0]
    use_lat = prev is None
    ctx_blk = n_lat // n_ctx
    small = [lq1.reshape(1, -1), lk1.reshape(1, -1), lq2.reshape(1, -1), lk2.reshape(1, -1), subw.reshape(1, -1)]
    small_specs = [pl.BlockSpec(a.shape, lambda h, qi: (0, 0)) for a in small]
    kc_spec = pl.BlockSpec((n_ctx, HEAD_W), lambda h, qi: (ctx_blk, A_HEADS + h))
    vc_spec = pl.BlockSpec((n_ctx, HEAD_W), lambda h, qi: (ctx_blk, 2 * A_HEADS + h))
    if use_lat:
        tq = _row_tile(n_lat, tuple(c for c in (1024, 512, 256, 128) if c <= tq_max))
        tk = _row_tile(n_lat, (1024, 512, 256, 128))
        n_chunks = n_lat // tk
        assert n_chunks == 1 or n_chunks % 2 == 0
        grid = (A_HEADS, n_lat // tq)
        in_specs = small_specs + [
            pl.BlockSpec((tq, HEAD_W), lambda h, qi: (qi, h)), kc_spec, vc_spec,
            pl.BlockSpec((n_lat, HEAD_W), lambda h, qi: (0, A_HEADS + h), pipeline_mode=pl.Buffered(1)),
            pl.BlockSpec((n_lat, HEAD_W), lambda h, qi: (0, 2 * A_HEADS + h), pipeline_mode=pl.Buffered(1))]
        args = small + [qkv, qkv, qkv, qkv, qkv]
        out_spec = pl.BlockSpec((tq, HEAD_W), lambda h, qi: (qi, h))
        aliases = {}
    else:
        tq, tk, n_chunks = n_ctx, 0, 0
        grid = (A_HEADS, 1)
        in_specs = small_specs + [
            pl.BlockSpec((tq, HEAD_W), lambda h, qi: (ctx_blk, h)), kc_spec, vc_spec,
            pl.BlockSpec(memory_space=pl.ANY)]
        args = small + [qkv, qkv, qkv, prev]
        out_spec = pl.BlockSpec((tq, HEAD_W), lambda h, qi: (ctx_blk, h))
        aliases = {len(args) - 1: 0}

    sw = max(tk, n_ctx)
    kern = functools.partial(_diff_attn_kernel, tq=tq, tk=tk, n_ctx=n_ctx, n_chunks=n_chunks, pv_rows_max=pv_rows,
                             lambda_init=lambda_init)
    if not use_lat:
        inner = kern

        def kern(*refs):
            n_in = len(args)
            inner(*refs[:n_in - 1], *refs[n_in:])

    return pl.pallas_call(
        kern,
        grid=grid,
        in_specs=in_specs,
        out_specs=out_spec,
        out_shape=jax.ShapeDtypeStruct((t, A_HEADS * HEAD_W), BF16),
        scratch_shapes=[pltpu.VMEM((2 * tq, HEAD_W), BF16),
                        pltpu.VMEM((2 * tq, sw), F32),
                        pltpu.VMEM((2 * tq, HEAD_W), F32),
                        pltpu.VMEM((2 * tq, sw), BF16),
                        pltpu.VMEM((2 * tq, HEAD_W), F32),
                        pltpu.VMEM((2 * tq, HEAD_W), F32),
                        pltpu.VMEM((2 * tq, HEAD_W), F32),
                        pltpu.VMEM((2 * tq, HEAD_W), F32)]
        + ([pltpu.VMEM((2 * tq, sw), F32),
            pltpu.VMEM((2 * tq, HEAD_W), F32)] if n_chunks > 1 else []),
        input_output_aliases=aliases,
        compiler_params=_cparams(("arbitrary", "arbitrary")),
        name="diff_attn" if use_lat else "diff_attn_ctx",
    )(*args)


def _na_bias_tables(rpb, rows):
    nblk = rows // NA_ROWS
    wr = min(WIN_R, rows)
    nh = rpb.shape[0]
    rq = jnp.arange(NA_ROWS)[:, None, None]
    slot = jnp.arange(3)[None, :, None]
    rk = jnp.arange(NA_ROWS)[None, None, :]
    row_sel, row_ok = [], []
    for b, dup in ((0, 0), (1, -1), (nblk - 1, 2)):
        r = NA_ROWS * b + rq
        r0 = jnp.clip(r - wr // 2, 0, rows - wr)
        rkey = NA_ROWS * (b - 1 + slot) + rk
        ok = (rkey >= r0) & (rkey < r0 + wr) & (slot != dup)
        drow = jnp.broadcast_to(rkey - r + (WIN_R - 1), ok.shape)
        row_sel.append((drow[..., None] == jnp.arange(2 * WIN_R - 1)) & ok[..., None])
        row_ok.append(ok)
    row_sel = jnp.stack(row_sel).astype(F32)
    row_ok = jnp.stack(row_ok)
    jq = jnp.arange(GRID_W)[:, None]
    jk = jnp.arange(GRID_W)[None, :]
    c0 = jnp.clip(jq - WIN_C // 2, 0, GRID_W - WIN_C)
    col_ok = (jk >= c0) & (jk < c0 + WIN_C)
    dcol = jnp.clip(jk - jq + (WIN_C - 1), 0, 2 * WIN_C - 2)
    col_sel = (dcol[..., None] == jnp.arange(2 * WIN_C - 1)).astype(F32)
    bias = jnp.einsum('vasbr,hrc,qkc->hvaqsbk', row_sel, rpb.astype(F32), col_sel,
                      precision=lax.Precision.HIGHEST)
    valid = row_ok[None, :, :, None, :, :, None] & col_ok[None, None, None, :, None, None, :]
    tab = jnp.where(valid, bias, NEG_INF)
    return tab.reshape(nh, 3, NA_ROWS * GRID_W, 3 * NA_ROWS * GRID_W)


def _natten_kernel(tab_ref, q_ref, k0_ref, k1_ref, k2_ref, v0_ref, v1_ref, v2_ref, kc_ref, vc_ref, o_ref):
    q = q_ref[...]
    nt = (((1,), (1,)), ((), ()))
    blk = q.shape[0]
    s_loc = [lax.dot_general(q, k_ref[...], nt, preferred_element_type=F32)
             + tab_ref[:, i * blk:(i + 1) * blk] for i, k_ref in enumerate((k0_ref, k1_ref, k2_ref))]
    s_ctx = lax.dot_general(q, kc_ref[...], nt, preferred_element_type=F32)
    m = jnp.max(s_ctx, axis=-1, keepdims=True)
    for s in s_loc:
        m = jnp.maximum(m, jnp.max(s, axis=-1, keepdims=True))
    p = jnp.exp(s_ctx - m)
    l = jnp.sum(p, axis=-1, keepdims=True)
    acc = jnp.dot(p.astype(BF16), vc_ref[...], preferred_element_type=F32)
    for s, v_ref in zip(s_loc, (v0_ref, v1_ref, v2_ref)):
        p = jnp.exp(s - m)
        l = l + jnp.sum(p, axis=-1, keepdims=True)
        acc = acc + jnp.dot(p.astype(BF16), v_ref[...], preferred_element_type=F32)
    o_ref[...] = (acc / l).astype(o_ref.dtype)


def _natten(qkv, tabs, n_lat, n_ctx):
    t = qkv.shape[0]
    blk = NA_ROWS * GRID_W
    nblk = n_lat // blk
    assert nblk >= 3 and blk == n_ctx
    qo, ko, vo = 3 * A_HEADS, 3 * A_HEADS + B_HEADS, 3 * A_HEADS + 2 * B_HEADS
    ctx_blk = n_lat // n_ctx

    def kv_spec(off, shift):
        return pl.BlockSpec((blk, HEAD_W), lambda h, b: (jnp.clip(b + shift, 0, nblk - 1), off + h))

    return pl.pallas_call(
        _natten_kernel,
        grid=(B_HEADS, nblk),
        in_specs=[pl.BlockSpec((None, None, blk, 3 * blk),
                               lambda h, b: (h, jnp.where(b == 0, 0, jnp.where(b == nblk - 1, 2, 1)), 0, 0)),
                  pl.BlockSpec((blk, HEAD_W), lambda h, b: (b, qo + h)),
                  kv_spec(ko, -1), kv_spec(ko, 0), kv_spec(ko, 1),
                  kv_spec(vo, -1), kv_spec(vo, 0), kv_spec(vo, 1),
                  pl.BlockSpec((n_ctx, HEAD_W), lambda h, b: (ctx_blk, ko + h)),
                  pl.BlockSpec((n_ctx, HEAD_W), lambda h, b: (ctx_blk, vo + h))],
        out_specs=pl.BlockSpec((blk, HEAD_W), lambda h, b: (b, h)),
        out_shape=jax.ShapeDtypeStruct((t, B_HEADS * HEAD_W), BF16),
        compiler_params=_cparams(("arbitrary", "arbitrary")),
        name="natten",
    )(tabs, qkv, qkv, qkv, qkv, qkv, qkv, qkv, qkv, qkv)


def _ctx_attn_kernel(q_ref, k_ref, v_ref, prev_ref, o_ref):
    del prev_ref
    _, l, acc = _softmax_block(q_ref[...], k_ref[...], v_ref[...])
    o_ref[...] = (acc / l).astype(o_ref.dtype)


def _ctx_attn(qkv, prev, n_lat, n_ctx):
    qo, ko, vo = 3 * A_HEADS, 3 * A_HEADS + B_HEADS, 3 * A_HEADS + 2 * B_HEADS
    ctx_blk = n_lat // n_ctx
    return pl.pallas_call(
        _ctx_attn_kernel,
        grid=(B_HEADS,),
        in_specs=[pl.BlockSpec((n_ctx, HEAD_W), lambda h: (ctx_blk, qo + h)),
                  pl.BlockSpec((n_ctx, HEAD_W), lambda h: (ctx_blk, ko + h)),
                  pl.BlockSpec((n_ctx, HEAD_W), lambda h: (ctx_blk, vo + h)),
                  pl.BlockSpec(memory_space=pl.ANY)],
        out_specs=pl.BlockSpec((n_ctx, HEAD_W), lambda h: (ctx_blk, h)),
        out_shape=jax.ShapeDtypeStruct(prev.shape, prev.dtype),
        input_output_aliases={3: 0},
        compiler_params=_cparams(("arbitrary",)),
        name="ctx_attn",
    )(qkv, qkv, qkv, prev)


def _s5_tables(lam_re, lam_im, b_re, b_im, c_re, c_im, log_step, d_skip):
    hp = lax.Precision.HIGHEST
    ln = S5_CHUNK
    g, p, c = S5_GROUPS, S5_STATE, S5_GROUP
    lr, li = lam_re.astype(F32), lam_im.astype(F32)
    dt = jnp.exp(log_step.astype(F32))[:, :, None]
    mag = jnp.exp(lr * dt)
    ar, ai = mag * jnp.cos(li * dt), mag * jnp.sin(li * dt)
    den = lr * lr + li * li
    nr, ni = ar - 1.0, ai
    fr = (nr * lr + ni * li) / den
    fi = (ni * lr - nr * li) / den
    br_, bi_ = b_re.astype(F32), b_im.astype(F32)
    bbr = fr[..., None] * br_ - fi[..., None] * bi_
    bbi = fr[..., None] * bi_ + fi[..., None] * br_
    lag = jnp.arange(ln + 1, dtype=F32)[:, None, None, None]
    magl = jnp.exp(lr * dt * lag)
    pr, pi_ = magl * jnp.cos(li * dt * lag), magl * jnp.sin(li * dt * lag)
    wr = pr[..., None] * bbr - pi_[..., None] * bbi
    wi = pr[..., None] * bbi + pi_[..., None] * bbr
    cr, ci = c_re.astype(F32), c_im.astype(F32)
    kern = (jnp.einsum('dgcp,ldgpe->ldgce', cr, wr[:ln], precision=hp)
            - jnp.einsum('dgcp,ldgpe->ldgce', ci, wi[:ln], precision=hp))
    s_idx = jnp.arange(ln)[:, None]
    t_idx = jnp.arange(ln)[None, :]
    lag_f = t_idx - s_idx
    kf = kern[jnp.clip(lag_f, 0, ln - 1), 0] * (lag_f >= 0)[:, :, None, None, None]
    kb = kern[jnp.clip(-lag_f, 0, ln - 1), 1] * (lag_f <= 0)[:, :, None, None, None]
    m_tab = (kf + kb).transpose(2, 0, 4, 1, 3).reshape(g, ln * c, ln * c)
    wf_r, wf_i = wr[:ln, 0][::-1], wi[:ln, 0][::-1]
    wb_r, wb_i = wr[:ln, 1], wi[:ln, 1]

    def inj(w):
        return w.transpose(1, 0, 3, 2).reshape(g, ln * c, p)

    b_tab = jnp.concatenate([inj(wf_r), inj(wf_i), inj(wb_r), inj(wb_i)], axis=-1)
    pf_r, pf_i = pr[1:ln + 1, 0], pi_[1:ln + 1, 0]
    pb_r, pb_i = pr[1:ln + 1, 1][::-1], pi_[1:ln + 1, 1][::-1]

    def rd(pw_r, pw_i, cre, cim):
        car = cre[None] * pw_r[:, :, None, :] - cim[None] * pw_i[:, :, None, :]
        cai = cre[None] * pw_i[:, :, None, :] + cim[None] * pw_r[:, :, None, :]
        to_rows = lambda a: a.transpose(1, 3, 0, 2).reshape(g, p, ln * c)
        return to_rows(car), to_rows(-cai)

    c_tab = jnp.concatenate(rd(pf_r, pf_i, cr[0], ci[0]) + rd(pb_r, pb_i, cr[1], ci[1]), axis=1)
    a_chunk = jnp.stack([jnp.stack([pr[ln, 0], pi_[ln, 0]]), jnp.stack([pr[ln, 1], pi_[ln, 1]])])
    d_tab = jnp.tile(d_skip.astype(F32).reshape(g, 1, c), (1, ln, 1)).reshape(g, 1, ln * c)
    return m_tab.astype(BF16), b_tab.astype(BF16), c_tab.astype(BF16), a_chunk, d_tab


def _s5_inject_kernel(u_ref, b_ref, o_ref):
    o_ref[...] = jnp.dot(u_ref[...], b_ref[...], preferred_element_type=F32)


def _s5_scan_kernel(a_ref, e_ref, o_ref, s_scr, *, nb):
    d = pl.program_id(0)

    @pl.when(pl.program_id(1) == 0)
    def _():
        s_scr[...] = jnp.zeros_like(s_scr)

    ar, ai = a_ref[0], a_ref[1]

    def body(i, carry):
        sr, si = carry
        c = jnp.where(d == 0, i, nb - 1 - i)
        o_ref[0, c] = sr
        o_ref[1, c] = si
        return ar * sr - ai * si + e_ref[0, c], ar * si + ai * sr + e_ref[1, c]

    sr, si = lax.fori_loop(0, nb, body, (s_scr[0], s_scr[1]))
    s_scr[0] = sr
    s_scr[1] = si


def _gelu_tanh(x):
    return 0.5 * x * (1.0 + jnp.tanh(math.sqrt(2.0 / math.pi) * (x + 0.044715 * (x * x * x))))


def _s5_readout_kernel(u_ref, m_ref, s_ref, c_ref, d_ref, o_ref):
    u = u_ref[...]
    y = jnp.dot(u, m_ref[...], preferred_element_type=F32) + u.astype(F32) * d_ref[...]
    for k in range(4):
        y = y + jnp.dot(s_ref[k].astype(BF16), c_ref[k * S5_STATE:(k + 1) * S5_STATE, :],
                        preferred_element_type=F32)
    o_ref[...] = _gelu_tanh(y)


def _s5_glu_kernel(z_ref, w_ref, o_ref):
    z = z_ref[...]
    o_ref[...] = (z * jax.nn.sigmoid(jnp.dot(z.astype(BF16), w_ref[...], preferred_element_type=F32))
                  ).astype(o_ref.dtype)


def _s5_mixer(u, tables, w_glu, n_lat, n_ctx):
    m_tab, b_tab, c_tab, a_chunk, d_tab = tables
    t = u.shape[0]
    g, c, ln, p = S5_GROUPS, S5_GROUP, S5_CHUNK, S5_STATE
    nc = t // ln
    w = ln * c
    ug = u.reshape(nc, ln, g, c).transpose(2, 0, 1, 3).reshape(g, nc, w)

    e = pl.pallas_call(
        _s5_inject_kernel,
        grid=(g,),
        in_specs=[pl.BlockSpec((None, nc, w), lambda i: (i, 0, 0)),
                  pl.BlockSpec((None, w, 4 * p), lambda i: (i, 0, 0))],
        out_specs=pl.BlockSpec((None, nc, 4 * p), lambda i: (i, 0, 0)),
        out_shape=jax.ShapeDtypeStruct((g, nc, 4 * p), F32),
        compiler_params=_cparams(("arbitrary",)),
        name="s5_inject",
    )(ug, b_tab)
    e = e.reshape(g, nc, 2, 2, p).transpose(2, 3, 1, 0, 4)

    nb = S5_BLOCK
    assert (n_lat // ln) % nb == 0 and n_ctx // ln == nb
    lat_blocks = n_lat // ln // nb

    def blk(d, s):
        return jnp.where(s == 0, lat_blocks, jnp.where(d == 0, s - 1, lat_blocks - s))

    s_in = pl.pallas_call(
        functools.partial(_s5_scan_kernel, nb=nb),
        grid=(2, lat_blocks + 1),
        in_specs=[pl.BlockSpec((None, 2, g, p), lambda d, s: (d, 0, 0, 0)),
                  pl.BlockSpec((None, 2, nb, g, p), lambda d, s: (d, 0, blk(d, s), 0, 0))],
        out_specs=pl.BlockSpec((None, 2, nb, g, p), lambda d, s: (d, 0, blk(d, s), 0, 0)),
        out_shape=jax.ShapeDtypeStruct((2, 2, nc, g, p), F32),
        scratch_shapes=[pltpu.VMEM((2, g, p), F32)],
        compiler_params=_cparams(("arbitrary", "arbitrary")),
        name="s5_scan",
    )(a_chunk, e)
    s_in = s_in.transpose(3, 0, 1, 2, 4).reshape(g, 4, nc, p)

    z = pl.pallas_call(
        _s5_readout_kernel,
        grid=(g,),
        in_specs=[pl.BlockSpec((None, nc, w), lambda i: (i, 0, 0)),
                  pl.BlockSpec((None, w, w), lambda i: (i, 0, 0)),
                  pl.BlockSpec((None, 4, nc, p), lambda i: (i, 0, 0, 0)),
                  pl.BlockSpec((None, 4 * p, w), lambda i: (i, 0, 0)),
                  pl.BlockSpec((None, 1, w), lambda i: (i, 0, 0))],
        out_specs=pl.BlockSpec((None, nc, w), lambda i: (i, 0, 0)),
        out_shape=jax.ShapeDtypeStruct((g, nc, w), F32),
        compiler_params=_cparams(("arbitrary",)),
        name="s5_readout",
    )(ug, m_tab, s_in, c_tab, d_tab)
    z = z.reshape(g, nc, ln, c).transpose(1, 2, 0, 3).reshape(t, g * c)

    tm = _row_tile(t, (1280, 640, 256, 128))
    return pl.pallas_call(
        _s5_glu_kernel,
        grid=(t // tm,),
        in_specs=[pl.BlockSpec((tm, g * c), lambda i: (i, 0)),
                  pl.BlockSpec((g * c, g * c), lambda i: (0, 0))],
        out_specs=pl.BlockSpec((tm, g * c), lambda i: (i, 0)),
        out_shape=jax.ShapeDtypeStruct((t, g * c), BF16),
        compiler_params=_cparams(("arbitrary",)),
        name="s5_glu",
    )(z, w_glu)


def _ret_tables(decay_logit):
    scale = HEAD_W ** -0.5
    lg = jax.nn.log_sigmoid(decay_logit.astype(F32))
    lf, lb = lg[0][:, None, None], lg[1][:, None, None]
    i = jnp.arange(RET_CHUNK, dtype=F32)[None, :, None]
    j = jnp.arange(RET_CHUNK, dtype=F32)[None, None, :]
    diff = i - j
    intra = (jnp.where(diff >= 0, jnp.exp(lf * jnp.maximum(diff, 0.0)), 0.0)
             + jnp.where(diff <= 0, jnp.exp(lb * jnp.maximum(-diff, 0.0)), 0.0)) * scale
    ones = jnp.ones((1, 1, HEAD_W), F32)
    q_f = jnp.exp(lf * (i + 1.0)) * ones
    k_f = jnp.exp(lf * (RET_CHUNK - 1.0 - i)) * scale * ones
    c_f = jnp.exp(lf * RET_CHUNK) * jnp.ones((1, RET_CHUNK, HEAD_W), F32)
    q_b = jnp.exp(lb * (RET_CHUNK - i)) * ones
    k_b = jnp.exp(lb * i) * scale * ones
    c_b = jnp.exp(lb * RET_CHUNK) * jnp.ones((1, RET_CHUNK, HEAD_W), F32)
    return intra, jnp.stack([q_f, k_f, c_f]), jnp.stack([q_b, k_b, c_b])


def _ret_state_step(q, k, v, dec_ref, s_scr, h):
    s = s_scr[h]
    qd = (q.astype(F32) * dec_ref[0, h]).astype(BF16)
    kd = (k.astype(F32) * dec_ref[1, h]).astype(BF16)
    o = jnp.dot(qd, s.astype(BF16), preferred_element_type=F32)
    s_scr[h] = s * dec_ref[2, h] + lax.dot_general(kd, v, (((0,), (0,)), ((), ())),
                                                    preferred_element_type=F32)
    return o


def _ret_fwd_kernel(intra_ref, dec_ref, q_ref, k_ref, v_ref, o_ref, s_scr):
    @pl.when(pl.program_id(0) == 0)
    def _():
        s_scr[...] = jnp.zeros_like(s_scr)

    for h in range(RET_HEADS):
        sl = slice(h * HEAD_W, (h + 1) * HEAD_W)
        q, k, v = q_ref[:, sl], k_ref[:, sl], v_ref[:, sl]
        att = lax.dot_general(q, k, (((1,), (1,)), ((), ())), preferred_element_type=F32) * intra_ref[h]
        o = jnp.dot(att.astype(BF16), v, preferred_element_type=F32)
        o_ref[:, sl] = o + _ret_state_step(q, k, v, dec_ref, s_scr, h)


def _ret_bwd_kernel(dec_ref, q_ref, k_ref, v_ref, g_ref, o1_ref, o_ref, s_scr):
    @pl.when(pl.program_id(0) == 0)
    def _():
        s_scr[...] = jnp.zeros_like(s_scr)

    for h in range(RET_HEADS):
        sl = slice(h * HEAD_W, (h + 1) * HEAD_W)
        q, k, v = q_ref[:, sl], k_ref[:, sl], v_ref[:, sl]
        o = o1_ref[:, sl] + _ret_state_step(q, k, v, dec_ref, s_scr, h)
        y = o * lax.rsqrt(jnp.mean(o * o, axis=-1, keepdims=True) + EPS)
        gate = g_ref[:, sl].astype(F32)
        o_ref[:, sl] = (y * (gate * jax.nn.sigmoid(gate))).astype(o_ref.dtype)


def _retention(proj, tables, n_lat, n_ctx):
    intra, dec_f, dec_b = tables
    t = proj.shape[0]
    wd = RET_HEADS * HEAD_W
    ck = RET_CHUNK
    n_lat_c, n_ctx_c = n_lat // ck, n_ctx // ck
    steps = n_lat_c + n_ctx_c

    def fwd_blk(s):
        return jnp.where(s < n_ctx_c, n_lat_c + s, s - n_ctx_c)

    def bwd_blk(s):
        return jnp.where(s < n_ctx_c, n_lat_c + n_ctx_c - 1 - s, n_lat_c - 1 - (s - n_ctx_c))

    def col_spec(col, order):
        return pl.BlockSpec((ck, wd), lambda s: (order(s), col))

    tab_spec = pl.BlockSpec((3, RET_HEADS, ck, HEAD_W), lambda s: (0, 0, 0, 0))
    o1 = pl.pallas_call(
        _ret_fwd_kernel,
        grid=(steps,),
        in_specs=[pl.BlockSpec((RET_HEADS, ck, ck), lambda s: (0, 0, 0)), tab_spec,
                  col_spec(0, fwd_blk), col_spec(1, fwd_blk), col_spec(2, fwd_blk)],
        out_specs=col_spec(0, fwd_blk),
        out_shape=jax.ShapeDtypeStruct((t, wd), F32),
        scratch_shapes=[pltpu.VMEM((RET_HEADS, HEAD_W, HEAD_W), F32)],
        compiler_params=_cparams(("arbitrary",)),
        name="retention_fwd",
    )(intra, dec_f, proj, proj, proj)
    return pl.pallas_call(
        _ret_bwd_kernel,
        grid=(steps,),
        in_specs=[tab_spec, col_spec(0, bwd_blk), col_spec(1, bwd_blk), col_spec(2, bwd_blk),
                  col_spec(3, bwd_blk), col_spec(0, bwd_blk)],
        out_specs=col_spec(0, bwd_blk),
        out_shape=jax.ShapeDtypeStruct((t, wd), BF16),
        scratch_shapes=[pltpu.VMEM((RET_HEADS, HEAD_W, HEAD_W), F32)],
        compiler_params=_cparams(("arbitrary",)),
        name="retention_bwd",
    )(dec_b, proj, proj, proj, proj, o1)


def _rope_tables(n_lat, n_ctx):
    n_freq = A_QK_DIM // 4
    freq = ROPE_BASE ** (-jnp.arange(n_freq, dtype=F32) / n_freq)
    tok = jnp.arange(n_lat)
    row = (tok // GRID_W).astype(F32)
    col = (tok % GRID_W).astype(F32)
    ang = jnp.stack([row[:, None] * freq, col[:, None] * freq], axis=1)
    ang = jnp.broadcast_to(ang[:, :, None, :], (n_lat, 2, 2, n_freq)).reshape(n_lat, A_QK_DIM)
    reps = HEAD_W // A_QK_DIM
    cos = jnp.concatenate([jnp.tile(jnp.cos(ang), (1, reps)), jnp.ones((n_ctx, HEAD_W), F32)], axis=0)
    sin = jnp.concatenate([jnp.tile(jnp.sin(ang), (1, reps)), jnp.zeros((n_ctx, HEAD_W), F32)], axis=0)
    return cos, sin


def kernel(x, c, ctx, c_ctx, ada_w, ada_b, norm1_w, norm2_w, ffn_w13, ffn_w2, e_w_in, e_w_out, diff_lq1, diff_lk1, diff_lq2, diff_lk2, diff_subln_w, na_rpb, o_w_in, o_w_out, s5_lam_re, s5_lam_im, s5_b_re, s5_b_im, s5_c_re, s5_c_im, s5_log_step, s5_d, s5_w_glu, ret_decay_logit, final_norm_w):
    bsz, n_lat, d = x.shape
    n_ctx = ctx.shape[1]
    depth = ada_w.shape[0]
    assert bsz == 1
    xs = jnp.concatenate([x[0], ctx[0]], axis=0)
    mods_all = _mods(c, c_ctx, ada_w, ada_b)
    cos, sin = _rope_tables(n_lat, n_ctx)
    ret_w = RET_HEADS * HEAD_W

    for i in range(depth):
        compute_ctx = i != depth - 1
        mods = mods_all[i]
        j = i // 2
        h = _normmod(xs, norm1_w[i], mods, n_lat, 0, 1)
        if i % 2 == 0:
            lambda_init = 0.8 - 0.6 * math.exp(-0.3 * i)
            qkv = _inproj_even(h, e_w_in, j, cos, sin)
            lam_args = (diff_lq1[j], diff_lk1[j], diff_lq2[j], diff_lk2[j], diff_subln_w[j])
            o_a = _diff_attn(qkv, *lam_args, n_lat, n_ctx, lambda_init, tq_max=512 if j == 0 else 1024,
                             pv_rows=512)
            o_b = _natten(qkv, _na_bias_tables(na_rpb[j], n_lat // GRID_W), n_lat, n_ctx)
            if compute_ctx:
                o_a = _diff_attn(qkv, *lam_args, n_lat, n_ctx, lambda_init, prev=o_a)
                o_b = _ctx_attn(qkv, o_b, n_lat, n_ctx)
            xs = _gated_residual([o_a, o_b], e_w_out, xs, mods, n_lat, 2, "outproj_even", layer=j)
        else:
            w_in = o_w_in[j]
            w_in = jnp.concatenate([w_in[:, S5_CH:], w_in[:, :S5_CH]], axis=1).astype(BF16)
            proj = _matmul(h, w_in, 512, "inproj_odd")
            s5_tabs = _s5_tables(s5_lam_re[j], s5_lam_im[j], s5_b_re[j], s5_b_im[j], s5_c_re[j], s5_c_im[j],
                                 s5_log_step[j], s5_d[j])
            y_c = _s5_mixer(proj[:, 4 * ret_w:], s5_tabs, s5_w_glu[j].astype(BF16), n_lat, n_ctx)
            y_d = _retention(proj, _ret_tables(ret_decay_logit[j]), n_lat, n_ctx)
            xs = _gated_residual([y_c, y_d], o_w_out, xs, mods, n_lat, 2, "outproj_odd", layer=j)
        h = _normmod(xs, norm2_w[i], mods, n_lat, 3, 4)
        a = _ffn_up(h, ffn_w13, i)
        xs = _gated_residual([a], ffn_w2[i].astype(BF16), xs, mods, n_lat, 5, "ffn_down")
    return _final_norm(xs, final_norm_w, n_lat)[None]
```

```python
import functools
import math

import jax
import jax.numpy as jnp
from jax import lax
from jax.experimental import pallas as pl
from jax.experimental.pallas import tpu as pltpu

F32 = jnp.float32
BF16 = jnp.bfloat16

GRID_W = 64
A_HEADS = 8
A_QK_DIM = 64
HEAD_W = 128
B_HEADS = 8
WIN_R = 8
WIN_C = 16
NA_ROWS = 4
S5_CH = 512
S5_GROUP = 16
S5_GROUPS = 32
S5_STATE = 64
S5_CHUNK = 16
S5_BLOCK = 16
RET_HEADS = 12
RET_CHUNK = 128
ROPE_BASE = 10000.0
EPS = 1e-6
NEG_INF = -1e30
VMEM_LIMIT = 56 * 1024 * 1024
N_MOD = 8


def _cparams(sem):
    return pltpu.CompilerParams(dimension_semantics=sem, vmem_limit_bytes=VMEM_LIMIT)


def _row_tile(t, candidates):
    for c in candidates:
        if t % c == 0:
            return c
    raise ValueError(f"no row tile for {t}")


def _pick_mod(mods_ref, idx, is_ctx):
    return jnp.where(is_ctx, mods_ref[1, idx:idx + 1, :], mods_ref[0, idx:idx + 1, :])


def _is_ctx_rows(tm, n_lat, axis):
    row = pl.program_id(axis) * tm + lax.broadcasted_iota(jnp.int32, (tm, 1), 0)
    return row >= n_lat


def _mods_kernel(s_ref, w_ref, b_ref, o_ref):
    s = s_ref[...]
    s = s * jax.nn.sigmoid(s)
    o_ref[...] = jnp.dot(s, w_ref[...], preferred_element_type=F32,
                         precision=lax.Precision.HIGHEST) + b_ref[...]


def _mods(c, c_ctx, ada_w, ada_b):
    depth, d, w6 = ada_w.shape
    s = jnp.zeros((8, d), F32).at[0].set(c[0]).at[1].set(c_ctx)
    tn = 1024
    out = pl.pallas_call(
        _mods_kernel,
        grid=(depth, w6 // tn),
        in_specs=[pl.BlockSpec((8, d), lambda l, j: (0, 0)),
                  pl.BlockSpec((None, d, tn), lambda l, j: (l, 0, j)),
                  pl.BlockSpec((None, 1, tn), lambda l, j: (l, 0, j))],
        out_specs=pl.BlockSpec((None, 8, tn), lambda l, j: (l, 0, j)),
        out_shape=jax.ShapeDtypeStruct((depth, 8, w6), F32),
        compiler_params=_cparams(("arbitrary", "arbitrary")),
        name="ada_mods",
    )(s, ada_w, ada_b.reshape(depth, 1, w6))
    m = out[:, :2].reshape(depth, 2, 6, d)
    return jnp.pad(m, ((0, 0), (0, 0), (0, N_MOD - 6), (0, 0)))


def _normmod_kernel(x_ref, w_ref, mods_ref, o_ref, *, n_lat, tm, shift_idx, scale_idx):
    x = x_ref[...]
    y = x * lax.rsqrt(jnp.mean(x * x, axis=-1, keepdims=True) + EPS) * w_ref[...]
    is_ctx = _is_ctx_rows(tm, n_lat, 0)
    shift = _pick_mod(mods_ref, shift_idx, is_ctx)
    scale = _pick_mod(mods_ref, scale_idx, is_ctx)
    o_ref[...] = (y * (1.0 + scale) + shift).astype(o_ref.dtype)


def _normmod(x, w, mods, n_lat, shift_idx, scale_idx):
    t, d = x.shape
    tm = _row_tile(t, (640, 256, 128))
    return pl.pallas_call(
        functools.partial(_normmod_kernel, n_lat=n_lat, tm=tm, shift_idx=shift_idx, scale_idx=scale_idx),
        grid=(t // tm,),
        in_specs=[pl.BlockSpec((tm, d), lambda i: (i, 0)),
                  pl.BlockSpec((1, d), lambda i: (0, 0)),
                  pl.BlockSpec((2, N_MOD, d), lambda i: (0, 0, 0))],
        out_specs=pl.BlockSpec((tm, d), lambda i: (i, 0)),
        out_shape=jax.ShapeDtypeStruct((t, d), BF16),
        compiler_params=_cparams(("arbitrary",)),
        name="norm_modulate",
    )(x, w.reshape(1, d), mods)


def _final_norm_kernel(x_ref, w_ref, o_ref):
    x = x_ref[...]
    o_ref[...] = x * lax.rsqrt(jnp.mean(x * x, axis=-1, keepdims=True) + EPS) * w_ref[...]


def _final_norm(x, w, n_lat):
    d = x.shape[1]
    tm = _row_tile(n_lat, (512, 256, 128))
    return pl.pallas_call(
        _final_norm_kernel,
        grid=(n_lat // tm,),
        in_specs=[pl.BlockSpec((tm, d), lambda i: (i, 0)),
                  pl.BlockSpec((1, d), lambda i: (0, 0))],
        out_specs=pl.BlockSpec((tm, d), lambda i: (i, 0)),
        out_shape=jax.ShapeDtypeStruct((n_lat, d), F32),
        compiler_params=_cparams(("arbitrary",)),
        name="final_norm",
    )(x, w.reshape(1, d))


def _rope_store(acc, cos, sin, o_ref, scale):
    first_half = (lax.broadcasted_iota(jnp.int32, (1, HEAD_W), 1) % 32) < 16
    for c in range(acc.shape[1] // HEAD_W):
        x = acc[:, c * HEAD_W:(c + 1) * HEAD_W]
        rot = jnp.where(first_half, -pltpu.roll(x, HEAD_W - 16, 1), pltpu.roll(x, 16, 1))
        o_ref[:, c * HEAD_W:(c + 1) * HEAD_W] = ((x * cos + rot * sin) * scale).astype(o_ref.dtype)


def _resident_bf16(w_ref, w_scr):
    @pl.when(pl.program_id(1) == 0)
    def _():
        w_scr[...] = w_ref[...].astype(BF16)

    return w_scr


def _inproj_even_kernel(h_ref, w_ref, cos_ref, sin_ref, o_ref, w_scr, *, a_scale, b_scale):
    j = pl.program_id(0)
    acc = jnp.dot(h_ref[...], _resident_bf16(w_ref, w_scr)[...], preferred_element_type=F32)

    @pl.when(j == 0)
    def _():
        _rope_store(acc, cos_ref[...], sin_ref[...], o_ref, a_scale)

    @pl.when(j == 1)
    def _():
        _rope_store(acc, cos_ref[...], sin_ref[...], o_ref, 1.0)

    @pl.when(j == 3)
    def _():
        o_ref[...] = (acc * b_scale).astype(o_ref.dtype)

    @pl.when((j == 2) | (j > 3))
    def _():
        o_ref[...] = acc.astype(o_ref.dtype)


def _inproj_even(h, w, layer, cos, sin):
    t, d = h.shape
    n = w.shape[2]
    tn = n // 6
    tm = _row_tile(t, (1280, 640, 256, 128))
    return pl.pallas_call(
        functools.partial(_inproj_even_kernel, a_scale=A_QK_DIM ** -0.5 * math.log2(math.e),
                          b_scale=HEAD_W ** -0.5),
        grid=(6, t // tm),
        in_specs=[pl.BlockSpec((tm, d), lambda j, i: (i, 0)),
                  pl.BlockSpec((None, d, tn), lambda j, i: (layer, 0, j)),
                  pl.BlockSpec((tm, HEAD_W), lambda j, i: (i, 0)),
                  pl.BlockSpec((tm, HEAD_W), lambda j, i: (i, 0))],
        out_specs=pl.BlockSpec((tm, tn), lambda j, i: (i, j)),
        out_shape=jax.ShapeDtypeStruct((t, n), BF16),
        scratch_shapes=[pltpu.VMEM((d, tn), BF16)],
        compiler_params=_cparams(("arbitrary", "arbitrary")),
        name="inproj_even",
    )(h, w, cos, sin)


def _matmul_kernel(h_ref, w_ref, o_ref):
    o_ref[...] = jnp.dot(h_ref[...], w_ref[...], preferred_element_type=F32).astype(o_ref.dtype)


def _matmul(h, w, tn, name):
    t, d = h.shape
    n = w.shape[1]
    tm = _row_tile(t, (1280, 640, 256, 128))
    return pl.pallas_call(
        _matmul_kernel,
        grid=(n // tn, t // tm),
        in_specs=[pl.BlockSpec((tm, d), lambda j, i: (i, 0)),
                  pl.BlockSpec((d, tn), lambda j, i: (0, j))],
        out_specs=pl.BlockSpec((tm, tn), lambda j, i: (i, j)),
        out_shape=jax.ShapeDtypeStruct((t, n), BF16),
        compiler_params=_cparams(("arbitrary", "arbitrary")),
        name=name,
    )(h, w)


def _ffn_up_kernel(h_ref, w1_ref, w3_ref, o_ref):
    h = h_ref[...]
    a = jnp.dot(h, w1_ref[...], preferred_element_type=F32)
    b = jnp.dot(h, w3_ref[...], preferred_element_type=F32)
    o_ref[...] = (a * jax.nn.sigmoid(a) * b).astype(o_ref.dtype)


def _ffn_up(h, w13, layer):
    t, d = h.shape
    d_ff = w13.shape[2] // 2
    tn = 512
    nj = d_ff // tn
    tm = _row_tile(t, (1280, 640, 256, 128))
    return pl.pallas_call(
        _ffn_up_kernel,
        grid=(nj, t // tm),
        in_specs=[pl.BlockSpec((tm, d), lambda j, i: (i, 0)),
                  pl.BlockSpec((None, d, tn), lambda j, i: (layer, 0, j)),
                  pl.BlockSpec((None, d, tn), lambda j, i: (layer, 0, j + nj))],
        out_specs=pl.BlockSpec((tm, tn), lambda j, i: (i, j)),
        out_shape=jax.ShapeDtypeStruct((t, d_ff), BF16),
        compiler_params=_cparams(("arbitrary", "arbitrary")),
        name="ffn_up",
    )(h, w13, w13)


def _gated_residual_kernel(*refs, n_a, n_lat, tm, gate_idx):
    a_refs = refs[:n_a]
    w_ref, x_ref, mods_ref, o_ref = refs[n_a:n_a + 4]
    if w_ref.dtype != BF16:
        w_ref = _resident_bf16(w_ref, refs[n_a + 4])
    k0 = 0
    y = None
    for a_ref in a_refs:
        kk = a_ref.shape[1]
        part = jnp.dot(a_ref[...], w_ref[k0:k0 + kk, :], preferred_element_type=F32)
        y = part if y is None else y + part
        k0 += kk
    gate = _pick_mod(mods_ref, gate_idx, _is_ctx_rows(tm, n_lat, 1))
    o_ref[...] = x_ref[...] + gate * y


def _gated_residual(a_list, w, x, mods, n_lat, gate_idx, name, layer=None):
    t, d = x.shape
    k = w.shape[-2]
    tn = 512
    tm = _row_tile(t, (640, 256, 128))
    n_a = len(a_list)
    in_specs = [pl.BlockSpec((tm, a.shape[1]), lambda j, i: (i, 0)) for a in a_list]
    w_spec = (pl.BlockSpec((k, tn), lambda j, i: (0, j)) if layer is None
              else pl.BlockSpec((None, k, tn), lambda j, i: (layer, 0, j)))
    in_specs += [w_spec,
                 pl.BlockSpec((tm, tn), lambda j, i: (i, j)),
                 pl.BlockSpec((2, N_MOD, tn), lambda j, i: (0, 0, j))]
    return pl.pallas_call(
        functools.partial(_gated_residual_kernel, n_a=n_a, n_lat=n_lat, tm=tm, gate_idx=gate_idx),
        grid=(d // tn, t // tm),
        in_specs=in_specs,
        out_specs=pl.BlockSpec((tm, tn), lambda j, i: (i, j)),
        out_shape=jax.ShapeDtypeStruct((t, d), F32),
        scratch_shapes=[] if w.dtype == BF16 else [pltpu.VMEM((k, tn), BF16)],
        input_output_aliases={n_a + 1: 0},
        compiler_params=_cparams(("arbitrary", "arbitrary")),
        name=name,
    )(*a_list, w, x, mods)


def _softmax_block(qs, k, v):
    s = lax.dot_general(qs, k, (((1,), (1,)), ((), ())), preferred_element_type=F32)
    m = jnp.max(s, axis=-1, keepdims=True)
    p = jnp.exp(s - m)
    l = jnp.sum(p, axis=-1, keepdims=True)
    acc = jnp.dot(p.astype(v.dtype), v, preferred_element_type=F32)
    return m, l, acc


DIFF_PV_ROWS = 256
DIFF_SOFTMAX_ROWS = 64


def _diff_attn_kernel(*refs, tq, tk, n_ctx, n_chunks, lambda_init, pv_rows_max, exp_dtype):
    if n_chunks > 1:
        (lq1, lk1, lq2, lk2, subw, q_ref, kc_ref, vc_ref, k_ref, v_ref, o_ref,
         qs_scr, sa_scr, mca_scr, p_scr, m_scr, l_scr, alpha_scr, acc_scr, sb_scr, mcb_scr) = refs
        buf_b = (sb_scr, mcb_scr)
    elif n_chunks == 1:
        (lq1, lk1, lq2, lk2, subw, q_ref, kc_ref, vc_ref, k_ref, v_ref, o_ref,
         qs_scr, sa_scr, mca_scr, p_scr, m_scr, l_scr, alpha_scr, acc_scr) = refs
    else:
        (lq1, lk1, lq2, lk2, subw, q_ref, kc_ref, vc_ref, o_ref,
         qs_scr, sa_scr, mca_scr, p_scr, m_scr, l_scr, alpha_scr, acc_scr) = refs
    buf_a = (sa_scr, mca_scr)
    nt = (((1,), (1,)), ((), ()))
    rows_all = 2 * tq
    pv_rows = min(pv_rows_max, rows_all)
    sm_rows = min(DIFF_SOFTMAX_ROWS, pv_rows)
    q = q_ref[...]
    comp1 = lax.broadcasted_iota(jnp.int32, (1, HEAD_W), 1) < A_QK_DIM
    zero = jnp.zeros_like(q)
    qs_scr[0:tq, :] = jnp.where(comp1, q, zero)
    qs_scr[tq:2 * tq, :] = jnp.where(comp1, zero, q)

    def scores(buf, rows, keys):
        s_buf, mc_buf = buf
        s = lax.dot_general(qs_scr[rows, :], keys, nt, preferred_element_type=F32)
        width = s.shape[1]
        s_buf[rows, 0:width] = s
        mp = s[:, 0:HEAD_W]
        for c in range(1, width // HEAD_W):
            mp = jnp.maximum(mp, s[:, c * HEAD_W:(c + 1) * HEAD_W])
        mc_buf[rows, :] = jnp.broadcast_to(jnp.max(mp, axis=-1, keepdims=True), mp.shape)

    def softmax_rows(buf, rows, n_tiles, first):
        s_buf, mc_buf = buf
        m_cur = mc_buf[rows, :]
        if first:
            m_new = m_cur
        else:
            m_prev = m_scr[rows, :]
            m_new = jnp.maximum(m_prev, m_cur)
        lp = None
        for c in range(n_tiles):
            pc = jnp.exp2((s_buf[rows, c * HEAD_W:(c + 1) * HEAD_W] - m_new).astype(exp_dtype))
            p_scr[rows, c * HEAD_W:(c + 1) * HEAD_W] = pc.astype(BF16)
            lp = pc if lp is None else lp + pc
        l_cur = jnp.sum(lp.astype(F32), axis=-1, keepdims=True)
        if first:
            l_scr[rows, :] = jnp.broadcast_to(l_cur, m_new.shape)
        else:
            alpha = jnp.exp2(m_prev - m_new)
            alpha_scr[rows, :] = alpha
            l_scr[rows, :] = alpha * l_scr[rows, :] + l_cur
        m_scr[rows, :] = m_new

    def softmax_pv(buf, width, load_v, first, after_group=None):
        for r in range(rows_all // pv_rows):
            for r2 in range(pv_rows // sm_rows):
                r0 = r * pv_rows + r2 * sm_rows
                softmax_rows(buf, slice(r0, r0 + sm_rows), width // HEAD_W, first)
            rows = slice(r * pv_rows, (r + 1) * pv_rows)
            pv = jnp.dot(p_scr[rows, 0:width], load_v(), preferred_element_type=F32)
            acc_scr[rows, :] = pv if first else alpha_scr[rows, :] * acc_scr[rows, :] + pv
            if after_group is not None:
                after_group(rows)

    def all_groups(fn):
        for r in range(rows_all // pv_rows):
            fn(slice(r * pv_rows, (r + 1) * pv_rows))

    all_groups(lambda rows: scores(buf_a, rows, kc_ref[...]))
    softmax_pv(buf_a, n_ctx, lambda: vc_ref[...], True)

    def qk_rows(j, buf, rows):
        off = pl.multiple_of(j * tk, tk)
        scores(buf, rows, k_ref[pl.ds(off, tk), :])

    def soft_pv(j, buf, j_next=None, buf_next=None):
        off = pl.multiple_of(j * tk, tk)
        after = None if j_next is None else (lambda rows: qk_rows(j_next, buf_next, rows))
        softmax_pv(buf, tk, lambda: v_ref[pl.ds(off, tk), :], False, after)

    if n_chunks == 1:
        all_groups(lambda rows: qk_rows(0, buf_a, rows))
        soft_pv(0, buf_a)
    elif n_chunks > 1:
        all_groups(lambda rows: qk_rows(0, buf_a, rows))

        def pair(j2, carry):
            soft_pv(2 * j2, buf_a, 2 * j2 + 1, buf_b)
            soft_pv(2 * j2 + 1, buf_b, 2 * j2 + 2, buf_a)
            return carry

        lax.fori_loop(0, n_chunks // 2 - 1, pair, 0)
        soft_pv(n_chunks - 2, buf_a, n_chunks - 1, buf_b)
        soft_pv(n_chunks - 1, buf_b)

    lam = (jnp.exp(jnp.sum(lq1[...] * lk1[...], axis=-1, keepdims=True))
           - jnp.exp(jnp.sum(lq2[...] * lk2[...], axis=-1, keepdims=True)) + lambda_init)
    o1 = acc_scr[0:tq, :] / l_scr[0:tq, :]
    o2 = acc_scr[tq:2 * tq, :] / l_scr[tq:2 * tq, :]
    o = o1 - lam * o2
    y = o * lax.rsqrt(jnp.mean(o * o, axis=-1, keepdims=True) + EPS) * subw[...]
    o_ref[...] = (y * (1.0 - lambda_init)).astype(o_ref.dtype)


def _diff_attn(qkv, lq1, lk1, lq2, lk2, subw, n_lat, n_ctx, lambda_init, prev=None, tq_max=512,
               pv_rows=DIFF_PV_ROWS, exp_dtype=F32):
    t = qkv.shape[0]
    use_lat = prev is None
    ctx_blk = n_lat // n_ctx
    small = [lq1.reshape(1, -1), lk1.reshape(1, -1), lq2.reshape(1, -1), lk2.reshape(1, -1), subw.reshape(1, -1)]
    small_specs = [pl.BlockSpec(a.shape, lambda h, qi: (0, 0)) for a in small]
    kc_spec = pl.BlockSpec((n_ctx, HEAD_W), lambda h, qi: (ctx_blk, A_HEADS + h))
    vc_spec = pl.BlockSpec((n_ctx, HEAD_W), lambda h, qi: (ctx_blk, 2 * A_HEADS + h))
    if use_lat:
        tq = _row_tile(n_lat, tuple(c for c in (1024, 512, 256, 128) if c <= tq_max))
        tk = _row_tile(n_lat, (1024, 512, 256, 128))
        n_chunks = n_lat // tk
        assert n_chunks == 1 or n_chunks % 2 == 0
        grid = (A_HEADS, n_lat // tq)
        in_specs = small_specs + [
            pl.BlockSpec((tq, HEAD_W), lambda h, qi: (qi, h)), kc_spec, vc_spec,
            pl.BlockSpec((n_lat, HEAD_W), lambda h, qi: (0, A_HEADS + h), pipeline_mode=pl.Buffered(1)),
            pl.BlockSpec((n_lat, HEAD_W), lambda h, qi: (0, 2 * A_HEADS + h), pipeline_mode=pl.Buffered(1))]
        args = small + [qkv, qkv, qkv, qkv, qkv]
        out_spec = pl.BlockSpec((tq, HEAD_W), lambda h, qi: (qi, h))
        aliases = {}
    else:
        tq, tk, n_chunks = n_ctx, 0, 0
        grid = (A_HEADS, 1)
        in_specs = small_specs + [
            pl.BlockSpec((tq, HEAD_W), lambda h, qi: (ctx_blk, h)), kc_spec, vc_spec,
            pl.BlockSpec(memory_space=pl.ANY)]
        args = small + [qkv, qkv, qkv, prev]
        out_spec = pl.BlockSpec((tq, HEAD_W), lambda h, qi: (ctx_blk, h))
        aliases = {len(args) - 1: 0}

    sw = max(tk, n_ctx)
    kern = functools.partial(_diff_attn_kernel, tq=tq, tk=tk, n_ctx=n_ctx, n_chunks=n_chunks, pv_rows_max=pv_rows, exp_dtype=exp_dtype,
                             lambda_init=lambda_init)
    if not use_lat:
        inner = kern

        def kern(*refs):
            n_in = len(args)
            inner(*refs[:n_in - 1], *refs[n_in:])

    return pl.pallas_call(
        kern,
        grid=grid,
        in_specs=in_specs,
        out_specs=out_spec,
        out_shape=jax.ShapeDtypeStruct((t, A_HEADS * HEAD_W), BF16),
        scratch_shapes=[pltpu.VMEM((2 * tq, HEAD_W), BF16),
                        pltpu.VMEM((2 * tq, sw), F32),
                        pltpu.VMEM((2 * tq, HEAD_W), F32),
                        pltpu.VMEM((2 * tq, sw), BF16),
                        pltpu.VMEM((2 * tq, HEAD_W), F32),
                        pltpu.VMEM((2 * tq, HEAD_W), F32),
                        pltpu.VMEM((2 * tq, HEAD_W), F32),
                        pltpu.VMEM((2 * tq, HEAD_W), F32)]
        + ([pltpu.VMEM((2 * tq, sw), F32),
            pltpu.VMEM((2 * tq, HEAD_W), F32)] if n_chunks > 1 else []),
        input_output_aliases=aliases,
        compiler_params=_cparams(("arbitrary", "arbitrary")),
        name="diff_attn" if use_lat else "diff_attn_ctx",
    )(*args)


def _na_bias_tables(rpb, rows):
    nblk = rows // NA_ROWS
    wr = min(WIN_R, rows)
    nh = rpb.shape[0]
    rq = jnp.arange(NA_ROWS)[:, None, None]
    slot = jnp.arange(3)[None, :, None]
    rk = jnp.arange(NA_ROWS)[None, None, :]
    row_sel, row_ok = [], []
    for b, dup in ((0, 0), (1, -1), (nblk - 1, 2)):
        r = NA_ROWS * b + rq
        r0 = jnp.clip(r - wr // 2, 0, rows - wr)
        rkey = NA_ROWS * (b - 1 + slot) + rk
        ok = (rkey >= r0) & (rkey < r0 + wr) & (slot != dup)
        drow = jnp.broadcast_to(rkey - r + (WIN_R - 1), ok.shape)
        row_sel.append((drow[..., None] == jnp.arange(2 * WIN_R - 1)) & ok[..., None])
        row_ok.append(ok)
    row_sel = jnp.stack(row_sel).astype(F32)
    row_ok = jnp.stack(row_ok)
    jq = jnp.arange(GRID_W)[:, None]
    jk = jnp.arange(GRID_W)[None, :]
    c0 = jnp.clip(jq - WIN_C // 2, 0, GRID_W - WIN_C)
    col_ok = (jk >= c0) & (jk < c0 + WIN_C)
    dcol = jnp.clip(jk - jq + (WIN_C - 1), 0, 2 * WIN_C - 2)
    col_sel = (dcol[..., None] == jnp.arange(2 * WIN_C - 1)).astype(F32)
    bias = jnp.einsum('vasbr,hrc,qkc->hvaqsbk', row_sel, rpb.astype(F32), col_sel,
                      precision=lax.Precision.HIGHEST)
    valid = row_ok[None, :, :, None, :, :, None] & col_ok[None, None, None, :, None, None, :]
    tab = jnp.where(valid, bias, NEG_INF)
    return tab.reshape(nh, 3, NA_ROWS * GRID_W, 3 * NA_ROWS * GRID_W)


def _natten_kernel(tab_ref, q_ref, k0_ref, k1_ref, k2_ref, v0_ref, v1_ref, v2_ref, kc_ref, vc_ref, o_ref):
    q = q_ref[...]
    nt = (((1,), (1,)), ((), ()))
    blk = q.shape[0]
    s_loc = [lax.dot_general(q, k_ref[...], nt, preferred_element_type=F32)
             + tab_ref[:, i * blk:(i + 1) * blk] for i, k_ref in enumerate((k0_ref, k1_ref, k2_ref))]
    s_ctx = lax.dot_general(q, kc_ref[...], nt, preferred_element_type=F32)
    m = jnp.max(s_ctx, axis=-1, keepdims=True)
    for s in s_loc:
        m = jnp.maximum(m, jnp.max(s, axis=-1, keepdims=True))
    p = jnp.exp(s_ctx - m)
    l = jnp.sum(p, axis=-1, keepdims=True)
    acc = jnp.dot(p.astype(BF16), vc_ref[...], preferred_element_type=F32)
    for s, v_ref in zip(s_loc, (v0_ref, v1_ref, v2_ref)):
        p = jnp.exp(s - m)
        l = l + jnp.sum(p, axis=-1, keepdims=True)
        acc = acc + jnp.dot(p.astype(BF16), v_ref[...], preferred_element_type=F32)
    o_ref[...] = (acc / l).astype(o_ref.dtype)


def _natten(qkv, tabs, n_lat, n_ctx):
    t = qkv.shape[0]
    blk = NA_ROWS * GRID_W
    nblk = n_lat // blk
    assert nblk >= 3 and blk == n_ctx
    qo, ko, vo = 3 * A_HEADS, 3 * A_HEADS + B_HEADS, 3 * A_HEADS + 2 * B_HEADS
    ctx_blk = n_lat // n_ctx

    def kv_spec(off, shift):
        return pl.BlockSpec((blk, HEAD_W), lambda h, b: (jnp.clip(b + shift, 0, nblk - 1), off + h))

    return pl.pallas_call(
        _natten_kernel,
        grid=(B_HEADS, nblk),
        in_specs=[pl.BlockSpec((None, None, blk, 3 * blk),
                               lambda h, b: (h, jnp.where(b == 0, 0, jnp.where(b == nblk - 1, 2, 1)), 0, 0)),
                  pl.BlockSpec((blk, HEAD_W), lambda h, b: (b, qo + h)),
                  kv_spec(ko, -1), kv_spec(ko, 0), kv_spec(ko, 1),
                  kv_spec(vo, -1), kv_spec(vo, 0), kv_spec(vo, 1),
                  pl.BlockSpec((n_ctx, HEAD_W), lambda h, b: (ctx_blk, ko + h)),
                  pl.BlockSpec((n_ctx, HEAD_W), lambda h, b: (ctx_blk, vo + h))],
        out_specs=pl.BlockSpec((blk, HEAD_W), lambda h, b: (b, h)),
        out_shape=jax.ShapeDtypeStruct((t, B_HEADS * HEAD_W), BF16),
        compiler_params=_cparams(("arbitrary", "arbitrary")),
        name="natten",
    )(tabs, qkv, qkv, qkv, qkv, qkv, qkv, qkv, qkv, qkv)


def _ctx_attn_kernel(q_ref, k_ref, v_ref, prev_ref, o_ref):
    del prev_ref
    _, l, acc = _softmax_block(q_ref[...], k_ref[...], v_ref[...])
    o_ref[...] = (acc / l).astype(o_ref.dtype)


def _ctx_attn(qkv, prev, n_lat, n_ctx):
    qo, ko, vo = 3 * A_HEADS, 3 * A_HEADS + B_HEADS, 3 * A_HEADS + 2 * B_HEADS
    ctx_blk = n_lat // n_ctx
    return pl.pallas_call(
        _ctx_attn_kernel,
        grid=(B_HEADS,),
        in_specs=[pl.BlockSpec((n_ctx, HEAD_W), lambda h: (ctx_blk, qo + h)),
                  pl.BlockSpec((n_ctx, HEAD_W), lambda h: (ctx_blk, ko + h)),
                  pl.BlockSpec((n_ctx, HEAD_W), lambda h: (ctx_blk, vo + h)),
                  pl.BlockSpec(memory_space=pl.ANY)],
        out_specs=pl.BlockSpec((n_ctx, HEAD_W), lambda h: (ctx_blk, h)),
        out_shape=jax.ShapeDtypeStruct(prev.shape, prev.dtype),
        input_output_aliases={3: 0},
        compiler_params=_cparams(("arbitrary",)),
        name="ctx_attn",
    )(qkv, qkv, qkv, prev)


def _s5_tables(lam_re, lam_im, b_re, b_im, c_re, c_im, log_step, d_skip):
    hp = lax.Precision.HIGHEST
    ln = S5_CHUNK
    g, p, c = S5_GROUPS, S5_STATE, S5_GROUP
    lr, li = lam_re.astype(F32), lam_im.astype(F32)
    dt = jnp.exp(log_step.astype(F32))[:, :, None]
    mag = jnp.exp(lr * dt)
    ar, ai = mag * jnp.cos(li * dt), mag * jnp.sin(li * dt)
    den = lr * lr + li * li
    nr, ni = ar - 1.0, ai
    fr = (nr * lr + ni * li) / den
    fi = (ni * lr - nr * li) / den
    br_, bi_ = b_re.astype(F32), b_im.astype(F32)
    bbr = fr[..., None] * br_ - fi[..., None] * bi_
    bbi = fr[..., None] * bi_ + fi[..., None] * br_
    lag = jnp.arange(ln + 1, dtype=F32)[:, None, None, None]
    magl = jnp.exp(lr * dt * lag)
    pr, pi_ = magl * jnp.cos(li * dt * lag), magl * jnp.sin(li * dt * lag)
    wr = pr[..., None] * bbr - pi_[..., None] * bbi
    wi = pr[..., None] * bbi + pi_[..., None] * bbr
    cr, ci = c_re.astype(F32), c_im.astype(F32)
    kern = (jnp.einsum('dgcp,ldgpe->ldgce', cr, wr[:ln], precision=hp)
            - jnp.einsum('dgcp,ldgpe->ldgce', ci, wi[:ln], precision=hp))
    s_idx = jnp.arange(ln)[:, None]
    t_idx = jnp.arange(ln)[None, :]
    lag_f = t_idx - s_idx
    kf = kern[jnp.clip(lag_f, 0, ln - 1), 0] * (lag_f >= 0)[:, :, None, None, None]
    kb = kern[jnp.clip(-lag_f, 0, ln - 1), 1] * (lag_f <= 0)[:, :, None, None, None]
    m_tab = (kf + kb).transpose(2, 0, 4, 1, 3).reshape(g, ln * c, ln * c)
    wf_r, wf_i = wr[:ln, 0][::-1], wi[:ln, 0][::-1]
    wb_r, wb_i = wr[:ln, 1], wi[:ln, 1]

    def inj(w):
        return w.transpose(1, 0, 3, 2).reshape(g, ln * c, p)

    b_tab = jnp.concatenate([inj(wf_r), inj(wf_i), inj(wb_r), inj(wb_i)], axis=-1)
    pf_r, pf_i = pr[1:ln + 1, 0], pi_[1:ln + 1, 0]
    pb_r, pb_i = pr[1:ln + 1, 1][::-1], pi_[1:ln + 1, 1][::-1]

    def rd(pw_r, pw_i, cre, cim):
        car = cre[None] * pw_r[:, :, None, :] - cim[None] * pw_i[:, :, None, :]
        cai = cre[None] * pw_i[:, :, None, :] + cim[None] * pw_r[:, :, None, :]
        to_rows = lambda a: a.transpose(1, 3, 0, 2).reshape(g, p, ln * c)
        return to_rows(car), to_rows(-cai)

    c_tab = jnp.concatenate(rd(pf_r, pf_i, cr[0], ci[0]) + rd(pb_r, pb_i, cr[1], ci[1]), axis=1)
    a_chunk = jnp.stack([jnp.stack([pr[ln, 0], pi_[ln, 0]]), jnp.stack([pr[ln, 1], pi_[ln, 1]])])
    d_tab = jnp.tile(d_skip.astype(F32).reshape(g, 1, c), (1, ln, 1)).reshape(g, 1, ln * c)
    return m_tab.astype(BF16), b_tab.astype(BF16), c_tab.astype(BF16), a_chunk, d_tab


def _s5_inject_kernel(u_ref, b_ref, o_ref):
    o_ref[...] = jnp.dot(u_ref[...], b_ref[...], preferred_element_type=F32)


def _s5_scan_kernel(a_ref, e_ref, o_ref, s_scr, *, nb):
    d = pl.program_id(0)

    @pl.when(pl.program_id(1) == 0)
    def _():
        s_scr[...] = jnp.zeros_like(s_scr)

    ar, ai = a_ref[0], a_ref[1]

    def body(i, carry):
        sr, si = carry
        c = jnp.where(d == 0, i, nb - 1 - i)
        o_ref[0, c] = sr
        o_ref[1, c] = si
        return ar * sr - ai * si + e_ref[0, c], ar * si + ai * sr + e_ref[1, c]

    sr, si = lax.fori_loop(0, nb, body, (s_scr[0], s_scr[1]))
    s_scr[0] = sr
    s_scr[1] = si


def _gelu_tanh(x):
    return 0.5 * x * (1.0 + jnp.tanh(math.sqrt(2.0 / math.pi) * (x + 0.044715 * (x * x * x))))


def _s5_readout_kernel(u_ref, m_ref, s_ref, c_ref, d_ref, o_ref):
    u = u_ref[...]
    y = jnp.dot(u, m_ref[...], preferred_element_type=F32) + u.astype(F32) * d_ref[...]
    for k in range(4):
        y = y + jnp.dot(s_ref[k].astype(BF16), c_ref[k * S5_STATE:(k + 1) * S5_STATE, :],
                        preferred_element_type=F32)
    o_ref[...] = _gelu_tanh(y)


def _s5_glu_kernel(z_ref, w_ref, o_ref):
    z = z_ref[...]
    o_ref[...] = (z * jax.nn.sigmoid(jnp.dot(z.astype(BF16), w_ref[...], preferred_element_type=F32))
                  ).astype(o_ref.dtype)


def _s5_mixer(u, tables, w_glu, n_lat, n_ctx):
    m_tab, b_tab, c_tab, a_chunk, d_tab = tables
    t = u.shape[0]
    g, c, ln, p = S5_GROUPS, S5_GROUP, S5_CHUNK, S5_STATE
    nc = t // ln
    w = ln * c
    ug = u.reshape(nc, ln, g, c).transpose(2, 0, 1, 3).reshape(g, nc, w)

    e = pl.pallas_call(
        _s5_inject_kernel,
        grid=(g,),
        in_specs=[pl.BlockSpec((None, nc, w), lambda i: (i, 0, 0)),
                  pl.BlockSpec((None, w, 4 * p), lambda i: (i, 0, 0))],
        out_specs=pl.BlockSpec((None, nc, 4 * p), lambda i: (i, 0, 0)),
        out_shape=jax.ShapeDtypeStruct((g, nc, 4 * p), F32),
        compiler_params=_cparams(("arbitrary",)),
        name="s5_inject",
    )(ug, b_tab)
    e = e.reshape(g, nc, 2, 2, p).transpose(2, 3, 1, 0, 4)

    nb = S5_BLOCK
    assert (n_lat // ln) % nb == 0 and n_ctx // ln == nb
    lat_blocks = n_lat // ln // nb

    def blk(d, s):
        return jnp.where(s == 0, lat_blocks, jnp.where(d == 0, s - 1, lat_blocks - s))

    s_in = pl.pallas_call(
        functools.partial(_s5_scan_kernel, nb=nb),
        grid=(2, lat_blocks + 1),
        in_specs=[pl.BlockSpec((None, 2, g, p), lambda d, s: (d, 0, 0, 0)),
                  pl.BlockSpec((None, 2, nb, g, p), lambda d, s: (d, 0, blk(d, s), 0, 0))],
        out_specs=pl.BlockSpec((None, 2, nb, g, p), lambda d, s: (d, 0, blk(d, s), 0, 0)),
        out_shape=jax.ShapeDtypeStruct((2, 2, nc, g, p), F32),
        scratch_shapes=[pltpu.VMEM((2, g, p), F32)],
        compiler_params=_cparams(("arbitrary", "arbitrary")),
        name="s5_scan",
    )(a_chunk, e)
    s_in = s_in.transpose(3, 0, 1, 2, 4).reshape(g, 4, nc, p)

    z = pl.pallas_call(
        _s5_readout_kernel,
        grid=(g,),
        in_specs=[pl.BlockSpec((None, nc, w), lambda i: (i, 0, 0)),
                  pl.BlockSpec((None, w, w), lambda i: (i, 0, 0)),
                  pl.BlockSpec((None, 4, nc, p), lambda i: (i, 0, 0, 0)),
                  pl.BlockSpec((None, 4 * p, w), lambda i: (i, 0, 0)),
                  pl.BlockSpec((None, 1, w), lambda i: (i, 0, 0))],
        out_specs=pl.BlockSpec((None, nc, w), lambda i: (i, 0, 0)),
        out_shape=jax.ShapeDtypeStruct((g, nc, w), F32),
        compiler_params=_cparams(("arbitrary",)),
        name="s5_readout",
    )(ug, m_tab, s_in, c_tab, d_tab)
    z = z.reshape(g, nc, ln, c).transpose(1, 2, 0, 3).reshape(t, g * c)

    tm = _row_tile(t, (1280, 640, 256, 128))
    return pl.pallas_call(
        _s5_glu_kernel,
        grid=(t // tm,),
        in_specs=[pl.BlockSpec((tm, g * c), lambda i: (i, 0)),
                  pl.BlockSpec((g * c, g * c), lambda i: (0, 0))],
        out_specs=pl.BlockSpec((tm, g * c), lambda i: (i, 0)),
        out_shape=jax.ShapeDtypeStruct((t, g * c), BF16),
        compiler_params=_cparams(("arbitrary",)),
        name="s5_glu",
    )(z, w_glu)


def _ret_tables(decay_logit):
    scale = HEAD_W ** -0.5
    lg = jax.nn.log_sigmoid(decay_logit.astype(F32))
    lf, lb = lg[0][:, None, None], lg[1][:, None, None]
    i = jnp.arange(RET_CHUNK, dtype=F32)[None, :, None]
    j = jnp.arange(RET_CHUNK, dtype=F32)[None, None, :]
    diff = i - j
    intra = (jnp.where(diff >= 0, jnp.exp(lf * jnp.maximum(diff, 0.0)), 0.0)
             + jnp.where(diff <= 0, jnp.exp(lb * jnp.maximum(-diff, 0.0)), 0.0)) * scale
    ones = jnp.ones((1, 1, HEAD_W), F32)
    q_f = jnp.exp(lf * (i + 1.0)) * ones
    k_f = jnp.exp(lf * (RET_CHUNK - 1.0 - i)) * scale * ones
    c_f = jnp.exp(lf * RET_CHUNK) * jnp.ones((1, RET_CHUNK, HEAD_W), F32)
    q_b = jnp.exp(lb * (RET_CHUNK - i)) * ones
    k_b = jnp.exp(lb * i) * scale * ones
    c_b = jnp.exp(lb * RET_CHUNK) * jnp.ones((1, RET_CHUNK, HEAD_W), F32)
    return intra, jnp.stack([q_f, k_f, c_f]), jnp.stack([q_b, k_b, c_b])


def _ret_state_step(q, k, v, dec_ref, s_scr, h):
    s = s_scr[h]
    qd = (q.astype(F32) * dec_ref[0, h]).astype(BF16)
    kd = (k.astype(F32) * dec_ref[1, h]).astype(BF16)
    o = jnp.dot(qd, s.astype(BF16), preferred_element_type=F32)
    s_scr[h] = s * dec_ref[2, h] + lax.dot_general(kd, v, (((0,), (0,)), ((), ())),
                                                    preferred_element_type=F32)
    return o


def _ret_fwd_kernel(intra_ref, dec_ref, q_ref, k_ref, v_ref, o_ref, s_scr):
    @pl.when(pl.program_id(0) == 0)
    def _():
        s_scr[...] = jnp.zeros_like(s_scr)

    for h in range(RET_HEADS):
        sl = slice(h * HEAD_W, (h + 1) * HEAD_W)
        q, k, v = q_ref[:, sl], k_ref[:, sl], v_ref[:, sl]
        att = lax.dot_general(q, k, (((1,), (1,)), ((), ())), preferred_element_type=F32) * intra_ref[h]
        o = jnp.dot(att.astype(BF16), v, preferred_element_type=F32)
        o_ref[:, sl] = o + _ret_state_step(q, k, v, dec_ref, s_scr, h)


def _ret_bwd_kernel(dec_ref, q_ref, k_ref, v_ref, g_ref, o1_ref, o_ref, s_scr):
    @pl.when(pl.program_id(0) == 0)
    def _():
        s_scr[...] = jnp.zeros_like(s_scr)

    for h in range(RET_HEADS):
        sl = slice(h * HEAD_W, (h + 1) * HEAD_W)
        q, k, v = q_ref[:, sl], k_ref[:, sl], v_ref[:, sl]
        o = o1_ref[:, sl] + _ret_state_step(q, k, v, dec_ref, s_scr, h)
        y = o * lax.rsqrt(jnp.mean(o * o, axis=-1, keepdims=True) + EPS)
        gate = g_ref[:, sl].astype(F32)
        o_ref[:, sl] = (y * (gate * jax.nn.sigmoid(gate))).astype(o_ref.dtype)


def _retention(proj, tables, n_lat, n_ctx):
    intra, dec_f, dec_b = tables
    t = proj.shape[0]
    wd = RET_HEADS * HEAD_W
    ck = RET_CHUNK
    n_lat_c, n_ctx_c = n_lat // ck, n_ctx // ck
    steps = n_lat_c + n_ctx_c

    def fwd_blk(s):
        return jnp.where(s < n_ctx_c, n_lat_c + s, s - n_ctx_c)

    def bwd_blk(s):
        return jnp.where(s < n_ctx_c, n_lat_c + n_ctx_c - 1 - s, n_lat_c - 1 - (s - n_ctx_c))

    def col_spec(col, order):
        return pl.BlockSpec((ck, wd), lambda s: (order(s), col))

    tab_spec = pl.BlockSpec((3, RET_HEADS, ck, HEAD_W), lambda s: (0, 0, 0, 0))
    o1 = pl.pallas_call(
        _ret_fwd_kernel,
        grid=(steps,),
        in_specs=[pl.BlockSpec((RET_HEADS, ck, ck), lambda s: (0, 0, 0)), tab_spec,
                  col_spec(0, fwd_blk), col_spec(1, fwd_blk), col_spec(2, fwd_blk)],
        out_specs=col_spec(0, fwd_blk),
        out_shape=jax.ShapeDtypeStruct((t, wd), F32),
        scratch_shapes=[pltpu.VMEM((RET_HEADS, HEAD_W, HEAD_W), F32)],
        compiler_params=_cparams(("arbitrary",)),
        name="retention_fwd",
    )(intra, dec_f, proj, proj, proj)
    return pl.pallas_call(
        _ret_bwd_kernel,
        grid=(steps,),
        in_specs=[tab_spec, col_spec(0, bwd_blk), col_spec(1, bwd_blk), col_spec(2, bwd_blk),
                  col_spec(3, bwd_blk), col_spec(0, bwd_blk)],
        out_specs=col_spec(0, bwd_blk),
        out_shape=jax.ShapeDtypeStruct((t, wd), BF16),
        scratch_shapes=[pltpu.VMEM((RET_HEADS, HEAD_W, HEAD_W), F32)],
        compiler_params=_cparams(("arbitrary",)),
        name="retention_bwd",
    )(dec_b, proj, proj, proj, proj, o1)


def _rope_tables(n_lat, n_ctx):
    n_freq = A_QK_DIM // 4
    freq = ROPE_BASE ** (-jnp.arange(n_freq, dtype=F32) / n_freq)
    tok = jnp.arange(n_lat)
    row = (tok // GRID_W).astype(F32)
    col = (tok % GRID_W).astype(F32)
    ang = jnp.stack([row[:, None] * freq, col[:, None] * freq], axis=1)
    ang = jnp.broadcast_to(ang[:, :, None, :], (n_lat, 2, 2, n_freq)).reshape(n_lat, A_QK_DIM)
    reps = HEAD_W // A_QK_DIM
    cos = jnp.concatenate([jnp.tile(jnp.cos(ang), (1, reps)), jnp.ones((n_ctx, HEAD_W), F32)], axis=0)
    sin = jnp.concatenate([jnp.tile(jnp.sin(ang), (1, reps)), jnp.zeros((n_ctx, HEAD_W), F32)], axis=0)
    return cos, sin


def kernel(x, c, ctx, c_ctx, ada_w, ada_b, norm1_w, norm2_w, ffn_w13, ffn_w2, e_w_in, e_w_out, diff_lq1, diff_lk1, diff_lq2, diff_lk2, diff_subln_w, na_rpb, o_w_in, o_w_out, s5_lam_re, s5_lam_im, s5_b_re, s5_b_im, s5_c_re, s5_c_im, s5_log_step, s5_d, s5_w_glu, ret_decay_logit, final_norm_w):
    bsz, n_lat, d = x.shape
    n_ctx = ctx.shape[1]
    depth = ada_w.shape[0]
    assert bsz == 1
    xs = jnp.concatenate([x[0], ctx[0]], axis=0)
    mods_all = _mods(c, c_ctx, ada_w, ada_b)
    cos, sin = _rope_tables(n_lat, n_ctx)
    ret_w = RET_HEADS * HEAD_W
    ffn_w13_bf16 = ffn_w13.astype(BF16)

    for i in range(depth):
        compute_ctx = i != depth - 1
        mods = mods_all[i]
        j = i // 2
        h = _normmod(xs, norm1_w[i], mods, n_lat, 0, 1)
        if i % 2 == 0:
            lambda_init = 0.8 - 0.6 * math.exp(-0.3 * i)
            qkv = _inproj_even(h, e_w_in, j, cos, sin)
            lam_args = (diff_lq1[j], diff_lk1[j], diff_lq2[j], diff_lk2[j], diff_subln_w[j])
            o_a = _diff_attn(qkv, *lam_args, n_lat, n_ctx, lambda_init, tq_max=1024, pv_rows=512,
                             exp_dtype=F32 if j == 0 else BF16)
            o_b = _natten(qkv, _na_bias_tables(na_rpb[j], n_lat // GRID_W), n_lat, n_ctx)
            if compute_ctx:
                o_a = _diff_attn(qkv, *lam_args, n_lat, n_ctx, lambda_init, prev=o_a)
                o_b = _ctx_attn(qkv, o_b, n_lat, n_ctx)
            xs = _gated_residual([o_a, o_b], e_w_out, xs, mods, n_lat, 2, "outproj_even", layer=j)
        else:
            w_in = o_w_in[j]
            w_in = jnp.concatenate([w_in[:, S5_CH:], w_in[:, :S5_CH]], axis=1).astype(BF16)
            proj = _matmul(h, w_in, 512, "inproj_odd")
            s5_tabs = _s5_tables(s5_lam_re[j], s5_lam_im[j], s5_b_re[j], s5_b_im[j], s5_c_re[j], s5_c_im[j],
                                 s5_log_step[j], s5_d[j])
            y_c = _s5_mixer(proj[:, 4 * ret_w:], s5_tabs, s5_w_glu[j].astype(BF16), n_lat, n_ctx)
            y_d = _retention(proj, _ret_tables(ret_decay_logit[j]), n_lat, n_ctx)
            xs = _gated_residual([y_c, y_d], o_w_out, xs, mods, n_lat, 2, "outproj_odd", layer=j)
        h = _normmod(xs, norm2_w[i], mods, n_lat, 3, 4)
        a = _ffn_up(h, ffn_w13_bf16, i)
        xs = _gated_residual([a], ffn_w2[i].astype(BF16), xs, mods, n_lat, 5, "ffn_down")
    return _final_norm(xs, final_norm_w, n_lat)[None]
```

```python
import functools
import math

import jax
import jax.numpy as jnp
from jax import lax
from jax.experimental import pallas as pl
from jax.experimental.pallas import tpu as pltpu

F32 = jnp.float32
BF16 = jnp.bfloat16

GRID_W = 64
A_HEADS = 8
A_QK_DIM = 64
HEAD_W = 128
B_HEADS = 8
WIN_R = 8
WIN_C = 16
NA_ROWS = 4
S5_CH = 512
S5_GROUP = 16
S5_GROUPS = 32
S5_STATE = 64
S5_CHUNK = 16
S5_BLOCK = 16
RET_HEADS = 12
RET_CHUNK = 128
ROPE_BASE = 10000.0
EPS = 1e-6
NEG_INF = -1e30
VMEM_LIMIT = 56 * 1024 * 1024
N_MOD = 8
NORM_ROW_CHUNK = 128


def _cparams(sem):
    return pltpu.CompilerParams(dimension_semantics=sem, vmem_limit_bytes=VMEM_LIMIT)


def _row_tile(t, candidates):
    for c in candidates:
        if t % c == 0:
            return c
    raise ValueError(f"no row tile for {t}")


def _pick_mod(mods_ref, idx, is_ctx):
    return jnp.where(is_ctx, mods_ref[1, idx:idx + 1, :], mods_ref[0, idx:idx + 1, :])


def _is_ctx_rows(tm, n_lat, axis):
    row = pl.program_id(axis) * tm + lax.broadcasted_iota(jnp.int32, (tm, 1), 0)
    return row >= n_lat


def _mods_kernel(s_ref, w_ref, b_ref, o_ref):
    s = s_ref[...]
    s = s * jax.nn.sigmoid(s)
    o_ref[...] = jnp.dot(s, w_ref[...], preferred_element_type=F32,
                         precision=lax.Precision.HIGHEST) + b_ref[...]


def _mods(c, c_ctx, ada_w, ada_b):
    depth, d, w6 = ada_w.shape
    s = jnp.zeros((8, d), F32).at[0].set(c[0]).at[1].set(c_ctx)
    tn = 1024
    out = pl.pallas_call(
        _mods_kernel,
        grid=(depth, w6 // tn),
        in_specs=[pl.BlockSpec((8, d), lambda l, j: (0, 0)),
                  pl.BlockSpec((None, d, tn), lambda l, j: (l, 0, j)),
                  pl.BlockSpec((None, 1, tn), lambda l, j: (l, 0, j))],
        out_specs=pl.BlockSpec((None, 8, tn), lambda l, j: (l, 0, j)),
        out_shape=jax.ShapeDtypeStruct((depth, 8, w6), F32),
        compiler_params=_cparams(("arbitrary", "arbitrary")),
        name="ada_mods",
    )(s, ada_w, ada_b.reshape(depth, 1, w6))
    m = out[:, :2].reshape(depth, 2, 6, d)
    return jnp.pad(m, ((0, 0), (0, 0), (0, N_MOD - 6), (0, 0)))


def _normmod_prologue(x_ref, nw_ref, mods_ref, h_scr, *, n_lat, tm, shift_idx, scale_idx):
    @pl.when(pl.program_id(1) == 0)
    def _():
        rc = NORM_ROW_CHUNK
        assert tm % rc == 0 and n_lat % rc == 0

        def chunk(c, carry):
            r0 = pl.multiple_of(c * rc, rc)
            seg = (pl.program_id(0) * tm + r0 >= n_lat).astype(jnp.int32)
            gain = nw_ref[...] * (1.0 + mods_ref[seg, scale_idx:scale_idx + 1, :])
            shift = mods_ref[seg, shift_idx:shift_idx + 1, :]
            x = x_ref[pl.ds(r0, rc), :]
            r = lax.rsqrt(jnp.mean(x * x, axis=-1, keepdims=True) + EPS)
            h_scr[pl.ds(r0, rc), :] = (x * r * gain + shift).astype(h_scr.dtype)
            return carry

        lax.fori_loop(0, tm // rc, chunk, 0)


def _norm_proj_call(kern, x, norm_w, mods, n_lat, shift_idx, scale_idx, w_args, w_specs, extra_args, extra_specs,
                    n_out, tn, tm, name, **kern_kw):
    t, d = x.shape
    norm = dict(n_lat=n_lat, tm=tm, shift_idx=shift_idx, scale_idx=scale_idx)
    return pl.pallas_call(
        functools.partial(kern, norm=norm, **kern_kw),
        grid=(t // tm, n_out // tn),
        in_specs=[pl.BlockSpec((tm, d), lambda i, j: (i, 0)),
                  pl.BlockSpec((1, d), lambda i, j: (0, 0)),
                  pl.BlockSpec((2, N_MOD, d), lambda i, j: (0, 0, 0))] + w_specs + extra_specs,
        out_specs=pl.BlockSpec((tm, tn), lambda i, j: (i, j)),
        out_shape=jax.ShapeDtypeStruct((t, n_out), BF16),
        scratch_shapes=[pltpu.VMEM((tm, d), BF16)],
        compiler_params=_cparams(("arbitrary", "arbitrary")),
        name=name,
    )(x, norm_w.reshape(1, d), mods, *w_args, *extra_args)


def _final_norm_kernel(x_ref, w_ref, o_ref):
    x = x_ref[...]
    o_ref[...] = x * lax.rsqrt(jnp.mean(x * x, axis=-1, keepdims=True) + EPS) * w_ref[...]


def _final_norm(x, w, n_lat):
    d = x.shape[1]
    tm = _row_tile(n_lat, (512, 256, 128))
    return pl.pallas_call(
        _final_norm_kernel,
        grid=(n_lat // tm,),
        in_specs=[pl.BlockSpec((tm, d), lambda i: (i, 0)),
                  pl.BlockSpec((1, d), lambda i: (0, 0))],
        out_specs=pl.BlockSpec((tm, d), lambda i: (i, 0)),
        out_shape=jax.ShapeDtypeStruct((n_lat, d), F32),
        compiler_params=_cparams(("arbitrary",)),
        name="final_norm",
    )(x, w.reshape(1, d))


def _rope_store(acc, cos, sin, o_ref, scale):
    first_half = (lax.broadcasted_iota(jnp.int32, (1, HEAD_W), 1) % 32) < 16
    for c in range(acc.shape[1] // HEAD_W):
        x = acc[:, c * HEAD_W:(c + 1) * HEAD_W]
        rot = jnp.where(first_half, -pltpu.roll(x, HEAD_W - 16, 1), pltpu.roll(x, 16, 1))
        o_ref[:, c * HEAD_W:(c + 1) * HEAD_W] = ((x * cos + rot * sin) * scale).astype(o_ref.dtype)


def _resident_bf16(w_ref, w_scr):
    @pl.when(pl.program_id(1) == 0)
    def _():
        w_scr[...] = w_ref[...].astype(BF16)

    return w_scr


def _inproj_even_kernel(x_ref, nw_ref, mods_ref, w_ref, cos_ref, sin_ref, o_ref, h_scr, *, norm, a_scale, b_scale):
    _normmod_prologue(x_ref, nw_ref, mods_ref, h_scr, **norm)
    j = pl.program_id(1)
    acc = jnp.dot(h_scr[...], w_ref[...], preferred_element_type=F32)

    @pl.when(j == 0)
    def _():
        _rope_store(acc, cos_ref[...], sin_ref[...], o_ref, a_scale)

    @pl.when(j == 1)
    def _():
        _rope_store(acc, cos_ref[...], sin_ref[...], o_ref, 1.0)

    @pl.when(j == 3)
    def _():
        o_ref[...] = (acc * b_scale).astype(o_ref.dtype)

    @pl.when((j == 2) | (j > 3))
    def _():
        o_ref[...] = acc.astype(o_ref.dtype)


def _inproj_even(x, norm_w, mods, n_lat, w, layer, cos, sin):
    t, d = x.shape
    n = w.shape[2]
    tn = n // 6
    tm = _row_tile(t, (1280, 640, 256, 128))
    rope_spec = pl.BlockSpec((tm, HEAD_W), lambda i, j: (i, 0))
    return _norm_proj_call(
        _inproj_even_kernel, x, norm_w, mods, n_lat, 0, 1,
        [w], [pl.BlockSpec((None, d, tn), lambda i, j: (layer, 0, j))],
        [cos, sin], [rope_spec, rope_spec], n, tn, tm, "inproj_even",
        a_scale=A_QK_DIM ** -0.5 * math.log2(math.e), b_scale=HEAD_W ** -0.5)


def _inproj_odd_kernel(x_ref, nw_ref, mods_ref, w_ref, o_ref, h_scr, *, norm):
    _normmod_prologue(x_ref, nw_ref, mods_ref, h_scr, **norm)
    o_ref[...] = jnp.dot(h_scr[...], w_ref[...], preferred_element_type=F32).astype(o_ref.dtype)


def _inproj_odd(x, norm_w, mods, n_lat, w):
    t, d = x.shape
    n = w.shape[1]
    tn = 512
    tm = _row_tile(t, (1280, 640, 256, 128))
    return _norm_proj_call(
        _inproj_odd_kernel, x, norm_w, mods, n_lat, 0, 1,
        [w], [pl.BlockSpec((d, tn), lambda i, j: (0, j))], [], [], n, tn, tm, "inproj_odd")


def _ffn_up_kernel(x_ref, nw_ref, mods_ref, w1_ref, w3_ref, o_ref, h_scr, *, norm):
    _normmod_prologue(x_ref, nw_ref, mods_ref, h_scr, **norm)
    h = h_scr[...]
    a = jnp.dot(h, w1_ref[...], preferred_element_type=F32)
    b = jnp.dot(h, w3_ref[...], preferred_element_type=F32)
    o_ref[...] = (a * jax.nn.sigmoid(a) * b).astype(o_ref.dtype)


def _ffn_up(x, norm_w, mods, n_lat, w13, layer):
    t, d = x.shape
    d_ff = w13.shape[2] // 2
    tn = 512
    nj = d_ff // tn
    tm = _row_tile(t, (1280, 640, 256, 128))
    return _norm_proj_call(
        _ffn_up_kernel, x, norm_w, mods, n_lat, 3, 4,
        [w13, w13], [pl.BlockSpec((None, d, tn), lambda i, j: (layer, 0, j)),
                     pl.BlockSpec((None, d, tn), lambda i, j: (layer, 0, j + nj))],
        [], [], d_ff, tn, tm, "ffn_up")


def _gated_residual_kernel(*refs, n_a, n_lat, tm, gate_idx):
    a_refs = refs[:n_a]
    w_ref, x_ref, mods_ref, o_ref = refs[n_a:n_a + 4]
    if w_ref.dtype != BF16:
        w_ref = _resident_bf16(w_ref, refs[n_a + 4])
    k0 = 0
    y = None
    for a_ref in a_refs:
        kk = a_ref.shape[1]
        part = jnp.dot(a_ref[...], w_ref[k0:k0 + kk, :], preferred_element_type=F32)
        y = part if y is None else y + part
        k0 += kk
    gate = _pick_mod(mods_ref, gate_idx, _is_ctx_rows(tm, n_lat, 1))
    o_ref[...] = x_ref[...] + gate * y


def _gated_residual(a_list, w, x, mods, n_lat, gate_idx, name, layer=None):
    t, d = x.shape
    k = w.shape[-2]
    tn = 512
    tm = _row_tile(t, (640, 256, 128))
    n_a = len(a_list)
    in_specs = [pl.BlockSpec((tm, a.shape[1]), lambda j, i: (i, 0)) for a in a_list]
    w_spec = (pl.BlockSpec((k, tn), lambda j, i: (0, j)) if layer is None
              else pl.BlockSpec((None, k, tn), lambda j, i: (layer, 0, j)))
    in_specs += [w_spec,
                 pl.BlockSpec((tm, tn), lambda j, i: (i, j)),
                 pl.BlockSpec((2, N_MOD, tn), lambda j, i: (0, 0, j))]
    return pl.pallas_call(
        functools.partial(_gated_residual_kernel, n_a=n_a, n_lat=n_lat, tm=tm, gate_idx=gate_idx),
        grid=(d // tn, t // tm),
        in_specs=in_specs,
        out_specs=pl.BlockSpec((tm, tn), lambda j, i: (i, j)),
        out_shape=jax.ShapeDtypeStruct((t, d), F32),
        scratch_shapes=[] if w.dtype == BF16 else [pltpu.VMEM((k, tn), BF16)],
        input_output_aliases={n_a + 1: 0},
        compiler_params=_cparams(("arbitrary", "arbitrary")),
        name=name,
    )(*a_list, w, x, mods)


def _softmax_block(qs, k, v):
    s = lax.dot_general(qs, k, (((1,), (1,)), ((), ())), preferred_element_type=F32)
    m = jnp.max(s, axis=-1, keepdims=True)
    p = jnp.exp(s - m)
    l = jnp.sum(p, axis=-1, keepdims=True)
    acc = jnp.dot(p.astype(v.dtype), v, preferred_element_type=F32)
    return m, l, acc


DIFF_PV_ROWS = 512
DIFF_SOFTMAX_ROWS = 64
DIFF_EXP_DTYPE = BF16


def _diff_attn_kernel(*refs, tq, tk, n_ctx, n_chunks, lambda_init):
    if n_chunks > 1:
        (lq1, lk1, lq2, lk2, subw, q_ref, kc_ref, vc_ref, k_ref, v_ref, o_ref,
         qs_scr, sa_scr, mca_scr, p_scr, m_scr, l_scr, alpha_scr, acc_scr, sb_scr, mcb_scr) = refs
        buf_b = (sb_scr, mcb_scr)
    elif n_chunks == 1:
        (lq1, lk1, lq2, lk2, subw, q_ref, kc_ref, vc_ref, k_ref, v_ref, o_ref,
         qs_scr, sa_scr, mca_scr, p_scr, m_scr, l_scr, alpha_scr, acc_scr) = refs
    else:
        (lq1, lk1, lq2, lk2, subw, q_ref, kc_ref, vc_ref, o_ref,
         qs_scr, sa_scr, mca_scr, p_scr, m_scr, l_scr, alpha_scr, acc_scr) = refs
    buf_a = (sa_scr, mca_scr)
    nt = (((1,), (1,)), ((), ()))
    rows_all = 2 * tq
    pv_rows = min(DIFF_PV_ROWS, rows_all)
    sm_rows = min(DIFF_SOFTMAX_ROWS, pv_rows)
    q = q_ref[...]
    comp1 = lax.broadcasted_iota(jnp.int32, (1, HEAD_W), 1) < A_QK_DIM
    zero = jnp.zeros_like(q)
    qs_scr[0:tq, :] = jnp.where(comp1, q, zero)
    qs_scr[tq:2 * tq, :] = jnp.where(comp1, zero, q)

    def scores(buf, rows, keys):
        s_buf, mc_buf = buf
        s = lax.dot_general(qs_scr[rows, :], keys, nt, preferred_element_type=F32)
        width = s.shape[1]
        s_buf[rows, 0:width] = s
        mp = s[:, 0:HEAD_W]
        for c in range(1, width // HEAD_W):
            mp = jnp.maximum(mp, s[:, c * HEAD_W:(c + 1) * HEAD_W])
        mc_buf[rows, :] = jnp.broadcast_to(jnp.max(mp, axis=-1, keepdims=True), mp.shape)

    def softmax_rows(buf, rows, n_tiles, first):
        s_buf, mc_buf = buf
        m_cur = mc_buf[rows, :]
        if first:
            m_new = m_cur
        else:
            m_prev = m_scr[rows, :]
            m_new = jnp.maximum(m_prev, m_cur)
        lp = None
        for c in range(n_tiles):
            pc = jnp.exp2((s_buf[rows, c * HEAD_W:(c + 1) * HEAD_W] - m_new).astype(DIFF_EXP_DTYPE))
            p_scr[rows, c * HEAD_W:(c + 1) * HEAD_W] = pc.astype(BF16)
            lp = pc if lp is None else lp + pc
        l_cur = jnp.sum(lp.astype(F32), axis=-1, keepdims=True)
        if first:
            l_scr[rows, :] = jnp.broadcast_to(l_cur, m_new.shape)
        else:
            alpha = jnp.exp2(m_prev - m_new)
            alpha_scr[rows, :] = alpha
            l_scr[rows, :] = alpha * l_scr[rows, :] + l_cur
        m_scr[rows, :] = m_new

    def softmax_pv(buf, width, load_v, first, after_group=None):
        for r in range(rows_all // pv_rows):
            for r2 in range(pv_rows // sm_rows):
                r0 = r * pv_rows + r2 * sm_rows
                softmax_rows(buf, slice(r0, r0 + sm_rows), width // HEAD_W, first)
            rows = slice(r * pv_rows, (r + 1) * pv_rows)
            pv = jnp.dot(p_scr[rows, 0:width], load_v(), preferred_element_type=F32)
            acc_scr[rows, :] = pv if first else alpha_scr[rows, :] * acc_scr[rows, :] + pv
            if after_group is not None:
                after_group(rows)

    def all_groups(fn):
        for r in range(rows_all // pv_rows):
            fn(slice(r * pv_rows, (r + 1) * pv_rows))

    all_groups(lambda rows: scores(buf_a, rows, kc_ref[...]))
    softmax_pv(buf_a, n_ctx, lambda: vc_ref[...], True)

    def qk_rows(j, buf, rows):
        off = pl.multiple_of(j * tk, tk)
        scores(buf, rows, k_ref[pl.ds(off, tk), :])

    def soft_pv(j, buf, j_next=None, buf_next=None):
        off = pl.multiple_of(j * tk, tk)
        after = None if j_next is None else (lambda rows: qk_rows(j_next, buf_next, rows))
        softmax_pv(buf, tk, lambda: v_ref[pl.ds(off, tk), :], False, after)

    if n_chunks == 1:
        all_groups(lambda rows: qk_rows(0, buf_a, rows))
        soft_pv(0, buf_a)
    elif n_chunks > 1:
        all_groups(lambda rows: qk_rows(0, buf_a, rows))

        def pair(j2, carry):
            soft_pv(2 * j2, buf_a, 2 * j2 + 1, buf_b)
            soft_pv(2 * j2 + 1, buf_b, 2 * j2 + 2, buf_a)
            return carry

        lax.fori_loop(0, n_chunks // 2 - 1, pair, 0)
        soft_pv(n_chunks - 2, buf_a, n_chunks - 1, buf_b)
        soft_pv(n_chunks - 1, buf_b)

    lam = (jnp.exp(jnp.sum(lq1[...] * lk1[...], axis=-1, keepdims=True))
           - jnp.exp(jnp.sum(lq2[...] * lk2[...], axis=-1, keepdims=True)) + lambda_init)
    o1 = acc_scr[0:tq, :] / l_scr[0:tq, :]
    o2 = acc_scr[tq:2 * tq, :] / l_scr[tq:2 * tq, :]
    o = o1 - lam * o2
    y = o * lax.rsqrt(jnp.mean(o * o, axis=-1, keepdims=True) + EPS) * subw[...]
    o_ref[...] = (y * (1.0 - lambda_init)).astype(o_ref.dtype)


def _diff_attn(qkv, lq1, lk1, lq2, lk2, subw, n_lat, n_ctx, lambda_init, prev=None):
    t = qkv.shape[0]
    use_lat = prev is None
    ctx_blk = n_lat // n_ctx
    small = [lq1.reshape(1, -1), lk1.reshape(1, -1), lq2.reshape(1, -1), lk2.reshape(1, -1), subw.reshape(1, -1)]
    small_specs = [pl.BlockSpec(a.shape, lambda h, qi: (0, 0)) for a in small]
    kc_spec = pl.BlockSpec((n_ctx, HEAD_W), lambda h, qi: (ctx_blk, A_HEADS + h))
    vc_spec = pl.BlockSpec((n_ctx, HEAD_W), lambda h, qi: (ctx_blk, 2 * A_HEADS + h))
    if use_lat:
        tq = _row_tile(n_lat, (1024, 512, 256, 128))
        tk = _row_tile(n_lat, (1024, 512, 256, 128))
        n_chunks = n_lat // tk
        assert n_chunks == 1 or n_chunks % 2 == 0
        grid = (A_HEADS, n_lat // tq)
        in_specs = small_specs + [
            pl.BlockSpec((tq, HEAD_W), lambda h, qi: (qi, h)), kc_spec, vc_spec,
            pl.BlockSpec((n_lat, HEAD_W), lambda h, qi: (0, A_HEADS + h), pipeline_mode=pl.Buffered(1)),
            pl.BlockSpec((n_lat, HEAD_W), lambda h, qi: (0, 2 * A_HEADS + h), pipeline_mode=pl.Buffered(1))]
        args = small + [qkv, qkv, qkv, qkv, qkv]
        out_spec = pl.BlockSpec((tq, HEAD_W), lambda h, qi: (qi, h))
        aliases = {}
    else:
        tq, tk, n_chunks = n_ctx, 0, 0
        grid = (A_HEADS, 1)
        in_specs = small_specs + [
            pl.BlockSpec((tq, HEAD_W), lambda h, qi: (ctx_blk, h)), kc_spec, vc_spec,
            pl.BlockSpec(memory_space=pl.ANY)]
        args = small + [qkv, qkv, qkv, prev]
        out_spec = pl.BlockSpec((tq, HEAD_W), lambda h, qi: (ctx_blk, h))
        aliases = {len(args) - 1: 0}

    sw = max(tk, n_ctx)
    kern = functools.partial(_diff_attn_kernel, tq=tq, tk=tk, n_ctx=n_ctx, n_chunks=n_chunks,
                             lambda_init=lambda_init)
    if not use_lat:
        inner = kern

        def kern(*refs):
            n_in = len(args)
            inner(*refs[:n_in - 1], *refs[n_in:])

    return pl.pallas_call(
        kern,
        grid=grid,
        in_specs=in_specs,
        out_specs=out_spec,
        out_shape=jax.ShapeDtypeStruct((t, A_HEADS * HEAD_W), BF16),
        scratch_shapes=[pltpu.VMEM((2 * tq, HEAD_W), BF16),
                        pltpu.VMEM((2 * tq, sw), F32),
                        pltpu.VMEM((2 * tq, HEAD_W), F32),
                        pltpu.VMEM((2 * tq, sw), BF16),
                        pltpu.VMEM((2 * tq, HEAD_W), F32),
                        pltpu.VMEM((2 * tq, HEAD_W), F32),
                        pltpu.VMEM((2 * tq, HEAD_W), F32),
                        pltpu.VMEM((2 * tq, HEAD_W), F32)]
        + ([pltpu.VMEM((2 * tq, sw), F32),
            pltpu.VMEM((2 * tq, HEAD_W), F32)] if n_chunks > 1 else []),
        input_output_aliases=aliases,
        compiler_params=_cparams(("arbitrary", "arbitrary")),
        name="diff_attn" if use_lat else "diff_attn_ctx",
    )(*args)


def _na_bias_tables(rpb, rows):
    nblk = rows // NA_ROWS
    wr = min(WIN_R, rows)
    nh = rpb.shape[0]
    rq = jnp.arange(NA_ROWS)[:, None, None]
    slot = jnp.arange(3)[None, :, None]
    rk = jnp.arange(NA_ROWS)[None, None, :]
    row_sel, row_ok = [], []
    for b, dup in ((0, 0), (1, -1), (nblk - 1, 2)):
        r = NA_ROWS * b + rq
        r0 = jnp.clip(r - wr // 2, 0, rows - wr)
        rkey = NA_ROWS * (b - 1 + slot) + rk
        ok = (rkey >= r0) & (rkey < r0 + wr) & (slot != dup)
        drow = jnp.broadcast_to(rkey - r + (WIN_R - 1), ok.shape)
        row_sel.append((drow[..., None] == jnp.arange(2 * WIN_R - 1)) & ok[..., None])
        row_ok.append(ok)
    row_sel = jnp.stack(row_sel).astype(F32)
    row_ok = jnp.stack(row_ok)
    jq = jnp.arange(GRID_W)[:, None]
    jk = jnp.arange(GRID_W)[None, :]
    c0 = jnp.clip(jq - WIN_C // 2, 0, GRID_W - WIN_C)
    col_ok = (jk >= c0) & (jk < c0 + WIN_C)
    dcol = jnp.clip(jk - jq + (WIN_C - 1), 0, 2 * WIN_C - 2)
    col_sel = (dcol[..., None] == jnp.arange(2 * WIN_C - 1)).astype(F32)
    bias = jnp.einsum('vasbr,hrc,qkc->hvaqsbk', row_sel, rpb.astype(F32), col_sel,
                      precision=lax.Precision.HIGHEST)
    valid = row_ok[None, :, :, None, :, :, None] & col_ok[None, None, None, :, None, None, :]
    tab = jnp.where(valid, bias, NEG_INF)
    return tab.reshape(nh, 3, NA_ROWS * GRID_W, 3 * NA_ROWS * GRID_W)


def _natten_kernel(tab_ref, q_ref, k0_ref, k1_ref, k2_ref, v0_ref, v1_ref, v2_ref, kc_ref, vc_ref, o_ref):
    q = q_ref[...]
    nt = (((1,), (1,)), ((), ()))
    blk = q.shape[0]
    s_loc = [lax.dot_general(q, k_ref[...], nt, preferred_element_type=F32)
             + tab_ref[:, i * blk:(i + 1) * blk] for i, k_ref in enumerate((k0_ref, k1_ref, k2_ref))]
    s_ctx = lax.dot_general(q, kc_ref[...], nt, preferred_element_type=F32)
    m = jnp.max(s_ctx, axis=-1, keepdims=True)
    for s in s_loc:
        m = jnp.maximum(m, jnp.max(s, axis=-1, keepdims=True))
    p = jnp.exp(s_ctx - m)
    l = jnp.sum(p, axis=-1, keepdims=True)
    acc = jnp.dot(p.astype(BF16), vc_ref[...], preferred_element_type=F32)
    for s, v_ref in zip(s_loc, (v0_ref, v1_ref, v2_ref)):
        p = jnp.exp(s - m)
        l = l + jnp.sum(p, axis=-1, keepdims=True)
        acc = acc + jnp.dot(p.astype(BF16), v_ref[...], preferred_element_type=F32)
    o_ref[...] = (acc / l).astype(o_ref.dtype)


def _natten(qkv, tabs, n_lat, n_ctx):
    t = qkv.shape[0]
    blk = NA_ROWS * GRID_W
    nblk = n_lat // blk
    assert nblk >= 3 and blk == n_ctx
    qo, ko, vo = 3 * A_HEADS, 3 * A_HEADS + B_HEADS, 3 * A_HEADS + 2 * B_HEADS
    ctx_blk = n_lat // n_ctx

    def kv_spec(off, shift):
        return pl.BlockSpec((blk, HEAD_W), lambda h, b: (jnp.clip(b + shift, 0, nblk - 1), off + h))

    return pl.pallas_call(
        _natten_kernel,
        grid=(B_HEADS, nblk),
        in_specs=[pl.BlockSpec((None, None, blk, 3 * blk),
                               lambda h, b: (h, jnp.where(b == 0, 0, jnp.where(b == nblk - 1, 2, 1)), 0, 0)),
                  pl.BlockSpec((blk, HEAD_W), lambda h, b: (b, qo + h)),
                  kv_spec(ko, -1), kv_spec(ko, 0), kv_spec(ko, 1),
                  kv_spec(vo, -1), kv_spec(vo, 0), kv_spec(vo, 1),
                  pl.BlockSpec((n_ctx, HEAD_W), lambda h, b: (ctx_blk, ko + h)),
                  pl.BlockSpec((n_ctx, HEAD_W), lambda h, b: (ctx_blk, vo + h))],
        out_specs=pl.BlockSpec((blk, HEAD_W), lambda h, b: (b, h)),
        out_shape=jax.ShapeDtypeStruct((t, B_HEADS * HEAD_W), BF16),
        compiler_params=_cparams(("arbitrary", "arbitrary")),
        name="natten",
    )(tabs, qkv, qkv, qkv, qkv, qkv, qkv, qkv, qkv, qkv)


def _ctx_attn_kernel(q_ref, k_ref, v_ref, prev_ref, o_ref):
    del prev_ref
    _, l, acc = _softmax_block(q_ref[...], k_ref[...], v_ref[...])
    o_ref[...] = (acc / l).astype(o_ref.dtype)


def _ctx_attn(qkv, prev, n_lat, n_ctx):
    qo, ko, vo = 3 * A_HEADS, 3 * A_HEADS + B_HEADS, 3 * A_HEADS + 2 * B_HEADS
    ctx_blk = n_lat // n_ctx
    return pl.pallas_call(
        _ctx_attn_kernel,
        grid=(B_HEADS,),
        in_specs=[pl.BlockSpec((n_ctx, HEAD_W), lambda h: (ctx_blk, qo + h)),
                  pl.BlockSpec((n_ctx, HEAD_W), lambda h: (ctx_blk, ko + h)),
                  pl.BlockSpec((n_ctx, HEAD_W), lambda h: (ctx_blk, vo + h)),
                  pl.BlockSpec(memory_space=pl.ANY)],
        out_specs=pl.BlockSpec((n_ctx, HEAD_W), lambda h: (ctx_blk, h)),
        out_shape=jax.ShapeDtypeStruct(prev.shape, prev.dtype),
        input_output_aliases={3: 0},
        compiler_params=_cparams(("arbitrary",)),
        name="ctx_attn",
    )(qkv, qkv, qkv, prev)


def _s5_tables(lam_re, lam_im, b_re, b_im, c_re, c_im, log_step, d_skip):
    hp = lax.Precision.HIGHEST
    ln = S5_CHUNK
    g, p, c = S5_GROUPS, S5_STATE, S5_GROUP
    lr, li = lam_re.astype(F32), lam_im.astype(F32)
    dt = jnp.exp(log_step.astype(F32))[:, :, None]
    mag = jnp.exp(lr * dt)
    ar, ai = mag * jnp.cos(li * dt), mag * jnp.sin(li * dt)
    den = lr * lr + li * li
    nr, ni = ar - 1.0, ai
    fr = (nr * lr + ni * li) / den
    fi = (ni * lr - nr * li) / den
    br_, bi_ = b_re.astype(F32), b_im.astype(F32)
    bbr = fr[..., None] * br_ - fi[..., None] * bi_
    bbi = fr[..., None] * bi_ + fi[..., None] * br_
    lag = jnp.arange(ln + 1, dtype=F32)[:, None, None, None]
    magl = jnp.exp(lr * dt * lag)
    pr, pi_ = magl * jnp.cos(li * dt * lag), magl * jnp.sin(li * dt * lag)
    wr = pr[..., None] * bbr - pi_[..., None] * bbi
    wi = pr[..., None] * bbi + pi_[..., None] * bbr
    cr, ci = c_re.astype(F32), c_im.astype(F32)
    kern = (jnp.einsum('dgcp,ldgpe->ldgce', cr, wr[:ln], precision=hp)
            - jnp.einsum('dgcp,ldgpe->ldgce', ci, wi[:ln], precision=hp))
    s_idx = jnp.arange(ln)[:, None]
    t_idx = jnp.arange(ln)[None, :]
    lag_f = t_idx - s_idx
    kf = kern[jnp.clip(lag_f, 0, ln - 1), 0] * (lag_f >= 0)[:, :, None, None, None]
    kb = kern[jnp.clip(-lag_f, 0, ln - 1), 1] * (lag_f <= 0)[:, :, None, None, None]
    m_tab = (kf + kb).transpose(2, 0, 4, 1, 3).reshape(g, ln * c, ln * c)
    wf_r, wf_i = wr[:ln, 0][::-1], wi[:ln, 0][::-1]
    wb_r, wb_i = wr[:ln, 1], wi[:ln, 1]

    def inj(w):
        return w.transpose(1, 0, 3, 2).reshape(g, ln * c, p)

    b_tab = jnp.concatenate([inj(wf_r), inj(wf_i), inj(wb_r), inj(wb_i)], axis=-1)
    pf_r, pf_i = pr[1:ln + 1, 0], pi_[1:ln + 1, 0]
    pb_r, pb_i = pr[1:ln + 1, 1][::-1], pi_[1:ln + 1, 1][::-1]

    def rd(pw_r, pw_i, cre, cim):
        car = cre[None] * pw_r[:, :, None, :] - cim[None] * pw_i[:, :, None, :]
        cai = cre[None] * pw_i[:, :, None, :] + cim[None] * pw_r[:, :, None, :]
        to_rows = lambda a: a.transpose(1, 3, 0, 2).reshape(g, p, ln * c)
        return to_rows(car), to_rows(-cai)

    c_tab = jnp.concatenate(rd(pf_r, pf_i, cr[0], ci[0]) + rd(pb_r, pb_i, cr[1], ci[1]), axis=1)
    a_chunk = jnp.stack([jnp.stack([pr[ln, 0], pi_[ln, 0]]), jnp.stack([pr[ln, 1], pi_[ln, 1]])])
    d_tab = jnp.tile(d_skip.astype(F32).reshape(g, 1, c), (1, ln, 1)).reshape(g, 1, ln * c)
    return m_tab.astype(BF16), b_tab.astype(BF16), c_tab.astype(BF16), a_chunk, d_tab


def _s5_inject_kernel(u_ref, b_ref, o_ref):
    o_ref[...] = jnp.dot(u_ref[...], b_ref[...], preferred_element_type=F32)


def _s5_scan_kernel(a_ref, e_ref, o_ref, s_scr, *, nb):
    d = pl.program_id(0)

    @pl.when(pl.program_id(1) == 0)
    def _():
        s_scr[...] = jnp.zeros_like(s_scr)

    ar, ai = a_ref[0], a_ref[1]

    def body(i, carry):
        sr, si = carry
        c = jnp.where(d == 0, i, nb - 1 - i)
        o_ref[0, c] = sr
        o_ref[1, c] = si
        return ar * sr - ai * si + e_ref[0, c], ar * si + ai * sr + e_ref[1, c]

    sr, si = lax.fori_loop(0, nb, body, (s_scr[0], s_scr[1]))
    s_scr[0] = sr
    s_scr[1] = si


def _gelu_tanh(x):
    return 0.5 * x * (1.0 + jnp.tanh(math.sqrt(2.0 / math.pi) * (x + 0.044715 * (x * x * x))))


def _s5_readout_kernel(u_ref, m_ref, s_ref, c_ref, d_ref, o_ref):
    u = u_ref[...]
    y = jnp.dot(u, m_ref[...], preferred_element_type=F32) + u.astype(F32) * d_ref[...]
    for k in range(4):
        y = y + jnp.dot(s_ref[k].astype(BF16), c_ref[k * S5_STATE:(k + 1) * S5_STATE, :],
                        preferred_element_type=F32)
    o_ref[...] = _gelu_tanh(y)


def _s5_glu_kernel(z_ref, w_ref, o_ref):
    z = z_ref[...]
    o_ref[...] = (z * jax.nn.sigmoid(jnp.dot(z.astype(BF16), w_ref[...], preferred_element_type=F32))
                  ).astype(o_ref.dtype)


def _s5_mixer(u, tables, w_glu, n_lat, n_ctx):
    m_tab, b_tab, c_tab, a_chunk, d_tab = tables
    t = u.shape[0]
    g, c, ln, p = S5_GROUPS, S5_GROUP, S5_CHUNK, S5_STATE
    nc = t // ln
    w = ln * c
    ug = u.reshape(nc, ln, g, c).transpose(2, 0, 1, 3).reshape(g, nc, w)

    e = pl.pallas_call(
        _s5_inject_kernel,
        grid=(g,),
        in_specs=[pl.BlockSpec((None, nc, w), lambda i: (i, 0, 0)),
                  pl.BlockSpec((None, w, 4 * p), lambda i: (i, 0, 0))],
        out_specs=pl.BlockSpec((None, nc, 4 * p), lambda i: (i, 0, 0)),
        out_shape=jax.ShapeDtypeStruct((g, nc, 4 * p), F32),
        compiler_params=_cparams(("arbitrary",)),
        name="s5_inject",
    )(ug, b_tab)
    e = e.reshape(g, nc, 2, 2, p).transpose(2, 3, 1, 0, 4)

    nb = S5_BLOCK
    assert (n_lat // ln) % nb == 0 and n_ctx // ln == nb
    lat_blocks = n_lat // ln // nb

    def blk(d, s):
        return jnp.where(s == 0, lat_blocks, jnp.where(d == 0, s - 1, lat_blocks - s))

    s_in = pl.pallas_call(
        functools.partial(_s5_scan_kernel, nb=nb),
        grid=(2, lat_blocks + 1),
        in_specs=[pl.BlockSpec((None, 2, g, p), lambda d, s: (d, 0, 0, 0)),
                  pl.BlockSpec((None, 2, nb, g, p), lambda d, s: (d, 0, blk(d, s), 0, 0))],
        out_specs=pl.BlockSpec((None, 2, nb, g, p), lambda d, s: (d, 0, blk(d, s), 0, 0)),
        out_shape=jax.ShapeDtypeStruct((2, 2, nc, g, p), F32),
        scratch_shapes=[pltpu.VMEM((2, g, p), F32)],
        compiler_params=_cparams(("arbitrary", "arbitrary")),
        name="s5_scan",
    )(a_chunk, e)
    s_in = s_in.transpose(3, 0, 1, 2, 4).reshape(g, 4, nc, p)

    z = pl.pallas_call(
        _s5_readout_kernel,
        grid=(g,),
        in_specs=[pl.BlockSpec((None, nc, w), lambda i: (i, 0, 0)),
                  pl.BlockSpec((None, w, w), lambda i: (i, 0, 0)),
                  pl.BlockSpec((None, 4, nc, p), lambda i: (i, 0, 0, 0)),
                  pl.BlockSpec((None, 4 * p, w), lambda i: (i, 0, 0)),
                  pl.BlockSpec((None, 1, w), lambda i: (i, 0, 0))],
        out_specs=pl.BlockSpec((None, nc, w), lambda i: (i, 0, 0)),
        out_shape=jax.ShapeDtypeStruct((g, nc, w), F32),
        compiler_params=_cparams(("arbitrary",)),
        name="s5_readout",
    )(ug, m_tab, s_in, c_tab, d_tab)
    z = z.reshape(g, nc, ln, c).transpose(1, 2, 0, 3).reshape(t, g * c)

    tm = _row_tile(t, (1280, 640, 256, 128))
    return pl.pallas_call(
        _s5_glu_kernel,
        grid=(t // tm,),
        in_specs=[pl.BlockSpec((tm, g * c), lambda i: (i, 0)),
                  pl.BlockSpec((g * c, g * c), lambda i: (0, 0))],
        out_specs=pl.BlockSpec((tm, g * c), lambda i: (i, 0)),
        out_shape=jax.ShapeDtypeStruct((t, g * c), BF16),
        compiler_params=_cparams(("arbitrary",)),
        name="s5_glu",
    )(z, w_glu)


def _ret_tables(decay_logit):
    scale = HEAD_W ** -0.5
    lg = jax.nn.log_sigmoid(decay_logit.astype(F32))
    lf, lb = lg[0][:, None, None], lg[1][:, None, None]
    i = jnp.arange(RET_CHUNK, dtype=F32)[None, :, None]
    j = jnp.arange(RET_CHUNK, dtype=F32)[None, None, :]
    diff = i - j
    intra = (jnp.where(diff >= 0, jnp.exp(lf * jnp.maximum(diff, 0.0)), 0.0)
             + jnp.where(diff <= 0, jnp.exp(lb * jnp.maximum(-diff, 0.0)), 0.0)) * scale
    ones = jnp.ones((1, 1, HEAD_W), F32)
    q_f = jnp.exp(lf * (i + 1.0)) * ones
    k_f = jnp.exp(lf * (RET_CHUNK - 1.0 - i)) * scale * ones
    c_f = jnp.exp(lf * RET_CHUNK) * jnp.ones((1, RET_CHUNK, HEAD_W), F32)
    q_b = jnp.exp(lb * (RET_CHUNK - i)) * ones
    k_b = jnp.exp(lb * i) * scale * ones
    c_b = jnp.exp(lb * RET_CHUNK) * jnp.ones((1, RET_CHUNK, HEAD_W), F32)
    return intra, jnp.stack([q_f, k_f, c_f]), jnp.stack([q_b, k_b, c_b])


def _ret_state_step(q, k, v, dec_ref, s_scr, h):
    s = s_scr[h]
    qd = (q.astype(F32) * dec_ref[0, h]).astype(BF16)
    kd = (k.astype(F32) * dec_ref[1, h]).astype(BF16)
    o = jnp.dot(qd, s.astype(BF16), preferred_element_type=F32)
    s_scr[h] = s * dec_ref[2, h] + lax.dot_general(kd, v, (((0,), (0,)), ((), ())),
                                                    preferred_element_type=F32)
    return o


def _ret_fwd_kernel(intra_ref, dec_ref, q_ref, k_ref, v_ref, o_ref, s_scr):
    @pl.when(pl.program_id(0) == 0)
    def _():
        s_scr[...] = jnp.zeros_like(s_scr)

    for h in range(RET_HEADS):
        sl = slice(h * HEAD_W, (h + 1) * HEAD_W)
        q, k, v = q_ref[:, sl], k_ref[:, sl], v_ref[:, sl]
        att = lax.dot_general(q, k, (((1,), (1,)), ((), ())), preferred_element_type=F32) * intra_ref[h]
        o = jnp.dot(att.astype(BF16), v, preferred_element_type=F32)
        o_ref[:, sl] = o + _ret_state_step(q, k, v, dec_ref, s_scr, h)


def _ret_bwd_kernel(dec_ref, q_ref, k_ref, v_ref, g_ref, o1_ref, o_ref, s_scr):
    @pl.when(pl.program_id(0) == 0)
    def _():
        s_scr[...] = jnp.zeros_like(s_scr)

    for h in range(RET_HEADS):
        sl = slice(h * HEAD_W, (h + 1) * HEAD_W)
        q, k, v = q_ref[:, sl], k_ref[:, sl], v_ref[:, sl]
        o = o1_ref[:, sl] + _ret_state_step(q, k, v, dec_ref, s_scr, h)
        y = o * lax.rsqrt(jnp.mean(o * o, axis=-1, keepdims=True) + EPS)
        gate = g_ref[:, sl].astype(F32)
        o_ref[:, sl] = (y * (gate * jax.nn.sigmoid(gate))).astype(o_ref.dtype)


def _retention(proj, tables, n_lat, n_ctx):
    intra, dec_f, dec_b = tables
    t = proj.shape[0]
    wd = RET_HEADS * HEAD_W
    ck = RET_CHUNK
    n_lat_c, n_ctx_c = n_lat // ck, n_ctx // ck
    steps = n_lat_c + n_ctx_c

    def fwd_blk(s):
        return jnp.where(s < n_ctx_c, n_lat_c + s, s - n_ctx_c)

    def bwd_blk(s):
        return jnp.where(s < n_ctx_c, n_lat_c + n_ctx_c - 1 - s, n_lat_c - 1 - (s - n_ctx_c))

    def col_spec(col, order):
        return pl.BlockSpec((ck, wd), lambda s: (order(s), col))

    tab_spec = pl.BlockSpec((3, RET_HEADS, ck, HEAD_W), lambda s: (0, 0, 0, 0))
    o1 = pl.pallas_call(
        _ret_fwd_kernel,
        grid=(steps,),
        in_specs=[pl.BlockSpec((RET_HEADS, ck, ck), lambda s: (0, 0, 0)), tab_spec,
                  col_spec(0, fwd_blk), col_spec(1, fwd_blk), col_spec(2, fwd_blk)],
        out_specs=col_spec(0, fwd_blk),
        out_shape=jax.ShapeDtypeStruct((t, wd), F32),
        scratch_shapes=[pltpu.VMEM((RET_HEADS, HEAD_W, HEAD_W), F32)],
        compiler_params=_cparams(("arbitrary",)),
        name="retention_fwd",
    )(intra, dec_f, proj, proj, proj)
    return pl.pallas_call(
        _ret_bwd_kernel,
        grid=(steps,),
        in_specs=[tab_spec, col_spec(0, bwd_blk), col_spec(1, bwd_blk), col_spec(2, bwd_blk),
                  col_spec(3, bwd_blk), col_spec(0, bwd_blk)],
        out_specs=col_spec(0, bwd_blk),
        out_shape=jax.ShapeDtypeStruct((t, wd), BF16),
        scratch_shapes=[pltpu.VMEM((RET_HEADS, HEAD_W, HEAD_W), F32)],
        compiler_params=_cparams(("arbitrary",)),
        name="retention_bwd",
    )(dec_b, proj, proj, proj, proj, o1)


def _rope_tables(n_lat, n_ctx):
    n_freq = A_QK_DIM // 4
    freq = ROPE_BASE ** (-jnp.arange(n_freq, dtype=F32) / n_freq)
    tok = jnp.arange(n_lat)
    row = (tok // GRID_W).astype(F32)
    col = (tok % GRID_W).astype(F32)
    ang = jnp.stack([row[:, None] * freq, col[:, None] * freq], axis=1)
    ang = jnp.broadcast_to(ang[:, :, None, :], (n_lat, 2, 2, n_freq)).reshape(n_lat, A_QK_DIM)
    reps = HEAD_W // A_QK_DIM
    cos = jnp.concatenate([jnp.tile(jnp.cos(ang), (1, reps)), jnp.ones((n_ctx, HEAD_W), F32)], axis=0)
    sin = jnp.concatenate([jnp.tile(jnp.sin(ang), (1, reps)), jnp.zeros((n_ctx, HEAD_W), F32)], axis=0)
    return cos, sin


def kernel(x, c, ctx, c_ctx, ada_w, ada_b, norm1_w, norm2_w, ffn_w13, ffn_w2, e_w_in, e_w_out, diff_lq1, diff_lk1, diff_lq2, diff_lk2, diff_subln_w, na_rpb, o_w_in, o_w_out, s5_lam_re, s5_lam_im, s5_b_re, s5_b_im, s5_c_re, s5_c_im, s5_log_step, s5_d, s5_w_glu, ret_decay_logit, final_norm_w):
    bsz, n_lat, d = x.shape
    n_ctx = ctx.shape[1]
    depth = ada_w.shape[0]
    assert bsz == 1
    xs = jnp.concatenate([x[0], ctx[0]], axis=0)
    mods_all = _mods(c, c_ctx, ada_w, ada_b)
    cos, sin = _rope_tables(n_lat, n_ctx)
    ret_w = RET_HEADS * HEAD_W
    ffn_w13_bf16 = ffn_w13.astype(BF16)
    e_w_in_bf16 = e_w_in.astype(BF16)

    for i in range(depth):
        compute_ctx = i != depth - 1
        mods = mods_all[i]
        j = i // 2
        if i % 2 == 0:
            lambda_init = 0.8 - 0.6 * math.exp(-0.3 * i)
            qkv = _inproj_even(xs, norm1_w[i], mods, n_lat, e_w_in_bf16, j, cos, sin)
            lam_args = (diff_lq1[j], diff_lk1[j], diff_lq2[j], diff_lk2[j], diff_subln_w[j])
            o_a = _diff_attn(qkv, *lam_args, n_lat, n_ctx, lambda_init)
            o_b = _natten(qkv, _na_bias_tables(na_rpb[j], n_lat // GRID_W), n_lat, n_ctx)
            if compute_ctx:
                o_a = _diff_attn(qkv, *lam_args, n_lat, n_ctx, lambda_init, prev=o_a)
                o_b = _ctx_attn(qkv, o_b, n_lat, n_ctx)
            xs = _gated_residual([o_a, o_b], e_w_out, xs, mods, n_lat, 2, "outproj_even", layer=j)
        else:
            w_in = o_w_in[j]
            w_in = jnp.concatenate([w_in[:, S5_CH:], w_in[:, :S5_CH]], axis=1).astype(BF16)
            proj = _inproj_odd(xs, norm1_w[i], mods, n_lat, w_in)
            s5_tabs = _s5_tables(s5_lam_re[j], s5_lam_im[j], s5_b_re[j], s5_b_im[j], s5_c_re[j], s5_c_im[j],
                                 s5_log_step[j], s5_d[j])
            y_c = _s5_mixer(proj[:, 4 * ret_w:], s5_tabs, s5_w_glu[j].astype(BF16), n_lat, n_ctx)
            y_d = _retention(proj, _ret_tables(ret_decay_logit[j]), n_lat, n_ctx)
            xs = _gated_residual([y_c, y_d], o_w_out, xs, mods, n_lat, 2, "outproj_odd", layer=j)
        a = _ffn_up(xs, norm2_w[i], mods, n_lat, ffn_w13_bf16, i)
        xs = _gated_residual([a], ffn_w2[i].astype(BF16), xs, mods, n_lat, 5, "ffn_down")
    return _final_norm(xs, final_norm_w, n_lat)[None]
```

```python
import functools
import math

import jax
import jax.numpy as jnp
import numpy as np
from jax import lax
from jax.experimental import pallas as pl
from jax.experimental.pallas import tpu as pltpu

F32 = jnp.float32
BF16 = jnp.bfloat16

GRID_W = 64
A_HEADS = 8
A_QK_DIM = 64
HEAD_W = 128
B_HEADS = 8
WIN_R = 8
WIN_C = 16
NA_ROWS = 4
S5_CH = 512
S5_GROUP = 16
S5_GROUPS = 32
S5_STATE = 64
S5_CHUNK = 16
S5_BLOCK = 16
S5_GB = 8
RET_HEADS = 12
RET_CHUNK = 128
ROPE_BASE = 10000.0
EPS = 1e-6
NEG_INF = -1e30
VMEM_LIMIT = 56 * 1024 * 1024
N_MOD = 8
NORM_ROW_CHUNK = 128


def _cparams(sem):
    return pltpu.CompilerParams(dimension_semantics=sem, vmem_limit_bytes=VMEM_LIMIT)


def _row_tile(t, candidates):
    for c in candidates:
        if t % c == 0:
            return c
    raise ValueError(f"no row tile for {t}")


def _pick_mod(mods_ref, idx, is_ctx):
    return jnp.where(is_ctx, mods_ref[1, idx:idx + 1, :], mods_ref[0, idx:idx + 1, :])


def _is_ctx_rows(tm, n_lat, axis):
    row = pl.program_id(axis) * tm + lax.broadcasted_iota(jnp.int32, (tm, 1), 0)
    return row >= n_lat


def _mods_kernel(s_ref, w_ref, b_ref, o_ref):
    s = s_ref[...]
    s = s * jax.nn.sigmoid(s)
    o_ref[...] = jnp.dot(s, w_ref[...], preferred_element_type=F32,
                         precision=lax.Precision.HIGHEST) + b_ref[...]


def _mods(c, c_ctx, ada_w, ada_b):
    depth, d, w6 = ada_w.shape
    s = jnp.zeros((8, d), F32).at[0].set(c[0]).at[1].set(c_ctx)
    tn = 1024
    out = pl.pallas_call(
        _mods_kernel,
        grid=(depth, w6 // tn),
        in_specs=[pl.BlockSpec((8, d), lambda l, j: (0, 0)),
                  pl.BlockSpec((None, d, tn), lambda l, j: (l, 0, j)),
                  pl.BlockSpec((None, 1, tn), lambda l, j: (l, 0, j))],
        out_specs=pl.BlockSpec((None, 8, tn), lambda l, j: (l, 0, j)),
        out_shape=jax.ShapeDtypeStruct((depth, 8, w6), F32),
        compiler_params=_cparams(("arbitrary", "arbitrary")),
        name="ada_mods",
    )(s, ada_w, ada_b.reshape(depth, 1, w6))
    m = out[:, :2].reshape(depth, 2, 6, d)
    return jnp.pad(m, ((0, 0), (0, 0), (0, N_MOD - 6), (0, 0)))


def _normmod_prologue(x_ref, nw_ref, mods_ref, h_scr, *, n_lat, tm, shift_idx, scale_idx):
    @pl.when(pl.program_id(1) == 0)
    def _():
        rc = NORM_ROW_CHUNK
        assert tm % rc == 0 and n_lat % rc == 0

        def chunk(c, carry):
            r0 = pl.multiple_of(c * rc, rc)
            seg = (pl.program_id(0) * tm + r0 >= n_lat).astype(jnp.int32)
            gain = nw_ref[...] * (1.0 + mods_ref[seg, scale_idx:scale_idx + 1, :])
            shift = mods_ref[seg, shift_idx:shift_idx + 1, :]
            x = x_ref[pl.ds(r0, rc), :]
            r = lax.rsqrt(jnp.mean(x * x, axis=-1, keepdims=True) + EPS)
            h_scr[pl.ds(r0, rc), :] = (x * r * gain + shift).astype(h_scr.dtype)
            return carry

        lax.fori_loop(0, tm // rc, chunk, 0)


def _norm_proj_call(kern, x, norm_w, mods, n_lat, shift_idx, scale_idx, w_args, w_specs, extra_args, extra_specs,
                    n_out, tn, tm, name, **kern_kw):
    t, d = x.shape
    norm = dict(n_lat=n_lat, tm=tm, shift_idx=shift_idx, scale_idx=scale_idx)
    return pl.pallas_call(
        functools.partial(kern, norm=norm, **kern_kw),
        grid=(t // tm, n_out // tn),
        in_specs=[pl.BlockSpec((tm, d), lambda i, j: (i, 0)),
                  pl.BlockSpec((1, d), lambda i, j: (0, 0)),
                  pl.BlockSpec((2, N_MOD, d), lambda i, j: (0, 0, 0))] + w_specs + extra_specs,
        out_specs=pl.BlockSpec((tm, tn), lambda i, j: (i, j)),
        out_shape=jax.ShapeDtypeStruct((t, n_out), BF16),
        scratch_shapes=[pltpu.VMEM((tm, d), BF16)],
        compiler_params=_cparams(("arbitrary", "arbitrary")),
        name=name,
    )(x, norm_w.reshape(1, d), mods, *w_args, *extra_args)


def _final_norm_kernel(x_ref, w_ref, o_ref):
    x = x_ref[...]
    o_ref[...] = x * lax.rsqrt(jnp.mean(x * x, axis=-1, keepdims=True) + EPS) * w_ref[...]


def _final_norm(x, w, n_lat):
    d = x.shape[1]
    tm = _row_tile(n_lat, (512, 256, 128))
    return pl.pallas_call(
        _final_norm_kernel,
        grid=(n_lat // tm,),
        in_specs=[pl.BlockSpec((tm, d), lambda i: (i, 0)),
                  pl.BlockSpec((1, d), lambda i: (0, 0))],
        out_specs=pl.BlockSpec((tm, d), lambda i: (i, 0)),
        out_shape=jax.ShapeDtypeStruct((n_lat, d), F32),
        compiler_params=_cparams(("arbitrary",)),
        name="final_norm",
    )(x, w.reshape(1, d))


def _rope_store(acc, cos, sin, o_ref, scale):
    first_half = (lax.broadcasted_iota(jnp.int32, (1, HEAD_W), 1) % 32) < 16
    for c in range(acc.shape[1] // HEAD_W):
        x = acc[:, c * HEAD_W:(c + 1) * HEAD_W]
        rot = jnp.where(first_half, -pltpu.roll(x, HEAD_W - 16, 1), pltpu.roll(x, 16, 1))
        o_ref[:, c * HEAD_W:(c + 1) * HEAD_W] = ((x * cos + rot * sin) * scale).astype(o_ref.dtype)


def _resident_bf16(w_ref, w_scr):
    @pl.when(pl.program_id(1) == 0)
    def _():
        w_scr[...] = w_ref[...].astype(BF16)

    return w_scr


def _inproj_even_kernel(x_ref, nw_ref, mods_ref, w_ref, cos_ref, sin_ref, o_ref, h_scr, *, norm, a_scale, b_scale):
    _normmod_prologue(x_ref, nw_ref, mods_ref, h_scr, **norm)
    j = pl.program_id(1)
    acc = jnp.dot(h_scr[...], w_ref[...], preferred_element_type=F32)

    @pl.when(j == 0)
    def _():
        _rope_store(acc, cos_ref[...], sin_ref[...], o_ref, a_scale)

    @pl.when(j == 1)
    def _():
        _rope_store(acc, cos_ref[...], sin_ref[...], o_ref, 1.0)

    @pl.when(j == 3)
    def _():
        o_ref[...] = (acc * b_scale).astype(o_ref.dtype)

    @pl.when((j == 2) | (j > 3))
    def _():
        o_ref[...] = acc.astype(o_ref.dtype)


def _inproj_even(x, norm_w, mods, n_lat, w, layer, cos, sin):
    t, d = x.shape
    n = w.shape[2]
    tn = n // 6
    tm = _row_tile(t, (1280, 640, 256, 128))
    rope_spec = pl.BlockSpec((tm, HEAD_W), lambda i, j: (i, 0))
    return _norm_proj_call(
        _inproj_even_kernel, x, norm_w, mods, n_lat, 0, 1,
        [w], [pl.BlockSpec((None, d, tn), lambda i, j: (layer, 0, j))],
        [cos, sin], [rope_spec, rope_spec], n, tn, tm, "inproj_even",
        a_scale=A_QK_DIM ** -0.5 * math.log2(math.e), b_scale=HEAD_W ** -0.5)


def _inproj_odd_kernel(x_ref, nw_ref, mods_ref, w_ref, o_ref, h_scr, *, norm):
    _normmod_prologue(x_ref, nw_ref, mods_ref, h_scr, **norm)
    o_ref[...] = jnp.dot(h_scr[...], w_ref[...], preferred_element_type=F32).astype(o_ref.dtype)


def _inproj_odd(x, norm_w, mods, n_lat, w):
    t, d = x.shape
    n = w.shape[1]
    tn = 512
    tm = _row_tile(t, (1280, 640, 256, 128))
    return _norm_proj_call(
        _inproj_odd_kernel, x, norm_w, mods, n_lat, 0, 1,
        [w], [pl.BlockSpec((d, tn), lambda i, j: (0, j))], [], [], n, tn, tm, "inproj_odd")


def _ffn_up_kernel(x_ref, nw_ref, mods_ref, w1_ref, w3_ref, o_ref, h_scr, *, norm):
    _normmod_prologue(x_ref, nw_ref, mods_ref, h_scr, **norm)
    h = h_scr[...]
    a = jnp.dot(h, w1_ref[...], preferred_element_type=F32)
    b = jnp.dot(h, w3_ref[...], preferred_element_type=F32)
    o_ref[...] = (a * jax.nn.sigmoid(a) * b).astype(o_ref.dtype)


def _ffn_up(x, norm_w, mods, n_lat, w13, layer):
    t, d = x.shape
    d_ff = w13.shape[2] // 2
    tn = 512
    nj = d_ff // tn
    tm = _row_tile(t, (1280, 640, 256, 128))
    return _norm_proj_call(
        _ffn_up_kernel, x, norm_w, mods, n_lat, 3, 4,
        [w13, w13], [pl.BlockSpec((None, d, tn), lambda i, j: (layer, 0, j)),
                     pl.BlockSpec((None, d, tn), lambda i, j: (layer, 0, j + nj))],
        [], [], d_ff, tn, tm, "ffn_up")


def _gated_residual_kernel(*refs, n_a, n_lat, tm, gate_idx):
    a_refs = refs[:n_a]
    w_ref, x_ref, mods_ref, o_ref = refs[n_a:n_a + 4]
    if w_ref.dtype != BF16:
        w_ref = _resident_bf16(w_ref, refs[n_a + 4])
    k0 = 0
    y = None
    for a_ref in a_refs:
        kk = a_ref.shape[1]
        part = jnp.dot(a_ref[...], w_ref[k0:k0 + kk, :], preferred_element_type=F32)
        y = part if y is None else y + part
        k0 += kk
    gate = _pick_mod(mods_ref, gate_idx, _is_ctx_rows(tm, n_lat, 1))
    o_ref[...] = x_ref[...] + gate * y


def _gated_residual(a_list, w, x, mods, n_lat, gate_idx, name, layer=None):
    t, d = x.shape
    k = w.shape[-2]
    tn = 512
    tm = _row_tile(t, (640, 256, 128))
    n_a = len(a_list)
    in_specs = [pl.BlockSpec((tm, a.shape[1]), lambda j, i: (i, 0)) for a in a_list]
    w_spec = (pl.BlockSpec((k, tn), lambda j, i: (0, j)) if layer is None
              else pl.BlockSpec((None, k, tn), lambda j, i: (layer, 0, j)))
    in_specs += [w_spec,
                 pl.BlockSpec((tm, tn), lambda j, i: (i, j)),
                 pl.BlockSpec((2, N_MOD, tn), lambda j, i: (0, 0, j))]
    return pl.pallas_call(
        functools.partial(_gated_residual_kernel, n_a=n_a, n_lat=n_lat, tm=tm, gate_idx=gate_idx),
        grid=(d // tn, t // tm),
        in_specs=in_specs,
        out_specs=pl.BlockSpec((tm, tn), lambda j, i: (i, j)),
        out_shape=jax.ShapeDtypeStruct((t, d), F32),
        scratch_shapes=[] if w.dtype == BF16 else [pltpu.VMEM((k, tn), BF16)],
        input_output_aliases={n_a + 1: 0},
        compiler_params=_cparams(("arbitrary", "arbitrary")),
        name=name,
    )(*a_list, w, x, mods)


def _softmax_block(qs, k, v):
    s = lax.dot_general(qs, k, (((1,), (1,)), ((), ())), preferred_element_type=F32)
    m = jnp.max(s, axis=-1, keepdims=True)
    p = jnp.exp(s - m)
    l = jnp.sum(p, axis=-1, keepdims=True)
    acc = jnp.dot(p.astype(v.dtype), v, preferred_element_type=F32)
    return m, l, acc


DIFF_PV_ROWS = 512
DIFF_SOFTMAX_ROWS = 64
DIFF_EXP_DTYPE = BF16


def _diff_attn_kernel(*refs, tq, tk, n_ctx, n_chunks, lambda_init):
    if n_chunks > 1:
        (lq1, lk1, lq2, lk2, subw, q_ref, kc_ref, vc_ref, k_ref, v_ref, o_ref,
         qs_scr, sa_scr, mca_scr, p_scr, m_scr, l_scr, alpha_scr, acc_scr, sb_scr, mcb_scr) = refs
        buf_b = (sb_scr, mcb_scr)
    elif n_chunks == 1:
        (lq1, lk1, lq2, lk2, subw, q_ref, kc_ref, vc_ref, k_ref, v_ref, o_ref,
         qs_scr, sa_scr, mca_scr, p_scr, m_scr, l_scr, alpha_scr, acc_scr) = refs
    else:
        (lq1, lk1, lq2, lk2, subw, q_ref, kc_ref, vc_ref, o_ref,
         qs_scr, sa_scr, mca_scr, p_scr, m_scr, l_scr, alpha_scr, acc_scr) = refs
    buf_a = (sa_scr, mca_scr)
    nt = (((1,), (1,)), ((), ()))
    rows_all = 2 * tq
    pv_rows = min(DIFF_PV_ROWS, rows_all)
    sm_rows = min(DIFF_SOFTMAX_ROWS, pv_rows)
    q = q_ref[...]
    comp1 = lax.broadcasted_iota(jnp.int32, (1, HEAD_W), 1) < A_QK_DIM
    zero = jnp.zeros_like(q)
    qs_scr[0:tq, :] = jnp.where(comp1, q, zero)
    qs_scr[tq:2 * tq, :] = jnp.where(comp1, zero, q)

    def scores(buf, rows, keys):
        s_buf, mc_buf = buf
        s = lax.dot_general(qs_scr[rows, :], keys, nt, preferred_element_type=F32)
        width = s.shape[1]
        s_buf[rows, 0:width] = s
        mp = s[:, 0:HEAD_W]
        for c in range(1, width // HEAD_W):
            mp = jnp.maximum(mp, s[:, c * HEAD_W:(c + 1) * HEAD_W])
        mc_buf[rows, :] = jnp.broadcast_to(jnp.max(mp, axis=-1, keepdims=True), mp.shape)

    def softmax_rows(buf, rows, n_tiles, first):
        s_buf, mc_buf = buf
        m_cur = mc_buf[rows, :]
        if first:
            m_new = m_cur
        else:
            m_prev = m_scr[rows, :]
            m_new = jnp.maximum(m_prev, m_cur)
        lp = None
        for c in range(n_tiles):
            pc = jnp.exp2((s_buf[rows, c * HEAD_W:(c + 1) * HEAD_W] - m_new).astype(DIFF_EXP_DTYPE))
            p_scr[rows, c * HEAD_W:(c + 1) * HEAD_W] = pc.astype(BF16)
            lp = pc if lp is None else lp + pc
        l_cur = jnp.sum(lp.astype(F32), axis=-1, keepdims=True)
        if first:
            l_scr[rows, :] = jnp.broadcast_to(l_cur, m_new.shape)
        else:
            alpha = jnp.exp2(m_prev - m_new)
            alpha_scr[rows, :] = alpha
            l_scr[rows, :] = alpha * l_scr[rows, :] + l_cur
        m_scr[rows, :] = m_new

    def softmax_pv(buf, width, load_v, first, after_group=None):
        for r in range(rows_all // pv_rows):
            for r2 in range(pv_rows // sm_rows):
                r0 = r * pv_rows + r2 * sm_rows
                softmax_rows(buf, slice(r0, r0 + sm_rows), width // HEAD_W, first)
            rows = slice(r * pv_rows, (r + 1) * pv_rows)
            pv = jnp.dot(p_scr[rows, 0:width], load_v(), preferred_element_type=F32)
            acc_scr[rows, :] = pv if first else alpha_scr[rows, :] * acc_scr[rows, :] + pv
            if after_group is not None:
                after_group(rows)

    def all_groups(fn):
        for r in range(rows_all // pv_rows):
            fn(slice(r * pv_rows, (r + 1) * pv_rows))

    all_groups(lambda rows: scores(buf_a, rows, kc_ref[...]))
    softmax_pv(buf_a, n_ctx, lambda: vc_ref[...], True)

    def qk_rows(j, buf, rows):
        off = pl.multiple_of(j * tk, tk)
        scores(buf, rows, k_ref[pl.ds(off, tk), :])

    def soft_pv(j, buf, j_next=None, buf_next=None):
        off = pl.multiple_of(j * tk, tk)
        after = None if j_next is None else (lambda rows: qk_rows(j_next, buf_next, rows))
        softmax_pv(buf, tk, lambda: v_ref[pl.ds(off, tk), :], False, after)

    if n_chunks == 1:
        all_groups(lambda rows: qk_rows(0, buf_a, rows))
        soft_pv(0, buf_a)
    elif n_chunks > 1:
        all_groups(lambda rows: qk_rows(0, buf_a, rows))

        def pair(j2, carry):
            soft_pv(2 * j2, buf_a, 2 * j2 + 1, buf_b)
            soft_pv(2 * j2 + 1, buf_b, 2 * j2 + 2, buf_a)
            return carry

        lax.fori_loop(0, n_chunks // 2 - 1, pair, 0)
        soft_pv(n_chunks - 2, buf_a, n_chunks - 1, buf_b)
        soft_pv(n_chunks - 1, buf_b)

    lam = (jnp.exp(jnp.sum(lq1[...] * lk1[...], axis=-1, keepdims=True))
           - jnp.exp(jnp.sum(lq2[...] * lk2[...], axis=-1, keepdims=True)) + lambda_init)
    o1 = acc_scr[0:tq, :] / l_scr[0:tq, :]
    o2 = acc_scr[tq:2 * tq, :] / l_scr[tq:2 * tq, :]
    o = o1 - lam * o2
    y = o * lax.rsqrt(jnp.mean(o * o, axis=-1, keepdims=True) + EPS) * subw[...]
    o_ref[...] = (y * (1.0 - lambda_init)).astype(o_ref.dtype)


def _diff_attn(qkv, lq1, lk1, lq2, lk2, subw, n_lat, n_ctx, lambda_init, prev=None):
    t = qkv.shape[0]
    use_lat = prev is None
    ctx_blk = n_lat // n_ctx
    small = [lq1.reshape(1, -1), lk1.reshape(1, -1), lq2.reshape(1, -1), lk2.reshape(1, -1), subw.reshape(1, -1)]
    small_specs = [pl.BlockSpec(a.shape, lambda h, qi: (0, 0)) for a in small]
    kc_spec = pl.BlockSpec((n_ctx, HEAD_W), lambda h, qi: (ctx_blk, A_HEADS + h))
    vc_spec = pl.BlockSpec((n_ctx, HEAD_W), lambda h, qi: (ctx_blk, 2 * A_HEADS + h))
    if use_lat:
        tq = _row_tile(n_lat, (1024, 512, 256, 128))
        tk = _row_tile(n_lat, (1024, 512, 256, 128))
        n_chunks = n_lat // tk
        assert n_chunks == 1 or n_chunks % 2 == 0
        grid = (A_HEADS, n_lat // tq)
        in_specs = small_specs + [
            pl.BlockSpec((tq, HEAD_W), lambda h, qi: (qi, h)), kc_spec, vc_spec,
            pl.BlockSpec((n_lat, HEAD_W), lambda h, qi: (0, A_HEADS + h), pipeline_mode=pl.Buffered(1)),
            pl.BlockSpec((n_lat, HEAD_W), lambda h, qi: (0, 2 * A_HEADS + h), pipeline_mode=pl.Buffered(1))]
        args = small + [qkv, qkv, qkv, qkv, qkv]
        out_spec = pl.BlockSpec((tq, HEAD_W), lambda h, qi: (qi, h))
        aliases = {}
    else:
        tq, tk, n_chunks = n_ctx, 0, 0
        grid = (A_HEADS, 1)
        in_specs = small_specs + [
            pl.BlockSpec((tq, HEAD_W), lambda h, qi: (ctx_blk, h)), kc_spec, vc_spec,
            pl.BlockSpec(memory_space=pl.ANY)]
        args = small + [qkv, qkv, qkv, prev]
        out_spec = pl.BlockSpec((tq, HEAD_W), lambda h, qi: (ctx_blk, h))
        aliases = {len(args) - 1: 0}

    sw = max(tk, n_ctx)
    kern = functools.partial(_diff_attn_kernel, tq=tq, tk=tk, n_ctx=n_ctx, n_chunks=n_chunks,
                             lambda_init=lambda_init)
    if not use_lat:
        inner = kern

        def kern(*refs):
            n_in = len(args)
            inner(*refs[:n_in - 1], *refs[n_in:])

    return pl.pallas_call(
        kern,
        grid=grid,
        in_specs=in_specs,
        out_specs=out_spec,
        out_shape=jax.ShapeDtypeStruct((t, A_HEADS * HEAD_W), BF16),
        scratch_shapes=[pltpu.VMEM((2 * tq, HEAD_W), BF16),
                        pltpu.VMEM((2 * tq, sw), F32),
                        pltpu.VMEM((2 * tq, HEAD_W), F32),
                        pltpu.VMEM((2 * tq, sw), BF16),
                        pltpu.VMEM((2 * tq, HEAD_W), F32),
                        pltpu.VMEM((2 * tq, HEAD_W), F32),
                        pltpu.VMEM((2 * tq, HEAD_W), F32),
                        pltpu.VMEM((2 * tq, HEAD_W), F32)]
        + ([pltpu.VMEM((2 * tq, sw), F32),
            pltpu.VMEM((2 * tq, HEAD_W), F32)] if n_chunks > 1 else []),
        input_output_aliases=aliases,
        compiler_params=_cparams(("arbitrary", "arbitrary")),
        name="diff_attn" if use_lat else "diff_attn_ctx",
    )(*args)


def _na_bias_tables(rpb, rows):
    nblk = rows // NA_ROWS
    wr = min(WIN_R, rows)
    nh = rpb.shape[0]
    rq = jnp.arange(NA_ROWS)[:, None, None]
    slot = jnp.arange(3)[None, :, None]
    rk = jnp.arange(NA_ROWS)[None, None, :]
    row_sel, row_ok = [], []
    for b, dup in ((0, 0), (1, -1), (nblk - 1, 2)):
        r = NA_ROWS * b + rq
        r0 = jnp.clip(r - wr // 2, 0, rows - wr)
        rkey = NA_ROWS * (b - 1 + slot) + rk
        ok = (rkey >= r0) & (rkey < r0 + wr) & (slot != dup)
        drow = jnp.broadcast_to(rkey - r + (WIN_R - 1), ok.shape)
        row_sel.append((drow[..., None] == jnp.arange(2 * WIN_R - 1)) & ok[..., None])
        row_ok.append(ok)
    row_sel = jnp.stack(row_sel).astype(F32)
    row_ok = jnp.stack(row_ok)
    jq = jnp.arange(GRID_W)[:, None]
    jk = jnp.arange(GRID_W)[None, :]
    c0 = jnp.clip(jq - WIN_C // 2, 0, GRID_W - WIN_C)
    col_ok = (jk >= c0) & (jk < c0 + WIN_C)
    dcol = jnp.clip(jk - jq + (WIN_C - 1), 0, 2 * WIN_C - 2)
    col_sel = (dcol[..., None] == jnp.arange(2 * WIN_C - 1)).astype(F32)
    bias = jnp.einsum('vasbr,hrc,qkc->hvaqsbk', row_sel, rpb.astype(F32), col_sel,
                      precision=lax.Precision.HIGHEST)
    valid = row_ok[None, :, :, None, :, :, None] & col_ok[None, None, None, :, None, None, :]
    tab = jnp.where(valid, bias, NEG_INF)
    return tab.reshape(nh, 3, NA_ROWS * GRID_W, 3 * NA_ROWS * GRID_W)


def _natten_kernel(tab_ref, q_ref, k0_ref, k1_ref, k2_ref, v0_ref, v1_ref, v2_ref, kc_ref, vc_ref, o_ref):
    q = q_ref[...]
    nt = (((1,), (1,)), ((), ()))
    blk = q.shape[0]
    s_loc = [lax.dot_general(q, k_ref[...], nt, preferred_element_type=F32)
             + tab_ref[:, i * blk:(i + 1) * blk] for i, k_ref in enumerate((k0_ref, k1_ref, k2_ref))]
    s_ctx = lax.dot_general(q, kc_ref[...], nt, preferred_element_type=F32)
    m = jnp.max(s_ctx, axis=-1, keepdims=True)
    for s in s_loc:
        m = jnp.maximum(m, jnp.max(s, axis=-1, keepdims=True))
    p = jnp.exp(s_ctx - m)
    l = jnp.sum(p, axis=-1, keepdims=True)
    acc = jnp.dot(p.astype(BF16), vc_ref[...], preferred_element_type=F32)
    for s, v_ref in zip(s_loc, (v0_ref, v1_ref, v2_ref)):
        p = jnp.exp(s - m)
        l = l + jnp.sum(p, axis=-1, keepdims=True)
        acc = acc + jnp.dot(p.astype(BF16), v_ref[...], preferred_element_type=F32)
    o_ref[...] = (acc / l).astype(o_ref.dtype)


def _natten(qkv, tabs, n_lat, n_ctx):
    t = qkv.shape[0]
    blk = NA_ROWS * GRID_W
    nblk = n_lat // blk
    assert nblk >= 3 and blk == n_ctx
    qo, ko, vo = 3 * A_HEADS, 3 * A_HEADS + B_HEADS, 3 * A_HEADS + 2 * B_HEADS
    ctx_blk = n_lat // n_ctx

    def kv_spec(off, shift):
        return pl.BlockSpec((blk, HEAD_W), lambda h, b: (jnp.clip(b + shift, 0, nblk - 1), off + h))

    return pl.pallas_call(
        _natten_kernel,
        grid=(B_HEADS, nblk),
        in_specs=[pl.BlockSpec((None, None, blk, 3 * blk),
                               lambda h, b: (h, jnp.where(b == 0, 0, jnp.where(b == nblk - 1, 2, 1)), 0, 0)),
                  pl.BlockSpec((blk, HEAD_W), lambda h, b: (b, qo + h)),
                  kv_spec(ko, -1), kv_spec(ko, 0), kv_spec(ko, 1),
                  kv_spec(vo, -1), kv_spec(vo, 0), kv_spec(vo, 1),
                  pl.BlockSpec((n_ctx, HEAD_W), lambda h, b: (ctx_blk, ko + h)),
                  pl.BlockSpec((n_ctx, HEAD_W), lambda h, b: (ctx_blk, vo + h))],
        out_specs=pl.BlockSpec((blk, HEAD_W), lambda h, b: (b, h)),
        out_shape=jax.ShapeDtypeStruct((t, B_HEADS * HEAD_W), BF16),
        compiler_params=_cparams(("arbitrary", "arbitrary")),
        name="natten",
    )(tabs, qkv, qkv, qkv, qkv, qkv, qkv, qkv, qkv, qkv)


def _ctx_attn_kernel(q_ref, k_ref, v_ref, prev_ref, o_ref):
    del prev_ref
    _, l, acc = _softmax_block(q_ref[...], k_ref[...], v_ref[...])
    o_ref[...] = (acc / l).astype(o_ref.dtype)


def _ctx_attn(qkv, prev, n_lat, n_ctx):
    qo, ko, vo = 3 * A_HEADS, 3 * A_HEADS + B_HEADS, 3 * A_HEADS + 2 * B_HEADS
    ctx_blk = n_lat // n_ctx
    return pl.pallas_call(
        _ctx_attn_kernel,
        grid=(B_HEADS,),
        in_specs=[pl.BlockSpec((n_ctx, HEAD_W), lambda h: (ctx_blk, qo + h)),
                  pl.BlockSpec((n_ctx, HEAD_W), lambda h: (ctx_blk, ko + h)),
                  pl.BlockSpec((n_ctx, HEAD_W), lambda h: (ctx_blk, vo + h)),
                  pl.BlockSpec(memory_space=pl.ANY)],
        out_specs=pl.BlockSpec((n_ctx, HEAD_W), lambda h: (ctx_blk, h)),
        out_shape=jax.ShapeDtypeStruct(prev.shape, prev.dtype),
        input_output_aliases={3: 0},
        compiler_params=_cparams(("arbitrary",)),
        name="ctx_attn",
    )(qkv, qkv, qkv, prev)


def _s5_tables(lam_re, lam_im, b_re, b_im, c_re, c_im, log_step, d_skip):
    hp = lax.Precision.HIGHEST
    ln = S5_CHUNK
    g, p, c = S5_GROUPS, S5_STATE, S5_GROUP
    lr, li = lam_re.astype(F32), lam_im.astype(F32)
    dt = jnp.exp(log_step.astype(F32))[:, :, None]
    mag = jnp.exp(lr * dt)
    ar, ai = mag * jnp.cos(li * dt), mag * jnp.sin(li * dt)
    den = lr * lr + li * li
    nr, ni = ar - 1.0, ai
    fr = (nr * lr + ni * li) / den
    fi = (ni * lr - nr * li) / den
    br_, bi_ = b_re.astype(F32), b_im.astype(F32)
    bbr = fr[..., None] * br_ - fi[..., None] * bi_
    bbi = fr[..., None] * bi_ + fi[..., None] * br_
    lag = jnp.arange(ln + 1, dtype=F32)[:, None, None, None]
    magl = jnp.exp(lr * dt * lag)
    pr, pi_ = magl * jnp.cos(li * dt * lag), magl * jnp.sin(li * dt * lag)
    wr = pr[..., None] * bbr - pi_[..., None] * bbi
    wi = pr[..., None] * bbi + pi_[..., None] * bbr
    cr, ci = c_re.astype(F32), c_im.astype(F32)
    kern = (jnp.einsum('dgcp,ldgpe->ldgce', cr, wr[:ln], precision=hp)
            - jnp.einsum('dgcp,ldgpe->ldgce', ci, wi[:ln], precision=hp))
    lag = np.arange(ln)[:, None, None]
    s_idx = np.arange(ln)[None, :, None]
    t_idx = np.arange(ln)[None, None, :]
    sel_f = jnp.asarray(t_idx - s_idx == lag, F32)
    sel_b = jnp.asarray(s_idx - t_idx == lag, F32)
    kfb = (jnp.einsum('lst,lgce->stgce', sel_f, kern[:, 0], precision=hp)
           + jnp.einsum('lst,lgce->stgce', sel_b, kern[:, 1], precision=hp))
    m_tab = kfb.transpose(2, 0, 4, 1, 3).reshape(g, ln * c, ln * c)
    wf_r, wf_i = wr[:ln, 0][::-1], wi[:ln, 0][::-1]
    wb_r, wb_i = wr[:ln, 1], wi[:ln, 1]

    def inj(w):
        return w.transpose(1, 0, 3, 2).reshape(g, ln * c, p)

    b_tab = jnp.concatenate([inj(wf_r), inj(wf_i), inj(wb_r), inj(wb_i)], axis=-1)
    pf_r, pf_i = pr[1:ln + 1, 0], pi_[1:ln + 1, 0]
    pb_r, pb_i = pr[1:ln + 1, 1][::-1], pi_[1:ln + 1, 1][::-1]

    def rd(pw_r, pw_i, cre, cim):
        car = cre[None] * pw_r[:, :, None, :] - cim[None] * pw_i[:, :, None, :]
        cai = cre[None] * pw_i[:, :, None, :] + cim[None] * pw_r[:, :, None, :]
        to_rows = lambda a: a.transpose(1, 3, 0, 2).reshape(g, p, ln * c)
        return to_rows(car), to_rows(-cai)

    c_tab = jnp.concatenate(rd(pf_r, pf_i, cr[0], ci[0]) + rd(pb_r, pb_i, cr[1], ci[1]), axis=1)
    a_chunk = jnp.stack([jnp.stack([pr[ln, 0], pi_[ln, 0]]), jnp.stack([pr[ln, 1], pi_[ln, 1]])])
    d_tab = jnp.tile(d_skip.astype(F32).reshape(g, 1, c), (1, ln, 1)).reshape(g, 1, ln * c)
    gb, nb = g // S5_GB, S5_GB
    w = ln * nb * c
    eye = jnp.eye(nb, dtype=BF16)
    m8 = jnp.einsum('ij,bisptr->bsiptjr', eye, m_tab.astype(BF16).reshape(gb, nb, ln, c, ln, c)).reshape(gb, w, w)
    b8 = jnp.einsum('ij,bispkq->bsipkjq', eye, b_tab.astype(BF16).reshape(gb, nb, ln, c, 4, p)
                    ).reshape(gb, w, 4 * nb * p)
    c8 = jnp.einsum('ij,bikqtr->bkiqtjr', eye, c_tab.astype(BF16).reshape(gb, nb, 4, p, ln, c)
                    ).reshape(gb, 4 * nb * p, w)
    d8 = d_tab.reshape(gb, nb, ln, c).transpose(0, 2, 1, 3).reshape(gb, 1, w)
    return m8, b8, c8, a_chunk, d8


def _s5_inject_kernel(u_ref, b_ref, o_ref):
    o_ref[...] = jnp.dot(u_ref[...], b_ref[...], preferred_element_type=F32)


def _s5_scan_kernel(a_ref, e_ref, o_ref, s_scr, *, nb):
    d = pl.program_id(0)

    @pl.when(pl.program_id(1) == 0)
    def _():
        s_scr[...] = jnp.zeros_like(s_scr)

    ar, ai = a_ref[0], a_ref[1]

    def body(i, carry):
        sr, si = carry
        c = jnp.where(d == 0, i, nb - 1 - i)
        o_ref[0, c] = sr
        o_ref[1, c] = si
        return ar * sr - ai * si + e_ref[0, c], ar * si + ai * sr + e_ref[1, c]

    sr, si = lax.fori_loop(0, nb, body, (s_scr[0], s_scr[1]))
    s_scr[0] = sr
    s_scr[1] = si


def _gelu_tanh(x):
    return 0.5 * x * (1.0 + jnp.tanh(math.sqrt(2.0 / math.pi) * (x + 0.044715 * (x * x * x))))


def _s5_readout_kernel(u_ref, m_ref, s_ref, c_ref, d_ref, o_ref):
    u = u_ref[...]
    y = (jnp.dot(u, m_ref[...], preferred_element_type=F32) + u.astype(F32) * d_ref[...]
         + jnp.dot(s_ref[...].astype(BF16), c_ref[...], preferred_element_type=F32))
    o_ref[...] = _gelu_tanh(y)


def _s5_glu_kernel(z_ref, w_ref, o_ref):
    z = z_ref[...]
    o_ref[...] = (z * jax.nn.sigmoid(jnp.dot(z.astype(BF16), w_ref[...], preferred_element_type=F32))
                  ).astype(o_ref.dtype)


def _s5_mixer(u, tables, w_glu, n_lat, n_ctx):
    m8, b8, c8, a_chunk, d8 = tables
    t = u.shape[0]
    g, c, ln, p = S5_GROUPS, S5_GROUP, S5_CHUNK, S5_STATE
    gb, lanes = g // S5_GB, S5_GB * c
    nc = t // ln
    w = ln * lanes
    sw = 4 * S5_GB * p
    u8 = u.reshape(nc, ln, gb, lanes).transpose(2, 0, 1, 3).reshape(gb, nc, w)
    rt = _row_tile(nc, (208, 80, 16))
    once = pl.Buffered(1)

    e8 = pl.pallas_call(
        _s5_inject_kernel,
        grid=(gb, nc // rt),
        in_specs=[pl.BlockSpec((None, rt, w), lambda b, i: (b, i, 0)),
                  pl.BlockSpec((None, w, sw), lambda b, i: (b, 0, 0), pipeline_mode=once)],
        out_specs=pl.BlockSpec((None, rt, sw), lambda b, i: (b, i, 0)),
        out_shape=jax.ShapeDtypeStruct((gb, nc, sw), F32),
        compiler_params=_cparams(("arbitrary", "arbitrary")),
        name="s5_inject",
    )(u8, b8)
    e = e8.reshape(gb, nc, 2, 2, S5_GB, p).transpose(2, 3, 1, 0, 4, 5).reshape(2, 2, nc, g, p)

    nb = S5_BLOCK
    assert (n_lat // ln) % nb == 0 and n_ctx // ln == nb
    lat_blocks = n_lat // ln // nb

    def blk(d, s):
        return jnp.where(s == 0, lat_blocks, jnp.where(d == 0, s - 1, lat_blocks - s))

    s_in = pl.pallas_call(
        functools.partial(_s5_scan_kernel, nb=nb),
        grid=(2, lat_blocks + 1),
        in_specs=[pl.BlockSpec((None, 2, g, p), lambda d, s: (d, 0, 0, 0)),
                  pl.BlockSpec((None, 2, nb, g, p), lambda d, s: (d, 0, blk(d, s), 0, 0))],
        out_specs=pl.BlockSpec((None, 2, nb, g, p), lambda d, s: (d, 0, blk(d, s), 0, 0)),
        out_shape=jax.ShapeDtypeStruct((2, 2, nc, g, p), F32),
        scratch_shapes=[pltpu.VMEM((2, g, p), F32)],
        compiler_params=_cparams(("arbitrary", "arbitrary")),
        name="s5_scan",
    )(a_chunk, e)
    s8 = s_in.reshape(2, 2, nc, gb, S5_GB, p).transpose(3, 2, 0, 1, 4, 5).reshape(gb, nc, sw)

    z8 = pl.pallas_call(
        _s5_readout_kernel,
        grid=(gb, nc // rt),
        in_specs=[pl.BlockSpec((None, rt, w), lambda b, i: (b, i, 0)),
                  pl.BlockSpec((None, w, w), lambda b, i: (b, 0, 0), pipeline_mode=once),
                  pl.BlockSpec((None, rt, sw), lambda b, i: (b, i, 0)),
                  pl.BlockSpec((None, sw, w), lambda b, i: (b, 0, 0), pipeline_mode=once),
                  pl.BlockSpec((None, 1, w), lambda b, i: (b, 0, 0))],
        out_specs=pl.BlockSpec((None, rt, w), lambda b, i: (b, i, 0)),
        out_shape=jax.ShapeDtypeStruct((gb, nc, w), F32),
        compiler_params=_cparams(("arbitrary", "arbitrary")),
        name="s5_readout",
    )(u8, m8, s8, c8, d8)
    z = z8.reshape(gb, nc, ln, lanes).transpose(1, 2, 0, 3).reshape(t, g * c)

    tm = _row_tile(t, (1280, 640, 256, 128))
    return pl.pallas_call(
        _s5_glu_kernel,
        grid=(t // tm,),
        in_specs=[pl.BlockSpec((tm, g * c), lambda i: (i, 0)),
                  pl.BlockSpec((g * c, g * c), lambda i: (0, 0))],
        out_specs=pl.BlockSpec((tm, g * c), lambda i: (i, 0)),
        out_shape=jax.ShapeDtypeStruct((t, g * c), BF16),
        compiler_params=_cparams(("arbitrary",)),
        name="s5_glu",
    )(z, w_glu)


def _ret_tables(decay_logit):
    scale = HEAD_W ** -0.5
    lg = jax.nn.log_sigmoid(decay_logit.astype(F32))
    lf, lb = lg[0][:, None, None], lg[1][:, None, None]
    i = jnp.arange(RET_CHUNK, dtype=F32)[None, :, None]
    j = jnp.arange(RET_CHUNK, dtype=F32)[None, None, :]
    diff = i - j
    intra = (jnp.where(diff >= 0, jnp.exp(lf * jnp.maximum(diff, 0.0)), 0.0)
             + jnp.where(diff <= 0, jnp.exp(lb * jnp.maximum(-diff, 0.0)), 0.0)) * scale
    ones = jnp.ones((1, 1, HEAD_W), F32)
    q_f = jnp.exp(lf * (i + 1.0)) * ones
    k_f = jnp.exp(lf * (RET_CHUNK - 1.0 - i)) * scale * ones
    c_f = jnp.exp(lf * RET_CHUNK) * jnp.ones((1, RET_CHUNK, HEAD_W), F32)
    q_b = jnp.exp(lb * (RET_CHUNK - i)) * ones
    k_b = jnp.exp(lb * i) * scale * ones
    c_b = jnp.exp(lb * RET_CHUNK) * jnp.ones((1, RET_CHUNK, HEAD_W), F32)
    return intra, jnp.stack([q_f, k_f, c_f]), jnp.stack([q_b, k_b, c_b])


def _ret_state_step(q, k, v, dec_ref, s_scr, h):
    s = s_scr[h]
    qd = (q.astype(F32) * dec_ref[0, h]).astype(BF16)
    kd = (k.astype(F32) * dec_ref[1, h]).astype(BF16)
    o = jnp.dot(qd, s.astype(BF16), preferred_element_type=F32)
    s_scr[h] = s * dec_ref[2, h] + lax.dot_general(kd, v, (((0,), (0,)), ((), ())),
                                                    preferred_element_type=F32)
    return o


def _ret_fwd_kernel(intra_ref, dec_ref, q_ref, k_ref, v_ref, o_ref, s_scr):
    @pl.when(pl.program_id(0) == 0)
    def _():
        s_scr[...] = jnp.zeros_like(s_scr)

    for h in range(RET_HEADS):
        sl = slice(h * HEAD_W, (h + 1) * HEAD_W)
        q, k, v = q_ref[:, sl], k_ref[:, sl], v_ref[:, sl]
        att = lax.dot_general(q, k, (((1,), (1,)), ((), ())), preferred_element_type=F32) * intra_ref[h]
        o = jnp.dot(att.astype(BF16), v, preferred_element_type=F32)
        o_ref[:, sl] = o + _ret_state_step(q, k, v, dec_ref, s_scr, h)


def _ret_bwd_kernel(dec_ref, q_ref, k_ref, v_ref, g_ref, o1_ref, o_ref, s_scr):
    @pl.when(pl.program_id(0) == 0)
    def _():
        s_scr[...] = jnp.zeros_like(s_scr)

    for h in range(RET_HEADS):
        sl = slice(h * HEAD_W, (h + 1) * HEAD_W)
        q, k, v = q_ref[:, sl], k_ref[:, sl], v_ref[:, sl]
        o = o1_ref[:, sl] + _ret_state_step(q, k, v, dec_ref, s_scr, h)
        y = o * lax.rsqrt(jnp.mean(o * o, axis=-1, keepdims=True) + EPS)
        gate = g_ref[:, sl].astype(F32)
        o_ref[:, sl] = (y * (gate * jax.nn.sigmoid(gate))).astype(o_ref.dtype)


def _retention(proj, tables, n_lat, n_ctx):
    intra, dec_f, dec_b = tables
    t = proj.shape[0]
    wd = RET_HEADS * HEAD_W
    ck = RET_CHUNK
    n_lat_c, n_ctx_c = n_lat // ck, n_ctx // ck
    steps = n_lat_c + n_ctx_c

    def fwd_blk(s):
        return jnp.where(s < n_ctx_c, n_lat_c + s, s - n_ctx_c)

    def bwd_blk(s):
        return jnp.where(s < n_ctx_c, n_lat_c + n_ctx_c - 1 - s, n_lat_c - 1 - (s - n_ctx_c))

    def col_spec(col, order):
        return pl.BlockSpec((ck, wd), lambda s: (order(s), col))

    tab_spec = pl.BlockSpec((3, RET_HEADS, ck, HEAD_W), lambda s: (0, 0, 0, 0))
    o1 = pl.pallas_call(
        _ret_fwd_kernel,
        grid=(steps,),
        in_specs=[pl.BlockSpec((RET_HEADS, ck, ck), lambda s: (0, 0, 0)), tab_spec,
                  col_spec(0, fwd_blk), col_spec(1, fwd_blk), col_spec(2, fwd_blk)],
        out_specs=col_spec(0, fwd_blk),
        out_shape=jax.ShapeDtypeStruct((t, wd), F32),
        scratch_shapes=[pltpu.VMEM((RET_HEADS, HEAD_W, HEAD_W), F32)],
        compiler_params=_cparams(("arbitrary",)),
        name="retention_fwd",
    )(intra, dec_f, proj, proj, proj)
    return pl.pallas_call(
        _ret_bwd_kernel,
        grid=(steps,),
        in_specs=[tab_spec, col_spec(0, bwd_blk), col_spec(1, bwd_blk), col_spec(2, bwd_blk),
                  col_spec(3, bwd_blk), col_spec(0, bwd_blk)],
        out_specs=col_spec(0, bwd_blk),
        out_shape=jax.ShapeDtypeStruct((t, wd), BF16),
        scratch_shapes=[pltpu.VMEM((RET_HEADS, HEAD_W, HEAD_W), F32)],
        compiler_params=_cparams(("arbitrary",)),
        name="retention_bwd",
    )(dec_b, proj, proj, proj, proj, o1)


def _rope_tables(n_lat, n_ctx):
    n_freq = A_QK_DIM // 4
    freq = ROPE_BASE ** (-jnp.arange(n_freq, dtype=F32) / n_freq)
    rows = n_lat // GRID_W
    ang_row = jnp.arange(rows, dtype=F32)[:, None] * freq
    ang_col = jnp.arange(GRID_W, dtype=F32)[:, None] * freq
    reps = HEAD_W // A_QK_DIM

    def table(fn, ctx_value):
        r = jnp.broadcast_to(jnp.tile(fn(ang_row), (1, 2))[:, None, :], (rows, GRID_W, 2 * n_freq))
        c = jnp.broadcast_to(jnp.tile(fn(ang_col), (1, 2))[None, :, :], (rows, GRID_W, 2 * n_freq))
        lat = jnp.tile(jnp.concatenate([r, c], axis=-1).reshape(n_lat, A_QK_DIM), (1, reps))
        return jnp.concatenate([lat, jnp.full((n_ctx, HEAD_W), ctx_value, F32)], axis=0)

    return table(jnp.cos, 1.0), table(jnp.sin, 0.0)


def kernel(x, c, ctx, c_ctx, ada_w, ada_b, norm1_w, norm2_w, ffn_w13, ffn_w2, e_w_in, e_w_out, diff_lq1, diff_lk1, diff_lq2, diff_lk2, diff_subln_w, na_rpb, o_w_in, o_w_out, s5_lam_re, s5_lam_im, s5_b_re, s5_b_im, s5_c_re, s5_c_im, s5_log_step, s5_d, s5_w_glu, ret_decay_logit, final_norm_w):
    bsz, n_lat, d = x.shape
    n_ctx = ctx.shape[1]
    depth = ada_w.shape[0]
    assert bsz == 1
    xs = jnp.concatenate([x[0], ctx[0]], axis=0)
    mods_all = _mods(c, c_ctx, ada_w, ada_b)
    cos, sin = _rope_tables(n_lat, n_ctx)
    ret_w = RET_HEADS * HEAD_W
    ffn_w13_bf16 = ffn_w13.astype(BF16)
    e_w_in_bf16 = e_w_in.astype(BF16)

    for i in range(depth):
        compute_ctx = i != depth - 1
        mods = mods_all[i]
        j = i // 2
        if i % 2 == 0:
            lambda_init = 0.8 - 0.6 * math.exp(-0.3 * i)
            qkv = _inproj_even(xs, norm1_w[i], mods, n_lat, e_w_in_bf16, j, cos, sin)
            lam_args = (diff_lq1[j], diff_lk1[j], diff_lq2[j], diff_lk2[j], diff_subln_w[j])
            o_a = _diff_attn(qkv, *lam_args, n_lat, n_ctx, lambda_init)
            o_b = _natten(qkv, _na_bias_tables(na_rpb[j], n_lat // GRID_W), n_lat, n_ctx)
            if compute_ctx:
                o_a = _diff_attn(qkv, *lam_args, n_lat, n_ctx, lambda_init, prev=o_a)
                o_b = _ctx_attn(qkv, o_b, n_lat, n_ctx)
            xs = _gated_residual([o_a, o_b], e_w_out, xs, mods, n_lat, 2, "outproj_even", layer=j)
        else:
            w_in = o_w_in[j]
            w_in = jnp.concatenate([w_in[:, S5_CH:], w_in[:, :S5_CH]], axis=1).astype(BF16)
            proj = _inproj_odd(xs, norm1_w[i], mods, n_lat, w_in)
            s5_tabs = _s5_tables(s5_lam_re[j], s5_lam_im[j], s5_b_re[j], s5_b_im[j], s5_c_re[j], s5_c_im[j],
                                 s5_log_step[j], s5_d[j])
            y_c = _s5_mixer(proj[:, 4 * ret_w:], s5_tabs, s5_w_glu[j].astype(BF16), n_lat, n_ctx)
            y_d = _retention(proj, _ret_tables(ret_decay_logit[j]), n_lat, n_ctx)
            xs = _gated_residual([y_c, y_d], o_w_out, xs, mods, n_lat, 2, "outproj_odd", layer=j)
        a = _ffn_up(xs, norm2_w[i], mods, n_lat, ffn_w13_bf16, i)
        xs = _gated_residual([a], ffn_w2[i].astype(BF16), xs, mods, n_lat, 5, "ffn_down")
    return _final_norm(xs, final_norm_w, n_lat)[None]
```

```python
import functools
import math

import jax
import jax.numpy as jnp
import numpy as np
from jax import lax
from jax.experimental import pallas as pl
from jax.experimental.pallas import tpu as pltpu

F32 = jnp.float32
BF16 = jnp.bfloat16

GRID_W = 64
A_HEADS = 8
A_QK_DIM = 64
HEAD_W = 128
B_HEADS = 8
WIN_R = 8
WIN_C = 16
NA_ROWS = 4
S5_CH = 512
S5_GROUP = 16
S5_GROUPS = 32
S5_STATE = 64
S5_CHUNK = 16
S5_BLOCK = 16
RET_HEADS = 12
RET_CHUNK = 128
ROPE_BASE = 10000.0
EPS = 1e-6
NEG_INF = -1e30
VMEM_LIMIT = 56 * 1024 * 1024
N_MOD = 8
NORM_ROW_CHUNK = 128


def _cparams(sem):
    return pltpu.CompilerParams(dimension_semantics=sem, vmem_limit_bytes=VMEM_LIMIT)


def _row_tile(t, candidates):
    for c in candidates:
        if t % c == 0:
            return c
    raise ValueError(f"no row tile for {t}")


def _pick_mod(mods_ref, idx, is_ctx):
    return jnp.where(is_ctx, mods_ref[1, idx:idx + 1, :], mods_ref[0, idx:idx + 1, :])


def _is_ctx_rows(tm, n_lat, axis):
    row = pl.program_id(axis) * tm + lax.broadcasted_iota(jnp.int32, (tm, 1), 0)
    return row >= n_lat


def _mods_kernel(s_ref, w_ref, b_ref, o_ref):
    s = s_ref[...]
    s = s * jax.nn.sigmoid(s)
    o_ref[...] = jnp.dot(s, w_ref[...], preferred_element_type=F32,
                         precision=lax.Precision.HIGHEST) + b_ref[...]


def _mods(c, c_ctx, ada_w, ada_b):
    depth, d, w6 = ada_w.shape
    s = jnp.zeros((8, d), F32).at[0].set(c[0]).at[1].set(c_ctx)
    tn = 1024
    out = pl.pallas_call(
        _mods_kernel,
        grid=(depth, w6 // tn),
        in_specs=[pl.BlockSpec((8, d), lambda l, j: (0, 0)),
                  pl.BlockSpec((None, d, tn), lambda l, j: (l, 0, j)),
                  pl.BlockSpec((None, 1, tn), lambda l, j: (l, 0, j))],
        out_specs=pl.BlockSpec((None, 8, tn), lambda l, j: (l, 0, j)),
        out_shape=jax.ShapeDtypeStruct((depth, 8, w6), F32),
        compiler_params=_cparams(("arbitrary", "arbitrary")),
        name="ada_mods",
    )(s, ada_w, ada_b.reshape(depth, 1, w6))
    m = out[:, :2].reshape(depth, 2, 6, d)
    return jnp.pad(m, ((0, 0), (0, 0), (0, N_MOD - 6), (0, 0)))


def _normmod_prologue(x_ref, nw_ref, mods_ref, h_scr, *, n_lat, tm, shift_idx, scale_idx):
    @pl.when(pl.program_id(1) == 0)
    def _():
        rc = NORM_ROW_CHUNK
        assert tm % rc == 0 and n_lat % rc == 0

        def chunk(c, carry):
            r0 = pl.multiple_of(c * rc, rc)
            seg = (pl.program_id(0) * tm + r0 >= n_lat).astype(jnp.int32)
            gain = nw_ref[...] * (1.0 + mods_ref[seg, scale_idx:scale_idx + 1, :])
            shift = mods_ref[seg, shift_idx:shift_idx + 1, :]
            x = x_ref[pl.ds(r0, rc), :]
            r = lax.rsqrt(jnp.mean(x * x, axis=-1, keepdims=True) + EPS)
            h_scr[pl.ds(r0, rc), :] = (x * r * gain + shift).astype(h_scr.dtype)
            return carry

        lax.fori_loop(0, tm // rc, chunk, 0)


def _norm_proj_call(kern, x, norm_w, mods, n_lat, shift_idx, scale_idx, w_args, w_specs, extra_args, extra_specs,
                    n_out, tn, tm, name, **kern_kw):
    t, d = x.shape
    norm = dict(n_lat=n_lat, tm=tm, shift_idx=shift_idx, scale_idx=scale_idx)
    return pl.pallas_call(
        functools.partial(kern, norm=norm, **kern_kw),
        grid=(t // tm, n_out // tn),
        in_specs=[pl.BlockSpec((tm, d), lambda i, j: (i, 0)),
                  pl.BlockSpec((1, d), lambda i, j: (0, 0)),
                  pl.BlockSpec((2, N_MOD, d), lambda i, j: (0, 0, 0))] + w_specs + extra_specs,
        out_specs=pl.BlockSpec((tm, tn), lambda i, j: (i, j)),
        out_shape=jax.ShapeDtypeStruct((t, n_out), BF16),
        scratch_shapes=[pltpu.VMEM((tm, d), BF16)],
        compiler_params=_cparams(("arbitrary", "arbitrary")),
        name=name,
    )(x, norm_w.reshape(1, d), mods, *w_args, *extra_args)


def _final_norm_kernel(x_ref, w_ref, o_ref):
    x = x_ref[...]
    o_ref[...] = x * lax.rsqrt(jnp.mean(x * x, axis=-1, keepdims=True) + EPS) * w_ref[...]


def _final_norm(x, w, n_lat):
    d = x.shape[1]
    tm = _row_tile(n_lat, (512, 256, 128))
    return pl.pallas_call(
        _final_norm_kernel,
        grid=(n_lat // tm,),
        in_specs=[pl.BlockSpec((tm, d), lambda i: (i, 0)),
                  pl.BlockSpec((1, d), lambda i: (0, 0))],
        out_specs=pl.BlockSpec((tm, d), lambda i: (i, 0)),
        out_shape=jax.ShapeDtypeStruct((n_lat, d), F32),
        compiler_params=_cparams(("arbitrary",)),
        name="final_norm",
    )(x, w.reshape(1, d))


def _rope_store(acc, cos, sin, o_ref, scale):
    first_half = (lax.broadcasted_iota(jnp.int32, (1, HEAD_W), 1) % 32) < 16
    for c in range(acc.shape[1] // HEAD_W):
        x = acc[:, c * HEAD_W:(c + 1) * HEAD_W]
        rot = jnp.where(first_half, -pltpu.roll(x, HEAD_W - 16, 1), pltpu.roll(x, 16, 1))
        o_ref[:, c * HEAD_W:(c + 1) * HEAD_W] = ((x * cos + rot * sin) * scale).astype(o_ref.dtype)


def _resident_bf16(w_ref, w_scr):
    @pl.when(pl.program_id(1) == 0)
    def _():
        w_scr[...] = w_ref[...].astype(BF16)

    return w_scr


def _inproj_even_kernel(x_ref, nw_ref, mods_ref, w_ref, cos_ref, sin_ref, o_ref, h_scr, *, norm, a_scale, b_scale):
    _normmod_prologue(x_ref, nw_ref, mods_ref, h_scr, **norm)
    j = pl.program_id(1)
    acc = jnp.dot(h_scr[...], w_ref[...], preferred_element_type=F32)

    @pl.when(j == 0)
    def _():
        _rope_store(acc, cos_ref[...], sin_ref[...], o_ref, a_scale)

    @pl.when(j == 1)
    def _():
        _rope_store(acc, cos_ref[...], sin_ref[...], o_ref, 1.0)

    @pl.when(j == 3)
    def _():
        o_ref[...] = (acc * b_scale).astype(o_ref.dtype)

    @pl.when((j == 2) | (j > 3))
    def _():
        o_ref[...] = acc.astype(o_ref.dtype)


def _inproj_even(x, norm_w, mods, n_lat, w, layer, cos, sin):
    t, d = x.shape
    n = w.shape[2]
    tn = n // 6
    tm = _row_tile(t, (1280, 640, 256, 128))
    rope_spec = pl.BlockSpec((tm, HEAD_W), lambda i, j: (i, 0))
    return _norm_proj_call(
        _inproj_even_kernel, x, norm_w, mods, n_lat, 0, 1,
        [w], [pl.BlockSpec((None, d, tn), lambda i, j: (layer, 0, j))],
        [cos, sin], [rope_spec, rope_spec], n, tn, tm, "inproj_even",
        a_scale=A_QK_DIM ** -0.5 * math.log2(math.e), b_scale=HEAD_W ** -0.5)


def _inproj_odd_kernel(x_ref, nw_ref, mods_ref, w_ref, o_ref, h_scr, *, norm):
    _normmod_prologue(x_ref, nw_ref, mods_ref, h_scr, **norm)
    o_ref[...] = jnp.dot(h_scr[...], w_ref[...], preferred_element_type=F32).astype(o_ref.dtype)


def _inproj_odd(x, norm_w, mods, n_lat, w):
    t, d = x.shape
    n = w.shape[1]
    tn = 512
    tm = _row_tile(t, (1280, 640, 256, 128))
    return _norm_proj_call(
        _inproj_odd_kernel, x, norm_w, mods, n_lat, 0, 1,
        [w], [pl.BlockSpec((d, tn), lambda i, j: (0, j))], [], [], n, tn, tm, "inproj_odd")


def _ffn_up_kernel(x_ref, nw_ref, mods_ref, w1_ref, w3_ref, o_ref, h_scr, *, norm):
    _normmod_prologue(x_ref, nw_ref, mods_ref, h_scr, **norm)
    h = h_scr[...]
    a = jnp.dot(h, w1_ref[...], preferred_element_type=F32)
    b = jnp.dot(h, w3_ref[...], preferred_element_type=F32)
    o_ref[...] = (a * jax.nn.sigmoid(a) * b).astype(o_ref.dtype)


def _ffn_up(x, norm_w, mods, n_lat, w13, layer):
    t, d = x.shape
    d_ff = w13.shape[2] // 2
    tn = 512
    nj = d_ff // tn
    tm = _row_tile(t, (1280, 640, 256, 128))
    return _norm_proj_call(
        _ffn_up_kernel, x, norm_w, mods, n_lat, 3, 4,
        [w13, w13], [pl.BlockSpec((None, d, tn), lambda i, j: (layer, 0, j)),
                     pl.BlockSpec((None, d, tn), lambda i, j: (layer, 0, j + nj))],
        [], [], d_ff, tn, tm, "ffn_up")


def _gated_residual_kernel(*refs, n_a, n_lat, tm, gate_idx):
    a_refs = refs[:n_a]
    w_ref, x_ref, mods_ref, o_ref = refs[n_a:n_a + 4]
    if w_ref.dtype != BF16:
        w_ref = _resident_bf16(w_ref, refs[n_a + 4])
    k0 = 0
    y = None
    for a_ref in a_refs:
        kk = a_ref.shape[1]
        part = jnp.dot(a_ref[...], w_ref[k0:k0 + kk, :], preferred_element_type=F32)
        y = part if y is None else y + part
        k0 += kk
    gate = _pick_mod(mods_ref, gate_idx, _is_ctx_rows(tm, n_lat, 1))
    o_ref[...] = x_ref[...] + gate * y


def _gated_residual(a_list, w, x, mods, n_lat, gate_idx, name, layer=None):
    t, d = x.shape
    k = w.shape[-2]
    tn = 512
    tm = _row_tile(t, (640, 256, 128))
    n_a = len(a_list)
    in_specs = [pl.BlockSpec((tm, a.shape[1]), lambda j, i: (i, 0)) for a in a_list]
    w_spec = (pl.BlockSpec((k, tn), lambda j, i: (0, j)) if layer is None
              else pl.BlockSpec((None, k, tn), lambda j, i: (layer, 0, j)))
    in_specs += [w_spec,
                 pl.BlockSpec((tm, tn), lambda j, i: (i, j)),
                 pl.BlockSpec((2, N_MOD, tn), lambda j, i: (0, 0, j))]
    return pl.pallas_call(
        functools.partial(_gated_residual_kernel, n_a=n_a, n_lat=n_lat, tm=tm, gate_idx=gate_idx),
        grid=(d // tn, t // tm),
        in_specs=in_specs,
        out_specs=pl.BlockSpec((tm, tn), lambda j, i: (i, j)),
        out_shape=jax.ShapeDtypeStruct((t, d), F32),
        scratch_shapes=[] if w.dtype == BF16 else [pltpu.VMEM((k, tn), BF16)],
        input_output_aliases={n_a + 1: 0},
        compiler_params=_cparams(("arbitrary", "arbitrary")),
        name=name,
    )(*a_list, w, x, mods)


def _softmax_block(qs, k, v):
    s = lax.dot_general(qs, k, (((1,), (1,)), ((), ())), preferred_element_type=F32)
    m = jnp.max(s, axis=-1, keepdims=True)
    p = jnp.exp(s - m)
    l = jnp.sum(p, axis=-1, keepdims=True)
    acc = jnp.dot(p.astype(v.dtype), v, preferred_element_type=F32)
    return m, l, acc


DIFF_PV_ROWS = 512
DIFF_SOFTMAX_ROWS = 64
DIFF_EXP_DTYPE = BF16


def _diff_attn_kernel(*refs, tq, tk, n_ctx, n_chunks, lambda_init, pv_rows_max):
    if n_chunks > 1:
        (lq1, lk1, lq2, lk2, subw, q_ref, kc_ref, vc_ref, k_ref, v_ref, o_ref,
         qs_scr, sa_scr, mca_scr, p_scr, m_scr, l_scr, alpha_scr, acc_scr, sb_scr, mcb_scr) = refs
        buf_b = (sb_scr, mcb_scr)
    elif n_chunks == 1:
        (lq1, lk1, lq2, lk2, subw, q_ref, kc_ref, vc_ref, k_ref, v_ref, o_ref,
         qs_scr, sa_scr, mca_scr, p_scr, m_scr, l_scr, alpha_scr, acc_scr) = refs
    else:
        (lq1, lk1, lq2, lk2, subw, q_ref, kc_ref, vc_ref, o_ref,
         qs_scr, sa_scr, mca_scr, p_scr, m_scr, l_scr, alpha_scr, acc_scr) = refs
    buf_a = (sa_scr, mca_scr)
    nt = (((1,), (1,)), ((), ()))
    rows_all = 2 * tq
    pv_rows = min(pv_rows_max, rows_all)
    sm_rows = min(DIFF_SOFTMAX_ROWS, pv_rows)
    q = q_ref[...]
    comp1 = lax.broadcasted_iota(jnp.int32, (1, HEAD_W), 1) < A_QK_DIM
    zero = jnp.zeros_like(q)
    qs_scr[0:tq, :] = jnp.where(comp1, q, zero)
    qs_scr[tq:2 * tq, :] = jnp.where(comp1, zero, q)

    def scores(buf, rows, keys):
        s_buf, mc_buf = buf
        s = lax.dot_general(qs_scr[rows, :], keys, nt, preferred_element_type=F32)
        width = s.shape[1]
        s_buf[rows, 0:width] = s
        mp = s[:, 0:HEAD_W]
        for c in range(1, width // HEAD_W):
            mp = jnp.maximum(mp, s[:, c * HEAD_W:(c + 1) * HEAD_W])
        mc_buf[rows, :] = jnp.broadcast_to(jnp.max(mp, axis=-1, keepdims=True), mp.shape)

    def softmax_rows(buf, rows, n_tiles, first):
        s_buf, mc_buf = buf
        m_cur = mc_buf[rows, :]
        if first:
            m_new = m_cur
        else:
            m_prev = m_scr[rows, :]
            m_new = jnp.maximum(m_prev, m_cur)
        lp = None
        for c in range(n_tiles):
            pc = jnp.exp2((s_buf[rows, c * HEAD_W:(c + 1) * HEAD_W] - m_new).astype(DIFF_EXP_DTYPE))
            p_scr[rows, c * HEAD_W:(c + 1) * HEAD_W] = pc.astype(BF16)
            lp = pc if lp is None else lp + pc
        l_cur = jnp.sum(lp.astype(F32), axis=-1, keepdims=True)
        if first:
            l_scr[rows, :] = jnp.broadcast_to(l_cur, m_new.shape)
        else:
            alpha = jnp.exp2(m_prev - m_new)
            alpha_scr[rows, :] = alpha
            l_scr[rows, :] = alpha * l_scr[rows, :] + l_cur
        m_scr[rows, :] = m_new

    def softmax_pv(buf, width, load_v, first, after_group=None):
        for r in range(rows_all // pv_rows):
            for r2 in range(pv_rows // sm_rows):
                r0 = r * pv_rows + r2 * sm_rows
                softmax_rows(buf, slice(r0, r0 + sm_rows), width // HEAD_W, first)
            rows = slice(r * pv_rows, (r + 1) * pv_rows)
            pv = jnp.dot(p_scr[rows, 0:width], load_v(), preferred_element_type=F32)
            acc_scr[rows, :] = pv if first else alpha_scr[rows, :] * acc_scr[rows, :] + pv
            if after_group is not None:
                after_group(rows)

    def all_groups(fn):
        for r in range(rows_all // pv_rows):
            fn(slice(r * pv_rows, (r + 1) * pv_rows))

    all_groups(lambda rows: scores(buf_a, rows, kc_ref[...]))
    softmax_pv(buf_a, n_ctx, lambda: vc_ref[...], True)

    def qk_rows(j, buf, rows):
        off = pl.multiple_of(j * tk, tk)
        scores(buf, rows, k_ref[pl.ds(off, tk), :])

    def soft_pv(j, buf, j_next=None, buf_next=None):
        off = pl.multiple_of(j * tk, tk)
        after = None if j_next is None else (lambda rows: qk_rows(j_next, buf_next, rows))
        softmax_pv(buf, tk, lambda: v_ref[pl.ds(off, tk), :], False, after)

    if n_chunks == 1:
        all_groups(lambda rows: qk_rows(0, buf_a, rows))
        soft_pv(0, buf_a)
    elif n_chunks > 1:
        all_groups(lambda rows: qk_rows(0, buf_a, rows))

        def pair(j2, carry):
            soft_pv(2 * j2, buf_a, 2 * j2 + 1, buf_b)
            soft_pv(2 * j2 + 1, buf_b, 2 * j2 + 2, buf_a)
            return carry

        lax.fori_loop(0, n_chunks // 2 - 1, pair, 0)
        soft_pv(n_chunks - 2, buf_a, n_chunks - 1, buf_b)
        soft_pv(n_chunks - 1, buf_b)

    lam = (jnp.exp(jnp.sum(lq1[...] * lk1[...], axis=-1, keepdims=True))
           - jnp.exp(jnp.sum(lq2[...] * lk2[...], axis=-1, keepdims=True)) + lambda_init)
    o1 = acc_scr[0:tq, :] / l_scr[0:tq, :]
    o2 = acc_scr[tq:2 * tq, :] / l_scr[tq:2 * tq, :]
    o = o1 - lam * o2
    y = o * lax.rsqrt(jnp.mean(o * o, axis=-1, keepdims=True) + EPS) * subw[...]
    o_ref[...] = (y * (1.0 - lambda_init)).astype(o_ref.dtype)


def _diff_attn(qkv, lq1, lk1, lq2, lk2, subw, n_lat, n_ctx, lambda_init, prev=None,
               tiles=(1024, 1024, DIFF_PV_ROWS)):
    t = qkv.shape[0]
    use_lat = prev is None
    ctx_blk = n_lat // n_ctx
    small = [lq1.reshape(1, -1), lk1.reshape(1, -1), lq2.reshape(1, -1), lk2.reshape(1, -1), subw.reshape(1, -1)]
    small_specs = [pl.BlockSpec(a.shape, lambda h, qi: (0, 0)) for a in small]
    kc_spec = pl.BlockSpec((n_ctx, HEAD_W), lambda h, qi: (ctx_blk, A_HEADS + h))
    vc_spec = pl.BlockSpec((n_ctx, HEAD_W), lambda h, qi: (ctx_blk, 2 * A_HEADS + h))
    if use_lat:
        tq = _row_tile(n_lat, tuple(c for c in (1024, 512, 256, 128) if c <= tiles[0]))
        tk = _row_tile(n_lat, tuple(c for c in (2048, 1024, 512, 256, 128) if c <= tiles[1]))
        n_chunks = n_lat // tk
        assert n_chunks == 1 or n_chunks % 2 == 0
        grid = (A_HEADS, n_lat // tq)
        in_specs = small_specs + [
            pl.BlockSpec((tq, HEAD_W), lambda h, qi: (qi, h)), kc_spec, vc_spec,
            pl.BlockSpec((n_lat, HEAD_W), lambda h, qi: (0, A_HEADS + h), pipeline_mode=pl.Buffered(1)),
            pl.BlockSpec((n_lat, HEAD_W), lambda h, qi: (0, 2 * A_HEADS + h), pipeline_mode=pl.Buffered(1))]
        args = small + [qkv, qkv, qkv, qkv, qkv]
        out_spec = pl.BlockSpec((tq, HEAD_W), lambda h, qi: (qi, h))
        aliases = {}
    else:
        tq, tk, n_chunks = n_ctx, 0, 0
        grid = (A_HEADS, 1)
        in_specs = small_specs + [
            pl.BlockSpec((tq, HEAD_W), lambda h, qi: (ctx_blk, h)), kc_spec, vc_spec,
            pl.BlockSpec(memory_space=pl.ANY)]
        args = small + [qkv, qkv, qkv, prev]
        out_spec = pl.BlockSpec((tq, HEAD_W), lambda h, qi: (ctx_blk, h))
        aliases = {len(args) - 1: 0}

    sw = max(tk, n_ctx)
    kern = functools.partial(_diff_attn_kernel, tq=tq, tk=tk, n_ctx=n_ctx, n_chunks=n_chunks, pv_rows_max=tiles[2],
                             lambda_init=lambda_init)
    if not use_lat:
        inner = kern

        def kern(*refs):
            n_in = len(args)
            inner(*refs[:n_in - 1], *refs[n_in:])

    return pl.pallas_call(
        kern,
        grid=grid,
        in_specs=in_specs,
        out_specs=out_spec,
        out_shape=jax.ShapeDtypeStruct((t, A_HEADS * HEAD_W), BF16),
        scratch_shapes=[pltpu.VMEM((2 * tq, HEAD_W), BF16),
                        pltpu.VMEM((2 * tq, sw), F32),
                        pltpu.VMEM((2 * tq, HEAD_W), F32),
                        pltpu.VMEM((2 * tq, sw), BF16),
                        pltpu.VMEM((2 * tq, HEAD_W), F32),
                        pltpu.VMEM((2 * tq, HEAD_W), F32),
                        pltpu.VMEM((2 * tq, HEAD_W), F32),
                        pltpu.VMEM((2 * tq, HEAD_W), F32)]
        + ([pltpu.VMEM((2 * tq, sw), F32),
            pltpu.VMEM((2 * tq, HEAD_W), F32)] if n_chunks > 1 else []),
        input_output_aliases=aliases,
        compiler_params=_cparams(("arbitrary", "arbitrary")),
        name="diff_attn" if use_lat else "diff_attn_ctx",
    )(*args)


def _na_bias_tables(rpb, rows):
    nblk = rows // NA_ROWS
    wr = min(WIN_R, rows)
    nh = rpb.shape[0]
    rq = jnp.arange(NA_ROWS)[:, None, None]
    slot = jnp.arange(3)[None, :, None]
    rk = jnp.arange(NA_ROWS)[None, None, :]
    row_sel, row_ok = [], []
    for b, dup in ((0, 0), (1, -1), (nblk - 1, 2)):
        r = NA_ROWS * b + rq
        r0 = jnp.clip(r - wr // 2, 0, rows - wr)
        rkey = NA_ROWS * (b - 1 + slot) + rk
        ok = (rkey >= r0) & (rkey < r0 + wr) & (slot != dup)
        drow = jnp.broadcast_to(rkey - r + (WIN_R - 1), ok.shape)
        row_sel.append((drow[..., None] == jnp.arange(2 * WIN_R - 1)) & ok[..., None])
        row_ok.append(ok)
    row_sel = jnp.stack(row_sel).astype(F32)
    row_ok = jnp.stack(row_ok)
    jq = jnp.arange(GRID_W)[:, None]
    jk = jnp.arange(GRID_W)[None, :]
    c0 = jnp.clip(jq - WIN_C // 2, 0, GRID_W - WIN_C)
    col_ok = (jk >= c0) & (jk < c0 + WIN_C)
    dcol = jnp.clip(jk - jq + (WIN_C - 1), 0, 2 * WIN_C - 2)
    col_sel = (dcol[..., None] == jnp.arange(2 * WIN_C - 1)).astype(F32)
    bias = jnp.einsum('vasbr,hrc,qkc->hvaqsbk', row_sel, rpb.astype(F32), col_sel,
                      precision=lax.Precision.HIGHEST)
    valid = row_ok[None, :, :, None, :, :, None] & col_ok[None, None, None, :, None, None, :]
    tab = jnp.where(valid, bias, NEG_INF)
    return tab.reshape(nh, 3, NA_ROWS * GRID_W, 3 * NA_ROWS * GRID_W)


def _natten_kernel(tab_ref, q_ref, k0_ref, k1_ref, k2_ref, v0_ref, v1_ref, v2_ref, kc_ref, vc_ref, o_ref):
    q = q_ref[...]
    nt = (((1,), (1,)), ((), ()))
    blk = q.shape[0]
    s_loc = [lax.dot_general(q, k_ref[...], nt, preferred_element_type=F32)
             + tab_ref[:, i * blk:(i + 1) * blk] for i, k_ref in enumerate((k0_ref, k1_ref, k2_ref))]
    s_ctx = lax.dot_general(q, kc_ref[...], nt, preferred_element_type=F32)
    m = jnp.max(s_ctx, axis=-1, keepdims=True)
    for s in s_loc:
        m = jnp.maximum(m, jnp.max(s, axis=-1, keepdims=True))
    p = jnp.exp(s_ctx - m)
    l = jnp.sum(p, axis=-1, keepdims=True)
    acc = jnp.dot(p.astype(BF16), vc_ref[...], preferred_element_type=F32)
    for s, v_ref in zip(s_loc, (v0_ref, v1_ref, v2_ref)):
        p = jnp.exp(s - m)
        l = l + jnp.sum(p, axis=-1, keepdims=True)
        acc = acc + jnp.dot(p.astype(BF16), v_ref[...], preferred_element_type=F32)
    o_ref[...] = (acc / l).astype(o_ref.dtype)


def _natten(qkv, tabs, n_lat, n_ctx):
    t = qkv.shape[0]
    blk = NA_ROWS * GRID_W
    nblk = n_lat // blk
    assert nblk >= 3 and blk == n_ctx
    qo, ko, vo = 3 * A_HEADS, 3 * A_HEADS + B_HEADS, 3 * A_HEADS + 2 * B_HEADS
    ctx_blk = n_lat // n_ctx

    def kv_spec(off, shift):
        return pl.BlockSpec((blk, HEAD_W), lambda h, b: (jnp.clip(b + shift, 0, nblk - 1), off + h))

    return pl.pallas_call(
        _natten_kernel,
        grid=(B_HEADS, nblk),
        in_specs=[pl.BlockSpec((None, None, blk, 3 * blk),
                               lambda h, b: (h, jnp.where(b == 0, 0, jnp.where(b == nblk - 1, 2, 1)), 0, 0)),
                  pl.BlockSpec((blk, HEAD_W), lambda h, b: (b, qo + h)),
                  kv_spec(ko, -1), kv_spec(ko, 0), kv_spec(ko, 1),
                  kv_spec(vo, -1), kv_spec(vo, 0), kv_spec(vo, 1),
                  pl.BlockSpec((n_ctx, HEAD_W), lambda h, b: (ctx_blk, ko + h)),
                  pl.BlockSpec((n_ctx, HEAD_W), lambda h, b: (ctx_blk, vo + h))],
        out_specs=pl.BlockSpec((blk, HEAD_W), lambda h, b: (b, h)),
        out_shape=jax.ShapeDtypeStruct((t, B_HEADS * HEAD_W), BF16),
        compiler_params=_cparams(("arbitrary", "arbitrary")),
        name="natten",
    )(tabs, qkv, qkv, qkv, qkv, qkv, qkv, qkv, qkv, qkv)


def _ctx_attn_kernel(q_ref, k_ref, v_ref, prev_ref, o_ref):
    del prev_ref
    _, l, acc = _softmax_block(q_ref[...], k_ref[...], v_ref[...])
    o_ref[...] = (acc / l).astype(o_ref.dtype)


def _ctx_attn(qkv, prev, n_lat, n_ctx):
    qo, ko, vo = 3 * A_HEADS, 3 * A_HEADS + B_HEADS, 3 * A_HEADS + 2 * B_HEADS
    ctx_blk = n_lat // n_ctx
    return pl.pallas_call(
        _ctx_attn_kernel,
        grid=(B_HEADS,),
        in_specs=[pl.BlockSpec((n_ctx, HEAD_W), lambda h: (ctx_blk, qo + h)),
                  pl.BlockSpec((n_ctx, HEAD_W), lambda h: (ctx_blk, ko + h)),
                  pl.BlockSpec((n_ctx, HEAD_W), lambda h: (ctx_blk, vo + h)),
                  pl.BlockSpec(memory_space=pl.ANY)],
        out_specs=pl.BlockSpec((n_ctx, HEAD_W), lambda h: (ctx_blk, h)),
        out_shape=jax.ShapeDtypeStruct(prev.shape, prev.dtype),
        input_output_aliases={3: 0},
        compiler_params=_cparams(("arbitrary",)),
        name="ctx_attn",
    )(qkv, qkv, qkv, prev)


def _s5_tables(lam_re, lam_im, b_re, b_im, c_re, c_im, log_step, d_skip):
    hp = lax.Precision.HIGHEST
    ln = S5_CHUNK
    g, p, c = S5_GROUPS, S5_STATE, S5_GROUP
    lr, li = lam_re.astype(F32), lam_im.astype(F32)
    dt = jnp.exp(log_step.astype(F32))[:, :, None]
    mag = jnp.exp(lr * dt)
    ar, ai = mag * jnp.cos(li * dt), mag * jnp.sin(li * dt)
    den = lr * lr + li * li
    nr, ni = ar - 1.0, ai
    fr = (nr * lr + ni * li) / den
    fi = (ni * lr - nr * li) / den
    br_, bi_ = b_re.astype(F32), b_im.astype(F32)
    bbr = fr[..., None] * br_ - fi[..., None] * bi_
    bbi = fr[..., None] * bi_ + fi[..., None] * br_
    lag = jnp.arange(ln + 1, dtype=F32)[:, None, None, None]
    magl = jnp.exp(lr * dt * lag)
    pr, pi_ = magl * jnp.cos(li * dt * lag), magl * jnp.sin(li * dt * lag)
    wr = pr[..., None] * bbr - pi_[..., None] * bbi
    wi = pr[..., None] * bbi + pi_[..., None] * bbr
    cr, ci = c_re.astype(F32), c_im.astype(F32)
    kern = (jnp.einsum('dgcp,ldgpe->ldgce', cr, wr[:ln], precision=hp)
            - jnp.einsum('dgcp,ldgpe->ldgce', ci, wi[:ln], precision=hp))
    lag = np.arange(ln)[:, None, None]
    s_idx = np.arange(ln)[None, :, None]
    t_idx = np.arange(ln)[None, None, :]
    sel_f = jnp.asarray(t_idx - s_idx == lag, F32)
    sel_b = jnp.asarray(s_idx - t_idx == lag, F32)
    kfb = (jnp.einsum('lst,lgce->stgce', sel_f, kern[:, 0], precision=hp)
           + jnp.einsum('lst,lgce->stgce', sel_b, kern[:, 1], precision=hp))
    m_tab = kfb.transpose(2, 0, 4, 1, 3).reshape(g, ln * c, ln * c)
    wf_r, wf_i = wr[:ln, 0][::-1], wi[:ln, 0][::-1]
    wb_r, wb_i = wr[:ln, 1], wi[:ln, 1]

    def inj(w):
        return w.transpose(1, 0, 3, 2).reshape(g, ln * c, p)

    b_tab = jnp.concatenate([inj(wf_r), inj(wf_i), inj(wb_r), inj(wb_i)], axis=-1)
    pf_r, pf_i = pr[1:ln + 1, 0], pi_[1:ln + 1, 0]
    pb_r, pb_i = pr[1:ln + 1, 1][::-1], pi_[1:ln + 1, 1][::-1]

    def rd(pw_r, pw_i, cre, cim):
        car = cre[None] * pw_r[:, :, None, :] - cim[None] * pw_i[:, :, None, :]
        cai = cre[None] * pw_i[:, :, None, :] + cim[None] * pw_r[:, :, None, :]
        to_rows = lambda a: a.transpose(1, 3, 0, 2).reshape(g, p, ln * c)
        return to_rows(car), to_rows(-cai)

    c_tab = jnp.concatenate(rd(pf_r, pf_i, cr[0], ci[0]) + rd(pb_r, pb_i, cr[1], ci[1]), axis=1)
    a_chunk = jnp.stack([jnp.stack([pr[ln, 0], pi_[ln, 0]]), jnp.stack([pr[ln, 1], pi_[ln, 1]])])
    d_tab = jnp.tile(d_skip.astype(F32).reshape(g, 1, c), (1, ln, 1)).reshape(g, 1, ln * c)
    return m_tab.astype(BF16), b_tab.astype(BF16), c_tab.astype(BF16), a_chunk, d_tab


def _s5_inject_kernel(u_ref, b_ref, o_ref):
    o_ref[...] = jnp.dot(u_ref[...], b_ref[...], preferred_element_type=F32)


def _s5_scan_kernel(a_ref, e_ref, o_ref, s_scr, *, nb):
    d = pl.program_id(0)

    @pl.when(pl.program_id(1) == 0)
    def _():
        s_scr[...] = jnp.zeros_like(s_scr)

    ar, ai = a_ref[0], a_ref[1]

    def body(i, carry):
        sr, si = carry
        c = jnp.where(d == 0, i, nb - 1 - i)
        o_ref[0, c] = sr
        o_ref[1, c] = si
        return ar * sr - ai * si + e_ref[0, c], ar * si + ai * sr + e_ref[1, c]

    sr, si = lax.fori_loop(0, nb, body, (s_scr[0], s_scr[1]))
    s_scr[0] = sr
    s_scr[1] = si


def _gelu_tanh(x):
    return 0.5 * x * (1.0 + jnp.tanh(math.sqrt(2.0 / math.pi) * (x + 0.044715 * (x * x * x))))


def _s5_readout_kernel(u_ref, m_ref, s_ref, c_ref, d_ref, o_ref):
    u = u_ref[...]
    y = jnp.dot(u, m_ref[...], preferred_element_type=F32) + u.astype(F32) * d_ref[...]
    for k in range(4):
        y = y + jnp.dot(s_ref[k].astype(BF16), c_ref[k * S5_STATE:(k + 1) * S5_STATE, :],
                        preferred_element_type=F32)
    o_ref[...] = _gelu_tanh(y)


def _s5_glu_kernel(z_ref, w_ref, o_ref):
    z = z_ref[...]
    o_ref[...] = (z * jax.nn.sigmoid(jnp.dot(z.astype(BF16), w_ref[...], preferred_element_type=F32))
                  ).astype(o_ref.dtype)


def _s5_mixer(u, tables, w_glu, n_lat, n_ctx):
    m_tab, b_tab, c_tab, a_chunk, d_tab = tables
    t = u.shape[0]
    g, c, ln, p = S5_GROUPS, S5_GROUP, S5_CHUNK, S5_STATE
    nc = t // ln
    w = ln * c
    ug = u.reshape(nc, ln, g, c).transpose(2, 0, 1, 3).reshape(g, nc, w)

    e = pl.pallas_call(
        _s5_inject_kernel,
        grid=(g,),
        in_specs=[pl.BlockSpec((None, nc, w), lambda i: (i, 0, 0)),
                  pl.BlockSpec((None, w, 4 * p), lambda i: (i, 0, 0))],
        out_specs=pl.BlockSpec((None, nc, 4 * p), lambda i: (i, 0, 0)),
        out_shape=jax.ShapeDtypeStruct((g, nc, 4 * p), F32),
        compiler_params=_cparams(("arbitrary",)),
        name="s5_inject",
    )(ug, b_tab)
    e = e.reshape(g, nc, 2, 2, p).transpose(2, 3, 1, 0, 4)

    nb = S5_BLOCK
    assert (n_lat // ln) % nb == 0 and n_ctx // ln == nb
    lat_blocks = n_lat // ln // nb

    def blk(d, s):
        return jnp.where(s == 0, lat_blocks, jnp.where(d == 0, s - 1, lat_blocks - s))

    s_in = pl.pallas_call(
        functools.partial(_s5_scan_kernel, nb=nb),
        grid=(2, lat_blocks + 1),
        in_specs=[pl.BlockSpec((None, 2, g, p), lambda d, s: (d, 0, 0, 0)),
                  pl.BlockSpec((None, 2, nb, g, p), lambda d, s: (d, 0, blk(d, s), 0, 0))],
        out_specs=pl.BlockSpec((None, 2, nb, g, p), lambda d, s: (d, 0, blk(d, s), 0, 0)),
        out_shape=jax.ShapeDtypeStruct((2, 2, nc, g, p), F32),
        scratch_shapes=[pltpu.VMEM((2, g, p), F32)],
        compiler_params=_cparams(("arbitrary", "arbitrary")),
        name="s5_scan",
    )(a_chunk, e)
    s_in = s_in.transpose(3, 0, 1, 2, 4).reshape(g, 4, nc, p)

    z = pl.pallas_call(
        _s5_readout_kernel,
        grid=(g,),
        in_specs=[pl.BlockSpec((None, nc, w), lambda i: (i, 0, 0)),
                  pl.BlockSpec((None, w, w), lambda i: (i, 0, 0)),
                  pl.BlockSpec((None, 4, nc, p), lambda i: (i, 0, 0, 0)),
                  pl.BlockSpec((None, 4 * p, w), lambda i: (i, 0, 0)),
                  pl.BlockSpec((None, 1, w), lambda i: (i, 0, 0))],
        out_specs=pl.BlockSpec((None, nc, w), lambda i: (i, 0, 0)),
        out_shape=jax.ShapeDtypeStruct((g, nc, w), F32),
        compiler_params=_cparams(("arbitrary",)),
        name="s5_readout",
    )(ug, m_tab, s_in, c_tab, d_tab)
    z = z.reshape(g, nc, ln, c).transpose(1, 2, 0, 3).reshape(t, g * c)

    tm = _row_tile(t, (1280, 640, 256, 128))
    return pl.pallas_call(
        _s5_glu_kernel,
        grid=(t // tm,),
        in_specs=[pl.BlockSpec((tm, g * c), lambda i: (i, 0)),
                  pl.BlockSpec((g * c, g * c), lambda i: (0, 0))],
        out_specs=pl.BlockSpec((tm, g * c), lambda i: (i, 0)),
        out_shape=jax.ShapeDtypeStruct((t, g * c), BF16),
        compiler_params=_cparams(("arbitrary",)),
        name="s5_glu",
    )(z, w_glu)


def _ret_tables(decay_logit):
    scale = HEAD_W ** -0.5
    lg = jax.nn.log_sigmoid(decay_logit.astype(F32))
    lf, lb = lg[0][:, None, None], lg[1][:, None, None]
    i = jnp.arange(RET_CHUNK, dtype=F32)[None, :, None]
    j = jnp.arange(RET_CHUNK, dtype=F32)[None, None, :]
    diff = i - j
    intra = (jnp.where(diff >= 0, jnp.exp(lf * jnp.maximum(diff, 0.0)), 0.0)
             + jnp.where(diff <= 0, jnp.exp(lb * jnp.maximum(-diff, 0.0)), 0.0)) * scale
    ones = jnp.ones((1, 1, HEAD_W), F32)
    q_f = jnp.exp(lf * (i + 1.0)) * ones
    k_f = jnp.exp(lf * (RET_CHUNK - 1.0 - i)) * scale * ones
    c_f = jnp.exp(lf * RET_CHUNK) * jnp.ones((1, RET_CHUNK, HEAD_W), F32)
    q_b = jnp.exp(lb * (RET_CHUNK - i)) * ones
    k_b = jnp.exp(lb * i) * scale * ones
    c_b = jnp.exp(lb * RET_CHUNK) * jnp.ones((1, RET_CHUNK, HEAD_W), F32)
    return intra, jnp.stack([q_f, k_f, c_f]), jnp.stack([q_b, k_b, c_b])


def _ret_state_step(q, k, v, dec_ref, s_scr, h):
    s = s_scr[h]
    qd = (q.astype(F32) * dec_ref[0, h]).astype(BF16)
    kd = (k.astype(F32) * dec_ref[1, h]).astype(BF16)
    o = jnp.dot(qd, s.astype(BF16), preferred_element_type=F32)
    s_scr[h] = s * dec_ref[2, h] + lax.dot_general(kd, v, (((0,), (0,)), ((), ())),
                                                    preferred_element_type=F32)
    return o


def _ret_fwd_kernel(intra_ref, dec_ref, q_ref, k_ref, v_ref, o_ref, s_scr):
    @pl.when(pl.program_id(0) == 0)
    def _():
        s_scr[...] = jnp.zeros_like(s_scr)

    for h in range(RET_HEADS):
        sl = slice(h * HEAD_W, (h + 1) * HEAD_W)
        q, k, v = q_ref[:, sl], k_ref[:, sl], v_ref[:, sl]
        att = lax.dot_general(q, k, (((1,), (1,)), ((), ())), preferred_element_type=F32) * intra_ref[h]
        o = jnp.dot(att.astype(BF16), v, preferred_element_type=F32)
        o_ref[:, sl] = o + _ret_state_step(q, k, v, dec_ref, s_scr, h)


def _ret_bwd_kernel(dec_ref, q_ref, k_ref, v_ref, g_ref, o1_ref, o_ref, s_scr):
    @pl.when(pl.program_id(0) == 0)
    def _():
        s_scr[...] = jnp.zeros_like(s_scr)

    for h in range(RET_HEADS):
        sl = slice(h * HEAD_W, (h + 1) * HEAD_W)
        q, k, v = q_ref[:, sl], k_ref[:, sl], v_ref[:, sl]
        o = o1_ref[:, sl] + _ret_state_step(q, k, v, dec_ref, s_scr, h)
        y = o * lax.rsqrt(jnp.mean(o * o, axis=-1, keepdims=True) + EPS)
        gate = g_ref[:, sl].astype(F32)
        o_ref[:, sl] = (y * (gate * jax.nn.sigmoid(gate))).astype(o_ref.dtype)


def _retention(proj, tables, n_lat, n_ctx):
    intra, dec_f, dec_b = tables
    t = proj.shape[0]
    wd = RET_HEADS * HEAD_W
    ck = RET_CHUNK
    n_lat_c, n_ctx_c = n_lat // ck, n_ctx // ck
    steps = n_lat_c + n_ctx_c

    def fwd_blk(s):
        return jnp.where(s < n_ctx_c, n_lat_c + s, s - n_ctx_c)

    def bwd_blk(s):
        return jnp.where(s < n_ctx_c, n_lat_c + n_ctx_c - 1 - s, n_lat_c - 1 - (s - n_ctx_c))

    def col_spec(col, order):
        return pl.BlockSpec((ck, wd), lambda s: (order(s), col))

    tab_spec = pl.BlockSpec((3, RET_HEADS, ck, HEAD_W), lambda s: (0, 0, 0, 0))
    o1 = pl.pallas_call(
        _ret_fwd_kernel,
        grid=(steps,),
        in_specs=[pl.BlockSpec((RET_HEADS, ck, ck), lambda s: (0, 0, 0)), tab_spec,
                  col_spec(0, fwd_blk), col_spec(1, fwd_blk), col_spec(2, fwd_blk)],
        out_specs=col_spec(0, fwd_blk),
        out_shape=jax.ShapeDtypeStruct((t, wd), F32),
        scratch_shapes=[pltpu.VMEM((RET_HEADS, HEAD_W, HEAD_W), F32)],
        compiler_params=_cparams(("arbitrary",)),
        name="retention_fwd",
    )(intra, dec_f, proj, proj, proj)
    return pl.pallas_call(
        _ret_bwd_kernel,
        grid=(steps,),
        in_specs=[tab_spec, col_spec(0, bwd_blk), col_spec(1, bwd_blk), col_spec(2, bwd_blk),
                  col_spec(3, bwd_blk), col_spec(0, bwd_blk)],
        out_specs=col_spec(0, bwd_blk),
        out_shape=jax.ShapeDtypeStruct((t, wd), BF16),
        scratch_shapes=[pltpu.VMEM((RET_HEADS, HEAD_W, HEAD_W), F32)],
        compiler_params=_cparams(("arbitrary",)),
        name="retention_bwd",
    )(dec_b, proj, proj, proj, proj, o1)


def _rope_tables(n_lat, n_ctx):
    n_freq = A_QK_DIM // 4
    freq = ROPE_BASE ** (-jnp.arange(n_freq, dtype=F32) / n_freq)
    rows = n_lat // GRID_W
    ang_row = jnp.arange(rows, dtype=F32)[:, None] * freq
    ang_col = jnp.arange(GRID_W, dtype=F32)[:, None] * freq
    reps = HEAD_W // A_QK_DIM

    def table(fn, ctx_value):
        r = jnp.broadcast_to(jnp.tile(fn(ang_row), (1, 2))[:, None, :], (rows, GRID_W, 2 * n_freq))
        c = jnp.broadcast_to(jnp.tile(fn(ang_col), (1, 2))[None, :, :], (rows, GRID_W, 2 * n_freq))
        lat = jnp.tile(jnp.concatenate([r, c], axis=-1).reshape(n_lat, A_QK_DIM), (1, reps))
        return jnp.concatenate([lat, jnp.full((n_ctx, HEAD_W), ctx_value, F32)], axis=0)

    return table(jnp.cos, 1.0), table(jnp.sin, 0.0)


def kernel(x, c, ctx, c_ctx, ada_w, ada_b, norm1_w, norm2_w, ffn_w13, ffn_w2, e_w_in, e_w_out, diff_lq1, diff_lk1, diff_lq2, diff_lk2, diff_subln_w, na_rpb, o_w_in, o_w_out, s5_lam_re, s5_lam_im, s5_b_re, s5_b_im, s5_c_re, s5_c_im, s5_log_step, s5_d, s5_w_glu, ret_decay_logit, final_norm_w):
    bsz, n_lat, d = x.shape
    n_ctx = ctx.shape[1]
    depth = ada_w.shape[0]
    assert bsz == 1
    xs = jnp.concatenate([x[0], ctx[0]], axis=0)
    mods_all = _mods(c, c_ctx, ada_w, ada_b)
    cos, sin = _rope_tables(n_lat, n_ctx)
    ret_w = RET_HEADS * HEAD_W
    ffn_w13_bf16 = ffn_w13.astype(BF16)
    e_w_in_bf16 = e_w_in.astype(BF16)

    for i in range(depth):
        compute_ctx = i != depth - 1
        mods = mods_all[i]
        j = i // 2
        if i % 2 == 0:
            lambda_init = 0.8 - 0.6 * math.exp(-0.3 * i)
            qkv = _inproj_even(xs, norm1_w[i], mods, n_lat, e_w_in_bf16, j, cos, sin)
            lam_args = (diff_lq1[j], diff_lk1[j], diff_lq2[j], diff_lk2[j], diff_subln_w[j])
            o_a = _diff_attn(qkv, *lam_args, n_lat, n_ctx, lambda_init,
                             tiles=(512, 2048, 512) if j == 0 else (1024, 1024, 1024))
            o_b = _natten(qkv, _na_bias_tables(na_rpb[j], n_lat // GRID_W), n_lat, n_ctx)
            if compute_ctx:
                o_a = _diff_attn(qkv, *lam_args, n_lat, n_ctx, lambda_init, prev=o_a)
                o_b = _ctx_attn(qkv, o_b, n_lat, n_ctx)
            xs = _gated_residual([o_a, o_b], e_w_out, xs, mods, n_lat, 2, "outproj_even", layer=j)
        else:
            w_in = o_w_in[j]
            w_in = jnp.concatenate([w_in[:, S5_CH:], w_in[:, :S5_CH]], axis=1).astype(BF16)
            proj = _inproj_odd(xs, norm1_w[i], mods, n_lat, w_in)
            s5_tabs = _s5_tables(s5_lam_re[j], s5_lam_im[j], s5_b_re[j], s5_b_im[j], s5_c_re[j], s5_c_im[j],
                                 s5_log_step[j], s5_d[j])
            y_c = _s5_mixer(proj[:, 4 * ret_w:], s5_tabs, s5_w_glu[j].astype(BF16), n_lat, n_ctx)
            y_d = _retention(proj, _ret_tables(ret_decay_logit[j]), n_lat, n_ctx)
            xs = _gated_residual([y_c, y_d], o_w_out, xs, mods, n_lat, 2, "outproj_odd", layer=j)
        a = _ffn_up(xs, norm2_w[i], mods, n_lat, ffn_w13_bf16, i)
        xs = _gated_residual([a], ffn_w2[i].astype(BF16), xs, mods, n_lat, 5, "ffn_down")
    return _final_norm(xs, final_norm_w, n_lat)[None]
```

```python
import functools
import math

import jax
import jax.numpy as jnp
import numpy as np
from jax import lax
from jax.experimental import pallas as pl
from jax.experimental.pallas import tpu as pltpu

F32 = jnp.float32
BF16 = jnp.bfloat16

GRID_W = 64
A_HEADS = 8
A_QK_DIM = 64
HEAD_W = 128
B_HEADS = 8
WIN_R = 8
WIN_C = 16
NA_ROWS = 4
S5_CH = 512
S5_GROUP = 16
S5_GROUPS = 32
S5_STATE = 64
S5_CHUNK = 16
S5_BLOCK = 16
RET_HEADS = 12
RET_CHUNK = 128
ROPE_BASE = 10000.0
EPS = 1e-6
NEG_INF = -1e30
VMEM_LIMIT = 56 * 1024 * 1024
N_MOD = 8
NORM_ROW_CHUNK = 128


def _cparams(sem):
    return pltpu.CompilerParams(dimension_semantics=sem, vmem_limit_bytes=VMEM_LIMIT)


def _row_tile(t, candidates):
    for c in candidates:
        if t % c == 0:
            return c
    raise ValueError(f"no row tile for {t}")


def _pick_mod(mods_ref, idx, is_ctx):
    return jnp.where(is_ctx, mods_ref[1, idx:idx + 1, :], mods_ref[0, idx:idx + 1, :])


def _is_ctx_rows(tm, n_lat, axis):
    row = pl.program_id(axis) * tm + lax.broadcasted_iota(jnp.int32, (tm, 1), 0)
    return row >= n_lat


def _mods_kernel(s_ref, w_ref, b_ref, o_ref):
    s = s_ref[...]
    s = s * jax.nn.sigmoid(s)
    o_ref[...] = jnp.dot(s, w_ref[...], preferred_element_type=F32,
                         precision=lax.Precision.HIGHEST) + b_ref[...]


def _mods(c, c_ctx, ada_w, ada_b):
    depth, d, w6 = ada_w.shape
    s = jnp.zeros((8, d), F32).at[0].set(c[0]).at[1].set(c_ctx)
    tn = 1024
    out = pl.pallas_call(
        _mods_kernel,
        grid=(depth, w6 // tn),
        in_specs=[pl.BlockSpec((8, d), lambda l, j: (0, 0)),
                  pl.BlockSpec((None, d, tn), lambda l, j: (l, 0, j)),
                  pl.BlockSpec((None, 1, tn), lambda l, j: (l, 0, j))],
        out_specs=pl.BlockSpec((None, 8, tn), lambda l, j: (l, 0, j)),
        out_shape=jax.ShapeDtypeStruct((depth, 8, w6), F32),
        compiler_params=_cparams(("arbitrary", "arbitrary")),
        name="ada_mods",
    )(s, ada_w, ada_b.reshape(depth, 1, w6))
    m = out[:, :2].reshape(depth, 2, 6, d)
    return jnp.pad(m, ((0, 0), (0, 0), (0, N_MOD - 6), (0, 0)))


def _normmod_prologue(x_ref, nw_ref, mods_ref, h_scr, *, n_lat, tm, shift_idx, scale_idx):
    @pl.when(pl.program_id(1) == 0)
    def _():
        rc = NORM_ROW_CHUNK
        assert tm % rc == 0 and n_lat % rc == 0

        def chunk(c, carry):
            r0 = pl.multiple_of(c * rc, rc)
            seg = (pl.program_id(0) * tm + r0 >= n_lat).astype(jnp.int32)
            gain = nw_ref[...] * (1.0 + mods_ref[seg, scale_idx:scale_idx + 1, :])
            shift = mods_ref[seg, shift_idx:shift_idx + 1, :]
            x = x_ref[pl.ds(r0, rc), :]
            r = lax.rsqrt(jnp.mean(x * x, axis=-1, keepdims=True) + EPS)
            h_scr[pl.ds(r0, rc), :] = (x * r * gain + shift).astype(h_scr.dtype)
            return carry

        lax.fori_loop(0, tm // rc, chunk, 0)


def _norm_proj_call(kern, x, norm_w, mods, n_lat, shift_idx, scale_idx, w_args, w_specs, extra_args, extra_specs,
                    n_out, tn, tm, name, **kern_kw):
    t, d = x.shape
    norm = dict(n_lat=n_lat, tm=tm, shift_idx=shift_idx, scale_idx=scale_idx)
    return pl.pallas_call(
        functools.partial(kern, norm=norm, **kern_kw),
        grid=(t // tm, n_out // tn),
        in_specs=[pl.BlockSpec((tm, d), lambda i, j: (i, 0)),
                  pl.BlockSpec((1, d), lambda i, j: (0, 0)),
                  pl.BlockSpec((2, N_MOD, d), lambda i, j: (0, 0, 0))] + w_specs + extra_specs,
        out_specs=pl.BlockSpec((tm, tn), lambda i, j: (i, j)),
        out_shape=jax.ShapeDtypeStruct((t, n_out), BF16),
        scratch_shapes=[pltpu.VMEM((tm, d), BF16)],
        compiler_params=_cparams(("arbitrary", "arbitrary")),
        name=name,
    )(x, norm_w.reshape(1, d), mods, *w_args, *extra_args)


def _final_norm_kernel(x_ref, w_ref, o_ref):
    x = x_ref[...]
    o_ref[...] = x * lax.rsqrt(jnp.mean(x * x, axis=-1, keepdims=True) + EPS) * w_ref[...]


def _final_norm(x, w, n_lat):
    d = x.shape[1]
    tm = _row_tile(n_lat, (512, 256, 128))
    return pl.pallas_call(
        _final_norm_kernel,
        grid=(n_lat // tm,),
        in_specs=[pl.BlockSpec((tm, d), lambda i: (i, 0)),
                  pl.BlockSpec((1, d), lambda i: (0, 0))],
        out_specs=pl.BlockSpec((tm, d), lambda i: (i, 0)),
        out_shape=jax.ShapeDtypeStruct((n_lat, d), F32),
        compiler_params=_cparams(("arbitrary",)),
        name="final_norm",
    )(x, w.reshape(1, d))


def _rope_store(acc, cos, sin, o_ref, scale):
    first_half = (lax.broadcasted_iota(jnp.int32, (1, HEAD_W), 1) % 32) < 16
    for c in range(acc.shape[1] // HEAD_W):
        x = acc[:, c * HEAD_W:(c + 1) * HEAD_W]
        rot = jnp.where(first_half, -pltpu.roll(x, HEAD_W - 16, 1), pltpu.roll(x, 16, 1))
        o_ref[:, c * HEAD_W:(c + 1) * HEAD_W] = ((x * cos + rot * sin) * scale).astype(o_ref.dtype)


def _resident_bf16(w_ref, w_scr):
    @pl.when(pl.program_id(1) == 0)
    def _():
        w_scr[...] = w_ref[...].astype(BF16)

    return w_scr


def _inproj_even_kernel(x_ref, nw_ref, mods_ref, w_ref, cos_ref, sin_ref, o_ref, h_scr, *, norm, a_scale, b_scale):
    _normmod_prologue(x_ref, nw_ref, mods_ref, h_scr, **norm)
    j = pl.program_id(1)
    acc = jnp.dot(h_scr[...], w_ref[...], preferred_element_type=F32)

    @pl.when(j == 0)
    def _():
        _rope_store(acc, cos_ref[...], sin_ref[...], o_ref, a_scale)

    @pl.when(j == 1)
    def _():
        _rope_store(acc, cos_ref[...], sin_ref[...], o_ref, 1.0)

    @pl.when(j == 3)
    def _():
        o_ref[...] = (acc * b_scale).astype(o_ref.dtype)

    @pl.when((j == 2) | (j > 3))
    def _():
        o_ref[...] = acc.astype(o_ref.dtype)


def _inproj_even(x, norm_w, mods, n_lat, w, layer, cos, sin):
    t, d = x.shape
    n = w.shape[2]
    tn = n // 6
    tm = _row_tile(t, (1280, 640, 256, 128))
    rope_spec = pl.BlockSpec((tm, HEAD_W), lambda i, j: (i, 0))
    return _norm_proj_call(
        _inproj_even_kernel, x, norm_w, mods, n_lat, 0, 1,
        [w], [pl.BlockSpec((None, d, tn), lambda i, j: (layer, 0, j))],
        [cos, sin], [rope_spec, rope_spec], n, tn, tm, "inproj_even",
        a_scale=A_QK_DIM ** -0.5 * math.log2(math.e), b_scale=HEAD_W ** -0.5 * math.log2(math.e))


def _inproj_odd_kernel(x_ref, nw_ref, mods_ref, w_ref, o_ref, h_scr, *, norm):
    _normmod_prologue(x_ref, nw_ref, mods_ref, h_scr, **norm)
    o_ref[...] = jnp.dot(h_scr[...], w_ref[...], preferred_element_type=F32).astype(o_ref.dtype)


def _inproj_odd(x, norm_w, mods, n_lat, w):
    t, d = x.shape
    n = w.shape[1]
    tn = 512
    tm = _row_tile(t, (1280, 640, 256, 128))
    return _norm_proj_call(
        _inproj_odd_kernel, x, norm_w, mods, n_lat, 0, 1,
        [w], [pl.BlockSpec((d, tn), lambda i, j: (0, j))], [], [], n, tn, tm, "inproj_odd")


def _ffn_up_kernel(x_ref, nw_ref, mods_ref, w1_ref, w3_ref, o_ref, h_scr, *, norm):
    _normmod_prologue(x_ref, nw_ref, mods_ref, h_scr, **norm)
    h = h_scr[...]
    a = jnp.dot(h, w1_ref[...], preferred_element_type=F32)
    b = jnp.dot(h, w3_ref[...], preferred_element_type=F32)
    o_ref[...] = (a * jax.nn.sigmoid(a) * b).astype(o_ref.dtype)


def _ffn_up(x, norm_w, mods, n_lat, w13, layer):
    t, d = x.shape
    d_ff = w13.shape[2] // 2
    tn = 512
    nj = d_ff // tn
    tm = _row_tile(t, (1280, 640, 256, 128))
    return _norm_proj_call(
        _ffn_up_kernel, x, norm_w, mods, n_lat, 3, 4,
        [w13, w13], [pl.BlockSpec((None, d, tn), lambda i, j: (layer, 0, j)),
                     pl.BlockSpec((None, d, tn), lambda i, j: (layer, 0, j + nj))],
        [], [], d_ff, tn, tm, "ffn_up")


def _gated_residual_kernel(*refs, n_a, n_lat, tm, gate_idx):
    a_refs = refs[:n_a]
    w_ref, x_ref, mods_ref, o_ref = refs[n_a:n_a + 4]
    if w_ref.dtype != BF16:
        w_ref = _resident_bf16(w_ref, refs[n_a + 4])
    k0 = 0
    y = None
    for a_ref in a_refs:
        kk = a_ref.shape[1]
        part = jnp.dot(a_ref[...], w_ref[k0:k0 + kk, :], preferred_element_type=F32)
        y = part if y is None else y + part
        k0 += kk
    gate = _pick_mod(mods_ref, gate_idx, _is_ctx_rows(tm, n_lat, 1))
    o_ref[...] = x_ref[...] + gate * y


def _gated_residual(a_list, w, x, mods, n_lat, gate_idx, name, layer=None):
    t, d = x.shape
    k = w.shape[-2]
    tn = 512
    tm = _row_tile(t, (640, 256, 128))
    n_a = len(a_list)
    in_specs = [pl.BlockSpec((tm, a.shape[1]), lambda j, i: (i, 0)) for a in a_list]
    w_spec = (pl.BlockSpec((k, tn), lambda j, i: (0, j)) if layer is None
              else pl.BlockSpec((None, k, tn), lambda j, i: (layer, 0, j)))
    in_specs += [w_spec,
                 pl.BlockSpec((tm, tn), lambda j, i: (i, j)),
                 pl.BlockSpec((2, N_MOD, tn), lambda j, i: (0, 0, j))]
    return pl.pallas_call(
        functools.partial(_gated_residual_kernel, n_a=n_a, n_lat=n_lat, tm=tm, gate_idx=gate_idx),
        grid=(d // tn, t // tm),
        in_specs=in_specs,
        out_specs=pl.BlockSpec((tm, tn), lambda j, i: (i, j)),
        out_shape=jax.ShapeDtypeStruct((t, d), F32),
        scratch_shapes=[] if w.dtype == BF16 else [pltpu.VMEM((k, tn), BF16)],
        input_output_aliases={n_a + 1: 0},
        compiler_params=_cparams(("arbitrary", "arbitrary")),
        name=name,
    )(*a_list, w, x, mods)


def _softmax_block(qs, k, v):
    s = lax.dot_general(qs, k, (((1,), (1,)), ((), ())), preferred_element_type=F32)
    m = jnp.max(s, axis=-1, keepdims=True)
    p = jnp.exp2(s - m)
    l = jnp.sum(p, axis=-1, keepdims=True)
    acc = jnp.dot(p.astype(v.dtype), v, preferred_element_type=F32)
    return m, l, acc


DIFF_PV_ROWS = 512
DIFF_SOFTMAX_ROWS = 64
DIFF_EXP_DTYPE = BF16


def _diff_attn_kernel(*refs, tq, tk, n_ctx, n_chunks, lambda_init, pv_rows_max):
    if n_chunks > 1:
        (lq1, lk1, lq2, lk2, subw, q_ref, kc_ref, vc_ref, k_ref, v_ref, o_ref,
         qs_scr, sa_scr, mca_scr, p_scr, m_scr, l_scr, alpha_scr, acc_scr, sb_scr, mcb_scr) = refs
        buf_b = (sb_scr, mcb_scr)
    elif n_chunks == 1:
        (lq1, lk1, lq2, lk2, subw, q_ref, kc_ref, vc_ref, k_ref, v_ref, o_ref,
         qs_scr, sa_scr, mca_scr, p_scr, m_scr, l_scr, alpha_scr, acc_scr) = refs
    else:
        (lq1, lk1, lq2, lk2, subw, q_ref, kc_ref, vc_ref, o_ref,
         qs_scr, sa_scr, mca_scr, p_scr, m_scr, l_scr, alpha_scr, acc_scr) = refs
    buf_a = (sa_scr, mca_scr)
    nt = (((1,), (1,)), ((), ()))
    rows_all = 2 * tq
    pv_rows = min(pv_rows_max, rows_all)
    sm_rows = min(DIFF_SOFTMAX_ROWS, pv_rows)
    q = q_ref[...]
    comp1 = lax.broadcasted_iota(jnp.int32, (1, HEAD_W), 1) < A_QK_DIM
    zero = jnp.zeros_like(q)
    qs_scr[0:tq, :] = jnp.where(comp1, q, zero)
    qs_scr[tq:2 * tq, :] = jnp.where(comp1, zero, q)

    def scores(buf, rows, keys):
        s_buf, mc_buf = buf
        s = lax.dot_general(qs_scr[rows, :], keys, nt, preferred_element_type=F32)
        width = s.shape[1]
        s_buf[rows, 0:width] = s
        mp = s[:, 0:HEAD_W]
        for c in range(1, width // HEAD_W):
            mp = jnp.maximum(mp, s[:, c * HEAD_W:(c + 1) * HEAD_W])
        mc_buf[rows, :] = jnp.broadcast_to(jnp.max(mp, axis=-1, keepdims=True), mp.shape)

    def softmax_rows(buf, rows, n_tiles, first):
        s_buf, mc_buf = buf
        m_cur = mc_buf[rows, :]
        if first:
            m_new = m_cur
        else:
            m_prev = m_scr[rows, :]
            m_new = jnp.maximum(m_prev, m_cur)
        lp = None
        for c in range(n_tiles):
            pc = jnp.exp2((s_buf[rows, c * HEAD_W:(c + 1) * HEAD_W] - m_new).astype(DIFF_EXP_DTYPE))
            p_scr[rows, c * HEAD_W:(c + 1) * HEAD_W] = pc.astype(BF16)
            lp = pc if lp is None else lp + pc
        l_cur = jnp.sum(lp.astype(F32), axis=-1, keepdims=True)
        if first:
            l_scr[rows, :] = jnp.broadcast_to(l_cur, m_new.shape)
        else:
            alpha = jnp.exp2(m_prev - m_new)
            alpha_scr[rows, :] = alpha
            l_scr[rows, :] = alpha * l_scr[rows, :] + l_cur
        m_scr[rows, :] = m_new

    def softmax_pv(buf, width, load_v, first, after_group=None):
        for r in range(rows_all // pv_rows):
            for r2 in range(pv_rows // sm_rows):
                r0 = r * pv_rows + r2 * sm_rows
                softmax_rows(buf, slice(r0, r0 + sm_rows), width // HEAD_W, first)
            rows = slice(r * pv_rows, (r + 1) * pv_rows)
            pv = jnp.dot(p_scr[rows, 0:width], load_v(), preferred_element_type=F32)
            acc_scr[rows, :] = pv if first else alpha_scr[rows, :] * acc_scr[rows, :] + pv
            if after_group is not None:
                after_group(rows)

    def all_groups(fn):
        for r in range(rows_all // pv_rows):
            fn(slice(r * pv_rows, (r + 1) * pv_rows))

    all_groups(lambda rows: scores(buf_a, rows, kc_ref[...]))
    softmax_pv(buf_a, n_ctx, lambda: vc_ref[...], True)

    def qk_rows(j, buf, rows):
        off = pl.multiple_of(j * tk, tk)
        scores(buf, rows, k_ref[pl.ds(off, tk), :])

    def soft_pv(j, buf, j_next=None, buf_next=None):
        off = pl.multiple_of(j * tk, tk)
        after = None if j_next is None else (lambda rows: qk_rows(j_next, buf_next, rows))
        softmax_pv(buf, tk, lambda: v_ref[pl.ds(off, tk), :], False, after)

    if n_chunks == 1:
        all_groups(lambda rows: qk_rows(0, buf_a, rows))
        soft_pv(0, buf_a)
    elif n_chunks > 1:
        all_groups(lambda rows: qk_rows(0, buf_a, rows))

        def pair(j2, carry):
            soft_pv(2 * j2, buf_a, 2 * j2 + 1, buf_b)
            soft_pv(2 * j2 + 1, buf_b, 2 * j2 + 2, buf_a)
            return carry

        lax.fori_loop(0, n_chunks // 2 - 1, pair, 0)
        soft_pv(n_chunks - 2, buf_a, n_chunks - 1, buf_b)
        soft_pv(n_chunks - 1, buf_b)

    lam = (jnp.exp(jnp.sum(lq1[...] * lk1[...], axis=-1, keepdims=True))
           - jnp.exp(jnp.sum(lq2[...] * lk2[...], axis=-1, keepdims=True)) + lambda_init)
    o1 = acc_scr[0:tq, :] / l_scr[0:tq, :]
    o2 = acc_scr[tq:2 * tq, :] / l_scr[tq:2 * tq, :]
    o = o1 - lam * o2
    y = o * lax.rsqrt(jnp.mean(o * o, axis=-1, keepdims=True) + EPS) * subw[...]
    o_ref[...] = (y * (1.0 - lambda_init)).astype(o_ref.dtype)


def _diff_attn(qkv, lq1, lk1, lq2, lk2, subw, n_lat, n_ctx, lambda_init, prev=None,
               tiles=(1024, 1024, DIFF_PV_ROWS)):
    t = qkv.shape[0]
    use_lat = prev is None
    ctx_blk = n_lat // n_ctx
    small = [lq1.reshape(1, -1), lk1.reshape(1, -1), lq2.reshape(1, -1), lk2.reshape(1, -1), subw.reshape(1, -1)]
    small_specs = [pl.BlockSpec(a.shape, lambda h, qi: (0, 0)) for a in small]
    kc_spec = pl.BlockSpec((n_ctx, HEAD_W), lambda h, qi: (ctx_blk, A_HEADS + h))
    vc_spec = pl.BlockSpec((n_ctx, HEAD_W), lambda h, qi: (ctx_blk, 2 * A_HEADS + h))
    if use_lat:
        tq = _row_tile(n_lat, tuple(c for c in (1024, 512, 256, 128) if c <= tiles[0]))
        tk = _row_tile(n_lat, tuple(c for c in (4096, 2048, 1024, 512, 256, 128) if c <= tiles[1]))
        n_chunks = n_lat // tk
        assert n_chunks == 1 or n_chunks % 2 == 0
        grid = (A_HEADS, n_lat // tq)
        in_specs = small_specs + [
            pl.BlockSpec((tq, HEAD_W), lambda h, qi: (qi, h)), kc_spec, vc_spec,
            pl.BlockSpec((n_lat, HEAD_W), lambda h, qi: (0, A_HEADS + h), pipeline_mode=pl.Buffered(1)),
            pl.BlockSpec((n_lat, HEAD_W), lambda h, qi: (0, 2 * A_HEADS + h), pipeline_mode=pl.Buffered(1))]
        args = small + [qkv, qkv, qkv, qkv, qkv]
        out_spec = pl.BlockSpec((tq, HEAD_W), lambda h, qi: (qi, h))
        aliases = {}
    else:
        tq, tk, n_chunks = n_ctx, 0, 0
        grid = (A_HEADS, 1)
        in_specs = small_specs + [
            pl.BlockSpec((tq, HEAD_W), lambda h, qi: (ctx_blk, h)), kc_spec, vc_spec,
            pl.BlockSpec(memory_space=pl.ANY)]
        args = small + [qkv, qkv, qkv, prev]
        out_spec = pl.BlockSpec((tq, HEAD_W), lambda h, qi: (ctx_blk, h))
        aliases = {len(args) - 1: 0}

    sw = max(tk, n_ctx)
    kern = functools.partial(_diff_attn_kernel, tq=tq, tk=tk, n_ctx=n_ctx, n_chunks=n_chunks, pv_rows_max=tiles[2],
                             lambda_init=lambda_init)
    if not use_lat:
        inner = kern

        def kern(*refs):
            n_in = len(args)
            inner(*refs[:n_in - 1], *refs[n_in:])

    return pl.pallas_call(
        kern,
        grid=grid,
        in_specs=in_specs,
        out_specs=out_spec,
        out_shape=jax.ShapeDtypeStruct((t, A_HEADS * HEAD_W), BF16),
        scratch_shapes=[pltpu.VMEM((2 * tq, HEAD_W), BF16),
                        pltpu.VMEM((2 * tq, sw), F32),
                        pltpu.VMEM((2 * tq, HEAD_W), F32),
                        pltpu.VMEM((2 * tq, sw), BF16),
                        pltpu.VMEM((2 * tq, HEAD_W), F32),
                        pltpu.VMEM((2 * tq, HEAD_W), F32),
                        pltpu.VMEM((2 * tq, HEAD_W), F32),
                        pltpu.VMEM((2 * tq, HEAD_W), F32)]
        + ([pltpu.VMEM((2 * tq, sw), F32),
            pltpu.VMEM((2 * tq, HEAD_W), F32)] if n_chunks > 1 else []),
        input_output_aliases=aliases,
        compiler_params=_cparams(("arbitrary", "arbitrary")),
        name="diff_attn" if use_lat else "diff_attn_ctx",
    )(*args)


def _na_bias_tables(rpb, rows):
    nblk = rows // NA_ROWS
    wr = min(WIN_R, rows)
    nh = rpb.shape[0]
    rq = jnp.arange(NA_ROWS)[:, None, None]
    slot = jnp.arange(3)[None, :, None]
    rk = jnp.arange(NA_ROWS)[None, None, :]
    row_sel, row_ok = [], []
    for b, dup in ((0, 0), (1, -1), (nblk - 1, 2)):
        r = NA_ROWS * b + rq
        r0 = jnp.clip(r - wr // 2, 0, rows - wr)
        rkey = NA_ROWS * (b - 1 + slot) + rk
        ok = (rkey >= r0) & (rkey < r0 + wr) & (slot != dup)
        drow = jnp.broadcast_to(rkey - r + (WIN_R - 1), ok.shape)
        row_sel.append((drow[..., None] == jnp.arange(2 * WIN_R - 1)) & ok[..., None])
        row_ok.append(ok)
    row_sel = jnp.stack(row_sel).astype(F32)
    row_ok = jnp.stack(row_ok)
    jq = jnp.arange(GRID_W)[:, None]
    jk = jnp.arange(GRID_W)[None, :]
    c0 = jnp.clip(jq - WIN_C // 2, 0, GRID_W - WIN_C)
    col_ok = (jk >= c0) & (jk < c0 + WIN_C)
    dcol = jnp.clip(jk - jq + (WIN_C - 1), 0, 2 * WIN_C - 2)
    col_sel = (dcol[..., None] == jnp.arange(2 * WIN_C - 1)).astype(F32)
    bias = jnp.einsum('vasbr,hrc,qkc->hvaqsbk', row_sel, rpb.astype(F32), col_sel,
                      precision=lax.Precision.HIGHEST)
    valid = row_ok[None, :, :, None, :, :, None] & col_ok[None, None, None, :, None, None, :]
    tab = jnp.where(valid, bias * math.log2(math.e), NEG_INF)
    return tab.reshape(nh, 3, NA_ROWS * GRID_W, 3 * NA_ROWS * GRID_W)


def _natten_kernel(tab_ref, q_ref, k0_ref, k1_ref, k2_ref, v0_ref, v1_ref, v2_ref, kc_ref, vc_ref, o_ref):
    q = q_ref[...]
    nt = (((1,), (1,)), ((), ()))
    blk = q.shape[0]
    s_loc = [lax.dot_general(q, k_ref[...], nt, preferred_element_type=F32)
             + tab_ref[:, i * blk:(i + 1) * blk] for i, k_ref in enumerate((k0_ref, k1_ref, k2_ref))]
    s_ctx = lax.dot_general(q, kc_ref[...], nt, preferred_element_type=F32)
    m = jnp.max(s_ctx, axis=-1, keepdims=True)
    for s in s_loc:
        m = jnp.maximum(m, jnp.max(s, axis=-1, keepdims=True))
    l = None
    acc = None
    for s, v_ref in zip([s_ctx] + s_loc, (vc_ref, v0_ref, v1_ref, v2_ref)):
        p = jnp.exp2((s - m).astype(BF16))
        pt = p[:, 0:HEAD_W]
        for c in range(1, p.shape[1] // HEAD_W):
            pt = pt + p[:, c * HEAD_W:(c + 1) * HEAD_W]
        lp = jnp.sum(pt.astype(F32), axis=-1, keepdims=True)
        pv = jnp.dot(p, v_ref[...], preferred_element_type=F32)
        l = lp if l is None else l + lp
        acc = pv if acc is None else acc + pv
    o_ref[...] = (acc / l).astype(o_ref.dtype)


def _natten(qkv, tabs, n_lat, n_ctx):
    t = qkv.shape[0]
    blk = NA_ROWS * GRID_W
    nblk = n_lat // blk
    assert nblk >= 3 and blk == n_ctx
    qo, ko, vo = 3 * A_HEADS, 3 * A_HEADS + B_HEADS, 3 * A_HEADS + 2 * B_HEADS
    ctx_blk = n_lat // n_ctx

    def kv_spec(off, shift):
        return pl.BlockSpec((blk, HEAD_W), lambda h, b: (jnp.clip(b + shift, 0, nblk - 1), off + h))

    return pl.pallas_call(
        _natten_kernel,
        grid=(B_HEADS, nblk),
        in_specs=[pl.BlockSpec((None, None, blk, 3 * blk),
                               lambda h, b: (h, jnp.where(b == 0, 0, jnp.where(b == nblk - 1, 2, 1)), 0, 0)),
                  pl.BlockSpec((blk, HEAD_W), lambda h, b: (b, qo + h)),
                  kv_spec(ko, -1), kv_spec(ko, 0), kv_spec(ko, 1),
                  kv_spec(vo, -1), kv_spec(vo, 0), kv_spec(vo, 1),
                  pl.BlockSpec((n_ctx, HEAD_W), lambda h, b: (ctx_blk, ko + h)),
                  pl.BlockSpec((n_ctx, HEAD_W), lambda h, b: (ctx_blk, vo + h))],
        out_specs=pl.BlockSpec((blk, HEAD_W), lambda h, b: (b, h)),
        out_shape=jax.ShapeDtypeStruct((t, B_HEADS * HEAD_W), BF16),
        compiler_params=_cparams(("arbitrary", "arbitrary")),
        name="natten",
    )(tabs, qkv, qkv, qkv, qkv, qkv, qkv, qkv, qkv, qkv)


def _ctx_attn_kernel(q_ref, k_ref, v_ref, prev_ref, o_ref):
    del prev_ref
    _, l, acc = _softmax_block(q_ref[...], k_ref[...], v_ref[...])
    o_ref[...] = (acc / l).astype(o_ref.dtype)


def _ctx_attn(qkv, prev, n_lat, n_ctx):
    qo, ko, vo = 3 * A_HEADS, 3 * A_HEADS + B_HEADS, 3 * A_HEADS + 2 * B_HEADS
    ctx_blk = n_lat // n_ctx
    return pl.pallas_call(
        _ctx_attn_kernel,
        grid=(B_HEADS,),
        in_specs=[pl.BlockSpec((n_ctx, HEAD_W), lambda h: (ctx_blk, qo + h)),
                  pl.BlockSpec((n_ctx, HEAD_W), lambda h: (ctx_blk, ko + h)),
                  pl.BlockSpec((n_ctx, HEAD_W), lambda h: (ctx_blk, vo + h)),
                  pl.BlockSpec(memory_space=pl.ANY)],
        out_specs=pl.BlockSpec((n_ctx, HEAD_W), lambda h: (ctx_blk, h)),
        out_shape=jax.ShapeDtypeStruct(prev.shape, prev.dtype),
        input_output_aliases={3: 0},
        compiler_params=_cparams(("arbitrary",)),
        name="ctx_attn",
    )(qkv, qkv, qkv, prev)


def _s5_tables(lam_re, lam_im, b_re, b_im, c_re, c_im, log_step, d_skip):
    hp = lax.Precision.HIGHEST
    ln = S5_CHUNK
    g, p, c = S5_GROUPS, S5_STATE, S5_GROUP
    lr, li = lam_re.astype(F32), lam_im.astype(F32)
    dt = jnp.exp(log_step.astype(F32))[:, :, None]
    mag = jnp.exp(lr * dt)
    ar, ai = mag * jnp.cos(li * dt), mag * jnp.sin(li * dt)
    den = lr * lr + li * li
    nr, ni = ar - 1.0, ai
    fr = (nr * lr + ni * li) / den
    fi = (ni * lr - nr * li) / den
    br_, bi_ = b_re.astype(F32), b_im.astype(F32)
    bbr = fr[..., None] * br_ - fi[..., None] * bi_
    bbi = fr[..., None] * bi_ + fi[..., None] * br_
    lag = jnp.arange(ln + 1, dtype=F32)[:, None, None, None]
    magl = jnp.exp(lr * dt * lag)
    pr, pi_ = magl * jnp.cos(li * dt * lag), magl * jnp.sin(li * dt * lag)
    wr = pr[..., None] * bbr - pi_[..., None] * bbi
    wi = pr[..., None] * bbi + pi_[..., None] * bbr
    cr, ci = c_re.astype(F32), c_im.astype(F32)
    kern = (jnp.einsum('dgcp,ldgpe->ldgce', cr, wr[:ln], precision=hp)
            - jnp.einsum('dgcp,ldgpe->ldgce', ci, wi[:ln], precision=hp))
    lag = np.arange(ln)[:, None, None]
    s_idx = np.arange(ln)[None, :, None]
    t_idx = np.arange(ln)[None, None, :]
    sel_f = jnp.asarray(t_idx - s_idx == lag, F32)
    sel_b = jnp.asarray(s_idx - t_idx == lag, F32)
    kfb = (jnp.einsum('lst,lgce->stgce', sel_f, kern[:, 0], precision=hp)
           + jnp.einsum('lst,lgce->stgce', sel_b, kern[:, 1], precision=hp))
    m_tab = kfb.transpose(2, 0, 4, 1, 3).reshape(g, ln * c, ln * c)
    wf_r, wf_i = wr[:ln, 0][::-1], wi[:ln, 0][::-1]
    wb_r, wb_i = wr[:ln, 1], wi[:ln, 1]

    def inj(w):
        return w.transpose(1, 0, 3, 2).reshape(g, ln * c, p)

    b_tab = jnp.concatenate([inj(wf_r), inj(wf_i), inj(wb_r), inj(wb_i)], axis=-1)
    pf_r, pf_i = pr[1:ln + 1, 0], pi_[1:ln + 1, 0]
    pb_r, pb_i = pr[1:ln + 1, 1][::-1], pi_[1:ln + 1, 1][::-1]

    def rd(pw_r, pw_i, cre, cim):
        car = cre[None] * pw_r[:, :, None, :] - cim[None] * pw_i[:, :, None, :]
        cai = cre[None] * pw_i[:, :, None, :] + cim[None] * pw_r[:, :, None, :]
        to_rows = lambda a: a.transpose(1, 3, 0, 2).reshape(g, p, ln * c)
        return to_rows(car), to_rows(-cai)

    c_tab = jnp.concatenate(rd(pf_r, pf_i, cr[0], ci[0]) + rd(pb_r, pb_i, cr[1], ci[1]), axis=1)
    a_chunk = jnp.stack([jnp.stack([pr[ln, 0], pi_[ln, 0]]), jnp.stack([pr[ln, 1], pi_[ln, 1]])])
    d_tab = jnp.tile(d_skip.astype(F32).reshape(g, 1, c), (1, ln, 1)).reshape(g, 1, ln * c)
    return m_tab.astype(BF16), b_tab.astype(BF16), c_tab.astype(BF16), a_chunk, d_tab


def _s5_inject_kernel(u_ref, b_ref, o_ref):
    o_ref[...] = jnp.dot(u_ref[...], b_ref[...], preferred_element_type=F32)


def _s5_scan_kernel(a_ref, e_ref, o_ref, s_scr, *, nb):
    d = pl.program_id(0)

    @pl.when(pl.program_id(1) == 0)
    def _():
        s_scr[...] = jnp.zeros_like(s_scr)

    ar, ai = a_ref[0], a_ref[1]

    def body(i, carry):
        sr, si = carry
        c = jnp.where(d == 0, i, nb - 1 - i)
        o_ref[0, c] = sr
        o_ref[1, c] = si
        return ar * sr - ai * si + e_ref[0, c], ar * si + ai * sr + e_ref[1, c]

    sr, si = lax.fori_loop(0, nb, body, (s_scr[0], s_scr[1]))
    s_scr[0] = sr
    s_scr[1] = si


def _gelu_tanh(x):
    return 0.5 * x * (1.0 + jnp.tanh(math.sqrt(2.0 / math.pi) * (x + 0.044715 * (x * x * x))))


def _s5_readout_kernel(u_ref, m_ref, s_ref, c_ref, d_ref, o_ref):
    u = u_ref[...]
    y = jnp.dot(u, m_ref[...], preferred_element_type=F32) + u.astype(F32) * d_ref[...]
    for k in range(4):
        y = y + jnp.dot(s_ref[k].astype(BF16), c_ref[k * S5_STATE:(k + 1) * S5_STATE, :],
                        preferred_element_type=F32)
    o_ref[...] = _gelu_tanh(y)


def _s5_glu_kernel(z_ref, w_ref, o_ref):
    z = z_ref[...]
    o_ref[...] = (z * jax.nn.sigmoid(jnp.dot(z.astype(BF16), w_ref[...], preferred_element_type=F32))
                  ).astype(o_ref.dtype)


def _s5_mixer(u, tables, w_glu, n_lat, n_ctx):
    m_tab, b_tab, c_tab, a_chunk, d_tab = tables
    t = u.shape[0]
    g, c, ln, p = S5_GROUPS, S5_GROUP, S5_CHUNK, S5_STATE
    nc = t // ln
    w = ln * c
    ug = u.reshape(nc, ln, g, c).transpose(2, 0, 1, 3).reshape(g, nc, w)

    e = pl.pallas_call(
        _s5_inject_kernel,
        grid=(g,),
        in_specs=[pl.BlockSpec((None, nc, w), lambda i: (i, 0, 0)),
                  pl.BlockSpec((None, w, 4 * p), lambda i: (i, 0, 0))],
        out_specs=pl.BlockSpec((None, nc, 4 * p), lambda i: (i, 0, 0)),
        out_shape=jax.ShapeDtypeStruct((g, nc, 4 * p), F32),
        compiler_params=_cparams(("arbitrary",)),
        name="s5_inject",
    )(ug, b_tab)
    e = e.reshape(g, nc, 2, 2, p).transpose(2, 3, 1, 0, 4)

    nb = S5_BLOCK
    assert (n_lat // ln) % nb == 0 and n_ctx // ln == nb
    lat_blocks = n_lat // ln // nb

    def blk(d, s):
        return jnp.where(s == 0, lat_blocks, jnp.where(d == 0, s - 1, lat_blocks - s))

    s_in = pl.pallas_call(
        functools.partial(_s5_scan_kernel, nb=nb),
        grid=(2, lat_blocks + 1),
        in_specs=[pl.BlockSpec((None, 2, g, p), lambda d, s: (d, 0, 0, 0)),
                  pl.BlockSpec((None, 2, nb, g, p), lambda d, s: (d, 0, blk(d, s), 0, 0))],
        out_specs=pl.BlockSpec((None, 2, nb, g, p), lambda d, s: (d, 0, blk(d, s), 0, 0)),
        out_shape=jax.ShapeDtypeStruct((2, 2, nc, g, p), F32),
        scratch_shapes=[pltpu.VMEM((2, g, p), F32)],
        compiler_params=_cparams(("arbitrary", "arbitrary")),
        name="s5_scan",
    )(a_chunk, e)
    s_in = s_in.transpose(3, 0, 1, 2, 4).reshape(g, 4, nc, p)

    z = pl.pallas_call(
        _s5_readout_kernel,
        grid=(g,),
        in_specs=[pl.BlockSpec((None, nc, w), lambda i: (i, 0, 0)),
                  pl.BlockSpec((None, w, w), lambda i: (i, 0, 0)),
                  pl.BlockSpec((None, 4, nc, p), lambda i: (i, 0, 0, 0)),
                  pl.BlockSpec((None, 4 * p, w), lambda i: (i, 0, 0)),
                  pl.BlockSpec((None, 1, w), lambda i: (i, 0, 0))],
        out_specs=pl.BlockSpec((None, nc, w), lambda i: (i, 0, 0)),
        out_shape=jax.ShapeDtypeStruct((g, nc, w), F32),
        compiler_params=_cparams(("arbitrary",)),
        name="s5_readout",
    )(ug, m_tab, s_in, c_tab, d_tab)
    z = z.reshape(g, nc, ln, c).transpose(1, 2, 0, 3).reshape(t, g * c)

    tm = _row_tile(t, (1280, 640, 256, 128))
    return pl.pallas_call(
        _s5_glu_kernel,
        grid=(t // tm,),
        in_specs=[pl.BlockSpec((tm, g * c), lambda i: (i, 0)),
                  pl.BlockSpec((g * c, g * c), lambda i: (0, 0))],
        out_specs=pl.BlockSpec((tm, g * c), lambda i: (i, 0)),
        out_shape=jax.ShapeDtypeStruct((t, g * c), BF16),
        compiler_params=_cparams(("arbitrary",)),
        name="s5_glu",
    )(z, w_glu)


def _ret_tables(decay_logit):
    scale = HEAD_W ** -0.5
    lg = jax.nn.log_sigmoid(decay_logit.astype(F32))
    lf, lb = lg[0][:, None, None], lg[1][:, None, None]
    i = jnp.arange(RET_CHUNK, dtype=F32)[None, :, None]
    j = jnp.arange(RET_CHUNK, dtype=F32)[None, None, :]
    diff = i - j
    intra = (jnp.where(diff >= 0, jnp.exp(lf * jnp.maximum(diff, 0.0)), 0.0)
             + jnp.where(diff <= 0, jnp.exp(lb * jnp.maximum(-diff, 0.0)), 0.0)) * scale
    ones = jnp.ones((1, 1, HEAD_W), F32)
    q_f = jnp.exp(lf * (i + 1.0)) * ones
    k_f = jnp.exp(lf * (RET_CHUNK - 1.0 - i)) * scale * ones
    c_f = jnp.exp(lf * RET_CHUNK) * jnp.ones((1, RET_CHUNK, HEAD_W), F32)
    q_b = jnp.exp(lb * (RET_CHUNK - i)) * ones
    k_b = jnp.exp(lb * i) * scale * ones
    c_b = jnp.exp(lb * RET_CHUNK) * jnp.ones((1, RET_CHUNK, HEAD_W), F32)
    return intra, jnp.stack([q_f, k_f, c_f]), jnp.stack([q_b, k_b, c_b])


def _ret_state_step(q, k, v, dec_ref, s_scr, h):
    s = s_scr[h]
    qd = (q.astype(F32) * dec_ref[0, h]).astype(BF16)
    kd = (k.astype(F32) * dec_ref[1, h]).astype(BF16)
    o = jnp.dot(qd, s.astype(BF16), preferred_element_type=F32)
    s_scr[h] = s * dec_ref[2, h] + lax.dot_general(kd, v, (((0,), (0,)), ((), ())),
                                                    preferred_element_type=F32)
    return o


def _ret_fwd_kernel(intra_ref, dec_ref, q_ref, k_ref, v_ref, o_ref, s_scr):
    @pl.when(pl.program_id(0) == 0)
    def _():
        s_scr[...] = jnp.zeros_like(s_scr)

    for h in range(RET_HEADS):
        sl = slice(h * HEAD_W, (h + 1) * HEAD_W)
        q, k, v = q_ref[:, sl], k_ref[:, sl], v_ref[:, sl]
        att = lax.dot_general(q, k, (((1,), (1,)), ((), ())), preferred_element_type=F32) * intra_ref[h]
        o = jnp.dot(att.astype(BF16), v, preferred_element_type=F32)
        o_ref[:, sl] = o + _ret_state_step(q, k, v, dec_ref, s_scr, h)


def _ret_bwd_kernel(dec_ref, q_ref, k_ref, v_ref, g_ref, o1_ref, o_ref, s_scr):
    @pl.when(pl.program_id(0) == 0)
    def _():
        s_scr[...] = jnp.zeros_like(s_scr)

    for h in range(RET_HEADS):
        sl = slice(h * HEAD_W, (h + 1) * HEAD_W)
        q, k, v = q_ref[:, sl], k_ref[:, sl], v_ref[:, sl]
        o = o1_ref[:, sl] + _ret_state_step(q, k, v, dec_ref, s_scr, h)
        y = o * lax.rsqrt(jnp.mean(o * o, axis=-1, keepdims=True) + EPS)
        gate = g_ref[:, sl].astype(F32)
        o_ref[:, sl] = (y * (gate * jax.nn.sigmoid(gate))).astype(o_ref.dtype)


def _retention(proj, tables, n_lat, n_ctx):
    intra, dec_f, dec_b = tables
    t = proj.shape[0]
    wd = RET_HEADS * HEAD_W
    ck = RET_CHUNK
    n_lat_c, n_ctx_c = n_lat // ck, n_ctx // ck
    steps = n_lat_c + n_ctx_c

    def fwd_blk(s):
        return jnp.where(s < n_ctx_c, n_lat_c + s, s - n_ctx_c)

    def bwd_blk(s):
        return jnp.where(s < n_ctx_c, n_lat_c + n_ctx_c - 1 - s, n_lat_c - 1 - (s - n_ctx_c))

    def col_spec(col, order):
        return pl.BlockSpec((ck, wd), lambda s: (order(s), col))

    tab_spec = pl.BlockSpec((3, RET_HEADS, ck, HEAD_W), lambda s: (0, 0, 0, 0))
    o1 = pl.pallas_call(
        _ret_fwd_kernel,
        grid=(steps,),
        in_specs=[pl.BlockSpec((RET_HEADS, ck, ck), lambda s: (0, 0, 0)), tab_spec,
                  col_spec(0, fwd_blk), col_spec(1, fwd_blk), col_spec(2, fwd_blk)],
        out_specs=col_spec(0, fwd_blk),
        out_shape=jax.ShapeDtypeStruct((t, wd), F32),
        scratch_shapes=[pltpu.VMEM((RET_HEADS, HEAD_W, HEAD_W), F32)],
        compiler_params=_cparams(("arbitrary",)),
        name="retention_fwd",
    )(intra, dec_f, proj, proj, proj)
    return pl.pallas_call(
        _ret_bwd_kernel,
        grid=(steps,),
        in_specs=[tab_spec, col_spec(0, bwd_blk), col_spec(1, bwd_blk), col_spec(2, bwd_blk),
                  col_spec(3, bwd_blk), col_spec(0, bwd_blk)],
        out_specs=col_spec(0, bwd_blk),
        out_shape=jax.ShapeDtypeStruct((t, wd), BF16),
        scratch_shapes=[pltpu.VMEM((RET_HEADS, HEAD_W, HEAD_W), F32)],
        compiler_params=_cparams(("arbitrary",)),
        name="retention_bwd",
    )(dec_b, proj, proj, proj, proj, o1)


def _rope_tables(n_lat, n_ctx):
    n_freq = A_QK_DIM // 4
    freq = ROPE_BASE ** (-jnp.arange(n_freq, dtype=F32) / n_freq)
    rows = n_lat // GRID_W
    ang_row = jnp.arange(rows, dtype=F32)[:, None] * freq
    ang_col = jnp.arange(GRID_W, dtype=F32)[:, None] * freq
    reps = HEAD_W // A_QK_DIM

    def table(fn, ctx_value):
        r = jnp.broadcast_to(jnp.tile(fn(ang_row), (1, 2))[:, None, :], (rows, GRID_W, 2 * n_freq))
        c = jnp.broadcast_to(jnp.tile(fn(ang_col), (1, 2))[None, :, :], (rows, GRID_W, 2 * n_freq))
        lat = jnp.tile(jnp.concatenate([r, c], axis=-1).reshape(n_lat, A_QK_DIM), (1, reps))
        return jnp.concatenate([lat, jnp.full((n_ctx, HEAD_W), ctx_value, F32)], axis=0)

    return table(jnp.cos, 1.0), table(jnp.sin, 0.0)


def kernel(x, c, ctx, c_ctx, ada_w, ada_b, norm1_w, norm2_w, ffn_w13, ffn_w2, e_w_in, e_w_out, diff_lq1, diff_lk1, diff_lq2, diff_lk2, diff_subln_w, na_rpb, o_w_in, o_w_out, s5_lam_re, s5_lam_im, s5_b_re, s5_b_im, s5_c_re, s5_c_im, s5_log_step, s5_d, s5_w_glu, ret_decay_logit, final_norm_w):
    bsz, n_lat, d = x.shape
    n_ctx = ctx.shape[1]
    depth = ada_w.shape[0]
    assert bsz == 1
    xs = jnp.concatenate([x[0], ctx[0]], axis=0)
    mods_all = _mods(c, c_ctx, ada_w, ada_b)
    cos, sin = _rope_tables(n_lat, n_ctx)
    ret_w = RET_HEADS * HEAD_W
    ffn_w13_bf16 = ffn_w13.astype(BF16)
    e_w_in_bf16 = e_w_in.astype(BF16)

    for i in range(depth):
        compute_ctx = i != depth - 1
        mods = mods_all[i]
        j = i // 2
        if i % 2 == 0:
            lambda_init = 0.8 - 0.6 * math.exp(-0.3 * i)
            qkv = _inproj_even(xs, norm1_w[i], mods, n_lat, e_w_in_bf16, j, cos, sin)
            lam_args = (diff_lq1[j], diff_lk1[j], diff_lq2[j], diff_lk2[j], diff_subln_w[j])
            o_a = _diff_attn(qkv, *lam_args, n_lat, n_ctx, lambda_init,
                             tiles=(512, 2048, 256) if j == 0 else (256, 4096, 256))
            o_b = _natten(qkv, _na_bias_tables(na_rpb[j], n_lat // GRID_W), n_lat, n_ctx)
            if compute_ctx:
                o_a = _diff_attn(qkv, *lam_args, n_lat, n_ctx, lambda_init, prev=o_a)
                o_b = _ctx_attn(qkv, o_b, n_lat, n_ctx)
            xs = _gated_residual([o_a, o_b], e_w_out, xs, mods, n_lat, 2, "outproj_even", layer=j)
        else:
            w_in = o_w_in[j]
            w_in = jnp.concatenate([w_in[:, S5_CH:], w_in[:, :S5_CH]], axis=1).astype(BF16)
            proj = _inproj_odd(xs, norm1_w[i], mods, n_lat, w_in)
            s5_tabs = _s5_tables(s5_lam_re[j], s5_lam_im[j], s5_b_re[j], s5_b_im[j], s5_c_re[j], s5_c_im[j],
                                 s5_log_step[j], s5_d[j])
            y_c = _s5_mixer(proj[:, 4 * ret_w:], s5_tabs, s5_w_glu[j].astype(BF16), n_lat, n_ctx)
            y_d = _retention(proj, _ret_tables(ret_decay_logit[j]), n_lat, n_ctx)
            xs = _gated_residual([y_c, y_d], o_w_out, xs, mods, n_lat, 2, "outproj_odd", layer=j)
        a = _ffn_up(xs, norm2_w[i], mods, n_lat, ffn_w13_bf16, i)
        xs = _gated_residual([a], ffn_w2[i].astype(BF16), xs, mods, n_lat, 5, "ffn_down")
    return _final_norm(xs, final_norm_w, n_lat)[None]
```

```python
import functools
import math

import jax
import jax.numpy as jnp
import numpy as np
from jax import lax
from jax.experimental import pallas as pl
from jax.experimental.pallas import tpu as pltpu

F32 = jnp.float32
BF16 = jnp.bfloat16

GRID_W = 64
A_HEADS = 8
A_QK_DIM = 64
HEAD_W = 128
B_HEADS = 8
WIN_R = 8
WIN_C = 16
NA_ROWS = 4
NA_HEADS_PER_STEP = 2
S5_CH = 512
S5_GROUP = 16
S5_GROUPS = 32
S5_STATE = 64
S5_CHUNK = 16
S5_BLOCK = 16
RET_HEADS = 12
RET_CHUNK = 128
ROPE_BASE = 10000.0
EPS = 1e-6
NEG_INF = -1e30
VMEM_LIMIT = 56 * 1024 * 1024
N_MOD = 8
NORM_ROW_CHUNK = 128


def _cparams(sem):
    return pltpu.CompilerParams(dimension_semantics=sem, vmem_limit_bytes=VMEM_LIMIT)


def _row_tile(t, candidates):
    for c in candidates:
        if t % c == 0:
            return c
    raise ValueError(f"no row tile for {t}")


def _pick_mod(mods_ref, idx, is_ctx):
    return jnp.where(is_ctx, mods_ref[1, idx:idx + 1, :], mods_ref[0, idx:idx + 1, :])


def _is_ctx_rows(tm, n_lat, axis):
    row = pl.program_id(axis) * tm + lax.broadcasted_iota(jnp.int32, (tm, 1), 0)
    return row >= n_lat


def _mods_kernel(s_ref, w_ref, b_ref, o_ref):
    s = s_ref[...]
    s = s * jax.nn.sigmoid(s)
    o_ref[...] = jnp.dot(s, w_ref[...], preferred_element_type=F32,
                         precision=lax.Precision.HIGHEST) + b_ref[...]


def _mods(c, c_ctx, ada_w, ada_b):
    depth, d, w6 = ada_w.shape
    s = jnp.zeros((8, d), F32).at[0].set(c[0]).at[1].set(c_ctx)
    tn = 1024
    out = pl.pallas_call(
        _mods_kernel,
        grid=(depth, w6 // tn),
        in_specs=[pl.BlockSpec((8, d), lambda l, j: (0, 0)),
                  pl.BlockSpec((None, d, tn), lambda l, j: (l, 0, j)),
                  pl.BlockSpec((None, 1, tn), lambda l, j: (l, 0, j))],
        out_specs=pl.BlockSpec((None, 8, tn), lambda l, j: (l, 0, j)),
        out_shape=jax.ShapeDtypeStruct((depth, 8, w6), F32),
        compiler_params=_cparams(("arbitrary", "arbitrary")),
        name="ada_mods",
    )(s, ada_w, ada_b.reshape(depth, 1, w6))
    m = out[:, :2].reshape(depth, 2, 6, d)
    return jnp.pad(m, ((0, 0), (0, 0), (0, N_MOD - 6), (0, 0)))


def _normmod_prologue(x_ref, nw_ref, mods_ref, h_scr, *, n_lat, tm, shift_idx, scale_idx):
    @pl.when(pl.program_id(1) == 0)
    def _():
        rc = NORM_ROW_CHUNK
        assert tm % rc == 0 and n_lat % rc == 0

        def chunk(c, carry):
            r0 = pl.multiple_of(c * rc, rc)
            seg = (pl.program_id(0) * tm + r0 >= n_lat).astype(jnp.int32)
            gain = nw_ref[...] * (1.0 + mods_ref[seg, scale_idx:scale_idx + 1, :])
            shift = mods_ref[seg, shift_idx:shift_idx + 1, :]
            x = x_ref[pl.ds(r0, rc), :]
            r = lax.rsqrt(jnp.mean(x * x, axis=-1, keepdims=True) + EPS)
            h_scr[pl.ds(r0, rc), :] = (x * r * gain + shift).astype(h_scr.dtype)
            return carry

        lax.fori_loop(0, tm // rc, chunk, 0)


def _norm_proj_call(kern, x, norm_w, mods, n_lat, shift_idx, scale_idx, w_args, w_specs, extra_args, extra_specs,
                    n_out, tn, tm, name, **kern_kw):
    t, d = x.shape
    norm = dict(n_lat=n_lat, tm=tm, shift_idx=shift_idx, scale_idx=scale_idx)
    return pl.pallas_call(
        functools.partial(kern, norm=norm, **kern_kw),
        grid=(t // tm, n_out // tn),
        in_specs=[pl.BlockSpec((tm, d), lambda i, j: (i, 0)),
                  pl.BlockSpec((1, d), lambda i, j: (0, 0)),
                  pl.BlockSpec((2, N_MOD, d), lambda i, j: (0, 0, 0))] + w_specs + extra_specs,
        out_specs=pl.BlockSpec((tm, tn), lambda i, j: (i, j)),
        out_shape=jax.ShapeDtypeStruct((t, n_out), BF16),
        scratch_shapes=[pltpu.VMEM((tm, d), BF16)],
        compiler_params=_cparams(("arbitrary", "arbitrary")),
        name=name,
    )(x, norm_w.reshape(1, d), mods, *w_args, *extra_args)


def _final_norm_kernel(x_ref, w_ref, o_ref):
    x = x_ref[...]
    o_ref[...] = x * lax.rsqrt(jnp.mean(x * x, axis=-1, keepdims=True) + EPS) * w_ref[...]


def _final_norm(x, w, n_lat):
    d = x.shape[1]
    tm = _row_tile(n_lat, (512, 256, 128))
    return pl.pallas_call(
        _final_norm_kernel,
        grid=(n_lat // tm,),
        in_specs=[pl.BlockSpec((tm, d), lambda i: (i, 0)),
                  pl.BlockSpec((1, d), lambda i: (0, 0))],
        out_specs=pl.BlockSpec((tm, d), lambda i: (i, 0)),
        out_shape=jax.ShapeDtypeStruct((n_lat, d), F32),
        compiler_params=_cparams(("arbitrary",)),
        name="final_norm",
    )(x, w.reshape(1, d))


def _rope_store(acc, cos, sin, o_ref, scale):
    first_half = (lax.broadcasted_iota(jnp.int32, (1, HEAD_W), 1) % 32) < 16
    for c in range(acc.shape[1] // HEAD_W):
        x = acc[:, c * HEAD_W:(c + 1) * HEAD_W]
        rot = jnp.where(first_half, -pltpu.roll(x, HEAD_W - 16, 1), pltpu.roll(x, 16, 1))
        o_ref[:, c * HEAD_W:(c + 1) * HEAD_W] = ((x * cos + rot * sin) * scale).astype(o_ref.dtype)


def _resident_bf16(w_ref, w_scr):
    @pl.when(pl.program_id(1) == 0)
    def _():
        w_scr[...] = w_ref[...].astype(BF16)

    return w_scr


def _inproj_even_kernel(x_ref, nw_ref, mods_ref, w_ref, cos_ref, sin_ref, o_ref, h_scr, *, norm, a_scale, b_scale):
    _normmod_prologue(x_ref, nw_ref, mods_ref, h_scr, **norm)
    j = pl.program_id(1)
    acc = jnp.dot(h_scr[...], w_ref[...], preferred_element_type=F32)

    @pl.when(j == 0)
    def _():
        _rope_store(acc, cos_ref[...], sin_ref[...], o_ref, a_scale)

    @pl.when(j == 1)
    def _():
        _rope_store(acc, cos_ref[...], sin_ref[...], o_ref, 1.0)

    @pl.when(j == 3)
    def _():
        o_ref[...] = (acc * b_scale).astype(o_ref.dtype)

    @pl.when((j == 2) | (j > 3))
    def _():
        o_ref[...] = acc.astype(o_ref.dtype)


def _inproj_even(x, norm_w, mods, n_lat, w, layer, cos, sin):
    t, d = x.shape
    n = w.shape[2]
    tn = n // 6
    tm = _row_tile(t, (1280, 640, 256, 128))
    rope_spec = pl.BlockSpec((tm, HEAD_W), lambda i, j: (i, 0))
    return _norm_proj_call(
        _inproj_even_kernel, x, norm_w, mods, n_lat, 0, 1,
        [w], [pl.BlockSpec((None, d, tn), lambda i, j: (layer, 0, j))],
        [cos, sin], [rope_spec, rope_spec], n, tn, tm, "inproj_even",
        a_scale=A_QK_DIM ** -0.5 * math.log2(math.e), b_scale=HEAD_W ** -0.5 * math.log2(math.e))


def _inproj_odd_kernel(x_ref, nw_ref, mods_ref, w_ref, o_ref, h_scr, *, norm):
    _normmod_prologue(x_ref, nw_ref, mods_ref, h_scr, **norm)
    o_ref[...] = jnp.dot(h_scr[...], w_ref[...], preferred_element_type=F32).astype(o_ref.dtype)


def _inproj_odd(x, norm_w, mods, n_lat, w):
    t, d = x.shape
    n = w.shape[1]
    tn = 512
    tm = _row_tile(t, (1280, 640, 256, 128))
    return _norm_proj_call(
        _inproj_odd_kernel, x, norm_w, mods, n_lat, 0, 1,
        [w], [pl.BlockSpec((d, tn), lambda i, j: (0, j))], [], [], n, tn, tm, "inproj_odd")


def _ffn_up_kernel(x_ref, nw_ref, mods_ref, w1_ref, w3_ref, o_ref, h_scr, *, norm):
    _normmod_prologue(x_ref, nw_ref, mods_ref, h_scr, **norm)
    h = h_scr[...]
    a = jnp.dot(h, w1_ref[...], preferred_element_type=F32)
    b = jnp.dot(h, w3_ref[...], preferred_element_type=F32)
    o_ref[...] = (a * jax.nn.sigmoid(a) * b).astype(o_ref.dtype)


def _ffn_up(x, norm_w, mods, n_lat, w13, layer):
    t, d = x.shape
    d_ff = w13.shape[2] // 2
    tn = 512
    nj = d_ff // tn
    tm = _row_tile(t, (1280, 640, 256, 128))
    return _norm_proj_call(
        _ffn_up_kernel, x, norm_w, mods, n_lat, 3, 4,
        [w13, w13], [pl.BlockSpec((None, d, tn), lambda i, j: (layer, 0, j)),
                     pl.BlockSpec((None, d, tn), lambda i, j: (layer, 0, j + nj))],
        [], [], d_ff, tn, tm, "ffn_up")


def _gated_residual_kernel(*refs, n_a, n_lat, tm, gate_idx):
    a_refs = refs[:n_a]
    w_ref, x_ref, mods_ref, o_ref = refs[n_a:n_a + 4]
    if w_ref.dtype != BF16:
        w_ref = _resident_bf16(w_ref, refs[n_a + 4])
    k0 = 0
    y = None
    for a_ref in a_refs:
        kk = a_ref.shape[1]
        part = jnp.dot(a_ref[...], w_ref[k0:k0 + kk, :], preferred_element_type=F32)
        y = part if y is None else y + part
        k0 += kk
    gate = _pick_mod(mods_ref, gate_idx, _is_ctx_rows(tm, n_lat, 1))
    o_ref[...] = x_ref[...] + gate * y


def _gated_residual(a_list, w, x, mods, n_lat, gate_idx, name, layer=None):
    t, d = x.shape
    k = w.shape[-2]
    tn = 512
    tm = _row_tile(t, (640, 256, 128))
    n_a = len(a_list)
    in_specs = [pl.BlockSpec((tm, a.shape[1]), lambda j, i: (i, 0)) for a in a_list]
    w_spec = (pl.BlockSpec((k, tn), lambda j, i: (0, j)) if layer is None
              else pl.BlockSpec((None, k, tn), lambda j, i: (layer, 0, j)))
    in_specs += [w_spec,
                 pl.BlockSpec((tm, tn), lambda j, i: (i, j)),
                 pl.BlockSpec((2, N_MOD, tn), lambda j, i: (0, 0, j))]
    return pl.pallas_call(
        functools.partial(_gated_residual_kernel, n_a=n_a, n_lat=n_lat, tm=tm, gate_idx=gate_idx),
        grid=(d // tn, t // tm),
        in_specs=in_specs,
        out_specs=pl.BlockSpec((tm, tn), lambda j, i: (i, j)),
        out_shape=jax.ShapeDtypeStruct((t, d), F32),
        scratch_shapes=[] if w.dtype == BF16 else [pltpu.VMEM((k, tn), BF16)],
        input_output_aliases={n_a + 1: 0},
        compiler_params=_cparams(("arbitrary", "arbitrary")),
        name=name,
    )(*a_list, w, x, mods)


def _softmax_block(qs, k, v):
    s = lax.dot_general(qs, k, (((1,), (1,)), ((), ())), preferred_element_type=F32)
    m = jnp.max(s, axis=-1, keepdims=True)
    p = jnp.exp2(s - m)
    l = jnp.sum(p, axis=-1, keepdims=True)
    acc = jnp.dot(p.astype(v.dtype), v, preferred_element_type=F32)
    return m, l, acc


DIFF_PV_ROWS = 512
DIFF_SOFTMAX_ROWS = 64
DIFF_EXP_DTYPE = BF16


def _diff_attn_kernel(*refs, tq, tk, n_ctx, n_chunks, lambda_init, pv_rows_max):
    if n_chunks > 1:
        (lq1, lk1, lq2, lk2, subw, q_ref, kc_ref, vc_ref, k_ref, v_ref, o_ref,
         qs_scr, sa_scr, mca_scr, p_scr, m_scr, l_scr, alpha_scr, acc_scr, sb_scr, mcb_scr) = refs
        buf_b = (sb_scr, mcb_scr)
    elif n_chunks == 1:
        (lq1, lk1, lq2, lk2, subw, q_ref, kc_ref, vc_ref, k_ref, v_ref, o_ref,
         qs_scr, sa_scr, mca_scr, p_scr, m_scr, l_scr, alpha_scr, acc_scr) = refs
    else:
        (lq1, lk1, lq2, lk2, subw, q_ref, kc_ref, vc_ref, o_ref,
         qs_scr, sa_scr, mca_scr, p_scr, m_scr, l_scr, alpha_scr, acc_scr) = refs
    buf_a = (sa_scr, mca_scr)
    nt = (((1,), (1,)), ((), ()))
    rows_all = 2 * tq
    pv_rows = min(pv_rows_max, rows_all)
    sm_rows = min(DIFF_SOFTMAX_ROWS, pv_rows)
    q = q_ref[...]
    comp1 = lax.broadcasted_iota(jnp.int32, (1, HEAD_W), 1) < A_QK_DIM
    zero = jnp.zeros_like(q)
    qs_scr[0:tq, :] = jnp.where(comp1, q, zero)
    qs_scr[tq:2 * tq, :] = jnp.where(comp1, zero, q)

    def scores(buf, rows, keys):
        s_buf, mc_buf = buf
        s = lax.dot_general(qs_scr[rows, :], keys, nt, preferred_element_type=F32)
        width = s.shape[1]
        s_buf[rows, 0:width] = s
        mp = s[:, 0:HEAD_W]
        for c in range(1, width // HEAD_W):
            mp = jnp.maximum(mp, s[:, c * HEAD_W:(c + 1) * HEAD_W])
        mc_buf[rows, :] = jnp.broadcast_to(jnp.max(mp, axis=-1, keepdims=True), mp.shape)

    def softmax_rows(buf, rows, n_tiles, first):
        s_buf, mc_buf = buf
        m_cur = mc_buf[rows, :]
        if first:
            m_new = m_cur
        else:
            m_prev = m_scr[rows, :]
            m_new = jnp.maximum(m_prev, m_cur)
        lp = None
        for c in range(n_tiles):
            pc = jnp.exp2((s_buf[rows, c * HEAD_W:(c + 1) * HEAD_W] - m_new).astype(DIFF_EXP_DTYPE))
            p_scr[rows, c * HEAD_W:(c + 1) * HEAD_W] = pc.astype(BF16)
            lp = pc if lp is None else lp + pc
        l_cur = jnp.sum(lp.astype(F32), axis=-1, keepdims=True)
        if first:
            l_scr[rows, :] = jnp.broadcast_to(l_cur, m_new.shape)
        else:
            alpha = jnp.exp2(m_prev - m_new)
            alpha_scr[rows, :] = alpha
            l_scr[rows, :] = alpha * l_scr[rows, :] + l_cur
        m_scr[rows, :] = m_new

    def softmax_pv(buf, width, load_v, first, after_group=None):
        for r in range(rows_all // pv_rows):
            for r2 in range(pv_rows // sm_rows):
                r0 = r * pv_rows + r2 * sm_rows
                softmax_rows(buf, slice(r0, r0 + sm_rows), width // HEAD_W, first)
            rows = slice(r * pv_rows, (r + 1) * pv_rows)
            pv = jnp.dot(p_scr[rows, 0:width], load_v(), preferred_element_type=F32)
            acc_scr[rows, :] = pv if first else alpha_scr[rows, :] * acc_scr[rows, :] + pv
            if after_group is not None:
                after_group(rows)

    def all_groups(fn):
        for r in range(rows_all // pv_rows):
            fn(slice(r * pv_rows, (r + 1) * pv_rows))

    def qk_rows(j, buf, rows):
        off = pl.multiple_of(j * tk, tk)
        scores(buf, rows, k_ref[pl.ds(off, tk), :])

    all_groups(lambda rows: scores(buf_a, rows, kc_ref[...]))
    softmax_pv(buf_a, n_ctx, lambda: vc_ref[...], True,
               (lambda rows: qk_rows(0, buf_a, rows)) if n_chunks else None)

    def soft_pv(j, buf, j_next=None, buf_next=None):
        off = pl.multiple_of(j * tk, tk)
        after = None if j_next is None else (lambda rows: qk_rows(j_next, buf_next, rows))
        softmax_pv(buf, tk, lambda: v_ref[pl.ds(off, tk), :], False, after)

    if n_chunks == 1:
        soft_pv(0, buf_a)
    elif n_chunks > 1:

        def pair(j2, carry):
            soft_pv(2 * j2, buf_a, 2 * j2 + 1, buf_b)
            soft_pv(2 * j2 + 1, buf_b, 2 * j2 + 2, buf_a)
            return carry

        lax.fori_loop(0, n_chunks // 2 - 1, pair, 0)
        soft_pv(n_chunks - 2, buf_a, n_chunks - 1, buf_b)
        soft_pv(n_chunks - 1, buf_b)

    lam = (jnp.exp(jnp.sum(lq1[...] * lk1[...], axis=-1, keepdims=True))
           - jnp.exp(jnp.sum(lq2[...] * lk2[...], axis=-1, keepdims=True)) + lambda_init)
    o1 = acc_scr[0:tq, :] / l_scr[0:tq, :]
    o2 = acc_scr[tq:2 * tq, :] / l_scr[tq:2 * tq, :]
    o = o1 - lam * o2
    y = o * lax.rsqrt(jnp.mean(o * o, axis=-1, keepdims=True) + EPS) * subw[...]
    o_ref[...] = (y * (1.0 - lambda_init)).astype(o_ref.dtype)


def _diff_attn(qkv, lq1, lk1, lq2, lk2, subw, n_lat, n_ctx, lambda_init, prev=None,
               tiles=(1024, 1024, DIFF_PV_ROWS)):
    t = qkv.shape[0]
    use_lat = prev is None
    ctx_blk = n_lat // n_ctx
    small = [lq1.reshape(1, -1), lk1.reshape(1, -1), lq2.reshape(1, -1), lk2.reshape(1, -1), subw.reshape(1, -1)]
    small_specs = [pl.BlockSpec(a.shape, lambda h, qi: (0, 0)) for a in small]
    kc_spec = pl.BlockSpec((n_ctx, HEAD_W), lambda h, qi: (ctx_blk, A_HEADS + h))
    vc_spec = pl.BlockSpec((n_ctx, HEAD_W), lambda h, qi: (ctx_blk, 2 * A_HEADS + h))
    if use_lat:
        tq = _row_tile(n_lat, tuple(c for c in (1024, 512, 256, 128) if c <= tiles[0]))
        tk = _row_tile(n_lat, tuple(c for c in (4096, 2048, 1024, 512, 256, 128) if c <= tiles[1]))
        n_chunks = n_lat // tk
        assert n_chunks == 1 or n_chunks % 2 == 0
        grid = (A_HEADS, n_lat // tq)
        in_specs = small_specs + [
            pl.BlockSpec((tq, HEAD_W), lambda h, qi: (qi, h)), kc_spec, vc_spec,
            pl.BlockSpec((n_lat, HEAD_W), lambda h, qi: (0, A_HEADS + h), pipeline_mode=pl.Buffered(1)),
            pl.BlockSpec((n_lat, HEAD_W), lambda h, qi: (0, 2 * A_HEADS + h), pipeline_mode=pl.Buffered(1))]
        args = small + [qkv, qkv, qkv, qkv, qkv]
        out_spec = pl.BlockSpec((tq, HEAD_W), lambda h, qi: (qi, h))
        aliases = {}
    else:
        tq, tk, n_chunks = n_ctx, 0, 0
        grid = (A_HEADS, 1)
        in_specs = small_specs + [
            pl.BlockSpec((tq, HEAD_W), lambda h, qi: (ctx_blk, h)), kc_spec, vc_spec,
            pl.BlockSpec(memory_space=pl.ANY)]
        args = small + [qkv, qkv, qkv, prev]
        out_spec = pl.BlockSpec((tq, HEAD_W), lambda h, qi: (ctx_blk, h))
        aliases = {len(args) - 1: 0}

    sw = max(tk, n_ctx)
    kern = functools.partial(_diff_attn_kernel, tq=tq, tk=tk, n_ctx=n_ctx, n_chunks=n_chunks, pv_rows_max=tiles[2],
                             lambda_init=lambda_init)
    if not use_lat:
        inner = kern

        def kern(*refs):
            n_in = len(args)
            inner(*refs[:n_in - 1], *refs[n_in:])

    return pl.pallas_call(
        kern,
        grid=grid,
        in_specs=in_specs,
        out_specs=out_spec,
        out_shape=jax.ShapeDtypeStruct((t, A_HEADS * HEAD_W), BF16),
        scratch_shapes=[pltpu.VMEM((2 * tq, HEAD_W), BF16),
                        pltpu.VMEM((2 * tq, sw), F32),
                        pltpu.VMEM((2 * tq, HEAD_W), F32),
                        pltpu.VMEM((2 * tq, sw), BF16),
                        pltpu.VMEM((2 * tq, HEAD_W), F32),
                        pltpu.VMEM((2 * tq, HEAD_W), F32),
                        pltpu.VMEM((2 * tq, HEAD_W), F32),
                        pltpu.VMEM((2 * tq, HEAD_W), F32)]
        + ([pltpu.VMEM((2 * tq, sw), F32),
            pltpu.VMEM((2 * tq, HEAD_W), F32)] if n_chunks > 1 else []),
        input_output_aliases=aliases,
        compiler_params=_cparams(("arbitrary", "arbitrary")),
        name="diff_attn" if use_lat else "diff_attn_ctx",
    )(*args)


def _na_bias_tables(rpb, rows):
    nblk = rows // NA_ROWS
    wr = min(WIN_R, rows)
    nh = rpb.shape[0]
    rq = jnp.arange(NA_ROWS)[:, None, None]
    slot = jnp.arange(3)[None, :, None]
    rk = jnp.arange(NA_ROWS)[None, None, :]
    row_sel, row_ok = [], []
    for b, dup in ((0, 0), (1, -1), (nblk - 1, 2)):
        r = NA_ROWS * b + rq
        r0 = jnp.clip(r - wr // 2, 0, rows - wr)
        rkey = NA_ROWS * (b - 1 + slot) + rk
        ok = (rkey >= r0) & (rkey < r0 + wr) & (slot != dup)
        drow = jnp.broadcast_to(rkey - r + (WIN_R - 1), ok.shape)
        row_sel.append((drow[..., None] == jnp.arange(2 * WIN_R - 1)) & ok[..., None])
        row_ok.append(ok)
    row_sel = jnp.stack(row_sel).astype(F32)
    row_ok = jnp.stack(row_ok)
    jq = jnp.arange(GRID_W)[:, None]
    jk = jnp.arange(GRID_W)[None, :]
    c0 = jnp.clip(jq - WIN_C // 2, 0, GRID_W - WIN_C)
    col_ok = (jk >= c0) & (jk < c0 + WIN_C)
    dcol = jnp.clip(jk - jq + (WIN_C - 1), 0, 2 * WIN_C - 2)
    col_sel = (dcol[..., None] == jnp.arange(2 * WIN_C - 1)).astype(F32)
    bias = jnp.einsum('vasbr,hrc,qkc->hvaqsbk', row_sel, rpb.astype(F32), col_sel,
                      precision=lax.Precision.HIGHEST)
    valid = row_ok[None, :, :, None, :, :, None] & col_ok[None, None, None, :, None, None, :]
    tab = jnp.where(valid, bias * math.log2(math.e), NEG_INF)
    return tab.reshape(nh, 3, NA_ROWS * GRID_W, 3 * NA_ROWS * GRID_W)


def _natten_kernel(tab_ref, q_ref, k0_ref, k1_ref, k2_ref, v0_ref, v1_ref, v2_ref, kc_ref, vc_ref, o_ref):
    nt = (((1,), (1,)), ((), ()))
    blk = q_ref.shape[0]
    for hh in range(NA_HEADS_PER_STEP):
        hs = slice(hh * HEAD_W, (hh + 1) * HEAD_W)
        q = q_ref[:, hs]
        s_loc = [lax.dot_general(q, k_ref[:, hs], nt, preferred_element_type=F32)
                 + tab_ref[hh, :, i * blk:(i + 1) * blk] for i, k_ref in enumerate((k0_ref, k1_ref, k2_ref))]
        s_ctx = lax.dot_general(q, kc_ref[:, hs], nt, preferred_element_type=F32)
        m = jnp.max(s_ctx, axis=-1, keepdims=True)
        for s in s_loc:
            m = jnp.maximum(m, jnp.max(s, axis=-1, keepdims=True))
        l = None
        acc = None
        for s, v_ref in zip([s_ctx] + s_loc, (vc_ref, v0_ref, v1_ref, v2_ref)):
            p = jnp.exp2((s - m).astype(BF16))
            pt = p[:, 0:HEAD_W]
            for c in range(1, p.shape[1] // HEAD_W):
                pt = pt + p[:, c * HEAD_W:(c + 1) * HEAD_W]
            lp = jnp.sum(pt.astype(F32), axis=-1, keepdims=True)
            pv = jnp.dot(p, v_ref[:, hs], preferred_element_type=F32)
            l = lp if l is None else l + lp
            acc = pv if acc is None else acc + pv
        o_ref[:, hs] = (acc / l).astype(o_ref.dtype)


def _natten(qkv, tabs, n_lat, n_ctx):
    t = qkv.shape[0]
    blk = NA_ROWS * GRID_W
    nblk = n_lat // blk
    assert nblk >= 3 and blk == n_ctx
    nh = NA_HEADS_PER_STEP
    hw = nh * HEAD_W
    qo, ko, vo = (3 * A_HEADS) // nh, (3 * A_HEADS + B_HEADS) // nh, (3 * A_HEADS + 2 * B_HEADS) // nh
    ctx_blk = n_lat // n_ctx

    def kv_spec(off, shift):
        return pl.BlockSpec((blk, hw), lambda h, b: (jnp.clip(b + shift, 0, nblk - 1), off + h))

    return pl.pallas_call(
        _natten_kernel,
        grid=(B_HEADS // nh, nblk),
        in_specs=[pl.BlockSpec((nh, None, blk, 3 * blk),
                               lambda h, b: (h, jnp.where(b == 0, 0, jnp.where(b == nblk - 1, 2, 1)), 0, 0)),
                  pl.BlockSpec((blk, hw), lambda h, b: (b, qo + h)),
                  kv_spec(ko, -1), kv_spec(ko, 0), kv_spec(ko, 1),
                  kv_spec(vo, -1), kv_spec(vo, 0), kv_spec(vo, 1),
                  pl.BlockSpec((n_ctx, hw), lambda h, b: (ctx_blk, ko + h)),
                  pl.BlockSpec((n_ctx, hw), lambda h, b: (ctx_blk, vo + h))],
        out_specs=pl.BlockSpec((blk, hw), lambda h, b: (b, h)),
        out_shape=jax.ShapeDtypeStruct((t, B_HEADS * HEAD_W), BF16),
        compiler_params=_cparams(("arbitrary", "arbitrary")),
        name="natten",
    )(tabs, qkv, qkv, qkv, qkv, qkv, qkv, qkv, qkv, qkv)


def _ctx_attn_kernel(q_ref, k_ref, v_ref, prev_ref, o_ref):
    del prev_ref
    _, l, acc = _softmax_block(q_ref[...], k_ref[...], v_ref[...])
    o_ref[...] = (acc / l).astype(o_ref.dtype)


def _ctx_attn(qkv, prev, n_lat, n_ctx):
    qo, ko, vo = 3 * A_HEADS, 3 * A_HEADS + B_HEADS, 3 * A_HEADS + 2 * B_HEADS
    ctx_blk = n_lat // n_ctx
    return pl.pallas_call(
        _ctx_attn_kernel,
        grid=(B_HEADS,),
        in_specs=[pl.BlockSpec((n_ctx, HEAD_W), lambda h: (ctx_blk, qo + h)),
                  pl.BlockSpec((n_ctx, HEAD_W), lambda h: (ctx_blk, ko + h)),
                  pl.BlockSpec((n_ctx, HEAD_W), lambda h: (ctx_blk, vo + h)),
                  pl.BlockSpec(memory_space=pl.ANY)],
        out_specs=pl.BlockSpec((n_ctx, HEAD_W), lambda h: (ctx_blk, h)),
        out_shape=jax.ShapeDtypeStruct(prev.shape, prev.dtype),
        input_output_aliases={3: 0},
        compiler_params=_cparams(("arbitrary",)),
        name="ctx_attn",
    )(qkv, qkv, qkv, prev)


def _s5_tables(lam_re, lam_im, b_re, b_im, c_re, c_im, log_step, d_skip):
    hp = lax.Precision.HIGHEST
    ln = S5_CHUNK
    g, p, c = S5_GROUPS, S5_STATE, S5_GROUP
    lr, li = lam_re.astype(F32), lam_im.astype(F32)
    dt = jnp.exp(log_step.astype(F32))[:, :, None]
    mag = jnp.exp(lr * dt)
    ar, ai = mag * jnp.cos(li * dt), mag * jnp.sin(li * dt)
    den = lr * lr + li * li
    nr, ni = ar - 1.0, ai
    fr = (nr * lr + ni * li) / den
    fi = (ni * lr - nr * li) / den
    br_, bi_ = b_re.astype(F32), b_im.astype(F32)
    bbr = fr[..., None] * br_ - fi[..., None] * bi_
    bbi = fr[..., None] * bi_ + fi[..., None] * br_
    lag = jnp.arange(ln + 1, dtype=F32)[:, None, None, None]
    magl = jnp.exp(lr * dt * lag)
    pr, pi_ = magl * jnp.cos(li * dt * lag), magl * jnp.sin(li * dt * lag)
    wr = pr[..., None] * bbr - pi_[..., None] * bbi
    wi = pr[..., None] * bbi + pi_[..., None] * bbr
    cr, ci = c_re.astype(F32), c_im.astype(F32)
    kern = (jnp.einsum('dgcp,ldgpe->ldgce', cr, wr[:ln], precision=hp)
            - jnp.einsum('dgcp,ldgpe->ldgce', ci, wi[:ln], precision=hp))
    lag = np.arange(ln)[:, None, None]
    s_idx = np.arange(ln)[None, :, None]
    t_idx = np.arange(ln)[None, None, :]
    sel_f = jnp.asarray(t_idx - s_idx == lag, F32)
    sel_b = jnp.asarray(s_idx - t_idx == lag, F32)
    kfb = (jnp.einsum('lst,lgce->stgce', sel_f, kern[:, 0], precision=hp)
           + jnp.einsum('lst,lgce->stgce', sel_b, kern[:, 1], precision=hp))
    m_tab = kfb.transpose(2, 0, 4, 1, 3).reshape(g, ln * c, ln * c)
    wf_r, wf_i = wr[:ln, 0][::-1], wi[:ln, 0][::-1]
    wb_r, wb_i = wr[:ln, 1], wi[:ln, 1]

    def inj(w):
        return w.transpose(1, 0, 3, 2).reshape(g, ln * c, p)

    b_tab = jnp.concatenate([inj(wf_r), inj(wf_i), inj(wb_r), inj(wb_i)], axis=-1)
    pf_r, pf_i = pr[1:ln + 1, 0], pi_[1:ln + 1, 0]
    pb_r, pb_i = pr[1:ln + 1, 1][::-1], pi_[1:ln + 1, 1][::-1]

    def rd(pw_r, pw_i, cre, cim):
        car = cre[None] * pw_r[:, :, None, :] - cim[None] * pw_i[:, :, None, :]
        cai = cre[None] * pw_i[:, :, None, :] + cim[None] * pw_r[:, :, None, :]
        to_rows = lambda a: a.transpose(1, 3, 0, 2).reshape(g, p, ln * c)
        return to_rows(car), to_rows(-cai)

    c_tab = jnp.concatenate(rd(pf_r, pf_i, cr[0], ci[0]) + rd(pb_r, pb_i, cr[1], ci[1]), axis=1)
    a_chunk = jnp.stack([jnp.stack([pr[ln, 0], pi_[ln, 0]]), jnp.stack([pr[ln, 1], pi_[ln, 1]])])
    d_tab = jnp.tile(d_skip.astype(F32).reshape(g, 1, c), (1, ln, 1)).reshape(g, 1, ln * c)
    return m_tab.astype(BF16), b_tab.astype(BF16), c_tab.astype(BF16), a_chunk, d_tab


def _s5_inject_kernel(u_ref, b_ref, o_ref):
    o_ref[...] = jnp.dot(u_ref[...], b_ref[...], preferred_element_type=F32)


def _s5_scan_kernel(a_ref, e_ref, o_ref, s_scr, *, nb):
    d = pl.program_id(0)

    @pl.when(pl.program_id(1) == 0)
    def _():
        s_scr[...] = jnp.zeros_like(s_scr)

    ar, ai = a_ref[0], a_ref[1]

    def body(i, carry):
        sr, si = carry
        c = jnp.where(d == 0, i, nb - 1 - i)
        o_ref[0, c] = sr
        o_ref[1, c] = si
        return ar * sr - ai * si + e_ref[0, c], ar * si + ai * sr + e_ref[1, c]

    sr, si = lax.fori_loop(0, nb, body, (s_scr[0], s_scr[1]))
    s_scr[0] = sr
    s_scr[1] = si


def _gelu_tanh(x):
    return 0.5 * x * (1.0 + jnp.tanh(math.sqrt(2.0 / math.pi) * (x + 0.044715 * (x * x * x))))


def _s5_readout_kernel(u_ref, m_ref, s_ref, c_ref, d_ref, o_ref):
    u = u_ref[...]
    y = jnp.dot(u, m_ref[...], preferred_element_type=F32) + u.astype(F32) * d_ref[...]
    for k in range(4):
        y = y + jnp.dot(s_ref[k].astype(BF16), c_ref[k * S5_STATE:(k + 1) * S5_STATE, :],
                        preferred_element_type=F32)
    o_ref[...] = _gelu_tanh(y).astype(o_ref.dtype)


def _s5_glu_kernel(z_ref, w_ref, o_ref):
    z = z_ref[...]
    gate = jax.nn.sigmoid(jnp.dot(z.astype(BF16), w_ref[...], preferred_element_type=F32))
    o_ref[...] = (z.astype(F32) * gate).astype(o_ref.dtype)


def _s5_mixer(u, tables, w_glu, n_lat, n_ctx):
    m_tab, b_tab, c_tab, a_chunk, d_tab = tables
    t = u.shape[0]
    g, c, ln, p = S5_GROUPS, S5_GROUP, S5_CHUNK, S5_STATE
    nc = t // ln
    w = ln * c
    ug = u.reshape(nc, ln, g, c).transpose(2, 0, 1, 3).reshape(g, nc, w)

    e = pl.pallas_call(
        _s5_inject_kernel,
        grid=(g,),
        in_specs=[pl.BlockSpec((None, nc, w), lambda i: (i, 0, 0)),
                  pl.BlockSpec((None, w, 4 * p), lambda i: (i, 0, 0))],
        out_specs=pl.BlockSpec((None, nc, 4 * p), lambda i: (i, 0, 0)),
        out_shape=jax.ShapeDtypeStruct((g, nc, 4 * p), F32),
        compiler_params=_cparams(("arbitrary",)),
        name="s5_inject",
    )(ug, b_tab)
    e = e.reshape(g, nc, 2, 2, p).transpose(2, 3, 1, 0, 4)

    nb = S5_BLOCK
    assert (n_lat // ln) % nb == 0 and n_ctx // ln == nb
    lat_blocks = n_lat // ln // nb

    def blk(d, s):
        return jnp.where(s == 0, lat_blocks, jnp.where(d == 0, s - 1, lat_blocks - s))

    s_in = pl.pallas_call(
        functools.partial(_s5_scan_kernel, nb=nb),
        grid=(2, lat_blocks + 1),
        in_specs=[pl.BlockSpec((None, 2, g, p), lambda d, s: (d, 0, 0, 0)),
                  pl.BlockSpec((None, 2, nb, g, p), lambda d, s: (d, 0, blk(d, s), 0, 0))],
        out_specs=pl.BlockSpec((None, 2, nb, g, p), lambda d, s: (d, 0, blk(d, s), 0, 0)),
        out_shape=jax.ShapeDtypeStruct((2, 2, nc, g, p), F32),
        scratch_shapes=[pltpu.VMEM((2, g, p), F32)],
        compiler_params=_cparams(("arbitrary", "arbitrary")),
        name="s5_scan",
    )(a_chunk, e)
    s_in = s_in.transpose(3, 0, 1, 2, 4).reshape(g, 4, nc, p)

    z = pl.pallas_call(
        _s5_readout_kernel,
        grid=(g,),
        in_specs=[pl.BlockSpec((None, nc, w), lambda i: (i, 0, 0)),
                  pl.BlockSpec((None, w, w), lambda i: (i, 0, 0)),
                  pl.BlockSpec((None, 4, nc, p), lambda i: (i, 0, 0, 0)),
                  pl.BlockSpec((None, 4 * p, w), lambda i: (i, 0, 0)),
                  pl.BlockSpec((None, 1, w), lambda i: (i, 0, 0))],
        out_specs=pl.BlockSpec((None, nc, w), lambda i: (i, 0, 0)),
        out_shape=jax.ShapeDtypeStruct((g, nc, w), BF16),
        compiler_params=_cparams(("arbitrary",)),
        name="s5_readout",
    )(ug, m_tab, s_in, c_tab, d_tab)
    z = z.reshape(g, nc, ln, c).transpose(1, 2, 0, 3).reshape(t, g * c)

    tm = _row_tile(t, (1280, 640, 256, 128))
    return pl.pallas_call(
        _s5_glu_kernel,
        grid=(t // tm,),
        in_specs=[pl.BlockSpec((tm, g * c), lambda i: (i, 0)),
                  pl.BlockSpec((g * c, g * c), lambda i: (0, 0))],
        out_specs=pl.BlockSpec((tm, g * c), lambda i: (i, 0)),
        out_shape=jax.ShapeDtypeStruct((t, g * c), BF16),
        compiler_params=_cparams(("arbitrary",)),
        name="s5_glu",
    )(z, w_glu)


def _ret_tables(decay_logit):
    scale = HEAD_W ** -0.5
    lg = jax.nn.log_sigmoid(decay_logit.astype(F32))
    lf, lb = lg[0][:, None, None], lg[1][:, None, None]
    i = jnp.arange(RET_CHUNK, dtype=F32)[None, :, None]
    j = jnp.arange(RET_CHUNK, dtype=F32)[None, None, :]
    diff = i - j
    intra = (jnp.where(diff >= 0, jnp.exp(lf * jnp.maximum(diff, 0.0)), 0.0)
             + jnp.where(diff <= 0, jnp.exp(lb * jnp.maximum(-diff, 0.0)), 0.0)) * scale
    ones = jnp.ones((1, 1, HEAD_W), F32)
    q_f = jnp.exp(lf * (i + 1.0)) * ones
    k_f = jnp.exp(lf * (RET_CHUNK - 1.0 - i)) * scale * ones
    c_f = jnp.exp(lf * RET_CHUNK) * jnp.ones((1, RET_CHUNK, HEAD_W), F32)
    q_b = jnp.exp(lb * (RET_CHUNK - i)) * ones
    k_b = jnp.exp(lb * i) * scale * ones
    c_b = jnp.exp(lb * RET_CHUNK) * jnp.ones((1, RET_CHUNK, HEAD_W), F32)
    return intra, jnp.stack([q_f, k_f, c_f]), jnp.stack([q_b, k_b, c_b])


def _ret_state_step(q, k, v, dec_ref, s_scr, h):
    s = s_scr[h]
    qd = (q.astype(F32) * dec_ref[0, h]).astype(BF16)
    kd = (k.astype(F32) * dec_ref[1, h]).astype(BF16)
    o = jnp.dot(qd, s.astype(BF16), preferred_element_type=F32)
    s_scr[h] = s * dec_ref[2, h] + lax.dot_general(kd, v, (((0,), (0,)), ((), ())),
                                                    preferred_element_type=F32)
    return o


def _ret_fwd_kernel(intra_ref, dec_ref, q_ref, k_ref, v_ref, o_ref, s_scr):
    @pl.when(pl.program_id(0) == 0)
    def _():
        s_scr[...] = jnp.zeros_like(s_scr)

    for h in range(RET_HEADS):
        sl = slice(h * HEAD_W, (h + 1) * HEAD_W)
        q, k, v = q_ref[:, sl], k_ref[:, sl], v_ref[:, sl]
        att = lax.dot_general(q, k, (((1,), (1,)), ((), ())), preferred_element_type=F32) * intra_ref[h]
        o = jnp.dot(att.astype(BF16), v, preferred_element_type=F32)
        o_ref[:, sl] = o + _ret_state_step(q, k, v, dec_ref, s_scr, h)


def _ret_bwd_kernel(dec_ref, q_ref, k_ref, v_ref, g_ref, o1_ref, o_ref, s_scr):
    @pl.when(pl.program_id(0) == 0)
    def _():
        s_scr[...] = jnp.zeros_like(s_scr)

    for h in range(RET_HEADS):
        sl = slice(h * HEAD_W, (h + 1) * HEAD_W)
        q, k, v = q_ref[:, sl], k_ref[:, sl], v_ref[:, sl]
        o = o1_ref[:, sl] + _ret_state_step(q, k, v, dec_ref, s_scr, h)
        y = o * lax.rsqrt(jnp.mean(o * o, axis=-1, keepdims=True) + EPS)
        gate = g_ref[:, sl].astype(F32)
        o_ref[:, sl] = (y * (gate * jax.nn.sigmoid(gate))).astype(o_ref.dtype)


def _retention(proj, tables, n_lat, n_ctx):
    intra, dec_f, dec_b = tables
    t = proj.shape[0]
    wd = RET_HEADS * HEAD_W
    ck = RET_CHUNK
    n_lat_c, n_ctx_c = n_lat // ck, n_ctx // ck
    steps = n_lat_c + n_ctx_c

    def fwd_blk(s):
        return jnp.where(s < n_ctx_c, n_lat_c + s, s - n_ctx_c)

    def bwd_blk(s):
        return jnp.where(s < n_ctx_c, n_lat_c + n_ctx_c - 1 - s, n_lat_c - 1 - (s - n_ctx_c))

    def col_spec(col, order):
        return pl.BlockSpec((ck, wd), lambda s: (order(s), col))

    tab_spec = pl.BlockSpec((3, RET_HEADS, ck, HEAD_W), lambda s: (0, 0, 0, 0))
    o1 = pl.pallas_call(
        _ret_fwd_kernel,
        grid=(steps,),
        in_specs=[pl.BlockSpec((RET_HEADS, ck, ck), lambda s: (0, 0, 0)), tab_spec,
                  col_spec(0, fwd_blk), col_spec(1, fwd_blk), col_spec(2, fwd_blk)],
        out_specs=col_spec(0, fwd_blk),
        out_shape=jax.ShapeDtypeStruct((t, wd), F32),
        scratch_shapes=[pltpu.VMEM((RET_HEADS, HEAD_W, HEAD_W), F32)],
        compiler_params=_cparams(("arbitrary",)),
        name="retention_fwd",
    )(intra, dec_f, proj, proj, proj)
    return pl.pallas_call(
        _ret_bwd_kernel,
        grid=(steps,),
        in_specs=[tab_spec, col_spec(0, bwd_blk), col_spec(1, bwd_blk), col_spec(2, bwd_blk),
                  col_spec(3, bwd_blk), col_spec(0, bwd_blk)],
        out_specs=col_spec(0, bwd_blk),
        out_shape=jax.ShapeDtypeStruct((t, wd), BF16),
        scratch_shapes=[pltpu.VMEM((RET_HEADS, HEAD_W, HEAD_W), F32)],
        compiler_params=_cparams(("arbitrary",)),
        name="retention_bwd",
    )(dec_b, proj, proj, proj, proj, o1)


def _rope_tables(n_lat, n_ctx):
    n_freq = A_QK_DIM // 4
    freq = ROPE_BASE ** (-jnp.arange(n_freq, dtype=F32) / n_freq)
    rows = n_lat // GRID_W
    ang_row = jnp.arange(rows, dtype=F32)[:, None] * freq
    ang_col = jnp.arange(GRID_W, dtype=F32)[:, None] * freq
    reps = HEAD_W // A_QK_DIM

    def table(fn, ctx_value):
        r = jnp.broadcast_to(jnp.tile(fn(ang_row), (1, 2))[:, None, :], (rows, GRID_W, 2 * n_freq))
        c = jnp.broadcast_to(jnp.tile(fn(ang_col), (1, 2))[None, :, :], (rows, GRID_W, 2 * n_freq))
        lat = jnp.tile(jnp.concatenate([r, c], axis=-1).reshape(n_lat, A_QK_DIM), (1, reps))
        return jnp.concatenate([lat, jnp.full((n_ctx, HEAD_W), ctx_value, F32)], axis=0)

    return table(jnp.cos, 1.0), table(jnp.sin, 0.0)


def kernel(x, c, ctx, c_ctx, ada_w, ada_b, norm1_w, norm2_w, ffn_w13, ffn_w2, e_w_in, e_w_out, diff_lq1, diff_lk1, diff_lq2, diff_lk2, diff_subln_w, na_rpb, o_w_in, o_w_out, s5_lam_re, s5_lam_im, s5_b_re, s5_b_im, s5_c_re, s5_c_im, s5_log_step, s5_d, s5_w_glu, ret_decay_logit, final_norm_w):
    bsz, n_lat, d = x.shape
    n_ctx = ctx.shape[1]
    depth = ada_w.shape[0]
    assert bsz == 1
    xs = jnp.concatenate([x[0], ctx[0]], axis=0)
    mods_all = _mods(c, c_ctx, ada_w, ada_b)
    cos, sin = _rope_tables(n_lat, n_ctx)
    ret_w = RET_HEADS * HEAD_W
    ffn_w13_bf16 = ffn_w13.astype(BF16)
    e_w_in_bf16 = e_w_in.astype(BF16)

    for i in range(depth):
        compute_ctx = i != depth - 1
        mods = mods_all[i]
        j = i // 2
        if i % 2 == 0:
            lambda_init = 0.8 - 0.6 * math.exp(-0.3 * i)
            qkv = _inproj_even(xs, norm1_w[i], mods, n_lat, e_w_in_bf16, j, cos, sin)
            lam_args = (diff_lq1[j], diff_lk1[j], diff_lq2[j], diff_lk2[j], diff_subln_w[j])
            o_a = _diff_attn(qkv, *lam_args, n_lat, n_ctx, lambda_init,
                             tiles=(512, 2048, 512) if j == 0 else (512, 2048, 1024))
            o_b = _natten(qkv, _na_bias_tables(na_rpb[j], n_lat // GRID_W), n_lat, n_ctx)
            if compute_ctx:
                o_a = _diff_attn(qkv, *lam_args, n_lat, n_ctx, lambda_init, prev=o_a)
                o_b = _ctx_attn(qkv, o_b, n_lat, n_ctx)
            xs = _gated_residual([o_a, o_b], e_w_out, xs, mods, n_lat, 2, "outproj_even", layer=j)
        else:
            w_in = o_w_in[j]
            w_in = jnp.concatenate([w_in[:, S5_CH:], w_in[:, :S5_CH]], axis=1).astype(BF16)
            proj = _inproj_odd(xs, norm1_w[i], mods, n_lat, w_in)
            s5_tabs = _s5_tables(s5_lam_re[j], s5_lam_im[j], s5_b_re[j], s5_b_im[j], s5_c_re[j], s5_c_im[j],
                                 s5_log_step[j], s5_d[j])
            y_c = _s5_mixer(proj[:, 4 * ret_w:], s5_tabs, s5_w_glu[j].astype(BF16), n_lat, n_ctx)
            y_d = _retention(proj, _ret_tables(ret_decay_logit[j]), n_lat, n_ctx)
            xs = _gated_residual([y_c, y_d], o_w_out, xs, mods, n_lat, 2, "outproj_odd", layer=j)
        a = _ffn_up(xs, norm2_w[i], mods, n_lat, ffn_w13_bf16, i)
        xs = _gated_residual([a], ffn_w2[i].astype(BF16), xs, mods, n_lat, 5, "ffn_down")
    return _final_norm(xs, final_norm_w, n_lat)[None]
```

```python
import functools
import math

import jax
import jax.numpy as jnp
import numpy as np
from jax import lax
from jax.experimental import pallas as pl
from jax.experimental.pallas import tpu as pltpu

F32 = jnp.float32
BF16 = jnp.bfloat16

GRID_W = 64
A_HEADS = 8
A_QK_DIM = 64
HEAD_W = 128
B_HEADS = 8
WIN_R = 8
WIN_C = 16
NA_ROWS = 4
NA_HEADS_PER_STEP = 4
S5_CH = 512
S5_GROUP = 16
S5_GROUPS = 32
S5_STATE = 64
S5_CHUNK = 16
S5_BLOCK = 16
RET_HEADS = 12
RET_CHUNK = 128
RET_STEP_CHUNKS = 2
ROPE_BASE = 10000.0
EPS = 1e-6
NEG_INF = -1e30
VMEM_LIMIT = 56 * 1024 * 1024
N_MOD = 8
NORM_ROW_CHUNK = 128


def _cparams(sem):
    return pltpu.CompilerParams(dimension_semantics=sem, vmem_limit_bytes=VMEM_LIMIT)


def _row_tile(t, candidates):
    for c in candidates:
        if t % c == 0:
            return c
    raise ValueError(f"no row tile for {t}")


def _pick_mod(mods_ref, idx, is_ctx):
    return jnp.where(is_ctx, mods_ref[1, idx:idx + 1, :], mods_ref[0, idx:idx + 1, :])


def _is_ctx_rows(tm, n_lat, axis):
    row = pl.program_id(axis) * tm + lax.broadcasted_iota(jnp.int32, (tm, 1), 0)
    return row >= n_lat


def _mods_kernel(s_ref, w_ref, b_ref, o_ref):
    s = s_ref[...]
    s = s * jax.nn.sigmoid(s)
    o_ref[...] = jnp.dot(s, w_ref[...], preferred_element_type=F32,
                         precision=lax.Precision.HIGHEST) + b_ref[...]


def _mods(c, c_ctx, ada_w, ada_b):
    depth, d, w6 = ada_w.shape
    s = jnp.zeros((8, d), F32).at[0].set(c[0]).at[1].set(c_ctx)
    tn = 1024
    out = pl.pallas_call(
        _mods_kernel,
        grid=(depth, w6 // tn),
        in_specs=[pl.BlockSpec((8, d), lambda l, j: (0, 0)),
                  pl.BlockSpec((None, d, tn), lambda l, j: (l, 0, j)),
                  pl.BlockSpec((None, 1, tn), lambda l, j: (l, 0, j))],
        out_specs=pl.BlockSpec((None, 8, tn), lambda l, j: (l, 0, j)),
        out_shape=jax.ShapeDtypeStruct((depth, 8, w6), F32),
        compiler_params=_cparams(("arbitrary", "arbitrary")),
        name="ada_mods",
    )(s, ada_w, ada_b.reshape(depth, 1, w6))
    m = out[:, :2].reshape(depth, 2, 6, d)
    return jnp.pad(m, ((0, 0), (0, 0), (0, N_MOD - 6), (0, 0)))


def _normmod_prologue(x_ref, nw_ref, mods_ref, h_scr, *, n_lat, tm, shift_idx, scale_idx):
    @pl.when(pl.program_id(1) == 0)
    def _():
        rc = NORM_ROW_CHUNK
        assert tm % rc == 0 and n_lat % rc == 0

        def chunk(c, carry):
            r0 = pl.multiple_of(c * rc, rc)
            seg = (pl.program_id(0) * tm + r0 >= n_lat).astype(jnp.int32)
            gain = nw_ref[...] * (1.0 + mods_ref[seg, scale_idx:scale_idx + 1, :])
            shift = mods_ref[seg, shift_idx:shift_idx + 1, :]
            x = x_ref[pl.ds(r0, rc), :]
            r = lax.rsqrt(jnp.mean(x * x, axis=-1, keepdims=True) + EPS)
            h_scr[pl.ds(r0, rc), :] = (x * r * gain + shift).astype(h_scr.dtype)
            return carry

        lax.fori_loop(0, tm // rc, chunk, 0)


def _norm_proj_call(kern, x, norm_w, mods, n_lat, shift_idx, scale_idx, w_args, w_specs, extra_args, extra_specs,
                    n_out, tn, tm, name, **kern_kw):
    t, d = x.shape
    norm = dict(n_lat=n_lat, tm=tm, shift_idx=shift_idx, scale_idx=scale_idx)
    return pl.pallas_call(
        functools.partial(kern, norm=norm, **kern_kw),
        grid=(t // tm, n_out // tn),
        in_specs=[pl.BlockSpec((tm, d), lambda i, j: (i, 0)),
                  pl.BlockSpec((1, d), lambda i, j: (0, 0)),
                  pl.BlockSpec((2, N_MOD, d), lambda i, j: (0, 0, 0))] + w_specs + extra_specs,
        out_specs=pl.BlockSpec((tm, tn), lambda i, j: (i, j)),
        out_shape=jax.ShapeDtypeStruct((t, n_out), BF16),
        scratch_shapes=[pltpu.VMEM((tm, d), BF16)],
        compiler_params=_cparams(("arbitrary", "arbitrary")),
        name=name,
    )(x, norm_w.reshape(1, d), mods, *w_args, *extra_args)


def _final_norm_kernel(x_ref, w_ref, o_ref):
    x = x_ref[...]
    o_ref[...] = x * lax.rsqrt(jnp.mean(x * x, axis=-1, keepdims=True) + EPS) * w_ref[...]


def _final_norm(x, w, n_lat):
    d = x.shape[1]
    tm = _row_tile(n_lat, (512, 256, 128))
    return pl.pallas_call(
        _final_norm_kernel,
        grid=(n_lat // tm,),
        in_specs=[pl.BlockSpec((tm, d), lambda i: (i, 0)),
                  pl.BlockSpec((1, d), lambda i: (0, 0))],
        out_specs=pl.BlockSpec((tm, d), lambda i: (i, 0)),
        out_shape=jax.ShapeDtypeStruct((n_lat, d), F32),
        compiler_params=_cparams(("arbitrary",)),
        name="final_norm",
    )(x, w.reshape(1, d))


def _rope_store(acc, cos, sin, o_ref, scale):
    first_half = (lax.broadcasted_iota(jnp.int32, (1, HEAD_W), 1) % 32) < 16
    for c in range(acc.shape[1] // HEAD_W):
        x = acc[:, c * HEAD_W:(c + 1) * HEAD_W]
        rot = jnp.where(first_half, -pltpu.roll(x, HEAD_W - 16, 1), pltpu.roll(x, 16, 1))
        o_ref[:, c * HEAD_W:(c + 1) * HEAD_W] = ((x * cos + rot * sin) * scale).astype(o_ref.dtype)


def _resident_bf16(w_ref, w_scr):
    @pl.when(pl.program_id(1) == 0)
    def _():
        w_scr[...] = w_ref[...].astype(BF16)

    return w_scr


def _inproj_even_kernel(x_ref, nw_ref, mods_ref, w_ref, cos_ref, sin_ref, o_ref, h_scr, *, norm, a_scale, b_scale):
    _normmod_prologue(x_ref, nw_ref, mods_ref, h_scr, **norm)
    j = pl.program_id(1)
    acc = jnp.dot(h_scr[...], w_ref[...], preferred_element_type=F32)

    @pl.when(j == 0)
    def _():
        _rope_store(acc, cos_ref[...], sin_ref[...], o_ref, a_scale)

    @pl.when(j == 1)
    def _():
        _rope_store(acc, cos_ref[...], sin_ref[...], o_ref, 1.0)

    @pl.when(j == 3)
    def _():
        o_ref[...] = (acc * b_scale).astype(o_ref.dtype)

    @pl.when((j == 2) | (j > 3))
    def _():
        o_ref[...] = acc.astype(o_ref.dtype)


def _inproj_even(x, norm_w, mods, n_lat, w, layer, cos, sin):
    t, d = x.shape
    n = w.shape[2]
    tn = n // 6
    tm = _row_tile(t, (1280, 640, 256, 128))
    rope_spec = pl.BlockSpec((tm, HEAD_W), lambda i, j: (i, 0))
    return _norm_proj_call(
        _inproj_even_kernel, x, norm_w, mods, n_lat, 0, 1,
        [w], [pl.BlockSpec((None, d, tn), lambda i, j: (layer, 0, j))],
        [cos, sin], [rope_spec, rope_spec], n, tn, tm, "inproj_even",
        a_scale=A_QK_DIM ** -0.5 * math.log2(math.e), b_scale=HEAD_W ** -0.5 * math.log2(math.e))


def _inproj_odd_kernel(x_ref, nw_ref, mods_ref, w_ref, o_ref, h_scr, *, norm):
    _normmod_prologue(x_ref, nw_ref, mods_ref, h_scr, **norm)
    o_ref[...] = jnp.dot(h_scr[...], w_ref[...], preferred_element_type=F32).astype(o_ref.dtype)


def _inproj_odd(x, norm_w, mods, n_lat, w):
    t, d = x.shape
    n = w.shape[1]
    tn = 512
    tm = _row_tile(t, (1280, 640, 256, 128))
    return _norm_proj_call(
        _inproj_odd_kernel, x, norm_w, mods, n_lat, 0, 1,
        [w], [pl.BlockSpec((d, tn), lambda i, j: (0, j))], [], [], n, tn, tm, "inproj_odd")


def _ffn_up_kernel(x_ref, nw_ref, mods_ref, w1_ref, w3_ref, o_ref, h_scr, *, norm):
    _normmod_prologue(x_ref, nw_ref, mods_ref, h_scr, **norm)
    h = h_scr[...]
    a = jnp.dot(h, w1_ref[...], preferred_element_type=F32)
    b = jnp.dot(h, w3_ref[...], preferred_element_type=F32)
    o_ref[...] = (a * jax.nn.sigmoid(a) * b).astype(o_ref.dtype)


def _ffn_up(x, norm_w, mods, n_lat, w13, layer):
    t, d = x.shape
    d_ff = w13.shape[2] // 2
    tn = 512
    nj = d_ff // tn
    tm = _row_tile(t, (1280, 640, 256, 128))
    return _norm_proj_call(
        _ffn_up_kernel, x, norm_w, mods, n_lat, 3, 4,
        [w13, w13], [pl.BlockSpec((None, d, tn), lambda i, j: (layer, 0, j)),
                     pl.BlockSpec((None, d, tn), lambda i, j: (layer, 0, j + nj))],
        [], [], d_ff, tn, tm, "ffn_up")


def _gated_residual_kernel(*refs, n_a, n_lat, tm, gate_idx):
    a_refs = refs[:n_a]
    w_ref, x_ref, mods_ref, o_ref = refs[n_a:n_a + 4]
    if w_ref.dtype != BF16:
        w_ref = _resident_bf16(w_ref, refs[n_a + 4])
    k0 = 0
    y = None
    for a_ref in a_refs:
        kk = a_ref.shape[1]
        part = jnp.dot(a_ref[...], w_ref[k0:k0 + kk, :], preferred_element_type=F32)
        y = part if y is None else y + part
        k0 += kk
    gate = _pick_mod(mods_ref, gate_idx, _is_ctx_rows(tm, n_lat, 1))
    o_ref[...] = x_ref[...] + gate * y


def _gated_residual(a_list, w, x, mods, n_lat, gate_idx, name, layer=None):
    t, d = x.shape
    k = w.shape[-2]
    tn = 512
    tm = _row_tile(t, (640, 256, 128))
    n_a = len(a_list)
    in_specs = [pl.BlockSpec((tm, a.shape[1]), lambda j, i: (i, 0)) for a in a_list]
    w_spec = (pl.BlockSpec((k, tn), lambda j, i: (0, j)) if layer is None
              else pl.BlockSpec((None, k, tn), lambda j, i: (layer, 0, j)))
    in_specs += [w_spec,
                 pl.BlockSpec((tm, tn), lambda j, i: (i, j)),
                 pl.BlockSpec((2, N_MOD, tn), lambda j, i: (0, 0, j))]
    return pl.pallas_call(
        functools.partial(_gated_residual_kernel, n_a=n_a, n_lat=n_lat, tm=tm, gate_idx=gate_idx),
        grid=(d // tn, t // tm),
        in_specs=in_specs,
        out_specs=pl.BlockSpec((tm, tn), lambda j, i: (i, j)),
        out_shape=jax.ShapeDtypeStruct((t, d), F32),
        scratch_shapes=[] if w.dtype == BF16 else [pltpu.VMEM((k, tn), BF16)],
        input_output_aliases={n_a + 1: 0},
        compiler_params=_cparams(("arbitrary", "arbitrary")),
        name=name,
    )(*a_list, w, x, mods)


def _softmax_block(qs, k, v):
    s = lax.dot_general(qs, k, (((1,), (1,)), ((), ())), preferred_element_type=F32)
    m = jnp.max(s, axis=-1, keepdims=True)
    p = jnp.exp2(s - m)
    l = jnp.sum(p, axis=-1, keepdims=True)
    acc = jnp.dot(p.astype(v.dtype), v, preferred_element_type=F32)
    return m, l, acc


DIFF_Q_TILE = 512
DIFF_K_CHUNK = 2048
DIFF_PV_ROWS = 512
DIFF_SOFTMAX_ROWS = 64
DIFF_EXP_DTYPE = BF16


def _diff_attn_kernel(*refs, tq, tk, n_ctx, n_chunks, lambda_init):
    if n_chunks > 1:
        (lq1, lk1, lq2, lk2, subw, q_ref, kc_ref, vc_ref, k_ref, v_ref, o_ref,
         qs_scr, sa_scr, mca_scr, p_scr, m_scr, l_scr, alpha_scr, acc_scr, sb_scr, mcb_scr) = refs
        buf_b = (sb_scr, mcb_scr)
    elif n_chunks == 1:
        (lq1, lk1, lq2, lk2, subw, q_ref, kc_ref, vc_ref, k_ref, v_ref, o_ref,
         qs_scr, sa_scr, mca_scr, p_scr, m_scr, l_scr, alpha_scr, acc_scr) = refs
    else:
        (lq1, lk1, lq2, lk2, subw, q_ref, kc_ref, vc_ref, o_ref,
         qs_scr, sa_scr, mca_scr, p_scr, m_scr, l_scr, alpha_scr, acc_scr) = refs
    buf_a = (sa_scr, mca_scr)
    nt = (((1,), (1,)), ((), ()))
    rows_all = 2 * tq
    pv_rows = min(DIFF_PV_ROWS, rows_all)
    sm_rows = min(DIFF_SOFTMAX_ROWS, pv_rows)
    q = q_ref[...]
    comp1 = lax.broadcasted_iota(jnp.int32, (1, HEAD_W), 1) < A_QK_DIM
    zero = jnp.zeros_like(q)
    qs_scr[0:tq, :] = jnp.where(comp1, q, zero)
    qs_scr[tq:2 * tq, :] = jnp.where(comp1, zero, q)

    def scores(buf, rows, keys):
        s_buf, mc_buf = buf
        s = lax.dot_general(qs_scr[rows, :], keys, nt, preferred_element_type=F32)
        width = s.shape[1]
        s_buf[rows, 0:width] = s
        mp = s[:, 0:HEAD_W]
        for c in range(1, width // HEAD_W):
            mp = jnp.maximum(mp, s[:, c * HEAD_W:(c + 1) * HEAD_W])
        mc_buf[rows, :] = jnp.broadcast_to(jnp.max(mp, axis=-1, keepdims=True), mp.shape)

    def softmax_rows(buf, rows, n_tiles, first):
        s_buf, mc_buf = buf
        m_cur = mc_buf[rows, :]
        if first:
            m_new = m_cur
        else:
            m_prev = m_scr[rows, :]
            m_new = jnp.maximum(m_prev, m_cur)
        lp = None
        for c in range(n_tiles):
            pc = jnp.exp2((s_buf[rows, c * HEAD_W:(c + 1) * HEAD_W] - m_new).astype(DIFF_EXP_DTYPE))
            p_scr[rows, c * HEAD_W:(c + 1) * HEAD_W] = pc.astype(BF16)
            lp = pc if lp is None else lp + pc
        l_cur = jnp.sum(lp.astype(F32), axis=-1, keepdims=True)
        if first:
            l_scr[rows, :] = jnp.broadcast_to(l_cur, m_new.shape)
        else:
            alpha = jnp.exp2(m_prev - m_new)
            alpha_scr[rows, :] = alpha
            l_scr[rows, :] = alpha * l_scr[rows, :] + l_cur
        m_scr[rows, :] = m_new

    def softmax_pv(buf, width, load_v, first, after_group=None):
        for r in range(rows_all // pv_rows):
            for r2 in range(pv_rows // sm_rows):
                r0 = r * pv_rows + r2 * sm_rows
                softmax_rows(buf, slice(r0, r0 + sm_rows), width // HEAD_W, first)
            rows = slice(r * pv_rows, (r + 1) * pv_rows)
            pv = jnp.dot(p_scr[rows, 0:width], load_v(), preferred_element_type=F32)
            acc_scr[rows, :] = pv if first else alpha_scr[rows, :] * acc_scr[rows, :] + pv
            if after_group is not None:
                after_group(rows)

    def all_groups(fn):
        for r in range(rows_all // pv_rows):
            fn(slice(r * pv_rows, (r + 1) * pv_rows))

    def qk_rows(j, buf, rows):
        off = pl.multiple_of(j * tk, tk)
        scores(buf, rows, k_ref[pl.ds(off, tk), :])

    all_groups(lambda rows: scores(buf_a, rows, kc_ref[...]))
    softmax_pv(buf_a, n_ctx, lambda: vc_ref[...], True,
               (lambda rows: qk_rows(0, buf_a, rows)) if n_chunks else None)

    def soft_pv(j, buf, j_next=None, buf_next=None):
        off = pl.multiple_of(j * tk, tk)
        after = None if j_next is None else (lambda rows: qk_rows(j_next, buf_next, rows))
        softmax_pv(buf, tk, lambda: v_ref[pl.ds(off, tk), :], False, after)

    if n_chunks == 1:
        soft_pv(0, buf_a)
    elif n_chunks > 1:

        def pair(j2, carry):
            soft_pv(2 * j2, buf_a, 2 * j2 + 1, buf_b)
            soft_pv(2 * j2 + 1, buf_b, 2 * j2 + 2, buf_a)
            return carry

        lax.fori_loop(0, n_chunks // 2 - 1, pair, 0)
        soft_pv(n_chunks - 2, buf_a, n_chunks - 1, buf_b)
        soft_pv(n_chunks - 1, buf_b)

    lam = (jnp.exp(jnp.sum(lq1[...] * lk1[...], axis=-1, keepdims=True))
           - jnp.exp(jnp.sum(lq2[...] * lk2[...], axis=-1, keepdims=True)) + lambda_init)
    o1 = acc_scr[0:tq, :] / l_scr[0:tq, :]
    o2 = acc_scr[tq:2 * tq, :] / l_scr[tq:2 * tq, :]
    o = o1 - lam * o2
    y = o * lax.rsqrt(jnp.mean(o * o, axis=-1, keepdims=True) + EPS) * subw[...]
    o_ref[...] = (y * (1.0 - lambda_init)).astype(o_ref.dtype)


def _diff_attn(qkv, lq1, lk1, lq2, lk2, subw, n_lat, n_ctx, lambda_init, prev=None):
    t = qkv.shape[0]
    use_lat = prev is None
    ctx_blk = n_lat // n_ctx
    small = [lq1.reshape(1, -1), lk1.reshape(1, -1), lq2.reshape(1, -1), lk2.reshape(1, -1), subw.reshape(1, -1)]
    small_specs = [pl.BlockSpec(a.shape, lambda h, qi: (0, 0)) for a in small]
    kc_spec = pl.BlockSpec((n_ctx, HEAD_W), lambda h, qi: (ctx_blk, A_HEADS + h))
    vc_spec = pl.BlockSpec((n_ctx, HEAD_W), lambda h, qi: (ctx_blk, 2 * A_HEADS + h))
    if use_lat:
        tq = _row_tile(n_lat, tuple(c for c in (512, 256, 128) if c <= DIFF_Q_TILE))
        tk = _row_tile(n_lat, tuple(c for c in (2048, 1024, 512, 256, 128) if c <= DIFF_K_CHUNK))
        n_chunks = n_lat // tk
        assert n_chunks == 1 or n_chunks % 2 == 0
        grid = (A_HEADS, n_lat // tq)
        in_specs = small_specs + [
            pl.BlockSpec((tq, HEAD_W), lambda h, qi: (qi, h)), kc_spec, vc_spec,
            pl.BlockSpec((n_lat, HEAD_W), lambda h, qi: (0, A_HEADS + h), pipeline_mode=pl.Buffered(1)),
            pl.BlockSpec((n_lat, HEAD_W), lambda h, qi: (0, 2 * A_HEADS + h), pipeline_mode=pl.Buffered(1))]
        args = small + [qkv, qkv, qkv, qkv, qkv]
        out_spec = pl.BlockSpec((tq, HEAD_W), lambda h, qi: (qi, h))
        aliases = {}
    else:
        tq, tk, n_chunks = n_ctx, 0, 0
        grid = (A_HEADS, 1)
        in_specs = small_specs + [
            pl.BlockSpec((tq, HEAD_W), lambda h, qi: (ctx_blk, h)), kc_spec, vc_spec,
            pl.BlockSpec(memory_space=pl.ANY)]
        args = small + [qkv, qkv, qkv, prev]
        out_spec = pl.BlockSpec((tq, HEAD_W), lambda h, qi: (ctx_blk, h))
        aliases = {len(args) - 1: 0}

    sw = max(tk, n_ctx)
    kern = functools.partial(_diff_attn_kernel, tq=tq, tk=tk, n_ctx=n_ctx, n_chunks=n_chunks,
                             lambda_init=lambda_init)
    if not use_lat:
        inner = kern

        def kern(*refs):
            n_in = len(args)
            inner(*refs[:n_in - 1], *refs[n_in:])

    return pl.pallas_call(
        kern,
        grid=grid,
        in_specs=in_specs,
        out_specs=out_spec,
        out_shape=jax.ShapeDtypeStruct((t, A_HEADS * HEAD_W), BF16),
        scratch_shapes=[pltpu.VMEM((2 * tq, HEAD_W), BF16),
                        pltpu.VMEM((2 * tq, sw), F32),
                        pltpu.VMEM((2 * tq, HEAD_W), F32),
                        pltpu.VMEM((2 * tq, sw), BF16),
                        pltpu.VMEM((2 * tq, HEAD_W), F32),
                        pltpu.VMEM((2 * tq, HEAD_W), F32),
                        pltpu.VMEM((2 * tq, HEAD_W), F32),
                        pltpu.VMEM((2 * tq, HEAD_W), F32)]
        + ([pltpu.VMEM((2 * tq, sw), F32),
            pltpu.VMEM((2 * tq, HEAD_W), F32)] if n_chunks > 1 else []),
        input_output_aliases=aliases,
        compiler_params=_cparams(("arbitrary", "arbitrary")),
        name="diff_attn" if use_lat else "diff_attn_ctx",
    )(*args)


def _na_bias_tables(rpb, rows):
    nblk = rows // NA_ROWS
    wr = min(WIN_R, rows)
    nh = rpb.shape[0]
    rq = jnp.arange(NA_ROWS)[:, None, None]
    slot = jnp.arange(3)[None, :, None]
    rk = jnp.arange(NA_ROWS)[None, None, :]
    row_sel, row_ok = [], []
    for b, dup in ((0, 0), (1, -1), (nblk - 1, 2)):
        r = NA_ROWS * b + rq
        r0 = jnp.clip(r - wr // 2, 0, rows - wr)
        rkey = NA_ROWS * (b - 1 + slot) + rk
        ok = (rkey >= r0) & (rkey < r0 + wr) & (slot != dup)
        drow = jnp.broadcast_to(rkey - r + (WIN_R - 1), ok.shape)
        row_sel.append((drow[..., None] == jnp.arange(2 * WIN_R - 1)) & ok[..., None])
        row_ok.append(ok)
    row_sel = jnp.stack(row_sel).astype(F32)
    row_ok = jnp.stack(row_ok)
    jq = jnp.arange(GRID_W)[:, None]
    jk = jnp.arange(GRID_W)[None, :]
    c0 = jnp.clip(jq - WIN_C // 2, 0, GRID_W - WIN_C)
    col_ok = (jk >= c0) & (jk < c0 + WIN_C)
    dcol = jnp.clip(jk - jq + (WIN_C - 1), 0, 2 * WIN_C - 2)
    col_sel = (dcol[..., None] == jnp.arange(2 * WIN_C - 1)).astype(F32)
    bias = jnp.einsum('vasbr,hrc,qkc->hvaqsbk', row_sel, rpb.astype(F32), col_sel,
                      precision=lax.Precision.HIGHEST)
    valid = row_ok[None, :, :, None, :, :, None] & col_ok[None, None, None, :, None, None, :]
    tab = jnp.where(valid, bias * math.log2(math.e), NEG_INF)
    return tab.reshape(nh, 3, NA_ROWS * GRID_W, 3 * NA_ROWS * GRID_W)


def _natten_kernel(tab_ref, q_ref, k0_ref, k1_ref, k2_ref, v0_ref, v1_ref, v2_ref, kc_ref, vc_ref, o_ref):
    nt = (((1,), (1,)), ((), ()))
    blk = q_ref.shape[0]
    for hh in range(NA_HEADS_PER_STEP):
        hs = slice(hh * HEAD_W, (hh + 1) * HEAD_W)
        q = q_ref[:, hs]
        s_loc = [lax.dot_general(q, k_ref[:, hs], nt, preferred_element_type=F32)
                 + tab_ref[hh, :, i * blk:(i + 1) * blk] for i, k_ref in enumerate((k0_ref, k1_ref, k2_ref))]
        s_ctx = lax.dot_general(q, kc_ref[:, hs], nt, preferred_element_type=F32)
        m = jnp.max(s_ctx, axis=-1, keepdims=True)
        for s in s_loc:
            m = jnp.maximum(m, jnp.max(s, axis=-1, keepdims=True))
        l = None
        acc = None
        for s, v_ref in zip([s_ctx] + s_loc, (vc_ref, v0_ref, v1_ref, v2_ref)):
            p = jnp.exp2((s - m).astype(BF16))
            pt = p[:, 0:HEAD_W]
            for c in range(1, p.shape[1] // HEAD_W):
                pt = pt + p[:, c * HEAD_W:(c + 1) * HEAD_W]
            lp = jnp.sum(pt.astype(F32), axis=-1, keepdims=True)
            pv = jnp.dot(p, v_ref[:, hs], preferred_element_type=F32)
            l = lp if l is None else l + lp
            acc = pv if acc is None else acc + pv
        o_ref[:, hs] = (acc / l).astype(o_ref.dtype)


def _natten(qkv, tabs, n_lat, n_ctx):
    t = qkv.shape[0]
    blk = NA_ROWS * GRID_W
    nblk = n_lat // blk
    assert nblk >= 3 and blk == n_ctx
    nh = NA_HEADS_PER_STEP
    hw = nh * HEAD_W
    qo, ko, vo = (3 * A_HEADS) // nh, (3 * A_HEADS + B_HEADS) // nh, (3 * A_HEADS + 2 * B_HEADS) // nh
    ctx_blk = n_lat // n_ctx

    def kv_spec(off, shift):
        return pl.BlockSpec((blk, hw), lambda h, b: (jnp.clip(b + shift, 0, nblk - 1), off + h))

    return pl.pallas_call(
        _natten_kernel,
        grid=(B_HEADS // nh, nblk),
        in_specs=[pl.BlockSpec((nh, None, blk, 3 * blk),
                               lambda h, b: (h, jnp.where(b == 0, 0, jnp.where(b == nblk - 1, 2, 1)), 0, 0)),
                  pl.BlockSpec((blk, hw), lambda h, b: (b, qo + h)),
                  kv_spec(ko, -1), kv_spec(ko, 0), kv_spec(ko, 1),
                  kv_spec(vo, -1), kv_spec(vo, 0), kv_spec(vo, 1),
                  pl.BlockSpec((n_ctx, hw), lambda h, b: (ctx_blk, ko + h)),
                  pl.BlockSpec((n_ctx, hw), lambda h, b: (ctx_blk, vo + h))],
        out_specs=pl.BlockSpec((blk, hw), lambda h, b: (b, h)),
        out_shape=jax.ShapeDtypeStruct((t, B_HEADS * HEAD_W), BF16),
        compiler_params=_cparams(("arbitrary", "arbitrary")),
        name="natten",
    )(tabs, qkv, qkv, qkv, qkv, qkv, qkv, qkv, qkv, qkv)


def _ctx_attn_kernel(q_ref, k_ref, v_ref, prev_ref, o_ref):
    del prev_ref
    _, l, acc = _softmax_block(q_ref[...], k_ref[...], v_ref[...])
    o_ref[...] = (acc / l).astype(o_ref.dtype)


def _ctx_attn(qkv, prev, n_lat, n_ctx):
    qo, ko, vo = 3 * A_HEADS, 3 * A_HEADS + B_HEADS, 3 * A_HEADS + 2 * B_HEADS
    ctx_blk = n_lat // n_ctx
    return pl.pallas_call(
        _ctx_attn_kernel,
        grid=(B_HEADS,),
        in_specs=[pl.BlockSpec((n_ctx, HEAD_W), lambda h: (ctx_blk, qo + h)),
                  pl.BlockSpec((n_ctx, HEAD_W), lambda h: (ctx_blk, ko + h)),
                  pl.BlockSpec((n_ctx, HEAD_W), lambda h: (ctx_blk, vo + h)),
                  pl.BlockSpec(memory_space=pl.ANY)],
        out_specs=pl.BlockSpec((n_ctx, HEAD_W), lambda h: (ctx_blk, h)),
        out_shape=jax.ShapeDtypeStruct(prev.shape, prev.dtype),
        input_output_aliases={3: 0},
        compiler_params=_cparams(("arbitrary",)),
        name="ctx_attn",
    )(qkv, qkv, qkv, prev)


def _s5_tables(lam_re, lam_im, b_re, b_im, c_re, c_im, log_step, d_skip):
    hp = lax.Precision.HIGHEST
    ln = S5_CHUNK
    g, p, c = S5_GROUPS, S5_STATE, S5_GROUP
    lr, li = lam_re.astype(F32), lam_im.astype(F32)
    dt = jnp.exp(log_step.astype(F32))[:, :, None]
    mag = jnp.exp(lr * dt)
    ar, ai = mag * jnp.cos(li * dt), mag * jnp.sin(li * dt)
    den = lr * lr + li * li
    nr, ni = ar - 1.0, ai
    fr = (nr * lr + ni * li) / den
    fi = (ni * lr - nr * li) / den
    br_, bi_ = b_re.astype(F32), b_im.astype(F32)
    bbr = fr[..., None] * br_ - fi[..., None] * bi_
    bbi = fr[..., None] * bi_ + fi[..., None] * br_
    lag = jnp.arange(ln + 1, dtype=F32)[:, None, None, None]
    magl = jnp.exp(lr * dt * lag)
    pr, pi_ = magl * jnp.cos(li * dt * lag), magl * jnp.sin(li * dt * lag)
    wr = pr[..., None] * bbr - pi_[..., None] * bbi
    wi = pr[..., None] * bbi + pi_[..., None] * bbr
    cr, ci = c_re.astype(F32), c_im.astype(F32)
    kern = (jnp.einsum('dgcp,ldgpe->ldgce', cr, wr[:ln], precision=hp)
            - jnp.einsum('dgcp,ldgpe->ldgce', ci, wi[:ln], precision=hp))
    lag = np.arange(ln)[:, None, None]
    s_idx = np.arange(ln)[None, :, None]
    t_idx = np.arange(ln)[None, None, :]
    sel_f = jnp.asarray(t_idx - s_idx == lag, F32)
    sel_b = jnp.asarray(s_idx - t_idx == lag, F32)
    kfb = (jnp.einsum('lst,lgce->stgce', sel_f, kern[:, 0], precision=hp)
           + jnp.einsum('lst,lgce->stgce', sel_b, kern[:, 1], precision=hp))
    m_tab = kfb.transpose(2, 0, 4, 1, 3).reshape(g, ln * c, ln * c)
    wf_r, wf_i = wr[:ln, 0][::-1], wi[:ln, 0][::-1]
    wb_r, wb_i = wr[:ln, 1], wi[:ln, 1]

    def inj(w):
        return w.transpose(1, 0, 3, 2).reshape(g, ln * c, p)

    b_tab = jnp.concatenate([inj(wf_r), inj(wf_i), inj(wb_r), inj(wb_i)], axis=-1)
    pf_r, pf_i = pr[1:ln + 1, 0], pi_[1:ln + 1, 0]
    pb_r, pb_i = pr[1:ln + 1, 1][::-1], pi_[1:ln + 1, 1][::-1]

    def rd(pw_r, pw_i, cre, cim):
        car = cre[None] * pw_r[:, :, None, :] - cim[None] * pw_i[:, :, None, :]
        cai = cre[None] * pw_i[:, :, None, :] + cim[None] * pw_r[:, :, None, :]
        to_rows = lambda a: a.transpose(1, 3, 0, 2).reshape(g, p, ln * c)
        return to_rows(car), to_rows(-cai)

    c_tab = jnp.concatenate(rd(pf_r, pf_i, cr[0], ci[0]) + rd(pb_r, pb_i, cr[1], ci[1]), axis=1)
    a_chunk = jnp.stack([jnp.stack([pr[ln, 0], pi_[ln, 0]]), jnp.stack([pr[ln, 1], pi_[ln, 1]])])
    d_tab = jnp.tile(d_skip.astype(F32).reshape(g, 1, c), (1, ln, 1)).reshape(g, 1, ln * c)
    return m_tab.astype(BF16), b_tab.astype(BF16), c_tab.astype(BF16), a_chunk, d_tab


def _s5_inject_kernel(u_ref, b_ref, o_ref):
    o_ref[...] = jnp.dot(u_ref[...], b_ref[...], preferred_element_type=F32)


def _s5_scan_kernel(a_ref, e_ref, o_ref, s_scr, *, nb):
    d = pl.program_id(0)

    @pl.when(pl.program_id(1) == 0)
    def _():
        s_scr[...] = jnp.zeros_like(s_scr)

    ar, ai = a_ref[0], a_ref[1]

    def body(i, carry):
        sr, si = carry
        c = jnp.where(d == 0, i, nb - 1 - i)
        o_ref[0, c] = sr
        o_ref[1, c] = si
        return ar * sr - ai * si + e_ref[0, c], ar * si + ai * sr + e_ref[1, c]

    sr, si = lax.fori_loop(0, nb, body, (s_scr[0], s_scr[1]))
    s_scr[0] = sr
    s_scr[1] = si


def _gelu_tanh(x):
    return 0.5 * x * (1.0 + jnp.tanh(math.sqrt(2.0 / math.pi) * (x + 0.044715 * (x * x * x))))


def _s5_readout_kernel(u_ref, m_ref, s_ref, c_ref, d_ref, o_ref):
    u = u_ref[...]
    y = jnp.dot(u, m_ref[...], preferred_element_type=F32) + u.astype(F32) * d_ref[...]
    for k in range(4):
        y = y + jnp.dot(s_ref[k].astype(BF16), c_ref[k * S5_STATE:(k + 1) * S5_STATE, :],
                        preferred_element_type=F32)
    o_ref[...] = _gelu_tanh(y).astype(o_ref.dtype)


def _s5_glu_kernel(z_ref, w_ref, o_ref):
    z = z_ref[...]
    gate = jax.nn.sigmoid(jnp.dot(z.astype(BF16), w_ref[...], preferred_element_type=F32))
    o_ref[...] = (z.astype(F32) * gate).astype(o_ref.dtype)


def _s5_mixer(u, tables, w_glu, n_lat, n_ctx):
    m_tab, b_tab, c_tab, a_chunk, d_tab = tables
    t = u.shape[0]
    g, c, ln, p = S5_GROUPS, S5_GROUP, S5_CHUNK, S5_STATE
    nc = t // ln
    w = ln * c
    ug = u.reshape(nc, ln, g, c).transpose(2, 0, 1, 3).reshape(g, nc, w)

    e = pl.pallas_call(
        _s5_inject_kernel,
        grid=(g,),
        in_specs=[pl.BlockSpec((None, nc, w), lambda i: (i, 0, 0)),
                  pl.BlockSpec((None, w, 4 * p), lambda i: (i, 0, 0))],
        out_specs=pl.BlockSpec((None, nc, 4 * p), lambda i: (i, 0, 0)),
        out_shape=jax.ShapeDtypeStruct((g, nc, 4 * p), F32),
        compiler_params=_cparams(("arbitrary",)),
        name="s5_inject",
    )(ug, b_tab)
    e = e.reshape(g, nc, 2, 2, p).transpose(2, 3, 1, 0, 4)

    nb = S5_BLOCK
    assert (n_lat // ln) % nb == 0 and n_ctx // ln == nb
    lat_blocks = n_lat // ln // nb

    def blk(d, s):
        return jnp.where(s == 0, lat_blocks, jnp.where(d == 0, s - 1, lat_blocks - s))

    s_in = pl.pallas_call(
        functools.partial(_s5_scan_kernel, nb=nb),
        grid=(2, lat_blocks + 1),
        in_specs=[pl.BlockSpec((None, 2, g, p), lambda d, s: (d, 0, 0, 0)),
                  pl.BlockSpec((None, 2, nb, g, p), lambda d, s: (d, 0, blk(d, s), 0, 0))],
        out_specs=pl.BlockSpec((None, 2, nb, g, p), lambda d, s: (d, 0, blk(d, s), 0, 0)),
        out_shape=jax.ShapeDtypeStruct((2, 2, nc, g, p), F32),
        scratch_shapes=[pltpu.VMEM((2, g, p), F32)],
        compiler_params=_cparams(("arbitrary", "arbitrary")),
        name="s5_scan",
    )(a_chunk, e)
    s_in = s_in.transpose(3, 0, 1, 2, 4).reshape(g, 4, nc, p)

    z = pl.pallas_call(
        _s5_readout_kernel,
        grid=(g,),
        in_specs=[pl.BlockSpec((None, nc, w), lambda i: (i, 0, 0)),
                  pl.BlockSpec((None, w, w), lambda i: (i, 0, 0)),
                  pl.BlockSpec((None, 4, nc, p), lambda i: (i, 0, 0, 0)),
                  pl.BlockSpec((None, 4 * p, w), lambda i: (i, 0, 0)),
                  pl.BlockSpec((None, 1, w), lambda i: (i, 0, 0))],
        out_specs=pl.BlockSpec((None, nc, w), lambda i: (i, 0, 0)),
        out_shape=jax.ShapeDtypeStruct((g, nc, w), BF16),
        compiler_params=_cparams(("arbitrary",)),
        name="s5_readout",
    )(ug, m_tab, s_in, c_tab, d_tab)
    z = z.reshape(g, nc, ln, c).transpose(1, 2, 0, 3).reshape(t, g * c)

    tm = _row_tile(t, (1280, 640, 256, 128))
    return pl.pallas_call(
        _s5_glu_kernel,
        grid=(t // tm,),
        in_specs=[pl.BlockSpec((tm, g * c), lambda i: (i, 0)),
                  pl.BlockSpec((g * c, g * c), lambda i: (0, 0))],
        out_specs=pl.BlockSpec((tm, g * c), lambda i: (i, 0)),
        out_shape=jax.ShapeDtypeStruct((t, g * c), BF16),
        compiler_params=_cparams(("arbitrary",)),
        name="s5_glu",
    )(z, w_glu)


def _ret_tables(decay_logit):
    scale = HEAD_W ** -0.5
    lg = jax.nn.log_sigmoid(decay_logit.astype(F32))
    lf, lb = lg[0][:, None, None], lg[1][:, None, None]
    i = jnp.arange(RET_CHUNK, dtype=F32)[None, :, None]
    j = jnp.arange(RET_CHUNK, dtype=F32)[None, None, :]
    diff = i - j
    intra = (jnp.where(diff >= 0, jnp.exp(lf * jnp.maximum(diff, 0.0)), 0.0)
             + jnp.where(diff <= 0, jnp.exp(lb * jnp.maximum(-diff, 0.0)), 0.0)) * scale
    ones = jnp.ones((1, 1, HEAD_W), F32)
    q_f = jnp.exp(lf * (i + 1.0)) * ones
    k_f = jnp.exp(lf * (RET_CHUNK - 1.0 - i)) * scale * ones
    c_f = jnp.exp(lf * RET_CHUNK) * jnp.ones((1, RET_CHUNK, HEAD_W), F32)
    q_b = jnp.exp(lb * (RET_CHUNK - i)) * ones
    k_b = jnp.exp(lb * i) * scale * ones
    c_b = jnp.exp(lb * RET_CHUNK) * jnp.ones((1, RET_CHUNK, HEAD_W), F32)
    return intra, jnp.stack([q_f, k_f, c_f]), jnp.stack([q_b, k_b, c_b])


def _ret_state_step(q, k, v, dec_ref, s_scr, h):
    s = s_scr[h]
    qd = (q.astype(F32) * dec_ref[0, h]).astype(BF16)
    kd = (k.astype(F32) * dec_ref[1, h]).astype(BF16)
    o = jnp.dot(qd, s.astype(BF16), preferred_element_type=F32)
    s_scr[h] = s * dec_ref[2, h] + lax.dot_general(kd, v, (((0,), (0,)), ((), ())),
                                                    preferred_element_type=F32)
    return o


def _ret_fwd_kernel(intra_ref, dec_ref, q_ref, k_ref, v_ref, o_ref, s_scr):
    @pl.when(pl.program_id(0) == 0)
    def _():
        s_scr[...] = jnp.zeros_like(s_scr)

    for c in range(RET_STEP_CHUNKS):
        rows = slice(c * RET_CHUNK, (c + 1) * RET_CHUNK)
        for h in range(RET_HEADS):
            sl = slice(h * HEAD_W, (h + 1) * HEAD_W)
            q, k, v = q_ref[rows, sl], k_ref[rows, sl], v_ref[rows, sl]
            att = lax.dot_general(q, k, (((1,), (1,)), ((), ())), preferred_element_type=F32) * intra_ref[h]
            o = jnp.dot(att.astype(BF16), v, preferred_element_type=F32)
            o_ref[rows, sl] = o + _ret_state_step(q, k, v, dec_ref, s_scr, h)


def _ret_bwd_kernel(dec_ref, q_ref, k_ref, v_ref, g_ref, o1_ref, o_ref, s_scr):
    @pl.when(pl.program_id(0) == 0)
    def _():
        s_scr[...] = jnp.zeros_like(s_scr)

    for c in reversed(range(RET_STEP_CHUNKS)):
        rows = slice(c * RET_CHUNK, (c + 1) * RET_CHUNK)
        for h in range(RET_HEADS):
            sl = slice(h * HEAD_W, (h + 1) * HEAD_W)
            q, k, v = q_ref[rows, sl], k_ref[rows, sl], v_ref[rows, sl]
            o = o1_ref[rows, sl] + _ret_state_step(q, k, v, dec_ref, s_scr, h)
            y = o * lax.rsqrt(jnp.mean(o * o, axis=-1, keepdims=True) + EPS)
            gate = g_ref[rows, sl].astype(F32)
            o_ref[rows, sl] = (y * (gate * jax.nn.sigmoid(gate))).astype(o_ref.dtype)


def _retention(proj, tables, n_lat, n_ctx):
    intra, dec_f, dec_b = tables
    t = proj.shape[0]
    wd = RET_HEADS * HEAD_W
    ck = RET_CHUNK
    bt = RET_STEP_CHUNKS * ck
    assert n_lat % bt == 0 and n_ctx % bt == 0
    n_lat_c, n_ctx_c = n_lat // bt, n_ctx // bt
    steps = n_lat_c + n_ctx_c

    def fwd_blk(s):
        return jnp.where(s < n_ctx_c, n_lat_c + s, s - n_ctx_c)

    def bwd_blk(s):
        return jnp.where(s < n_ctx_c, n_lat_c + n_ctx_c - 1 - s, n_lat_c - 1 - (s - n_ctx_c))

    def col_spec(col, order):
        return pl.BlockSpec((bt, wd), lambda s: (order(s), col))

    tab_spec = pl.BlockSpec((3, RET_HEADS, ck, HEAD_W), lambda s: (0, 0, 0, 0))
    o1 = pl.pallas_call(
        _ret_fwd_kernel,
        grid=(steps,),
        in_specs=[pl.BlockSpec((RET_HEADS, ck, ck), lambda s: (0, 0, 0)), tab_spec,
                  col_spec(0, fwd_blk), col_spec(1, fwd_blk), col_spec(2, fwd_blk)],
        out_specs=col_spec(0, fwd_blk),
        out_shape=jax.ShapeDtypeStruct((t, wd), F32),
        scratch_shapes=[pltpu.VMEM((RET_HEADS, HEAD_W, HEAD_W), F32)],
        compiler_params=_cparams(("arbitrary",)),
        name="retention_fwd",
    )(intra, dec_f, proj, proj, proj)
    return pl.pallas_call(
        _ret_bwd_kernel,
        grid=(steps,),
        in_specs=[tab_spec, col_spec(0, bwd_blk), col_spec(1, bwd_blk), col_spec(2, bwd_blk),
                  col_spec(3, bwd_blk), col_spec(0, bwd_blk)],
        out_specs=col_spec(0, bwd_blk),
        out_shape=jax.ShapeDtypeStruct((t, wd), BF16),
        scratch_shapes=[pltpu.VMEM((RET_HEADS, HEAD_W, HEAD_W), F32)],
        compiler_params=_cparams(("arbitrary",)),
        name="retention_bwd",
    )(dec_b, proj, proj, proj, proj, o1)


def _rope_tables(n_lat, n_ctx):
    n_freq = A_QK_DIM // 4
    freq = ROPE_BASE ** (-jnp.arange(n_freq, dtype=F32) / n_freq)
    rows = n_lat // GRID_W
    ang_row = jnp.arange(rows, dtype=F32)[:, None] * freq
    ang_col = jnp.arange(GRID_W, dtype=F32)[:, None] * freq
    reps = HEAD_W // A_QK_DIM

    def table(fn, ctx_value):
        r = jnp.broadcast_to(jnp.tile(fn(ang_row), (1, 2))[:, None, :], (rows, GRID_W, 2 * n_freq))
        c = jnp.broadcast_to(jnp.tile(fn(ang_col), (1, 2))[None, :, :], (rows, GRID_W, 2 * n_freq))
        lat = jnp.tile(jnp.concatenate([r, c], axis=-1).reshape(n_lat, A_QK_DIM), (1, reps))
        return jnp.concatenate([lat, jnp.full((n_ctx, HEAD_W), ctx_value, F32)], axis=0)

    return table(jnp.cos, 1.0), table(jnp.sin, 0.0)


def kernel(x, c, ctx, c_ctx, ada_w, ada_b, norm1_w, norm2_w, ffn_w13, ffn_w2, e_w_in, e_w_out, diff_lq1, diff_lk1, diff_lq2, diff_lk2, diff_subln_w, na_rpb, o_w_in, o_w_out, s5_lam_re, s5_lam_im, s5_b_re, s5_b_im, s5_c_re, s5_c_im, s5_log_step, s5_d, s5_w_glu, ret_decay_logit, final_norm_w):
    bsz, n_lat, d = x.shape
    n_ctx = ctx.shape[1]
    depth = ada_w.shape[0]
    assert bsz == 1
    xs = jnp.concatenate([x[0], ctx[0]], axis=0)
    mods_all = _mods(c, c_ctx, ada_w, ada_b)
    cos, sin = _rope_tables(n_lat, n_ctx)
    ret_w = RET_HEADS * HEAD_W
    ffn_w13_bf16 = ffn_w13.astype(BF16)
    e_w_in_bf16 = e_w_in.astype(BF16)

    for i in range(depth):
        compute_ctx = i != depth - 1
        mods = mods_all[i]
        j = i // 2
        if i % 2 == 0:
            lambda_init = 0.8 - 0.6 * math.exp(-0.3 * i)
            qkv = _inproj_even(xs, norm1_w[i], mods, n_lat, e_w_in_bf16, j, cos, sin)
            lam_args = (diff_lq1[j], diff_lk1[j], diff_lq2[j], diff_lk2[j], diff_subln_w[j])
            o_a = _diff_attn(qkv, *lam_args, n_lat, n_ctx, lambda_init)
            o_b = _natten(qkv, _na_bias_tables(na_rpb[j], n_lat // GRID_W), n_lat, n_ctx)
            if compute_ctx:
                o_a = _diff_attn(qkv, *lam_args, n_lat, n_ctx, lambda_init, prev=o_a)
                o_b = _ctx_attn(qkv, o_b, n_lat, n_ctx)
            xs = _gated_residual([o_a, o_b], e_w_out, xs, mods, n_lat, 2, "outproj_even", layer=j)
        else:
            w_in = o_w_in[j]
            w_in = jnp.concatenate([w_in[:, S5_CH:], w_in[:, :S5_CH]], axis=1).astype(BF16)
            proj = _inproj_odd(xs, norm1_w[i], mods, n_lat, w_in)
            s5_tabs = _s5_tables(s5_lam_re[j], s5_lam_im[j], s5_b_re[j], s5_b_im[j], s5_c_re[j], s5_c_im[j],
                                 s5_log_step[j], s5_d[j])
            y_c = _s5_mixer(proj[:, 4 * ret_w:], s5_tabs, s5_w_glu[j].astype(BF16), n_lat, n_ctx)
            y_d = _retention(proj, _ret_tables(ret_decay_logit[j]), n_lat, n_ctx)
            xs = _gated_residual([y_c, y_d], o_w_out, xs, mods, n_lat, 2, "outproj_odd", layer=j)
        a = _ffn_up(xs, norm2_w[i], mods, n_lat, ffn_w13_bf16, i)
        xs = _gated_residual([a], ffn_w2[i].astype(BF16), xs, mods, n_lat, 5, "ffn_down")
    return _final_norm(xs, final_norm_w, n_lat)[None]
```

```python
import functools
import math

import jax
import jax.numpy as jnp
import numpy as np
from jax import lax
from jax.experimental import pallas as pl
from jax.experimental.pallas import tpu as pltpu

F32 = jnp.float32
BF16 = jnp.bfloat16

GRID_W = 64
A_HEADS = 8
A_QK_DIM = 64
HEAD_W = 128
B_HEADS = 8
WIN_R = 8
WIN_C = 16
NA_ROWS = 4
NA_HEADS_PER_STEP = 8
S5_CH = 512
S5_GROUP = 16
S5_GROUPS = 32
S5_STATE = 64
S5_CHUNK = 16
S5_BLOCK = 16
RET_HEADS = 12
RET_CHUNK = 128
RET_STEP_CHUNKS = 2
ROPE_BASE = 10000.0
EPS = 1e-6
NEG_INF = -1e30
VMEM_LIMIT = 56 * 1024 * 1024
N_MOD = 8
NORM_ROW_CHUNK = 128


def _cparams(sem):
    return pltpu.CompilerParams(dimension_semantics=sem, vmem_limit_bytes=VMEM_LIMIT)


def _row_tile(t, candidates):
    for c in candidates:
        if t % c == 0:
            return c
    raise ValueError(f"no row tile for {t}")


def _pick_mod(mods_ref, idx, is_ctx):
    return jnp.where(is_ctx, mods_ref[1, idx:idx + 1, :], mods_ref[0, idx:idx + 1, :])


def _is_ctx_rows(tm, n_lat, axis):
    row = pl.program_id(axis) * tm + lax.broadcasted_iota(jnp.int32, (tm, 1), 0)
    return row >= n_lat


def _mods_kernel(s_ref, w_ref, b_ref, o_ref):
    s = s_ref[...]
    s = s * jax.nn.sigmoid(s)
    o_ref[...] = jnp.dot(s, w_ref[...], preferred_element_type=F32,
                         precision=lax.Precision.HIGHEST) + b_ref[...]


def _mods(c, c_ctx, ada_w, ada_b):
    depth, d, w6 = ada_w.shape
    s = jnp.zeros((8, d), F32).at[0].set(c[0]).at[1].set(c_ctx)
    tn = 1024
    out = pl.pallas_call(
        _mods_kernel,
        grid=(depth, w6 // tn),
        in_specs=[pl.BlockSpec((8, d), lambda l, j: (0, 0)),
                  pl.BlockSpec((None, d, tn), lambda l, j: (l, 0, j)),
                  pl.BlockSpec((None, 1, tn), lambda l, j: (l, 0, j))],
        out_specs=pl.BlockSpec((None, 8, tn), lambda l, j: (l, 0, j)),
        out_shape=jax.ShapeDtypeStruct((depth, 8, w6), F32),
        compiler_params=_cparams(("arbitrary", "arbitrary")),
        name="ada_mods",
    )(s, ada_w, ada_b.reshape(depth, 1, w6))
    m = out[:, :2].reshape(depth, 2, 6, d)
    return jnp.pad(m, ((0, 0), (0, 0), (0, N_MOD - 6), (0, 0)))


def _normmod_prologue(x_ref, nw_ref, mods_ref, h_scr, *, n_lat, tm, shift_idx, scale_idx):
    @pl.when(pl.program_id(1) == 0)
    def _():
        rc = NORM_ROW_CHUNK
        assert tm % rc == 0 and n_lat % rc == 0

        def chunk(c, carry):
            r0 = pl.multiple_of(c * rc, rc)
            seg = (pl.program_id(0) * tm + r0 >= n_lat).astype(jnp.int32)
            gain = nw_ref[...] * (1.0 + mods_ref[seg, scale_idx:scale_idx + 1, :])
            shift = mods_ref[seg, shift_idx:shift_idx + 1, :]
            x = x_ref[pl.ds(r0, rc), :]
            r = lax.rsqrt(jnp.mean(x * x, axis=-1, keepdims=True) + EPS)
            h_scr[pl.ds(r0, rc), :] = (x * r * gain + shift).astype(h_scr.dtype)
            return carry

        lax.fori_loop(0, tm // rc, chunk, 0)


def _norm_proj_call(kern, x, norm_w, mods, n_lat, shift_idx, scale_idx, w_args, w_specs, extra_args, extra_specs,
                    n_out, tn, tm, name, **kern_kw):
    t, d = x.shape
    norm = dict(n_lat=n_lat, tm=tm, shift_idx=shift_idx, scale_idx=scale_idx)
    return pl.pallas_call(
        functools.partial(kern, norm=norm, **kern_kw),
        grid=(t // tm, n_out // tn),
        in_specs=[pl.BlockSpec((tm, d), lambda i, j: (i, 0)),
                  pl.BlockSpec((1, d), lambda i, j: (0, 0)),
                  pl.BlockSpec((2, N_MOD, d), lambda i, j: (0, 0, 0))] + w_specs + extra_specs,
        out_specs=pl.BlockSpec((tm, tn), lambda i, j: (i, j)),
        out_shape=jax.ShapeDtypeStruct((t, n_out), BF16),
        scratch_shapes=[pltpu.VMEM((tm, d), BF16)],
        compiler_params=_cparams(("arbitrary", "arbitrary")),
        name=name,
    )(x, norm_w.reshape(1, d), mods, *w_args, *extra_args)


def _final_norm_kernel(x_ref, w_ref, o_ref):
    x = x_ref[...]
    o_ref[...] = x * lax.rsqrt(jnp.mean(x * x, axis=-1, keepdims=True) + EPS) * w_ref[...]


def _final_norm(x, w, n_lat):
    d = x.shape[1]
    tm = _row_tile(n_lat, (512, 256, 128))
    return pl.pallas_call(
        _final_norm_kernel,
        grid=(n_lat // tm,),
        in_specs=[pl.BlockSpec((tm, d), lambda i: (i, 0)),
                  pl.BlockSpec((1, d), lambda i: (0, 0))],
        out_specs=pl.BlockSpec((tm, d), lambda i: (i, 0)),
        out_shape=jax.ShapeDtypeStruct((n_lat, d), F32),
        compiler_params=_cparams(("arbitrary",)),
        name="final_norm",
    )(x, w.reshape(1, d))


def _rope_store(acc, cos, sin, o_ref, scale):
    first_half = (lax.broadcasted_iota(jnp.int32, (1, HEAD_W), 1) % 32) < 16
    for c in range(acc.shape[1] // HEAD_W):
        x = acc[:, c * HEAD_W:(c + 1) * HEAD_W]
        rot = jnp.where(first_half, -pltpu.roll(x, HEAD_W - 16, 1), pltpu.roll(x, 16, 1))
        o_ref[:, c * HEAD_W:(c + 1) * HEAD_W] = ((x * cos + rot * sin) * scale).astype(o_ref.dtype)


def _resident_bf16(w_ref, w_scr):
    @pl.when(pl.program_id(1) == 0)
    def _():
        w_scr[...] = w_ref[...].astype(BF16)

    return w_scr


def _inproj_even_kernel(x_ref, nw_ref, mods_ref, w_ref, cos_ref, sin_ref, o_ref, h_scr, *, norm, a_scale, b_scale):
    _normmod_prologue(x_ref, nw_ref, mods_ref, h_scr, **norm)
    j = pl.program_id(1)
    acc = jnp.dot(h_scr[...], w_ref[...], preferred_element_type=F32)

    @pl.when(j == 0)
    def _():
        _rope_store(acc, cos_ref[...], sin_ref[...], o_ref, a_scale)

    @pl.when(j == 1)
    def _():
        _rope_store(acc, cos_ref[...], sin_ref[...], o_ref, 1.0)

    @pl.when(j == 3)
    def _():
        o_ref[...] = (acc * b_scale).astype(o_ref.dtype)

    @pl.when((j == 2) | (j > 3))
    def _():
        o_ref[...] = acc.astype(o_ref.dtype)


def _inproj_even(x, norm_w, mods, n_lat, w, layer, cos, sin):
    t, d = x.shape
    n = w.shape[2]
    tn = n // 6
    tm = _row_tile(t, (1280, 640, 256, 128))
    rope_spec = pl.BlockSpec((tm, HEAD_W), lambda i, j: (i, 0))
    return _norm_proj_call(
        _inproj_even_kernel, x, norm_w, mods, n_lat, 0, 1,
        [w], [pl.BlockSpec((None, d, tn), lambda i, j: (layer, 0, j))],
        [cos, sin], [rope_spec, rope_spec], n, tn, tm, "inproj_even",
        a_scale=A_QK_DIM ** -0.5 * math.log2(math.e), b_scale=HEAD_W ** -0.5 * math.log2(math.e))


def _inproj_odd_kernel(x_ref, nw_ref, mods_ref, w_ref, o_ref, h_scr, *, norm):
    _normmod_prologue(x_ref, nw_ref, mods_ref, h_scr, **norm)
    o_ref[...] = jnp.dot(h_scr[...], w_ref[...], preferred_element_type=F32).astype(o_ref.dtype)


def _inproj_odd(x, norm_w, mods, n_lat, w):
    t, d = x.shape
    n = w.shape[1]
    tn = 512
    tm = _row_tile(t, (1280, 640, 256, 128))
    return _norm_proj_call(
        _inproj_odd_kernel, x, norm_w, mods, n_lat, 0, 1,
        [w], [pl.BlockSpec((d, tn), lambda i, j: (0, j))], [], [], n, tn, tm, "inproj_odd")


def _ffn_up_kernel(x_ref, nw_ref, mods_ref, w1_ref, w3_ref, o_ref, h_scr, *, norm):
    _normmod_prologue(x_ref, nw_ref, mods_ref, h_scr, **norm)
    h = h_scr[...]
    a = jnp.dot(h, w1_ref[...], preferred_element_type=F32)
    b = jnp.dot(h, w3_ref[...], preferred_element_type=F32)
    o_ref[...] = (a * jax.nn.sigmoid(a) * b).astype(o_ref.dtype)


def _ffn_up(x, norm_w, mods, n_lat, w13, layer):
    t, d = x.shape
    d_ff = w13.shape[2] // 2
    tn = 512
    nj = d_ff // tn
    tm = _row_tile(t, (1280, 640, 256, 128))
    return _norm_proj_call(
        _ffn_up_kernel, x, norm_w, mods, n_lat, 3, 4,
        [w13, w13], [pl.BlockSpec((None, d, tn), lambda i, j: (layer, 0, j)),
                     pl.BlockSpec((None, d, tn), lambda i, j: (layer, 0, j + nj))],
        [], [], d_ff, tn, tm, "ffn_up")


def _gated_residual_kernel(*refs, n_a, n_lat, tm, gate_idx):
    a_refs = refs[:n_a]
    w_ref, x_ref, mods_ref, o_ref = refs[n_a:n_a + 4]
    if w_ref.dtype != BF16:
        w_ref = _resident_bf16(w_ref, refs[n_a + 4])
    k0 = 0
    y = None
    for a_ref in a_refs:
        kk = a_ref.shape[1]
        part = jnp.dot(a_ref[...], w_ref[k0:k0 + kk, :], preferred_element_type=F32)
        y = part if y is None else y + part
        k0 += kk
    gate = _pick_mod(mods_ref, gate_idx, _is_ctx_rows(tm, n_lat, 1))
    o_ref[...] = x_ref[...] + gate * y


def _gated_residual(a_list, w, x, mods, n_lat, gate_idx, name, layer=None):
    t, d = x.shape
    k = w.shape[-2]
    tm = _row_tile(t, (640, 256, 128))
    n_a = len(a_list)
    if layer is None:
        tn = 1024 if d % 1024 == 0 else 512
        w_spec = pl.BlockSpec((k, tn), lambda j, i: (0, j), pipeline_mode=pl.Buffered(1))
    else:
        tn = 512
        w_spec = pl.BlockSpec((None, k, tn), lambda j, i: (layer, 0, j))
    in_specs = [pl.BlockSpec((tm, a.shape[1]), lambda j, i: (i, 0)) for a in a_list]
    in_specs += [w_spec,
                 pl.BlockSpec((tm, tn), lambda j, i: (i, j)),
                 pl.BlockSpec((2, N_MOD, tn), lambda j, i: (0, 0, j))]
    return pl.pallas_call(
        functools.partial(_gated_residual_kernel, n_a=n_a, n_lat=n_lat, tm=tm, gate_idx=gate_idx),
        grid=(d // tn, t // tm),
        in_specs=in_specs,
        out_specs=pl.BlockSpec((tm, tn), lambda j, i: (i, j)),
        out_shape=jax.ShapeDtypeStruct((t, d), F32),
        scratch_shapes=[] if w.dtype == BF16 else [pltpu.VMEM((k, tn), BF16)],
        input_output_aliases={n_a + 1: 0},
        compiler_params=_cparams(("arbitrary", "arbitrary")),
        name=name,
    )(*a_list, w, x, mods)


def _softmax_block(qs, k, v):
    s = lax.dot_general(qs, k, (((1,), (1,)), ((), ())), preferred_element_type=F32)
    m = jnp.max(s, axis=-1, keepdims=True)
    p = jnp.exp2(s - m)
    l = jnp.sum(p, axis=-1, keepdims=True)
    acc = jnp.dot(p.astype(v.dtype), v, preferred_element_type=F32)
    return m, l, acc


DIFF_Q_TILE = 512
DIFF_K_CHUNK = 2048
DIFF_PV_ROWS = 512
DIFF_SOFTMAX_ROWS = 64
DIFF_EXP_DTYPE = BF16


def _diff_attn_kernel(*refs, tq, tk, n_ctx, n_chunks, lambda_init):
    if n_chunks > 1:
        (lq1, lk1, lq2, lk2, subw, q_ref, kc_ref, vc_ref, k_ref, v_ref, o_ref,
         qs_scr, sa_scr, mca_scr, p_scr, m_scr, l_scr, alpha_scr, acc_scr, sb_scr, mcb_scr) = refs
        buf_b = (sb_scr, mcb_scr)
    elif n_chunks == 1:
        (lq1, lk1, lq2, lk2, subw, q_ref, kc_ref, vc_ref, k_ref, v_ref, o_ref,
         qs_scr, sa_scr, mca_scr, p_scr, m_scr, l_scr, alpha_scr, acc_scr) = refs
    else:
        (lq1, lk1, lq2, lk2, subw, q_ref, kc_ref, vc_ref, o_ref,
         qs_scr, sa_scr, mca_scr, p_scr, m_scr, l_scr, alpha_scr, acc_scr) = refs
    buf_a = (sa_scr, mca_scr)
    nt = (((1,), (1,)), ((), ()))
    rows_all = 2 * tq
    pv_rows = min(DIFF_PV_ROWS, rows_all)
    sm_rows = min(DIFF_SOFTMAX_ROWS, pv_rows)
    q = q_ref[...]
    comp1 = lax.broadcasted_iota(jnp.int32, (1, HEAD_W), 1) < A_QK_DIM
    zero = jnp.zeros_like(q)
    qs_scr[0:tq, :] = jnp.where(comp1, q, zero)
    qs_scr[tq:2 * tq, :] = jnp.where(comp1, zero, q)

    def scores(buf, rows, keys):
        s_buf, mc_buf = buf
        s = lax.dot_general(qs_scr[rows, :], keys, nt, preferred_element_type=F32)
        width = s.shape[1]
        s_buf[rows, 0:width] = s
        mp = s[:, 0:HEAD_W]
        for c in range(1, width // HEAD_W):
            mp = jnp.maximum(mp, s[:, c * HEAD_W:(c + 1) * HEAD_W])
        mc_buf[rows, :] = jnp.broadcast_to(jnp.max(mp, axis=-1, keepdims=True), mp.shape)

    def softmax_rows(buf, rows, n_tiles, first):
        s_buf, mc_buf = buf
        m_cur = mc_buf[rows, :]
        if first:
            m_new = m_cur
        else:
            m_prev = m_scr[rows, :]
            m_new = jnp.maximum(m_prev, m_cur)
        lp = None
        for c in range(n_tiles):
            pc = jnp.exp2((s_buf[rows, c * HEAD_W:(c + 1) * HEAD_W] - m_new).astype(DIFF_EXP_DTYPE))
            p_scr[rows, c * HEAD_W:(c + 1) * HEAD_W] = pc.astype(BF16)
            lp = pc if lp is None else lp + pc
        l_cur = jnp.sum(lp.astype(F32), axis=-1, keepdims=True)
        if first:
            l_scr[rows, :] = jnp.broadcast_to(l_cur, m_new.shape)
        else:
            alpha = jnp.exp2(m_prev - m_new)
            alpha_scr[rows, :] = alpha
            l_scr[rows, :] = alpha * l_scr[rows, :] + l_cur
        m_scr[rows, :] = m_new

    def softmax_pv(buf, width, load_v, first, after_group=None):
        for r in range(rows_all // pv_rows):
            for r2 in range(pv_rows // sm_rows):
                r0 = r * pv_rows + r2 * sm_rows
                softmax_rows(buf, slice(r0, r0 + sm_rows), width // HEAD_W, first)
            rows = slice(r * pv_rows, (r + 1) * pv_rows)
            pv = jnp.dot(p_scr[rows, 0:width], load_v(), preferred_element_type=F32)
            acc_scr[rows, :] = pv if first else alpha_scr[rows, :] * acc_scr[rows, :] + pv
            if after_group is not None:
                after_group(rows)

    def all_groups(fn):
        for r in range(rows_all // pv_rows):
            fn(slice(r * pv_rows, (r + 1) * pv_rows))

    def qk_rows(j, buf, rows):
        off = pl.multiple_of(j * tk, tk)
        scores(buf, rows, k_ref[pl.ds(off, tk), :])

    all_groups(lambda rows: scores(buf_a, rows, kc_ref[...]))
    softmax_pv(buf_a, n_ctx, lambda: vc_ref[...], True,
               (lambda rows: qk_rows(0, buf_a, rows)) if n_chunks else None)

    def soft_pv(j, buf, j_next=None, buf_next=None):
        off = pl.multiple_of(j * tk, tk)
        after = None if j_next is None else (lambda rows: qk_rows(j_next, buf_next, rows))
        softmax_pv(buf, tk, lambda: v_ref[pl.ds(off, tk), :], False, after)

    if n_chunks == 1:
        soft_pv(0, buf_a)
    elif n_chunks > 1:

        def pair(j2, carry):
            soft_pv(2 * j2, buf_a, 2 * j2 + 1, buf_b)
            soft_pv(2 * j2 + 1, buf_b, 2 * j2 + 2, buf_a)
            return carry

        lax.fori_loop(0, n_chunks // 2 - 1, pair, 0)
        soft_pv(n_chunks - 2, buf_a, n_chunks - 1, buf_b)
        soft_pv(n_chunks - 1, buf_b)

    lam = (jnp.exp(jnp.sum(lq1[...] * lk1[...], axis=-1, keepdims=True))
           - jnp.exp(jnp.sum(lq2[...] * lk2[...], axis=-1, keepdims=True)) + lambda_init)
    o1 = acc_scr[0:tq, :] / l_scr[0:tq, :]
    o2 = acc_scr[tq:2 * tq, :] / l_scr[tq:2 * tq, :]
    o = o1 - lam * o2
    y = o * lax.rsqrt(jnp.mean(o * o, axis=-1, keepdims=True) + EPS) * subw[...]
    o_ref[...] = (y * (1.0 - lambda_init)).astype(o_ref.dtype)


def _diff_attn(qkv, lq1, lk1, lq2, lk2, subw, n_lat, n_ctx, lambda_init, prev=None):
    t = qkv.shape[0]
    use_lat = prev is None
    ctx_blk = n_lat // n_ctx
    small = [lq1.reshape(1, -1), lk1.reshape(1, -1), lq2.reshape(1, -1), lk2.reshape(1, -1), subw.reshape(1, -1)]
    small_specs = [pl.BlockSpec(a.shape, lambda h, qi: (0, 0)) for a in small]
    kc_spec = pl.BlockSpec((n_ctx, HEAD_W), lambda h, qi: (ctx_blk, A_HEADS + h))
    vc_spec = pl.BlockSpec((n_ctx, HEAD_W), lambda h, qi: (ctx_blk, 2 * A_HEADS + h))
    if use_lat:
        tq = _row_tile(n_lat, tuple(c for c in (512, 256, 128) if c <= DIFF_Q_TILE))
        tk = _row_tile(n_lat, tuple(c for c in (2048, 1024, 512, 256, 128) if c <= DIFF_K_CHUNK))
        n_chunks = n_lat // tk
        assert n_chunks == 1 or n_chunks % 2 == 0
        grid = (A_HEADS, n_lat // tq)
        in_specs = small_specs + [
            pl.BlockSpec((tq, HEAD_W), lambda h, qi: (qi, h)), kc_spec, vc_spec,
            pl.BlockSpec((n_lat, HEAD_W), lambda h, qi: (0, A_HEADS + h), pipeline_mode=pl.Buffered(1)),
            pl.BlockSpec((n_lat, HEAD_W), lambda h, qi: (0, 2 * A_HEADS + h), pipeline_mode=pl.Buffered(1))]
        args = small + [qkv, qkv, qkv, qkv, qkv]
        out_spec = pl.BlockSpec((tq, HEAD_W), lambda h, qi: (qi, h))
        aliases = {}
    else:
        tq, tk, n_chunks = n_ctx, 0, 0
        grid = (A_HEADS, 1)
        in_specs = small_specs + [
            pl.BlockSpec((tq, HEAD_W), lambda h, qi: (ctx_blk, h)), kc_spec, vc_spec,
            pl.BlockSpec(memory_space=pl.ANY)]
        args = small + [qkv, qkv, qkv, prev]
        out_spec = pl.BlockSpec((tq, HEAD_W), lambda h, qi: (ctx_blk, h))
        aliases = {len(args) - 1: 0}

    sw = max(tk, n_ctx)
    kern = functools.partial(_diff_attn_kernel, tq=tq, tk=tk, n_ctx=n_ctx, n_chunks=n_chunks,
                             lambda_init=lambda_init)
    if not use_lat:
        inner = kern

        def kern(*refs):
            n_in = len(args)
            inner(*refs[:n_in - 1], *refs[n_in:])

    return pl.pallas_call(
        kern,
        grid=grid,
        in_specs=in_specs,
        out_specs=out_spec,
        out_shape=jax.ShapeDtypeStruct((t, A_HEADS * HEAD_W), BF16),
        scratch_shapes=[pltpu.VMEM((2 * tq, HEAD_W), BF16),
                        pltpu.VMEM((2 * tq, sw), F32),
                        pltpu.VMEM((2 * tq, HEAD_W), F32),
                        pltpu.VMEM((2 * tq, sw), BF16),
                        pltpu.VMEM((2 * tq, HEAD_W), F32),
                        pltpu.VMEM((2 * tq, HEAD_W), F32),
                        pltpu.VMEM((2 * tq, HEAD_W), F32),
                        pltpu.VMEM((2 * tq, HEAD_W), F32)]
        + ([pltpu.VMEM((2 * tq, sw), F32),
            pltpu.VMEM((2 * tq, HEAD_W), F32)] if n_chunks > 1 else []),
        input_output_aliases=aliases,
        compiler_params=_cparams(("arbitrary", "arbitrary")),
        name="diff_attn" if use_lat else "diff_attn_ctx",
    )(*args)


def _na_bias_tables(rpb, rows):
    nblk = rows // NA_ROWS
    wr = min(WIN_R, rows)
    nh = rpb.shape[0]
    rq = jnp.arange(NA_ROWS)[:, None, None]
    slot = jnp.arange(3)[None, :, None]
    rk = jnp.arange(NA_ROWS)[None, None, :]
    row_sel, row_ok = [], []
    for b, dup in ((0, 0), (1, -1), (nblk - 1, 2)):
        r = NA_ROWS * b + rq
        r0 = jnp.clip(r - wr // 2, 0, rows - wr)
        rkey = NA_ROWS * (b - 1 + slot) + rk
        ok = (rkey >= r0) & (rkey < r0 + wr) & (slot != dup)
        drow = jnp.broadcast_to(rkey - r + (WIN_R - 1), ok.shape)
        row_sel.append((drow[..., None] == jnp.arange(2 * WIN_R - 1)) & ok[..., None])
        row_ok.append(ok)
    row_sel = jnp.stack(row_sel).astype(F32)
    row_ok = jnp.stack(row_ok)
    jq = jnp.arange(GRID_W)[:, None]
    jk = jnp.arange(GRID_W)[None, :]
    c0 = jnp.clip(jq - WIN_C // 2, 0, GRID_W - WIN_C)
    col_ok = (jk >= c0) & (jk < c0 + WIN_C)
    dcol = jnp.clip(jk - jq + (WIN_C - 1), 0, 2 * WIN_C - 2)
    col_sel = (dcol[..., None] == jnp.arange(2 * WIN_C - 1)).astype(F32)
    bias = jnp.einsum('vasbr,hrc,qkc->hvaqsbk', row_sel, rpb.astype(F32), col_sel,
                      precision=lax.Precision.HIGHEST)
    valid = row_ok[None, :, :, None, :, :, None] & col_ok[None, None, None, :, None, None, :]
    tab = jnp.where(valid, bias * math.log2(math.e), NEG_INF)
    return tab.reshape(nh, 3, NA_ROWS * GRID_W, 3 * NA_ROWS * GRID_W)


def _natten_kernel(tab_ref, q_ref, k0_ref, k1_ref, k2_ref, v0_ref, v1_ref, v2_ref, kc_ref, vc_ref, o_ref):
    nt = (((1,), (1,)), ((), ()))
    blk = q_ref.shape[0]
    for hh in range(NA_HEADS_PER_STEP):
        hs = slice(hh * HEAD_W, (hh + 1) * HEAD_W)
        q = q_ref[:, hs]
        s_loc = [lax.dot_general(q, k_ref[:, hs], nt, preferred_element_type=F32)
                 + tab_ref[hh, :, i * blk:(i + 1) * blk] for i, k_ref in enumerate((k0_ref, k1_ref, k2_ref))]
        s_ctx = lax.dot_general(q, kc_ref[:, hs], nt, preferred_element_type=F32)
        m = jnp.max(s_ctx, axis=-1, keepdims=True)
        for s in s_loc:
            m = jnp.maximum(m, jnp.max(s, axis=-1, keepdims=True))
        l = None
        acc = None
        for s, v_ref in zip([s_ctx] + s_loc, (vc_ref, v0_ref, v1_ref, v2_ref)):
            p = jnp.exp2((s - m).astype(BF16))
            pt = p[:, 0:HEAD_W]
            for c in range(1, p.shape[1] // HEAD_W):
                pt = pt + p[:, c * HEAD_W:(c + 1) * HEAD_W]
            lp = jnp.sum(pt.astype(F32), axis=-1, keepdims=True)
            pv = jnp.dot(p, v_ref[:, hs], preferred_element_type=F32)
            l = lp if l is None else l + lp
            acc = pv if acc is None else acc + pv
        o_ref[:, hs] = (acc / l).astype(o_ref.dtype)


def _natten(qkv, tabs, n_lat, n_ctx):
    t = qkv.shape[0]
    blk = NA_ROWS * GRID_W
    nblk = n_lat // blk
    assert nblk >= 3 and blk == n_ctx
    nh = NA_HEADS_PER_STEP
    hw = nh * HEAD_W
    qo, ko, vo = (3 * A_HEADS) // nh, (3 * A_HEADS + B_HEADS) // nh, (3 * A_HEADS + 2 * B_HEADS) // nh
    ctx_blk = n_lat // n_ctx

    def kv_spec(off, shift):
        return pl.BlockSpec((blk, hw), lambda h, b: (jnp.clip(b + shift, 0, nblk - 1), off + h))

    return pl.pallas_call(
        _natten_kernel,
        grid=(B_HEADS // nh, nblk),
        in_specs=[pl.BlockSpec((nh, None, blk, 3 * blk),
                               lambda h, b: (h, jnp.where(b == 0, 0, jnp.where(b == nblk - 1, 2, 1)), 0, 0)),
                  pl.BlockSpec((blk, hw), lambda h, b: (b, qo + h)),
                  kv_spec(ko, -1), kv_spec(ko, 0), kv_spec(ko, 1),
                  kv_spec(vo, -1), kv_spec(vo, 0), kv_spec(vo, 1),
                  pl.BlockSpec((n_ctx, hw), lambda h, b: (ctx_blk, ko + h)),
                  pl.BlockSpec((n_ctx, hw), lambda h, b: (ctx_blk, vo + h))],
        out_specs=pl.BlockSpec((blk, hw), lambda h, b: (b, h)),
        out_shape=jax.ShapeDtypeStruct((t, B_HEADS * HEAD_W), BF16),
        compiler_params=_cparams(("arbitrary", "arbitrary")),
        name="natten",
    )(tabs, qkv, qkv, qkv, qkv, qkv, qkv, qkv, qkv, qkv)


def _ctx_attn_kernel(q_ref, k_ref, v_ref, prev_ref, o_ref):
    del prev_ref
    _, l, acc = _softmax_block(q_ref[...], k_ref[...], v_ref[...])
    o_ref[...] = (acc / l).astype(o_ref.dtype)


def _ctx_attn(qkv, prev, n_lat, n_ctx):
    qo, ko, vo = 3 * A_HEADS, 3 * A_HEADS + B_HEADS, 3 * A_HEADS + 2 * B_HEADS
    ctx_blk = n_lat // n_ctx
    return pl.pallas_call(
        _ctx_attn_kernel,
        grid=(B_HEADS,),
        in_specs=[pl.BlockSpec((n_ctx, HEAD_W), lambda h: (ctx_blk, qo + h)),
                  pl.BlockSpec((n_ctx, HEAD_W), lambda h: (ctx_blk, ko + h)),
                  pl.BlockSpec((n_ctx, HEAD_W), lambda h: (ctx_blk, vo + h)),
                  pl.BlockSpec(memory_space=pl.ANY)],
        out_specs=pl.BlockSpec((n_ctx, HEAD_W), lambda h: (ctx_blk, h)),
        out_shape=jax.ShapeDtypeStruct(prev.shape, prev.dtype),
        input_output_aliases={3: 0},
        compiler_params=_cparams(("arbitrary",)),
        name="ctx_attn",
    )(qkv, qkv, qkv, prev)


def _s5_tables(lam_re, lam_im, b_re, b_im, c_re, c_im, log_step, d_skip):
    hp = lax.Precision.HIGHEST
    ln = S5_CHUNK
    g, p, c = S5_GROUPS, S5_STATE, S5_GROUP
    lr, li = lam_re.astype(F32), lam_im.astype(F32)
    dt = jnp.exp(log_step.astype(F32))[:, :, None]
    mag = jnp.exp(lr * dt)
    ar, ai = mag * jnp.cos(li * dt), mag * jnp.sin(li * dt)
    den = lr * lr + li * li
    nr, ni = ar - 1.0, ai
    fr = (nr * lr + ni * li) / den
    fi = (ni * lr - nr * li) / den
    br_, bi_ = b_re.astype(F32), b_im.astype(F32)
    bbr = fr[..., None] * br_ - fi[..., None] * bi_
    bbi = fr[..., None] * bi_ + fi[..., None] * br_
    lag = jnp.arange(ln + 1, dtype=F32)[:, None, None, None]
    magl = jnp.exp(lr * dt * lag)
    pr, pi_ = magl * jnp.cos(li * dt * lag), magl * jnp.sin(li * dt * lag)
    wr = pr[..., None] * bbr - pi_[..., None] * bbi
    wi = pr[..., None] * bbi + pi_[..., None] * bbr
    cr, ci = c_re.astype(F32), c_im.astype(F32)
    kern = (jnp.einsum('dgcp,ldgpe->ldgce', cr, wr[:ln], precision=hp)
            - jnp.einsum('dgcp,ldgpe->ldgce', ci, wi[:ln], precision=hp))
    lag = np.arange(ln)[:, None, None]
    s_idx = np.arange(ln)[None, :, None]
    t_idx = np.arange(ln)[None, None, :]
    sel_f = jnp.asarray(t_idx - s_idx == lag, F32)
    sel_b = jnp.asarray(s_idx - t_idx == lag, F32)
    kfb = (jnp.einsum('lst,lgce->stgce', sel_f, kern[:, 0], precision=hp)
           + jnp.einsum('lst,lgce->stgce', sel_b, kern[:, 1], precision=hp))
    m_tab = kfb.transpose(2, 0, 4, 1, 3).reshape(g, ln * c, ln * c)
    wf_r, wf_i = wr[:ln, 0][::-1], wi[:ln, 0][::-1]
    wb_r, wb_i = wr[:ln, 1], wi[:ln, 1]

    def inj(w):
        return w.transpose(1, 0, 3, 2).reshape(g, ln * c, p)

    b_tab = jnp.concatenate([inj(wf_r), inj(wf_i), inj(wb_r), inj(wb_i)], axis=-1)
    pf_r, pf_i = pr[1:ln + 1, 0], pi_[1:ln + 1, 0]
    pb_r, pb_i = pr[1:ln + 1, 1][::-1], pi_[1:ln + 1, 1][::-1]

    def rd(pw_r, pw_i, cre, cim):
        car = cre[None] * pw_r[:, :, None, :] - cim[None] * pw_i[:, :, None, :]
        cai = cre[None] * pw_i[:, :, None, :] + cim[None] * pw_r[:, :, None, :]
        to_rows = lambda a: a.transpose(1, 3, 0, 2).reshape(g, p, ln * c)
        return to_rows(car), to_rows(-cai)

    c_tab = jnp.concatenate(rd(pf_r, pf_i, cr[0], ci[0]) + rd(pb_r, pb_i, cr[1], ci[1]), axis=1)
    a_chunk = jnp.stack([jnp.stack([pr[ln, 0], pi_[ln, 0]]), jnp.stack([pr[ln, 1], pi_[ln, 1]])])
    d_tab = jnp.tile(d_skip.astype(F32).reshape(g, 1, c), (1, ln, 1)).reshape(g, 1, ln * c)
    return m_tab.astype(BF16), b_tab.astype(BF16), c_tab.astype(BF16), a_chunk, d_tab


def _s5_inject_kernel(u_ref, b_ref, o_ref):
    o_ref[...] = jnp.dot(u_ref[...], b_ref[...], preferred_element_type=F32)


def _s5_scan_kernel(a_ref, e_ref, o_ref, s_scr, *, nb):
    d = pl.program_id(0)

    @pl.when(pl.program_id(1) == 0)
    def _():
        s_scr[...] = jnp.zeros_like(s_scr)

    ar, ai = a_ref[0], a_ref[1]

    def body(i, carry):
        sr, si = carry
        c = jnp.where(d == 0, i, nb - 1 - i)
        o_ref[0, c] = sr
        o_ref[1, c] = si
        return ar * sr - ai * si + e_ref[0, c], ar * si + ai * sr + e_ref[1, c]

    sr, si = lax.fori_loop(0, nb, body, (s_scr[0], s_scr[1]))
    s_scr[0] = sr
    s_scr[1] = si


def _gelu_tanh(x):
    return 0.5 * x * (1.0 + jnp.tanh(math.sqrt(2.0 / math.pi) * (x + 0.044715 * (x * x * x))))


def _s5_readout_kernel(u_ref, m_ref, s_ref, c_ref, d_ref, o_ref):
    u = u_ref[...]
    y = jnp.dot(u, m_ref[...], preferred_element_type=F32) + u.astype(F32) * d_ref[...]
    for k in range(4):
        y = y + jnp.dot(s_ref[k].astype(BF16), c_ref[k * S5_STATE:(k + 1) * S5_STATE, :],
                        preferred_element_type=F32)
    o_ref[...] = _gelu_tanh(y).astype(o_ref.dtype)


def _s5_glu_kernel(z_ref, w_ref, o_ref):
    z = z_ref[...]
    gate = jax.nn.sigmoid(jnp.dot(z.astype(BF16), w_ref[...], preferred_element_type=F32))
    o_ref[...] = (z.astype(F32) * gate).astype(o_ref.dtype)


def _s5_mixer(u, tables, w_glu, n_lat, n_ctx):
    m_tab, b_tab, c_tab, a_chunk, d_tab = tables
    t = u.shape[0]
    g, c, ln, p = S5_GROUPS, S5_GROUP, S5_CHUNK, S5_STATE
    nc = t // ln
    w = ln * c
    ug = u.reshape(nc, ln, g, c).transpose(2, 0, 1, 3).reshape(g, nc, w)

    e = pl.pallas_call(
        _s5_inject_kernel,
        grid=(g,),
        in_specs=[pl.BlockSpec((None, nc, w), lambda i: (i, 0, 0)),
                  pl.BlockSpec((None, w, 4 * p), lambda i: (i, 0, 0))],
        out_specs=pl.BlockSpec((None, nc, 4 * p), lambda i: (i, 0, 0)),
        out_shape=jax.ShapeDtypeStruct((g, nc, 4 * p), F32),
        compiler_params=_cparams(("arbitrary",)),
        name="s5_inject",
    )(ug, b_tab)
    e = e.reshape(g, nc, 2, 2, p).transpose(2, 3, 1, 0, 4)

    nb = S5_BLOCK
    assert (n_lat // ln) % nb == 0 and n_ctx // ln == nb
    lat_blocks = n_lat // ln // nb

    def blk(d, s):
        return jnp.where(s == 0, lat_blocks, jnp.where(d == 0, s - 1, lat_blocks - s))

    s_in = pl.pallas_call(
        functools.partial(_s5_scan_kernel, nb=nb),
        grid=(2, lat_blocks + 1),
        in_specs=[pl.BlockSpec((None, 2, g, p), lambda d, s: (d, 0, 0, 0)),
                  pl.BlockSpec((None, 2, nb, g, p), lambda d, s: (d, 0, blk(d, s), 0, 0))],
        out_specs=pl.BlockSpec((None, 2, nb, g, p), lambda d, s: (d, 0, blk(d, s), 0, 0)),
        out_shape=jax.ShapeDtypeStruct((2, 2, nc, g, p), F32),
        scratch_shapes=[pltpu.VMEM((2, g, p), F32)],
        compiler_params=_cparams(("arbitrary", "arbitrary")),
        name="s5_scan",
    )(a_chunk, e)
    s_in = s_in.transpose(3, 0, 1, 2, 4).reshape(g, 4, nc, p)

    z = pl.pallas_call(
        _s5_readout_kernel,
        grid=(g,),
        in_specs=[pl.BlockSpec((None, nc, w), lambda i: (i, 0, 0)),
                  pl.BlockSpec((None, w, w), lambda i: (i, 0, 0)),
                  pl.BlockSpec((None, 4, nc, p), lambda i: (i, 0, 0, 0)),
                  pl.BlockSpec((None, 4 * p, w), lambda i: (i, 0, 0)),
                  pl.BlockSpec((None, 1, w), lambda i: (i, 0, 0))],
        out_specs=pl.BlockSpec((None, nc, w), lambda i: (i, 0, 0)),
        out_shape=jax.ShapeDtypeStruct((g, nc, w), BF16),
        compiler_params=_cparams(("arbitrary",)),
        name="s5_readout",
    )(ug, m_tab, s_in, c_tab, d_tab)
    z = z.reshape(g, nc, ln, c).transpose(1, 2, 0, 3).reshape(t, g * c)

    tm = _row_tile(t, (1280, 640, 256, 128))
    return pl.pallas_call(
        _s5_glu_kernel,
        grid=(t // tm,),
        in_specs=[pl.BlockSpec((tm, g * c), lambda i: (i, 0)),
                  pl.BlockSpec((g * c, g * c), lambda i: (0, 0))],
        out_specs=pl.BlockSpec((tm, g * c), lambda i: (i, 0)),
        out_shape=jax.ShapeDtypeStruct((t, g * c), BF16),
        compiler_params=_cparams(("arbitrary",)),
        name="s5_glu",
    )(z, w_glu)


def _ret_tables(decay_logit):
    scale = HEAD_W ** -0.5
    lg = jax.nn.log_sigmoid(decay_logit.astype(F32))
    lf, lb = lg[0][:, None, None], lg[1][:, None, None]
    i = jnp.arange(RET_CHUNK, dtype=F32)[None, :, None]
    j = jnp.arange(RET_CHUNK, dtype=F32)[None, None, :]
    diff = i - j
    intra = (jnp.where(diff >= 0, jnp.exp(lf * jnp.maximum(diff, 0.0)), 0.0)
             + jnp.where(diff <= 0, jnp.exp(lb * jnp.maximum(-diff, 0.0)), 0.0)) * scale
    ones = jnp.ones((1, 1, HEAD_W), F32)
    q_f = jnp.exp(lf * (i + 1.0)) * ones
    k_f = jnp.exp(lf * (RET_CHUNK - 1.0 - i)) * scale * ones
    c_f = jnp.exp(lf * RET_CHUNK) * jnp.ones((1, RET_CHUNK, HEAD_W), F32)
    q_b = jnp.exp(lb * (RET_CHUNK - i)) * ones
    k_b = jnp.exp(lb * i) * scale * ones
    c_b = jnp.exp(lb * RET_CHUNK) * jnp.ones((1, RET_CHUNK, HEAD_W), F32)
    return intra, jnp.stack([q_f, k_f, c_f]), jnp.stack([q_b, k_b, c_b])


def _ret_state_step(q, k, v, dec_ref, s_scr, h):
    s = s_scr[h]
    qd = (q.astype(F32) * dec_ref[0, h]).astype(BF16)
    kd = (k.astype(F32) * dec_ref[1, h]).astype(BF16)
    o = jnp.dot(qd, s.astype(BF16), preferred_element_type=F32)
    s_scr[h] = s * dec_ref[2, h] + lax.dot_general(kd, v, (((0,), (0,)), ((), ())),
                                                    preferred_element_type=F32)
    return o


def _ret_fwd_kernel(intra_ref, dec_ref, q_ref, k_ref, v_ref, o_ref, s_scr):
    @pl.when(pl.program_id(0) == 0)
    def _():
        s_scr[...] = jnp.zeros_like(s_scr)

    for c in range(RET_STEP_CHUNKS):
        rows = slice(c * RET_CHUNK, (c + 1) * RET_CHUNK)
        for h in range(RET_HEADS):
            sl = slice(h * HEAD_W, (h + 1) * HEAD_W)
            q, k, v = q_ref[rows, sl], k_ref[rows, sl], v_ref[rows, sl]
            att = lax.dot_general(q, k, (((1,), (1,)), ((), ())), preferred_element_type=F32) * intra_ref[h]
            o = jnp.dot(att.astype(BF16), v, preferred_element_type=F32)
            o_ref[rows, sl] = o + _ret_state_step(q, k, v, dec_ref, s_scr, h)


def _ret_bwd_kernel(dec_ref, q_ref, k_ref, v_ref, g_ref, o1_ref, o_ref, s_scr):
    @pl.when(pl.program_id(0) == 0)
    def _():
        s_scr[...] = jnp.zeros_like(s_scr)

    for c in reversed(range(RET_STEP_CHUNKS)):
        rows = slice(c * RET_CHUNK, (c + 1) * RET_CHUNK)
        for h in range(RET_HEADS):
            sl = slice(h * HEAD_W, (h + 1) * HEAD_W)
            q, k, v = q_ref[rows, sl], k_ref[rows, sl], v_ref[rows, sl]
            o = o1_ref[rows, sl] + _ret_state_step(q, k, v, dec_ref, s_scr, h)
            y = o * lax.rsqrt(jnp.mean(o * o, axis=-1, keepdims=True) + EPS)
            gate = g_ref[rows, sl].astype(F32)
            o_ref[rows, sl] = (y * (gate * jax.nn.sigmoid(gate))).astype(o_ref.dtype)


def _retention(proj, tables, n_lat, n_ctx):
    intra, dec_f, dec_b = tables
    t = proj.shape[0]
    wd = RET_HEADS * HEAD_W
    ck = RET_CHUNK
    bt = RET_STEP_CHUNKS * ck
    assert n_lat % bt == 0 and n_ctx % bt == 0
    n_lat_c, n_ctx_c = n_lat // bt, n_ctx // bt
    steps = n_lat_c + n_ctx_c

    def fwd_blk(s):
        return jnp.where(s < n_ctx_c, n_lat_c + s, s - n_ctx_c)

    def bwd_blk(s):
        return jnp.where(s < n_ctx_c, n_lat_c + n_ctx_c - 1 - s, n_lat_c - 1 - (s - n_ctx_c))

    def col_spec(col, order):
        return pl.BlockSpec((bt, wd), lambda s: (order(s), col))

    tab_spec = pl.BlockSpec((3, RET_HEADS, ck, HEAD_W), lambda s: (0, 0, 0, 0))
    o1 = pl.pallas_call(
        _ret_fwd_kernel,
        grid=(steps,),
        in_specs=[pl.BlockSpec((RET_HEADS, ck, ck), lambda s: (0, 0, 0)), tab_spec,
                  col_spec(0, fwd_blk), col_spec(1, fwd_blk), col_spec(2, fwd_blk)],
        out_specs=col_spec(0, fwd_blk),
        out_shape=jax.ShapeDtypeStruct((t, wd), F32),
        scratch_shapes=[pltpu.VMEM((RET_HEADS, HEAD_W, HEAD_W), F32)],
        compiler_params=_cparams(("arbitrary",)),
        name="retention_fwd",
    )(intra, dec_f, proj, proj, proj)
    return pl.pallas_call(
        _ret_bwd_kernel,
        grid=(steps,),
        in_specs=[tab_spec, col_spec(0, bwd_blk), col_spec(1, bwd_blk), col_spec(2, bwd_blk),
                  col_spec(3, bwd_blk), col_spec(0, bwd_blk)],
        out_specs=col_spec(0, bwd_blk),
        out_shape=jax.ShapeDtypeStruct((t, wd), BF16),
        scratch_shapes=[pltpu.VMEM((RET_HEADS, HEAD_W, HEAD_W), F32)],
        compiler_params=_cparams(("arbitrary",)),
        name="retention_bwd",
    )(dec_b, proj, proj, proj, proj, o1)


def _rope_tables(n_lat, n_ctx):
    n_freq = A_QK_DIM // 4
    freq = ROPE_BASE ** (-jnp.arange(n_freq, dtype=F32) / n_freq)
    rows = n_lat // GRID_W
    ang_row = jnp.arange(rows, dtype=F32)[:, None] * freq
    ang_col = jnp.arange(GRID_W, dtype=F32)[:, None] * freq
    reps = HEAD_W // A_QK_DIM

    def table(fn, ctx_value):
        r = jnp.broadcast_to(jnp.tile(fn(ang_row), (1, 2))[:, None, :], (rows, GRID_W, 2 * n_freq))
        c = jnp.broadcast_to(jnp.tile(fn(ang_col), (1, 2))[None, :, :], (rows, GRID_W, 2 * n_freq))
        lat = jnp.tile(jnp.concatenate([r, c], axis=-1).reshape(n_lat, A_QK_DIM), (1, reps))
        return jnp.concatenate([lat, jnp.full((n_ctx, HEAD_W), ctx_value, F32)], axis=0)

    return table(jnp.cos, 1.0), table(jnp.sin, 0.0)


def kernel(x, c, ctx, c_ctx, ada_w, ada_b, norm1_w, norm2_w, ffn_w13, ffn_w2, e_w_in, e_w_out, diff_lq1, diff_lk1, diff_lq2, diff_lk2, diff_subln_w, na_rpb, o_w_in, o_w_out, s5_lam_re, s5_lam_im, s5_b_re, s5_b_im, s5_c_re, s5_c_im, s5_log_step, s5_d, s5_w_glu, ret_decay_logit, final_norm_w):
    bsz, n_lat, d = x.shape
    n_ctx = ctx.shape[1]
    depth = ada_w.shape[0]
    assert bsz == 1
    xs = jnp.concatenate([x[0], ctx[0]], axis=0)
    mods_all = _mods(c, c_ctx, ada_w, ada_b)
    cos, sin = _rope_tables(n_lat, n_ctx)
    ret_w = RET_HEADS * HEAD_W
    ffn_w13_bf16 = ffn_w13.astype(BF16)
    e_w_in_bf16 = e_w_in.astype(BF16)

    for i in range(depth):
        compute_ctx = i != depth - 1
        mods = mods_all[i]
        j = i // 2
        if i % 2 == 0:
            lambda_init = 0.8 - 0.6 * math.exp(-0.3 * i)
            qkv = _inproj_even(xs, norm1_w[i], mods, n_lat, e_w_in_bf16, j, cos, sin)
            lam_args = (diff_lq1[j], diff_lk1[j], diff_lq2[j], diff_lk2[j], diff_subln_w[j])
            o_a = _diff_attn(qkv, *lam_args, n_lat, n_ctx, lambda_init)
            o_b = _natten(qkv, _na_bias_tables(na_rpb[j], n_lat // GRID_W), n_lat, n_ctx)
            if compute_ctx:
                o_a = _diff_attn(qkv, *lam_args, n_lat, n_ctx, lambda_init, prev=o_a)
                o_b = _ctx_attn(qkv, o_b, n_lat, n_ctx)
            xs = _gated_residual([o_a, o_b], e_w_out, xs, mods, n_lat, 2, "outproj_even", layer=j)
        else:
            w_in = o_w_in[j]
            w_in = jnp.concatenate([w_in[:, S5_CH:], w_in[:, :S5_CH]], axis=1).astype(BF16)
            proj = _inproj_odd(xs, norm1_w[i], mods, n_lat, w_in)
            s5_tabs = _s5_tables(s5_lam_re[j], s5_lam_im[j], s5_b_re[j], s5_b_im[j], s5_c_re[j], s5_c_im[j],
                                 s5_log_step[j], s5_d[j])
            y_c = _s5_mixer(proj[:, 4 * ret_w:], s5_tabs, s5_w_glu[j].astype(BF16), n_lat, n_ctx)
            y_d = _retention(proj, _ret_tables(ret_decay_logit[j]), n_lat, n_ctx)
            xs = _gated_residual([y_c, y_d], o_w_out, xs, mods, n_lat, 2, "outproj_odd", layer=j)
        a = _ffn_up(xs, norm2_w[i], mods, n_lat, ffn_w13_bf16, i)
        xs = _gated_residual([a], ffn_w2[i].astype(BF16), xs, mods, n_lat, 5, "ffn_down")
    return _final_norm(xs, final_norm_w, n_lat)[None]
```

```python
import functools
import math

import jax
import jax.numpy as jnp
import numpy as np
from jax import lax
from jax.experimental import pallas as pl
from jax.experimental.pallas import tpu as pltpu

F32 = jnp.float32
BF16 = jnp.bfloat16

GRID_W = 64
A_HEADS = 8
A_QK_DIM = 64
HEAD_W = 128
B_HEADS = 8
WIN_R = 8
WIN_C = 16
NA_ROWS = 4
NA_HEADS_PER_STEP = 8
S5_CH = 512
S5_GROUP = 16
S5_GROUPS = 32
S5_STATE = 64
S5_CHUNK = 16
S5_BLOCK = 16
RET_HEADS = 12
RET_CHUNK = 128
RET_STEP_CHUNKS = 2
ROPE_BASE = 10000.0
EPS = 1e-6
NEG_INF = -1e30
VMEM_LIMIT = 56 * 1024 * 1024
N_MOD = 8
NORM_ROW_CHUNK = 128


def _cparams(sem):
    return pltpu.CompilerParams(dimension_semantics=sem, vmem_limit_bytes=VMEM_LIMIT)


def _row_tile(t, candidates):
    for c in candidates:
        if t % c == 0:
            return c
    raise ValueError(f"no row tile for {t}")


def _pick_mod(mods_ref, idx, is_ctx):
    return jnp.where(is_ctx, mods_ref[1, idx:idx + 1, :], mods_ref[0, idx:idx + 1, :])


def _is_ctx_rows(tm, n_lat, axis):
    row = pl.program_id(axis) * tm + lax.broadcasted_iota(jnp.int32, (tm, 1), 0)
    return row >= n_lat


def _mods_kernel(s_ref, w_ref, b_ref, o_ref):
    s = s_ref[...]
    s = s * jax.nn.sigmoid(s)
    o_ref[...] = jnp.dot(s, w_ref[...], preferred_element_type=F32,
                         precision=lax.Precision.HIGHEST) + b_ref[...]


def _mods(c, c_ctx, ada_w, ada_b):
    depth, d, w6 = ada_w.shape
    s = jnp.zeros((8, d), F32).at[0].set(c[0]).at[1].set(c_ctx)
    tn = 2048
    out = pl.pallas_call(
        _mods_kernel,
        grid=(depth, w6 // tn),
        in_specs=[pl.BlockSpec((8, d), lambda l, j: (0, 0)),
                  pl.BlockSpec((None, d, tn), lambda l, j: (l, 0, j)),
                  pl.BlockSpec((None, 1, tn), lambda l, j: (l, 0, j))],
        out_specs=pl.BlockSpec((None, 8, tn), lambda l, j: (l, 0, j)),
        out_shape=jax.ShapeDtypeStruct((depth, 8, w6), F32),
        compiler_params=_cparams(("arbitrary", "arbitrary")),
        name="ada_mods",
    )(s, ada_w, ada_b.reshape(depth, 1, w6))
    m = out[:, :2].reshape(depth, 2, 6, d)
    return jnp.pad(m, ((0, 0), (0, 0), (0, N_MOD - 6), (0, 0)))


def _normmod_prologue(x_ref, nw_ref, mods_ref, h_scr, *, n_lat, tm, shift_idx, scale_idx):
    @pl.when(pl.program_id(1) == 0)
    def _():
        rc = NORM_ROW_CHUNK
        assert tm % rc == 0 and n_lat % rc == 0

        def chunk(c, carry):
            r0 = pl.multiple_of(c * rc, rc)
            seg = (pl.program_id(0) * tm + r0 >= n_lat).astype(jnp.int32)
            gain = nw_ref[...] * (1.0 + mods_ref[seg, scale_idx:scale_idx + 1, :])
            shift = mods_ref[seg, shift_idx:shift_idx + 1, :]
            x = x_ref[pl.ds(r0, rc), :]
            r = lax.rsqrt(jnp.mean(x * x, axis=-1, keepdims=True) + EPS)
            h_scr[pl.ds(r0, rc), :] = (x * r * gain + shift).astype(h_scr.dtype)
            return carry

        lax.fori_loop(0, tm // rc, chunk, 0)

    return h_scr[...]


def _norm_proj_call(kern, x, norm_w, mods, n_lat, shift_idx, scale_idx, w_args, w_specs, extra_args, extra_specs,
                    n_out, tn, tm, name, **kern_kw):
    t, d = x.shape
    norm = dict(n_lat=n_lat, tm=tm, shift_idx=shift_idx, scale_idx=scale_idx)
    return pl.pallas_call(
        functools.partial(kern, norm=norm, **kern_kw),
        grid=(t // tm, n_out // tn),
        in_specs=[pl.BlockSpec((tm, d), lambda i, j: (i, 0)),
                  pl.BlockSpec((1, d), lambda i, j: (0, 0)),
                  pl.BlockSpec((2, N_MOD, d), lambda i, j: (0, 0, 0))] + w_specs + extra_specs,
        out_specs=pl.BlockSpec((tm, tn), lambda i, j: (i, j)),
        out_shape=jax.ShapeDtypeStruct((t, n_out), BF16),
        scratch_shapes=[pltpu.VMEM((tm, d), BF16)],
        compiler_params=_cparams(("arbitrary", "arbitrary")),
        name=name,
    )(x, norm_w.reshape(1, d), mods, *w_args, *extra_args)


def _final_norm_kernel(x_ref, w_ref, o_ref):
    x = x_ref[...]
    o_ref[...] = x * lax.rsqrt(jnp.mean(x * x, axis=-1, keepdims=True) + EPS) * w_ref[...]


def _final_norm(x, w, n_lat):
    d = x.shape[1]
    tm = _row_tile(n_lat, (512, 256, 128))
    return pl.pallas_call(
        _final_norm_kernel,
        grid=(n_lat // tm,),
        in_specs=[pl.BlockSpec((tm, d), lambda i: (i, 0)),
                  pl.BlockSpec((1, d), lambda i: (0, 0))],
        out_specs=pl.BlockSpec((tm, d), lambda i: (i, 0)),
        out_shape=jax.ShapeDtypeStruct((n_lat, d), F32),
        compiler_params=_cparams(("arbitrary",)),
        name="final_norm",
    )(x, w.reshape(1, d))


def _rope_store(acc, cos, sin, o_ref, scale):
    first_half = (lax.broadcasted_iota(jnp.int32, (1, HEAD_W), 1) % 32) < 16
    for c in range(acc.shape[1] // HEAD_W):
        x = acc[:, c * HEAD_W:(c + 1) * HEAD_W]
        rot = jnp.where(first_half, -pltpu.roll(x, HEAD_W - 16, 1), pltpu.roll(x, 16, 1))
        o_ref[:, c * HEAD_W:(c + 1) * HEAD_W] = ((x * cos + rot * sin) * scale).astype(o_ref.dtype)


def _inproj_even_kernel(x_ref, nw_ref, mods_ref, w_ref, cos_ref, sin_ref, o_ref, h_scr, *, norm, a_scale, b_scale):
    h = _normmod_prologue(x_ref, nw_ref, mods_ref, h_scr, **norm)
    j = pl.program_id(1)
    acc = jnp.dot(h, w_ref[...], preferred_element_type=F32)

    @pl.when(j == 0)
    def _():
        _rope_store(acc, cos_ref[...], sin_ref[...], o_ref, a_scale)

    @pl.when(j == 1)
    def _():
        _rope_store(acc, cos_ref[...], sin_ref[...], o_ref, 1.0)

    @pl.when(j == 3)
    def _():
        o_ref[...] = (acc * b_scale).astype(o_ref.dtype)

    @pl.when((j == 2) | (j > 3))
    def _():
        o_ref[...] = acc.astype(o_ref.dtype)


def _inproj_even(x, norm_w, mods, n_lat, w, layer, cos, sin):
    t, d = x.shape
    n = w.shape[2]
    tn = n // 6
    tm = _row_tile(t, (1280, 640, 256, 128))
    rope_spec = pl.BlockSpec((tm, HEAD_W), lambda i, j: (i, 0))
    return _norm_proj_call(
        _inproj_even_kernel, x, norm_w, mods, n_lat, 0, 1,
        [w], [pl.BlockSpec((None, d, tn), lambda i, j: (layer, 0, j))],
        [cos, sin], [rope_spec, rope_spec], n, tn, tm, "inproj_even",
        a_scale=A_QK_DIM ** -0.5 * math.log2(math.e), b_scale=HEAD_W ** -0.5 * math.log2(math.e))


def _inproj_odd_kernel(x_ref, nw_ref, mods_ref, w_ref, o_ref, h_scr, *, norm):
    h = _normmod_prologue(x_ref, nw_ref, mods_ref, h_scr, **norm)
    o_ref[...] = jnp.dot(h, w_ref[...], preferred_element_type=F32).astype(o_ref.dtype)


def _inproj_odd(x, norm_w, mods, n_lat, w):
    t, d = x.shape
    n = w.shape[1]
    tn = 512
    tm = _row_tile(t, (1280, 640, 256, 128))
    return _norm_proj_call(
        _inproj_odd_kernel, x, norm_w, mods, n_lat, 0, 1,
        [w], [pl.BlockSpec((d, tn), lambda i, j: (0, j))], [], [], n, tn, tm, "inproj_odd")


def _ffn_up_kernel(x_ref, nw_ref, mods_ref, w1_ref, w3_ref, o_ref, h_scr, *, norm):
    h = _normmod_prologue(x_ref, nw_ref, mods_ref, h_scr, **norm)
    a = jnp.dot(h, w1_ref[...], preferred_element_type=F32)
    b = jnp.dot(h, w3_ref[...], preferred_element_type=F32)
    o_ref[...] = (a * jax.nn.sigmoid(a) * b).astype(o_ref.dtype)


def _ffn_up(x, norm_w, mods, n_lat, w13, layer):
    t, d = x.shape
    d_ff = w13.shape[2] // 2
    tn = 512
    nj = d_ff // tn
    tm = _row_tile(t, (1280, 640, 256, 128))
    return _norm_proj_call(
        _ffn_up_kernel, x, norm_w, mods, n_lat, 3, 4,
        [w13, w13], [pl.BlockSpec((None, d, tn), lambda i, j: (layer, 0, j)),
                     pl.BlockSpec((None, d, tn), lambda i, j: (layer, 0, j + nj))],
        [], [], d_ff, tn, tm, "ffn_up")


def _gated_residual_kernel(*refs, n_a, n_lat, tm, gate_idx):
    a_refs = refs[:n_a]
    w_ref, x_ref, mods_ref, o_ref = refs[n_a:]
    k0 = 0
    y = None
    for a_ref in a_refs:
        kk = a_ref.shape[1]
        part = jnp.dot(a_ref[...], w_ref[k0:k0 + kk, :], preferred_element_type=F32)
        y = part if y is None else y + part
        k0 += kk
    gate = _pick_mod(mods_ref, gate_idx, _is_ctx_rows(tm, n_lat, 1))
    o_ref[...] = x_ref[...] + gate * y


def _gated_residual(a_list, w, x, mods, n_lat, gate_idx, name):
    t, d = x.shape
    k = w.shape[0]
    tm = _row_tile(t, (640, 256, 128))
    n_a = len(a_list)
    tn = 1024 if d % 1024 == 0 else 512
    w_spec = pl.BlockSpec((k, tn), lambda j, i: (0, j), pipeline_mode=pl.Buffered(1))
    in_specs = [pl.BlockSpec((tm, a.shape[1]), lambda j, i: (i, 0)) for a in a_list]
    in_specs += [w_spec,
                 pl.BlockSpec((tm, tn), lambda j, i: (i, j)),
                 pl.BlockSpec((2, N_MOD, tn), lambda j, i: (0, 0, j))]
    return pl.pallas_call(
        functools.partial(_gated_residual_kernel, n_a=n_a, n_lat=n_lat, tm=tm, gate_idx=gate_idx),
        grid=(d // tn, t // tm),
        in_specs=in_specs,
        out_specs=pl.BlockSpec((tm, tn), lambda j, i: (i, j)),
        out_shape=jax.ShapeDtypeStruct((t, d), F32),
        input_output_aliases={n_a + 1: 0},
        compiler_params=_cparams(("arbitrary", "arbitrary")),
        name=name,
    )(*a_list, w, x, mods)


def _softmax_block(qs, k, v):
    s = lax.dot_general(qs, k, (((1,), (1,)), ((), ())), preferred_element_type=F32)
    m = jnp.max(s, axis=-1, keepdims=True)
    p = jnp.exp2(s - m)
    l = jnp.sum(p, axis=-1, keepdims=True)
    acc = jnp.dot(p.astype(v.dtype), v, preferred_element_type=F32)
    return m, l, acc


DIFF_Q_TILE = 512
DIFF_K_CHUNK = 2048
DIFF_PV_ROWS = 512
DIFF_SOFTMAX_ROWS = 64
DIFF_EXP_DTYPE = BF16


def _diff_attn_kernel(*refs, tq, tk, n_ctx, n_chunks, lambda_init):
    if n_chunks > 1:
        (lq1, lk1, lq2, lk2, subw, q_ref, kc_ref, vc_ref, k_ref, v_ref, o_ref,
         qs_scr, sa_scr, mca_scr, p_scr, m_scr, l_scr, alpha_scr, acc_scr, sb_scr, mcb_scr) = refs
        buf_b = (sb_scr, mcb_scr)
    elif n_chunks == 1:
        (lq1, lk1, lq2, lk2, subw, q_ref, kc_ref, vc_ref, k_ref, v_ref, o_ref,
         qs_scr, sa_scr, mca_scr, p_scr, m_scr, l_scr, alpha_scr, acc_scr) = refs
    else:
        (lq1, lk1, lq2, lk2, subw, q_ref, kc_ref, vc_ref, o_ref,
         qs_scr, sa_scr, mca_scr, p_scr, m_scr, l_scr, alpha_scr, acc_scr) = refs
    buf_a = (sa_scr, mca_scr)
    nt = (((1,), (1,)), ((), ()))
    rows_all = 2 * tq
    pv_rows = min(DIFF_PV_ROWS, rows_all)
    sm_rows = min(DIFF_SOFTMAX_ROWS, pv_rows)
    q = q_ref[...]
    comp1 = lax.broadcasted_iota(jnp.int32, (1, HEAD_W), 1) < A_QK_DIM
    zero = jnp.zeros_like(q)
    qs_scr[0:tq, :] = jnp.where(comp1, q, zero)
    qs_scr[tq:2 * tq, :] = jnp.where(comp1, zero, q)

    def scores(buf, rows, keys):
        s_buf, mc_buf = buf
        s = lax.dot_general(qs_scr[rows, :], keys, nt, preferred_element_type=F32)
        width = s.shape[1]
        s_buf[rows, 0:width] = s
        mp = s[:, 0:HEAD_W]
        for c in range(1, width // HEAD_W):
            mp = jnp.maximum(mp, s[:, c * HEAD_W:(c + 1) * HEAD_W])
        mc_buf[rows, :] = jnp.broadcast_to(jnp.max(mp, axis=-1, keepdims=True), mp.shape)

    def softmax_rows(buf, rows, n_tiles, first):
        s_buf, mc_buf = buf
        m_cur = mc_buf[rows, :]
        if first:
            m_new = m_cur
        else:
            m_prev = m_scr[rows, :]
            m_new = jnp.maximum(m_prev, m_cur)
        lp = None
        for c in range(n_tiles):
            pc = jnp.exp2((s_buf[rows, c * HEAD_W:(c + 1) * HEAD_W] - m_new).astype(DIFF_EXP_DTYPE))
            p_scr[rows, c * HEAD_W:(c + 1) * HEAD_W] = pc.astype(BF16)
            lp = pc if lp is None else lp + pc
        l_cur = jnp.sum(lp.astype(F32), axis=-1, keepdims=True)
        if first:
            l_scr[rows, :] = jnp.broadcast_to(l_cur, m_new.shape)
        else:
            alpha = jnp.exp2(m_prev - m_new)
            alpha_scr[rows, :] = alpha
            l_scr[rows, :] = alpha * l_scr[rows, :] + l_cur
        m_scr[rows, :] = m_new

    def softmax_pv(buf, width, load_v, first, after_group=None):
        for r in range(rows_all // pv_rows):
            for r2 in range(pv_rows // sm_rows):
                r0 = r * pv_rows + r2 * sm_rows
                softmax_rows(buf, slice(r0, r0 + sm_rows), width // HEAD_W, first)
            rows = slice(r * pv_rows, (r + 1) * pv_rows)
            pv = jnp.dot(p_scr[rows, 0:width], load_v(), preferred_element_type=F32)
            acc_scr[rows, :] = pv if first else alpha_scr[rows, :] * acc_scr[rows, :] + pv
            if after_group is not None:
                after_group(rows)

    def all_groups(fn):
        for r in range(rows_all // pv_rows):
            fn(slice(r * pv_rows, (r + 1) * pv_rows))

    def qk_rows(j, buf, rows):
        off = pl.multiple_of(j * tk, tk)
        scores(buf, rows, k_ref[pl.ds(off, tk), :])

    all_groups(lambda rows: scores(buf_a, rows, kc_ref[...]))
    softmax_pv(buf_a, n_ctx, lambda: vc_ref[...], True,
               (lambda rows: qk_rows(0, buf_a, rows)) if n_chunks else None)

    def soft_pv(j, buf, j_next=None, buf_next=None):
        off = pl.multiple_of(j * tk, tk)
        after = None if j_next is None else (lambda rows: qk_rows(j_next, buf_next, rows))
        softmax_pv(buf, tk, lambda: v_ref[pl.ds(off, tk), :], False, after)

    if n_chunks == 1:
        soft_pv(0, buf_a)
    elif n_chunks > 1:

        def pair(j2, carry):
            soft_pv(2 * j2, buf_a, 2 * j2 + 1, buf_b)
            soft_pv(2 * j2 + 1, buf_b, 2 * j2 + 2, buf_a)
            return carry

        lax.fori_loop(0, n_chunks // 2 - 1, pair, 0)
        soft_pv(n_chunks - 2, buf_a, n_chunks - 1, buf_b)
        soft_pv(n_chunks - 1, buf_b)

    lam = (jnp.exp(jnp.sum(lq1[...] * lk1[...], axis=-1, keepdims=True))
           - jnp.exp(jnp.sum(lq2[...] * lk2[...], axis=-1, keepdims=True)) + lambda_init)
    o1 = acc_scr[0:tq, :] / l_scr[0:tq, :]
    o2 = acc_scr[tq:2 * tq, :] / l_scr[tq:2 * tq, :]
    o = o1 - lam * o2
    y = o * lax.rsqrt(jnp.mean(o * o, axis=-1, keepdims=True) + EPS) * subw[...]
    o_ref[...] = (y * (1.0 - lambda_init)).astype(o_ref.dtype)


def _diff_attn(qkv, lq1, lk1, lq2, lk2, subw, n_lat, n_ctx, lambda_init, prev=None):
    t = qkv.shape[0]
    use_lat = prev is None
    ctx_blk = n_lat // n_ctx
    small = [lq1.reshape(1, -1), lk1.reshape(1, -1), lq2.reshape(1, -1), lk2.reshape(1, -1), subw.reshape(1, -1)]
    small_specs = [pl.BlockSpec(a.shape, lambda h, qi: (0, 0)) for a in small]
    kc_spec = pl.BlockSpec((n_ctx, HEAD_W), lambda h, qi: (ctx_blk, A_HEADS + h))
    vc_spec = pl.BlockSpec((n_ctx, HEAD_W), lambda h, qi: (ctx_blk, 2 * A_HEADS + h))
    if use_lat:
        tq = _row_tile(n_lat, tuple(c for c in (512, 256, 128) if c <= DIFF_Q_TILE))
        tk = _row_tile(n_lat, tuple(c for c in (2048, 1024, 512, 256, 128) if c <= DIFF_K_CHUNK))
        n_chunks = n_lat // tk
        assert n_chunks == 1 or n_chunks % 2 == 0
        grid = (A_HEADS, n_lat // tq)
        in_specs = small_specs + [
            pl.BlockSpec((tq, HEAD_W), lambda h, qi: (qi, h)), kc_spec, vc_spec,
            pl.BlockSpec((n_lat, HEAD_W), lambda h, qi: (0, A_HEADS + h), pipeline_mode=pl.Buffered(1)),
            pl.BlockSpec((n_lat, HEAD_W), lambda h, qi: (0, 2 * A_HEADS + h), pipeline_mode=pl.Buffered(1))]
        args = small + [qkv, qkv, qkv, qkv, qkv]
        out_spec = pl.BlockSpec((tq, HEAD_W), lambda h, qi: (qi, h))
        aliases = {}
    else:
        tq, tk, n_chunks = n_ctx, 0, 0
        grid = (A_HEADS, 1)
        in_specs = small_specs + [
            pl.BlockSpec((tq, HEAD_W), lambda h, qi: (ctx_blk, h)), kc_spec, vc_spec,
            pl.BlockSpec(memory_space=pl.ANY)]
        args = small + [qkv, qkv, qkv, prev]
        out_spec = pl.BlockSpec((tq, HEAD_W), lambda h, qi: (ctx_blk, h))
        aliases = {len(args) - 1: 0}

    sw = max(tk, n_ctx)
    kern = functools.partial(_diff_attn_kernel, tq=tq, tk=tk, n_ctx=n_ctx, n_chunks=n_chunks,
                             lambda_init=lambda_init)
    if not use_lat:
        inner = kern

        def kern(*refs):
            n_in = len(args)
            inner(*refs[:n_in - 1], *refs[n_in:])

    return pl.pallas_call(
        kern,
        grid=grid,
        in_specs=in_specs,
        out_specs=out_spec,
        out_shape=jax.ShapeDtypeStruct((t, A_HEADS * HEAD_W), BF16),
        scratch_shapes=[pltpu.VMEM((2 * tq, HEAD_W), BF16),
                        pltpu.VMEM((2 * tq, sw), F32),
                        pltpu.VMEM((2 * tq, HEAD_W), F32),
                        pltpu.VMEM((2 * tq, sw), BF16),
                        pltpu.VMEM((2 * tq, HEAD_W), F32),
                        pltpu.VMEM((2 * tq, HEAD_W), F32),
                        pltpu.VMEM((2 * tq, HEAD_W), F32),
                        pltpu.VMEM((2 * tq, HEAD_W), F32)]
        + ([pltpu.VMEM((2 * tq, sw), F32),
            pltpu.VMEM((2 * tq, HEAD_W), F32)] if n_chunks > 1 else []),
        input_output_aliases=aliases,
        compiler_params=_cparams(("arbitrary", "arbitrary")),
        name="diff_attn" if use_lat else "diff_attn_ctx",
    )(*args)


def _na_bias_tables(rpb, rows):
    nblk = rows // NA_ROWS
    wr = min(WIN_R, rows)
    nh = rpb.shape[0]
    rq = jnp.arange(NA_ROWS)[:, None, None]
    slot = jnp.arange(3)[None, :, None]
    rk = jnp.arange(NA_ROWS)[None, None, :]
    row_sel, row_ok = [], []
    for b, dup in ((0, 0), (1, -1), (nblk - 1, 2)):
        r = NA_ROWS * b + rq
        r0 = jnp.clip(r - wr // 2, 0, rows - wr)
        rkey = NA_ROWS * (b - 1 + slot) + rk
        ok = (rkey >= r0) & (rkey < r0 + wr) & (slot != dup)
        drow = jnp.broadcast_to(rkey - r + (WIN_R - 1), ok.shape)
        row_sel.append((drow[..., None] == jnp.arange(2 * WIN_R - 1)) & ok[..., None])
        row_ok.append(ok)
    row_sel = jnp.stack(row_sel).astype(F32)
    row_ok = jnp.stack(row_ok)
    jq = jnp.arange(GRID_W)[:, None]
    jk = jnp.arange(GRID_W)[None, :]
    c0 = jnp.clip(jq - WIN_C // 2, 0, GRID_W - WIN_C)
    col_ok = (jk >= c0) & (jk < c0 + WIN_C)
    dcol = jnp.clip(jk - jq + (WIN_C - 1), 0, 2 * WIN_C - 2)
    col_sel = (dcol[..., None] == jnp.arange(2 * WIN_C - 1)).astype(F32)
    bias = jnp.einsum('vasbr,hrc,qkc->hvaqsbk', row_sel, rpb.astype(F32), col_sel,
                      precision=lax.Precision.HIGHEST)
    valid = row_ok[None, :, :, None, :, :, None] & col_ok[None, None, None, :, None, None, :]
    tab = jnp.where(valid, bias * math.log2(math.e), NEG_INF)
    return tab.reshape(nh, 3, NA_ROWS * GRID_W, 3 * NA_ROWS * GRID_W)


def _natten_kernel(tab_ref, q_ref, k0_ref, k1_ref, k2_ref, v0_ref, v1_ref, v2_ref, kc_ref, vc_ref, o_ref):
    nt = (((1,), (1,)), ((), ()))
    blk = q_ref.shape[0]
    for hh in range(NA_HEADS_PER_STEP):
        hs = slice(hh * HEAD_W, (hh + 1) * HEAD_W)
        q = q_ref[:, hs]
        s_loc = [lax.dot_general(q, k_ref[:, hs], nt, preferred_element_type=F32)
                 + tab_ref[hh, :, i * blk:(i + 1) * blk] for i, k_ref in enumerate((k0_ref, k1_ref, k2_ref))]
        s_ctx = lax.dot_general(q, kc_ref[:, hs], nt, preferred_element_type=F32)
        m = jnp.max(s_ctx, axis=-1, keepdims=True)
        for s in s_loc:
            m = jnp.maximum(m, jnp.max(s, axis=-1, keepdims=True))
        l = None
        acc = None
        for s, v_ref in zip([s_ctx] + s_loc, (vc_ref, v0_ref, v1_ref, v2_ref)):
            p = jnp.exp2((s - m).astype(BF16))
            pt = p[:, 0:HEAD_W]
            for c in range(1, p.shape[1] // HEAD_W):
                pt = pt + p[:, c * HEAD_W:(c + 1) * HEAD_W]
            lp = jnp.sum(pt.astype(F32), axis=-1, keepdims=True)
            pv = jnp.dot(p, v_ref[:, hs], preferred_element_type=F32)
            l = lp if l is None else l + lp
            acc = pv if acc is None else acc + pv
        o_ref[:, hs] = (acc / l).astype(o_ref.dtype)


def _natten(qkv, tabs, n_lat, n_ctx):
    t = qkv.shape[0]
    blk = NA_ROWS * GRID_W
    nblk = n_lat // blk
    assert nblk >= 3 and blk == n_ctx
    nh = NA_HEADS_PER_STEP
    hw = nh * HEAD_W
    qo, ko, vo = (3 * A_HEADS) // nh, (3 * A_HEADS + B_HEADS) // nh, (3 * A_HEADS + 2 * B_HEADS) // nh
    ctx_blk = n_lat // n_ctx

    def kv_spec(off, shift):
        return pl.BlockSpec((blk, hw), lambda h, b: (jnp.clip(b + shift, 0, nblk - 1), off + h))

    return pl.pallas_call(
        _natten_kernel,
        grid=(B_HEADS // nh, nblk),
        in_specs=[pl.BlockSpec((nh, None, blk, 3 * blk),
                               lambda h, b: (h, jnp.where(b == 0, 0, jnp.where(b == nblk - 1, 2, 1)), 0, 0)),
                  pl.BlockSpec((blk, hw), lambda h, b: (b, qo + h)),
                  kv_spec(ko, -1), kv_spec(ko, 0), kv_spec(ko, 1),
                  kv_spec(vo, -1), kv_spec(vo, 0), kv_spec(vo, 1),
                  pl.BlockSpec((n_ctx, hw), lambda h, b: (ctx_blk, ko + h)),
                  pl.BlockSpec((n_ctx, hw), lambda h, b: (ctx_blk, vo + h))],
        out_specs=pl.BlockSpec((blk, hw), lambda h, b: (b, h)),
        out_shape=jax.ShapeDtypeStruct((t, B_HEADS * HEAD_W), BF16),
        compiler_params=_cparams(("arbitrary", "arbitrary")),
        name="natten",
    )(tabs, qkv, qkv, qkv, qkv, qkv, qkv, qkv, qkv, qkv)


def _ctx_attn_kernel(q_ref, k_ref, v_ref, prev_ref, o_ref):
    del prev_ref
    _, l, acc = _softmax_block(q_ref[...], k_ref[...], v_ref[...])
    o_ref[...] = (acc / l).astype(o_ref.dtype)


def _ctx_attn(qkv, prev, n_lat, n_ctx):
    qo, ko, vo = 3 * A_HEADS, 3 * A_HEADS + B_HEADS, 3 * A_HEADS + 2 * B_HEADS
    ctx_blk = n_lat // n_ctx
    return pl.pallas_call(
        _ctx_attn_kernel,
        grid=(B_HEADS,),
        in_specs=[pl.BlockSpec((n_ctx, HEAD_W), lambda h: (ctx_blk, qo + h)),
                  pl.BlockSpec((n_ctx, HEAD_W), lambda h: (ctx_blk, ko + h)),
                  pl.BlockSpec((n_ctx, HEAD_W), lambda h: (ctx_blk, vo + h)),
                  pl.BlockSpec(memory_space=pl.ANY)],
        out_specs=pl.BlockSpec((n_ctx, HEAD_W), lambda h: (ctx_blk, h)),
        out_shape=jax.ShapeDtypeStruct(prev.shape, prev.dtype),
        input_output_aliases={3: 0},
        compiler_params=_cparams(("arbitrary",)),
        name="ctx_attn",
    )(qkv, qkv, qkv, prev)


def _s5_tables(lam_re, lam_im, b_re, b_im, c_re, c_im, log_step, d_skip):
    hp = lax.Precision.HIGHEST
    ln = S5_CHUNK
    g, p, c = S5_GROUPS, S5_STATE, S5_GROUP
    lr, li = lam_re.astype(F32), lam_im.astype(F32)
    dt = jnp.exp(log_step.astype(F32))[:, :, None]
    mag = jnp.exp(lr * dt)
    ar, ai = mag * jnp.cos(li * dt), mag * jnp.sin(li * dt)
    den = lr * lr + li * li
    nr, ni = ar - 1.0, ai
    fr = (nr * lr + ni * li) / den
    fi = (ni * lr - nr * li) / den
    br_, bi_ = b_re.astype(F32), b_im.astype(F32)
    bbr = fr[..., None] * br_ - fi[..., None] * bi_
    bbi = fr[..., None] * bi_ + fi[..., None] * br_
    lag = jnp.arange(ln + 1, dtype=F32)[:, None, None, None]
    magl = jnp.exp(lr * dt * lag)
    pr, pi_ = magl * jnp.cos(li * dt * lag), magl * jnp.sin(li * dt * lag)
    wr = pr[..., None] * bbr - pi_[..., None] * bbi
    wi = pr[..., None] * bbi + pi_[..., None] * bbr
    cr, ci = c_re.astype(F32), c_im.astype(F32)
    kern = (jnp.einsum('dgcp,ldgpe->ldgce', cr, wr[:ln], precision=hp)
            - jnp.einsum('dgcp,ldgpe->ldgce', ci, wi[:ln], precision=hp))
    lag = np.arange(ln)[:, None, None]
    s_idx = np.arange(ln)[None, :, None]
    t_idx = np.arange(ln)[None, None, :]
    sel_f = jnp.asarray(t_idx - s_idx == lag, F32)
    sel_b = jnp.asarray(s_idx - t_idx == lag, F32)
    kfb = (jnp.einsum('lst,lgce->stgce', sel_f, kern[:, 0], precision=hp)
           + jnp.einsum('lst,lgce->stgce', sel_b, kern[:, 1], precision=hp))
    m_tab = kfb.transpose(2, 0, 4, 1, 3).reshape(g, ln * c, ln * c)
    wf_r, wf_i = wr[:ln, 0][::-1], wi[:ln, 0][::-1]
    wb_r, wb_i = wr[:ln, 1], wi[:ln, 1]

    def inj(w):
        return w.transpose(1, 0, 3, 2).reshape(g, ln * c, p)

    b_tab = jnp.concatenate([inj(wf_r), inj(wf_i), inj(wb_r), inj(wb_i)], axis=-1)
    pf_r, pf_i = pr[1:ln + 1, 0], pi_[1:ln + 1, 0]
    pb_r, pb_i = pr[1:ln + 1, 1][::-1], pi_[1:ln + 1, 1][::-1]

    def rd(pw_r, pw_i, cre, cim):
        car = cre[None] * pw_r[:, :, None, :] - cim[None] * pw_i[:, :, None, :]
        cai = cre[None] * pw_i[:, :, None, :] + cim[None] * pw_r[:, :, None, :]
        to_rows = lambda a: a.transpose(1, 3, 0, 2).reshape(g, p, ln * c)
        return to_rows(car), to_rows(-cai)

    c_tab = jnp.concatenate(rd(pf_r, pf_i, cr[0], ci[0]) + rd(pb_r, pb_i, cr[1], ci[1]), axis=1)
    a_chunk = jnp.stack([jnp.stack([pr[ln, 0], pi_[ln, 0]]), jnp.stack([pr[ln, 1], pi_[ln, 1]])])
    d_tab = jnp.tile(d_skip.astype(F32).reshape(g, 1, c), (1, ln, 1)).reshape(g, 1, ln * c)
    return m_tab.astype(BF16), b_tab.astype(BF16), c_tab.astype(BF16), a_chunk, d_tab


def _s5_inject_kernel(u_ref, b_ref, o_ref):
    o_ref[...] = jnp.dot(u_ref[...], b_ref[...], preferred_element_type=F32)


def _s5_scan_kernel(a_ref, e_ref, o_ref, s_scr, *, nb):
    d = pl.program_id(0)

    @pl.when(pl.program_id(1) == 0)
    def _():
        s_scr[...] = jnp.zeros_like(s_scr)

    ar, ai = a_ref[0], a_ref[1]

    def body(i, carry):
        sr, si = carry
        c = jnp.where(d == 0, i, nb - 1 - i)
        o_ref[0, c] = sr
        o_ref[1, c] = si
        return ar * sr - ai * si + e_ref[0, c], ar * si + ai * sr + e_ref[1, c]

    sr, si = lax.fori_loop(0, nb, body, (s_scr[0], s_scr[1]))
    s_scr[0] = sr
    s_scr[1] = si


def _gelu_tanh(x):
    return 0.5 * x * (1.0 + jnp.tanh(math.sqrt(2.0 / math.pi) * (x + 0.044715 * (x * x * x))))


def _s5_readout_kernel(u_ref, m_ref, s_ref, c_ref, d_ref, o_ref):
    u = u_ref[...]
    y = jnp.dot(u, m_ref[...], preferred_element_type=F32) + u.astype(F32) * d_ref[...]
    for k in range(4):
        y = y + jnp.dot(s_ref[k].astype(BF16), c_ref[k * S5_STATE:(k + 1) * S5_STATE, :],
                        preferred_element_type=F32)
    o_ref[...] = _gelu_tanh(y).astype(o_ref.dtype)


def _s5_glu_kernel(z_ref, w_ref, o_ref):
    z = z_ref[...]
    gate = jax.nn.sigmoid(jnp.dot(z.astype(BF16), w_ref[...], preferred_element_type=F32))
    o_ref[...] = (z.astype(F32) * gate).astype(o_ref.dtype)


def _s5_mixer(u, tables, w_glu, n_lat, n_ctx):
    m_tab, b_tab, c_tab, a_chunk, d_tab = tables
    t = u.shape[0]
    g, c, ln, p = S5_GROUPS, S5_GROUP, S5_CHUNK, S5_STATE
    nc = t // ln
    w = ln * c
    ug = u.reshape(nc, ln, g, c).transpose(2, 0, 1, 3).reshape(g, nc, w)

    e = pl.pallas_call(
        _s5_inject_kernel,
        grid=(g,),
        in_specs=[pl.BlockSpec((None, nc, w), lambda i: (i, 0, 0)),
                  pl.BlockSpec((None, w, 4 * p), lambda i: (i, 0, 0))],
        out_specs=pl.BlockSpec((None, nc, 4 * p), lambda i: (i, 0, 0)),
        out_shape=jax.ShapeDtypeStruct((g, nc, 4 * p), F32),
        compiler_params=_cparams(("arbitrary",)),
        name="s5_inject",
    )(ug, b_tab)
    e = e.reshape(g, nc, 2, 2, p).transpose(2, 3, 1, 0, 4)

    nb = S5_BLOCK
    assert (n_lat // ln) % nb == 0 and n_ctx // ln == nb
    lat_blocks = n_lat // ln // nb

    def blk(d, s):
        return jnp.where(s == 0, lat_blocks, jnp.where(d == 0, s - 1, lat_blocks - s))

    s_in = pl.pallas_call(
        functools.partial(_s5_scan_kernel, nb=nb),
        grid=(2, lat_blocks + 1),
        in_specs=[pl.BlockSpec((None, 2, g, p), lambda d, s: (d, 0, 0, 0)),
                  pl.BlockSpec((None, 2, nb, g, p), lambda d, s: (d, 0, blk(d, s), 0, 0))],
        out_specs=pl.BlockSpec((None, 2, nb, g, p), lambda d, s: (d, 0, blk(d, s), 0, 0)),
        out_shape=jax.ShapeDtypeStruct((2, 2, nc, g, p), F32),
        scratch_shapes=[pltpu.VMEM((2, g, p), F32)],
        compiler_params=_cparams(("arbitrary", "arbitrary")),
        name="s5_scan",
    )(a_chunk, e)
    s_in = s_in.transpose(3, 0, 1, 2, 4).reshape(g, 4, nc, p)

    z = pl.pallas_call(
        _s5_readout_kernel,
        grid=(g,),
        in_specs=[pl.BlockSpec((None, nc, w), lambda i: (i, 0, 0)),
                  pl.BlockSpec((None, w, w), lambda i: (i, 0, 0)),
                  pl.BlockSpec((None, 4, nc, p), lambda i: (i, 0, 0, 0)),
                  pl.BlockSpec((None, 4 * p, w), lambda i: (i, 0, 0)),
                  pl.BlockSpec((None, 1, w), lambda i: (i, 0, 0))],
        out_specs=pl.BlockSpec((None, nc, w), lambda i: (i, 0, 0)),
        out_shape=jax.ShapeDtypeStruct((g, nc, w), BF16),
        compiler_params=_cparams(("arbitrary",)),
        name="s5_readout",
    )(ug, m_tab, s_in, c_tab, d_tab)
    z = z.reshape(g, nc, ln, c).transpose(1, 2, 0, 3).reshape(t, g * c)

    tm = _row_tile(t, (1280, 640, 256, 128))
    return pl.pallas_call(
        _s5_glu_kernel,
        grid=(t // tm,),
        in_specs=[pl.BlockSpec((tm, g * c), lambda i: (i, 0)),
                  pl.BlockSpec((g * c, g * c), lambda i: (0, 0))],
        out_specs=pl.BlockSpec((tm, g * c), lambda i: (i, 0)),
        out_shape=jax.ShapeDtypeStruct((t, g * c), BF16),
        compiler_params=_cparams(("arbitrary",)),
        name="s5_glu",
    )(z, w_glu)


def _ret_tables(decay_logit):
    scale = HEAD_W ** -0.5
    lg = jax.nn.log_sigmoid(decay_logit.astype(F32))
    lf, lb = lg[0][:, None, None], lg[1][:, None, None]
    i = jnp.arange(RET_CHUNK, dtype=F32)[None, :, None]
    j = jnp.arange(RET_CHUNK, dtype=F32)[None, None, :]
    diff = i - j
    intra = (jnp.where(diff >= 0, jnp.exp(lf * jnp.maximum(diff, 0.0)), 0.0)
             + jnp.where(diff <= 0, jnp.exp(lb * jnp.maximum(-diff, 0.0)), 0.0)) * scale
    ones = jnp.ones((1, 1, HEAD_W), F32)
    q_f = jnp.exp(lf * (i + 1.0)) * ones
    k_f = jnp.exp(lf * (RET_CHUNK - 1.0 - i)) * scale * ones
    c_f = jnp.exp(lf * RET_CHUNK) * jnp.ones((1, RET_CHUNK, HEAD_W), F32)
    q_b = jnp.exp(lb * (RET_CHUNK - i)) * ones
    k_b = jnp.exp(lb * i) * scale * ones
    c_b = jnp.exp(lb * RET_CHUNK) * jnp.ones((1, RET_CHUNK, HEAD_W), F32)
    return intra, jnp.stack([q_f, k_f, c_f]), jnp.stack([q_b, k_b, c_b])


def _ret_state_step(q, k, v, dec_ref, s_scr, h):
    s = s_scr[h]
    qd = (q.astype(F32) * dec_ref[0, h]).astype(BF16)
    kd = (k.astype(F32) * dec_ref[1, h]).astype(BF16)
    o = jnp.dot(qd, s.astype(BF16), preferred_element_type=F32)
    s_scr[h] = s * dec_ref[2, h] + lax.dot_general(kd, v, (((0,), (0,)), ((), ())),
                                                    preferred_element_type=F32)
    return o


def _ret_fwd_kernel(intra_ref, dec_ref, q_ref, k_ref, v_ref, o_ref, s_scr):
    @pl.when(pl.program_id(0) == 0)
    def _():
        s_scr[...] = jnp.zeros_like(s_scr)

    for c in range(RET_STEP_CHUNKS):
        rows = slice(c * RET_CHUNK, (c + 1) * RET_CHUNK)
        for h in range(RET_HEADS):
            sl = slice(h * HEAD_W, (h + 1) * HEAD_W)
            q, k, v = q_ref[rows, sl], k_ref[rows, sl], v_ref[rows, sl]
            att = lax.dot_general(q, k, (((1,), (1,)), ((), ())), preferred_element_type=F32) * intra_ref[h]
            o = jnp.dot(att.astype(BF16), v, preferred_element_type=F32)
            o_ref[rows, sl] = o + _ret_state_step(q, k, v, dec_ref, s_scr, h)


def _ret_bwd_kernel(dec_ref, q_ref, k_ref, v_ref, g_ref, o1_ref, o_ref, s_scr):
    @pl.when(pl.program_id(0) == 0)
    def _():
        s_scr[...] = jnp.zeros_like(s_scr)

    for c in reversed(range(RET_STEP_CHUNKS)):
        rows = slice(c * RET_CHUNK, (c + 1) * RET_CHUNK)
        for h in range(RET_HEADS):
            sl = slice(h * HEAD_W, (h + 1) * HEAD_W)
            q, k, v = q_ref[rows, sl], k_ref[rows, sl], v_ref[rows, sl]
            o = o1_ref[rows, sl] + _ret_state_step(q, k, v, dec_ref, s_scr, h)
            y = o * lax.rsqrt(jnp.mean(o * o, axis=-1, keepdims=True) + EPS)
            gate = g_ref[rows, sl].astype(F32)
            o_ref[rows, sl] = (y * (gate * jax.nn.sigmoid(gate))).astype(o_ref.dtype)


def _retention(proj, tables, n_lat, n_ctx):
    intra, dec_f, dec_b = tables
    t = proj.shape[0]
    wd = RET_HEADS * HEAD_W
    ck = RET_CHUNK
    bt = RET_STEP_CHUNKS * ck
    assert n_lat % bt == 0 and n_ctx % bt == 0
    n_lat_c, n_ctx_c = n_lat // bt, n_ctx // bt
    steps = n_lat_c + n_ctx_c

    def fwd_blk(s):
        return jnp.where(s < n_ctx_c, n_lat_c + s, s - n_ctx_c)

    def bwd_blk(s):
        return jnp.where(s < n_ctx_c, n_lat_c + n_ctx_c - 1 - s, n_lat_c - 1 - (s - n_ctx_c))

    def col_spec(col, order):
        return pl.BlockSpec((bt, wd), lambda s: (order(s), col))

    tab_spec = pl.BlockSpec((3, RET_HEADS, ck, HEAD_W), lambda s: (0, 0, 0, 0))
    o1 = pl.pallas_call(
        _ret_fwd_kernel,
        grid=(steps,),
        in_specs=[pl.BlockSpec((RET_HEADS, ck, ck), lambda s: (0, 0, 0)), tab_spec,
                  col_spec(0, fwd_blk), col_spec(1, fwd_blk), col_spec(2, fwd_blk)],
        out_specs=col_spec(0, fwd_blk),
        out_shape=jax.ShapeDtypeStruct((t, wd), F32),
        scratch_shapes=[pltpu.VMEM((RET_HEADS, HEAD_W, HEAD_W), F32)],
        compiler_params=_cparams(("arbitrary",)),
        name="retention_fwd",
    )(intra, dec_f, proj, proj, proj)
    return pl.pallas_call(
        _ret_bwd_kernel,
        grid=(steps,),
        in_specs=[tab_spec, col_spec(0, bwd_blk), col_spec(1, bwd_blk), col_spec(2, bwd_blk),
                  col_spec(3, bwd_blk), col_spec(0, bwd_blk)],
        out_specs=col_spec(0, bwd_blk),
        out_shape=jax.ShapeDtypeStruct((t, wd), BF16),
        scratch_shapes=[pltpu.VMEM((RET_HEADS, HEAD_W, HEAD_W), F32)],
        compiler_params=_cparams(("arbitrary",)),
        name="retention_bwd",
    )(dec_b, proj, proj, proj, proj, o1)


def _rope_tables(n_lat, n_ctx):
    n_freq = A_QK_DIM // 4
    freq = ROPE_BASE ** (-jnp.arange(n_freq, dtype=F32) / n_freq)
    rows = n_lat // GRID_W
    ang_row = jnp.arange(rows, dtype=F32)[:, None] * freq
    ang_col = jnp.arange(GRID_W, dtype=F32)[:, None] * freq
    reps = HEAD_W // A_QK_DIM

    def table(fn, ctx_value):
        r = jnp.broadcast_to(jnp.tile(fn(ang_row), (1, 2))[:, None, :], (rows, GRID_W, 2 * n_freq))
        c = jnp.broadcast_to(jnp.tile(fn(ang_col), (1, 2))[None, :, :], (rows, GRID_W, 2 * n_freq))
        lat = jnp.tile(jnp.concatenate([r, c], axis=-1).reshape(n_lat, A_QK_DIM), (1, reps))
        return jnp.concatenate([lat, jnp.full((n_ctx, HEAD_W), ctx_value, F32)], axis=0)

    return table(jnp.cos, 1.0), table(jnp.sin, 0.0)


def kernel(x, c, ctx, c_ctx, ada_w, ada_b, norm1_w, norm2_w, ffn_w13, ffn_w2, e_w_in, e_w_out, diff_lq1, diff_lk1, diff_lq2, diff_lk2, diff_subln_w, na_rpb, o_w_in, o_w_out, s5_lam_re, s5_lam_im, s5_b_re, s5_b_im, s5_c_re, s5_c_im, s5_log_step, s5_d, s5_w_glu, ret_decay_logit, final_norm_w):
    bsz, n_lat, d = x.shape
    n_ctx = ctx.shape[1]
    depth = ada_w.shape[0]
    assert bsz == 1
    xs = jnp.concatenate([x[0], ctx[0]], axis=0)
    mods_all = _mods(c, c_ctx, ada_w, ada_b)
    cos, sin = _rope_tables(n_lat, n_ctx)
    ret_w = RET_HEADS * HEAD_W
    ffn_w13_bf16 = ffn_w13.astype(BF16)
    e_w_in_bf16 = e_w_in.astype(BF16)

    for i in range(depth):
        compute_ctx = i != depth - 1
        mods = mods_all[i]
        j = i // 2
        if i % 2 == 0:
            lambda_init = 0.8 - 0.6 * math.exp(-0.3 * i)
            qkv = _inproj_even(xs, norm1_w[i], mods, n_lat, e_w_in_bf16, j, cos, sin)
            lam_args = (diff_lq1[j], diff_lk1[j], diff_lq2[j], diff_lk2[j], diff_subln_w[j])
            o_a = _diff_attn(qkv, *lam_args, n_lat, n_ctx, lambda_init)
            o_b = _natten(qkv, _na_bias_tables(na_rpb[j], n_lat // GRID_W), n_lat, n_ctx)
            if compute_ctx:
                o_a = _diff_attn(qkv, *lam_args, n_lat, n_ctx, lambda_init, prev=o_a)
                o_b = _ctx_attn(qkv, o_b, n_lat, n_ctx)
            xs = _gated_residual([o_a, o_b], e_w_out[j].astype(BF16), xs, mods, n_lat, 2, "outproj_even")
        else:
            w_in = o_w_in[j]
            w_in = jnp.concatenate([w_in[:, S5_CH:], w_in[:, :S5_CH]], axis=1).astype(BF16)
            proj = _inproj_odd(xs, norm1_w[i], mods, n_lat, w_in)
            s5_tabs = _s5_tables(s5_lam_re[j], s5_lam_im[j], s5_b_re[j], s5_b_im[j], s5_c_re[j], s5_c_im[j],
                                 s5_log_step[j], s5_d[j])
            y_c = _s5_mixer(proj[:, 4 * ret_w:], s5_tabs, s5_w_glu[j].astype(BF16), n_lat, n_ctx)
            y_d = _retention(proj, _ret_tables(ret_decay_logit[j]), n_lat, n_ctx)
            xs = _gated_residual([y_c, y_d], o_w_out[j].astype(BF16), xs, mods, n_lat, 2, "outproj_odd")
        a = _ffn_up(xs, norm2_w[i], mods, n_lat, ffn_w13_bf16, i)
        xs = _gated_residual([a], ffn_w2[i].astype(BF16), xs, mods, n_lat, 5, "ffn_down")
    return _final_norm(xs, final_norm_w, n_lat)[None]
```

```python
import functools
import math

import jax
import jax.numpy as jnp
import numpy as np
from jax import lax
from jax.experimental import pallas as pl
from jax.experimental.pallas import tpu as pltpu

F32 = jnp.float32
BF16 = jnp.bfloat16

GRID_W = 64
A_HEADS = 8
A_QK_DIM = 64
HEAD_W = 128
B_HEADS = 8
WIN_R = 8
WIN_C = 16
NA_ROWS = 4
NA_HEADS_PER_STEP = 8
S5_CH = 512
S5_GROUP = 16
S5_GROUPS = 32
S5_STATE = 64
S5_CHUNK = 16
S5_BLOCK = 16
RET_HEADS = 12
RET_CHUNK = 128
RET_STEP_CHUNKS = 2
ROPE_BASE = 10000.0
EPS = 1e-6
NEG_INF = -1e30
VMEM_LIMIT = 56 * 1024 * 1024
N_MOD = 8
NORM_ROW_CHUNK = 128


def _cparams(sem):
    return pltpu.CompilerParams(dimension_semantics=sem, vmem_limit_bytes=VMEM_LIMIT)


def _row_tile(t, candidates):
    for c in candidates:
        if t % c == 0:
            return c
    raise ValueError(f"no row tile for {t}")


def _pick_mod(mods_ref, idx, is_ctx):
    return jnp.where(is_ctx, mods_ref[1, idx:idx + 1, :], mods_ref[0, idx:idx + 1, :])


def _is_ctx_rows(tm, n_lat, axis):
    row = pl.program_id(axis) * tm + lax.broadcasted_iota(jnp.int32, (tm, 1), 0)
    return row >= n_lat


def _mods_kernel(s_ref, w_ref, b_ref, o_ref):
    s = s_ref[...]
    s = s * jax.nn.sigmoid(s)
    o_ref[...] = jnp.dot(s, w_ref[...], preferred_element_type=F32,
                         precision=lax.Precision.HIGHEST) + b_ref[...]


def _mods(c, c_ctx, ada_w, ada_b):
    depth, d, w6 = ada_w.shape
    s = jnp.zeros((8, d), F32).at[0].set(c[0]).at[1].set(c_ctx)
    tn = 2048
    out = pl.pallas_call(
        _mods_kernel,
        grid=(depth, w6 // tn),
        in_specs=[pl.BlockSpec((8, d), lambda l, j: (0, 0)),
                  pl.BlockSpec((None, d, tn), lambda l, j: (l, 0, j)),
                  pl.BlockSpec((None, 1, tn), lambda l, j: (l, 0, j))],
        out_specs=pl.BlockSpec((None, 8, tn), lambda l, j: (l, 0, j)),
        out_shape=jax.ShapeDtypeStruct((depth, 8, w6), F32),
        compiler_params=_cparams(("arbitrary", "arbitrary")),
        name="ada_mods",
    )(s, ada_w, ada_b.reshape(depth, 1, w6))
    m = out[:, :2].reshape(depth, 2, 6, d)
    return jnp.pad(m, ((0, 0), (0, 0), (0, N_MOD - 6), (0, 0)))


def _normmod_prologue(x_ref, nw_ref, mods_ref, h_scr, *, n_lat, tm, shift_idx, scale_idx):
    @pl.when(pl.program_id(1) == 0)
    def _():
        rc = NORM_ROW_CHUNK
        assert tm % rc == 0 and n_lat % rc == 0

        def chunk(c, carry):
            r0 = pl.multiple_of(c * rc, rc)
            seg = (pl.program_id(0) * tm + r0 >= n_lat).astype(jnp.int32)
            gain = nw_ref[...] * (1.0 + mods_ref[seg, scale_idx:scale_idx + 1, :])
            shift = mods_ref[seg, shift_idx:shift_idx + 1, :]
            x = x_ref[pl.ds(r0, rc), :]
            r = lax.rsqrt(jnp.mean(x * x, axis=-1, keepdims=True) + EPS)
            h_scr[pl.ds(r0, rc), :] = (x * r * gain + shift).astype(h_scr.dtype)
            return carry

        lax.fori_loop(0, tm // rc, chunk, 0)

    return h_scr[...]


def _norm_proj_call(kern, x, norm_w, mods, n_lat, shift_idx, scale_idx, w_args, w_specs, extra_args, extra_specs,
                    n_out, tn, tm, name, **kern_kw):
    t, d = x.shape
    norm = dict(n_lat=n_lat, tm=tm, shift_idx=shift_idx, scale_idx=scale_idx)
    return pl.pallas_call(
        functools.partial(kern, norm=norm, **kern_kw),
        grid=(t // tm, n_out // tn),
        in_specs=[pl.BlockSpec((tm, d), lambda i, j: (i, 0)),
                  pl.BlockSpec((1, d), lambda i, j: (0, 0)),
                  pl.BlockSpec((2, N_MOD, d), lambda i, j: (0, 0, 0))] + w_specs + extra_specs,
        out_specs=pl.BlockSpec((tm, tn), lambda i, j: (i, j)),
        out_shape=jax.ShapeDtypeStruct((t, n_out), BF16),
        scratch_shapes=[pltpu.VMEM((tm, d), BF16)],
        compiler_params=_cparams(("arbitrary", "arbitrary")),
        name=name,
    )(x, norm_w.reshape(1, d), mods, *w_args, *extra_args)


def _final_norm_kernel(x_ref, w_ref, o_ref):
    x = x_ref[...]
    o_ref[...] = x * lax.rsqrt(jnp.mean(x * x, axis=-1, keepdims=True) + EPS) * w_ref[...]


def _final_norm(x, w, n_lat):
    d = x.shape[1]
    tm = _row_tile(n_lat, (512, 256, 128))
    return pl.pallas_call(
        _final_norm_kernel,
        grid=(n_lat // tm,),
        in_specs=[pl.BlockSpec((tm, d), lambda i: (i, 0)),
                  pl.BlockSpec((1, d), lambda i: (0, 0))],
        out_specs=pl.BlockSpec((tm, d), lambda i: (i, 0)),
        out_shape=jax.ShapeDtypeStruct((n_lat, d), F32),
        compiler_params=_cparams(("arbitrary",)),
        name="final_norm",
    )(x, w.reshape(1, d))


def _rope_store(acc, cos, sin, o_ref, scale):
    first_half = (lax.broadcasted_iota(jnp.int32, (1, HEAD_W), 1) % 32) < 16
    for c in range(acc.shape[1] // HEAD_W):
        x = acc[:, c * HEAD_W:(c + 1) * HEAD_W]
        rot = jnp.where(first_half, -pltpu.roll(x, HEAD_W - 16, 1), pltpu.roll(x, 16, 1))
        o_ref[:, c * HEAD_W:(c + 1) * HEAD_W] = ((x * cos + rot * sin) * scale).astype(o_ref.dtype)


def _inproj_even_kernel(x_ref, nw_ref, mods_ref, w_ref, cos_ref, sin_ref, o_ref, h_scr, *, norm, a_scale, b_scale):
    h = _normmod_prologue(x_ref, nw_ref, mods_ref, h_scr, **norm)
    j = pl.program_id(1)
    acc = jnp.dot(h, w_ref[...], preferred_element_type=F32)

    @pl.when(j == 0)
    def _():
        _rope_store(acc, cos_ref[...], sin_ref[...], o_ref, a_scale)

    @pl.when(j == 1)
    def _():
        _rope_store(acc, cos_ref[...], sin_ref[...], o_ref, 1.0)

    @pl.when(j == 3)
    def _():
        o_ref[...] = (acc * b_scale).astype(o_ref.dtype)

    @pl.when((j == 2) | (j > 3))
    def _():
        o_ref[...] = acc.astype(o_ref.dtype)


def _inproj_even(x, norm_w, mods, n_lat, w, layer, cos, sin):
    t, d = x.shape
    n = w.shape[2]
    tn = n // 6
    tm = _row_tile(t, (1280, 640, 256, 128))
    rope_spec = pl.BlockSpec((tm, HEAD_W), lambda i, j: (i, 0))
    return _norm_proj_call(
        _inproj_even_kernel, x, norm_w, mods, n_lat, 0, 1,
        [w], [pl.BlockSpec((None, d, tn), lambda i, j: (layer, 0, j))],
        [cos, sin], [rope_spec, rope_spec], n, tn, tm, "inproj_even",
        a_scale=A_QK_DIM ** -0.5 * math.log2(math.e), b_scale=HEAD_W ** -0.5 * math.log2(math.e))


def _inproj_odd_kernel(x_ref, nw_ref, mods_ref, w_ref, o_ref, h_scr, *, norm):
    h = _normmod_prologue(x_ref, nw_ref, mods_ref, h_scr, **norm)
    o_ref[...] = jnp.dot(h, w_ref[...], preferred_element_type=F32).astype(o_ref.dtype)


def _inproj_odd(x, norm_w, mods, n_lat, w):
    t, d = x.shape
    n = w.shape[1]
    tn = 512
    tm = _row_tile(t, (1280, 640, 256, 128))
    return _norm_proj_call(
        _inproj_odd_kernel, x, norm_w, mods, n_lat, 0, 1,
        [w], [pl.BlockSpec((d, tn), lambda i, j: (0, j))], [], [], n, tn, tm, "inproj_odd")


def _ffn_up_kernel(x_ref, nw_ref, mods_ref, w1_ref, w3_ref, o_ref, h_scr, *, norm):
    h = _normmod_prologue(x_ref, nw_ref, mods_ref, h_scr, **norm)
    a = jnp.dot(h, w1_ref[...], preferred_element_type=F32)
    b = jnp.dot(h, w3_ref[...], preferred_element_type=F32)
    o_ref[...] = (a * jax.nn.sigmoid(a) * b).astype(o_ref.dtype)


def _ffn_up(x, norm_w, mods, n_lat, w13, layer):
    t, d = x.shape
    d_ff = w13.shape[2] // 2
    tn = 512
    nj = d_ff // tn
    tm = _row_tile(t, (1280, 640, 256, 128))
    return _norm_proj_call(
        _ffn_up_kernel, x, norm_w, mods, n_lat, 3, 4,
        [w13, w13], [pl.BlockSpec((None, d, tn), lambda i, j: (layer, 0, j)),
                     pl.BlockSpec((None, d, tn), lambda i, j: (layer, 0, j + nj))],
        [], [], d_ff, tn, tm, "ffn_up")


def _gated_residual_kernel(*refs, n_a, n_lat, tm, gate_idx):
    a_refs = refs[:n_a]
    w_ref, x_ref, mods_ref, o_ref = refs[n_a:]
    k0 = 0
    y = None
    for a_ref in a_refs:
        kk = a_ref.shape[1]
        part = jnp.dot(a_ref[...], w_ref[k0:k0 + kk, :], preferred_element_type=F32)
        y = part if y is None else y + part
        k0 += kk
    gate = _pick_mod(mods_ref, gate_idx, _is_ctx_rows(tm, n_lat, 1))
    o_ref[...] = x_ref[...] + gate * y


def _gated_residual(a_list, w, x, mods, n_lat, gate_idx, name):
    t, d = x.shape
    k = w.shape[0]
    tm = _row_tile(t, (640, 256, 128))
    n_a = len(a_list)
    tn = 1024 if d % 1024 == 0 else 512
    w_spec = pl.BlockSpec((k, tn), lambda j, i: (0, j), pipeline_mode=pl.Buffered(1))
    in_specs = [pl.BlockSpec((tm, a.shape[1]), lambda j, i: (i, 0)) for a in a_list]
    in_specs += [w_spec,
                 pl.BlockSpec((tm, tn), lambda j, i: (i, j)),
                 pl.BlockSpec((2, N_MOD, tn), lambda j, i: (0, 0, j))]
    return pl.pallas_call(
        functools.partial(_gated_residual_kernel, n_a=n_a, n_lat=n_lat, tm=tm, gate_idx=gate_idx),
        grid=(d // tn, t // tm),
        in_specs=in_specs,
        out_specs=pl.BlockSpec((tm, tn), lambda j, i: (i, j)),
        out_shape=jax.ShapeDtypeStruct((t, d), F32),
        input_output_aliases={n_a + 1: 0},
        compiler_params=_cparams(("arbitrary", "arbitrary")),
        name=name,
    )(*a_list, w, x, mods)


def _softmax_block(qs, k, v):
    s = lax.dot_general(qs, k, (((1,), (1,)), ((), ())), preferred_element_type=F32)
    m = jnp.max(s, axis=-1, keepdims=True)
    p = jnp.exp2(s - m)
    l = jnp.sum(p, axis=-1, keepdims=True)
    acc = jnp.dot(p.astype(v.dtype), v, preferred_element_type=F32)
    return m, l, acc


DIFF_Q_TILE = 512
DIFF_K_CHUNK = 2048
DIFF_PV_ROWS = 512
DIFF_SOFTMAX_ROWS = 64
DIFF_EXP_DTYPE = BF16


def _diff_attn_kernel(*refs, tq, tk, n_ctx, n_chunks, lambda_init):
    if n_chunks > 1:
        (lq1, lk1, lq2, lk2, subw, q_ref, kc_ref, vc_ref, k_ref, v_ref, o_ref,
         qs_scr, sa_scr, mca_scr, p_scr, m_scr, l_scr, alpha_scr, acc_scr, sb_scr, mcb_scr) = refs
        buf_b = (sb_scr, mcb_scr)
    elif n_chunks == 1:
        (lq1, lk1, lq2, lk2, subw, q_ref, kc_ref, vc_ref, k_ref, v_ref, o_ref,
         qs_scr, sa_scr, mca_scr, p_scr, m_scr, l_scr, alpha_scr, acc_scr) = refs
    else:
        (lq1, lk1, lq2, lk2, subw, q_ref, kc_ref, vc_ref, o_ref,
         qs_scr, sa_scr, mca_scr, p_scr, m_scr, l_scr, alpha_scr, acc_scr) = refs
    buf_a = (sa_scr, mca_scr)
    nt = (((1,), (1,)), ((), ()))
    rows_all = 2 * tq
    pv_rows = min(DIFF_PV_ROWS, rows_all)
    sm_rows = min(DIFF_SOFTMAX_ROWS, pv_rows)
    q = q_ref[...]
    comp1 = lax.broadcasted_iota(jnp.int32, (1, HEAD_W), 1) < A_QK_DIM
    zero = jnp.zeros_like(q)
    qs_scr[0:tq, :] = jnp.where(comp1, q, zero)
    qs_scr[tq:2 * tq, :] = jnp.where(comp1, zero, q)

    def scores(buf, rows, keys):
        s_buf, mc_buf = buf
        s = lax.dot_general(qs_scr[rows, :], keys, nt, preferred_element_type=F32)
        width = s.shape[1]
        s_buf[rows, 0:width] = s
        mp = s[:, 0:HEAD_W]
        for c in range(1, width // HEAD_W):
            mp = jnp.maximum(mp, s[:, c * HEAD_W:(c + 1) * HEAD_W])
        mc_buf[rows, :] = jnp.broadcast_to(jnp.max(mp, axis=-1, keepdims=True), mp.shape)

    def softmax_rows(buf, rows, n_tiles, first):
        s_buf, mc_buf = buf
        m_cur = mc_buf[rows, :]
        if first:
            m_new = m_cur
        else:
            m_prev = m_scr[rows, :]
            m_new = jnp.maximum(m_prev, m_cur)
        lp = None
        for c in range(n_tiles):
            pc = jnp.exp2((s_buf[rows, c * HEAD_W:(c + 1) * HEAD_W] - m_new).astype(DIFF_EXP_DTYPE))
            p_scr[rows, c * HEAD_W:(c + 1) * HEAD_W] = pc.astype(BF16)
            lp = pc if lp is None else lp + pc
        l_cur = jnp.sum(lp.astype(F32), axis=-1, keepdims=True)
        if first:
            l_scr[rows, :] = jnp.broadcast_to(l_cur, m_new.shape)
        else:
            alpha = jnp.exp2(m_prev - m_new)
            alpha_scr[rows, :] = alpha
            l_scr[rows, :] = alpha * l_scr[rows, :] + l_cur
        m_scr[rows, :] = m_new

    def softmax_pv(buf, width, load_v, first, after_group=None):
        for r in range(rows_all // pv_rows):
            for r2 in range(pv_rows // sm_rows):
                r0 = r * pv_rows + r2 * sm_rows
                softmax_rows(buf, slice(r0, r0 + sm_rows), width // HEAD_W, first)
            rows = slice(r * pv_rows, (r + 1) * pv_rows)
            pv = jnp.dot(p_scr[rows, 0:width], load_v(), preferred_element_type=F32)
            acc_scr[rows, :] = pv if first else alpha_scr[rows, :] * acc_scr[rows, :] + pv
            if after_group is not None:
                after_group(rows)

    def all_groups(fn):
        for r in range(rows_all // pv_rows):
            fn(slice(r * pv_rows, (r + 1) * pv_rows))

    def qk_rows(j, buf, rows):
        off = pl.multiple_of(j * tk, tk)
        scores(buf, rows, k_ref[pl.ds(off, tk), :])

    all_groups(lambda rows: scores(buf_a, rows, kc_ref[...]))
    softmax_pv(buf_a, n_ctx, lambda: vc_ref[...], True,
               (lambda rows: qk_rows(0, buf_a, rows)) if n_chunks else None)

    def soft_pv(j, buf, j_next=None, buf_next=None):
        off = pl.multiple_of(j * tk, tk)
        after = None if j_next is None else (lambda rows: qk_rows(j_next, buf_next, rows))
        softmax_pv(buf, tk, lambda: v_ref[pl.ds(off, tk), :], False, after)

    if n_chunks == 1:
        soft_pv(0, buf_a)
    elif n_chunks > 1:

        def pair(j2, carry):
            soft_pv(2 * j2, buf_a, 2 * j2 + 1, buf_b)
            soft_pv(2 * j2 + 1, buf_b, 2 * j2 + 2, buf_a)
            return carry

        lax.fori_loop(0, n_chunks // 2 - 1, pair, 0)
        soft_pv(n_chunks - 2, buf_a, n_chunks - 1, buf_b)
        soft_pv(n_chunks - 1, buf_b)

    lam = (jnp.exp(jnp.sum(lq1[...] * lk1[...], axis=-1, keepdims=True))
           - jnp.exp(jnp.sum(lq2[...] * lk2[...], axis=-1, keepdims=True)) + lambda_init)
    o1 = acc_scr[0:tq, :] / l_scr[0:tq, :]
    o2 = acc_scr[tq:2 * tq, :] / l_scr[tq:2 * tq, :]
    o = o1 - lam * o2
    y = o * lax.rsqrt(jnp.mean(o * o, axis=-1, keepdims=True) + EPS) * subw[...]
    o_ref[...] = (y * (1.0 - lambda_init)).astype(o_ref.dtype)


def _diff_attn(qkv, lq1, lk1, lq2, lk2, subw, n_lat, n_ctx, lambda_init, prev=None):
    t = qkv.shape[0]
    use_lat = prev is None
    ctx_blk = n_lat // n_ctx
    small = [lq1.reshape(1, -1), lk1.reshape(1, -1), lq2.reshape(1, -1), lk2.reshape(1, -1), subw.reshape(1, -1)]
    small_specs = [pl.BlockSpec(a.shape, lambda h, qi: (0, 0)) for a in small]
    kc_spec = pl.BlockSpec((n_ctx, HEAD_W), lambda h, qi: (ctx_blk, A_HEADS + h))
    vc_spec = pl.BlockSpec((n_ctx, HEAD_W), lambda h, qi: (ctx_blk, 2 * A_HEADS + h))
    if use_lat:
        tq = _row_tile(n_lat, tuple(c for c in (512, 256, 128) if c <= DIFF_Q_TILE))
        tk = _row_tile(n_lat, tuple(c for c in (2048, 1024, 512, 256, 128) if c <= DIFF_K_CHUNK))
        n_chunks = n_lat // tk
        assert n_chunks == 1 or n_chunks % 2 == 0
        grid = (A_HEADS, n_lat // tq)
        in_specs = small_specs + [
            pl.BlockSpec((tq, HEAD_W), lambda h, qi: (qi, h)), kc_spec, vc_spec,
            pl.BlockSpec((n_lat, HEAD_W), lambda h, qi: (0, A_HEADS + h)),
            pl.BlockSpec((n_lat, HEAD_W), lambda h, qi: (0, 2 * A_HEADS + h))]
        args = small + [qkv, qkv, qkv, qkv, qkv]
        out_spec = pl.BlockSpec((tq, HEAD_W), lambda h, qi: (qi, h))
        aliases = {}
    else:
        tq, tk, n_chunks = n_ctx, 0, 0
        grid = (A_HEADS, 1)
        in_specs = small_specs + [
            pl.BlockSpec((tq, HEAD_W), lambda h, qi: (ctx_blk, h)), kc_spec, vc_spec,
            pl.BlockSpec(memory_space=pl.ANY)]
        args = small + [qkv, qkv, qkv, prev]
        out_spec = pl.BlockSpec((tq, HEAD_W), lambda h, qi: (ctx_blk, h))
        aliases = {len(args) - 1: 0}

    sw = max(tk, n_ctx)
    kern = functools.partial(_diff_attn_kernel, tq=tq, tk=tk, n_ctx=n_ctx, n_chunks=n_chunks,
                             lambda_init=lambda_init)
    if not use_lat:
        inner = kern

        def kern(*refs):
            n_in = len(args)
            inner(*refs[:n_in - 1], *refs[n_in:])

    return pl.pallas_call(
        kern,
        grid=grid,
        in_specs=in_specs,
        out_specs=out_spec,
        out_shape=jax.ShapeDtypeStruct((t, A_HEADS * HEAD_W), BF16),
        scratch_shapes=[pltpu.VMEM((2 * tq, HEAD_W), BF16),
                        pltpu.VMEM((2 * tq, sw), F32),
                        pltpu.VMEM((2 * tq, HEAD_W), F32),
                        pltpu.VMEM((2 * tq, sw), BF16),
                        pltpu.VMEM((2 * tq, HEAD_W), F32),
                        pltpu.VMEM((2 * tq, HEAD_W), F32),
                        pltpu.VMEM((2 * tq, HEAD_W), F32),
                        pltpu.VMEM((2 * tq, HEAD_W), F32)]
        + ([pltpu.VMEM((2 * tq, sw), F32),
            pltpu.VMEM((2 * tq, HEAD_W), F32)] if n_chunks > 1 else []),
        input_output_aliases=aliases,
        compiler_params=_cparams(("arbitrary", "arbitrary")),
        name="diff_attn" if use_lat else "diff_attn_ctx",
    )(*args)


def _na_bias_tables(rpb, rows):
    nblk = rows // NA_ROWS
    wr = min(WIN_R, rows)
    nh = rpb.shape[0]
    rq = jnp.arange(NA_ROWS)[:, None, None]
    slot = jnp.arange(3)[None, :, None]
    rk = jnp.arange(NA_ROWS)[None, None, :]
    row_sel, row_ok = [], []
    for b, dup in ((0, 0), (1, -1), (nblk - 1, 2)):
        r = NA_ROWS * b + rq
        r0 = jnp.clip(r - wr // 2, 0, rows - wr)
        rkey = NA_ROWS * (b - 1 + slot) + rk
        ok = (rkey >= r0) & (rkey < r0 + wr) & (slot != dup)
        drow = jnp.broadcast_to(rkey - r + (WIN_R - 1), ok.shape)
        row_sel.append((drow[..., None] == jnp.arange(2 * WIN_R - 1)) & ok[..., None])
        row_ok.append(ok)
    row_sel = jnp.stack(row_sel).astype(F32)
    row_ok = jnp.stack(row_ok)
    jq = jnp.arange(GRID_W)[:, None]
    jk = jnp.arange(GRID_W)[None, :]
    c0 = jnp.clip(jq - WIN_C // 2, 0, GRID_W - WIN_C)
    col_ok = (jk >= c0) & (jk < c0 + WIN_C)
    dcol = jnp.clip(jk - jq + (WIN_C - 1), 0, 2 * WIN_C - 2)
    col_sel = (dcol[..., None] == jnp.arange(2 * WIN_C - 1)).astype(F32)
    bias = jnp.einsum('vasbr,hrc,qkc->hvaqsbk', row_sel, rpb.astype(F32), col_sel,
                      precision=lax.Precision.HIGHEST)
    valid = row_ok[None, :, :, None, :, :, None] & col_ok[None, None, None, :, None, None, :]
    tab = jnp.where(valid, bias * math.log2(math.e), NEG_INF)
    return tab.reshape(nh, 3, NA_ROWS * GRID_W, 3 * NA_ROWS * GRID_W)


def _natten_kernel(tab_ref, q_ref, k0_ref, k1_ref, k2_ref, v0_ref, v1_ref, v2_ref, kc_ref, vc_ref, o_ref):
    nt = (((1,), (1,)), ((), ()))
    blk = q_ref.shape[0]
    for hh in range(NA_HEADS_PER_STEP):
        hs = slice(hh * HEAD_W, (hh + 1) * HEAD_W)
        q = q_ref[:, hs]
        s_loc = [lax.dot_general(q, k_ref[:, hs], nt, preferred_element_type=F32)
                 + tab_ref[hh, :, i * blk:(i + 1) * blk] for i, k_ref in enumerate((k0_ref, k1_ref, k2_ref))]
        s_ctx = lax.dot_general(q, kc_ref[:, hs], nt, preferred_element_type=F32)
        m = jnp.max(s_ctx, axis=-1, keepdims=True)
        for s in s_loc:
            m = jnp.maximum(m, jnp.max(s, axis=-1, keepdims=True))
        l = None
        acc = None
        for s, v_ref in zip([s_ctx] + s_loc, (vc_ref, v0_ref, v1_ref, v2_ref)):
            p = jnp.exp2((s - m).astype(BF16))
            pt = p[:, 0:HEAD_W]
            for c in range(1, p.shape[1] // HEAD_W):
                pt = pt + p[:, c * HEAD_W:(c + 1) * HEAD_W]
            lp = jnp.sum(pt.astype(F32), axis=-1, keepdims=True)
            pv = jnp.dot(p, v_ref[:, hs], preferred_element_type=F32)
            l = lp if l is None else l + lp
            acc = pv if acc is None else acc + pv
        o_ref[:, hs] = (acc / l).astype(o_ref.dtype)


def _natten(qkv, tabs, n_lat, n_ctx):
    t = qkv.shape[0]
    blk = NA_ROWS * GRID_W
    nblk = n_lat // blk
    assert nblk >= 3 and blk == n_ctx
    nh = NA_HEADS_PER_STEP
    hw = nh * HEAD_W
    qo, ko, vo = (3 * A_HEADS) // nh, (3 * A_HEADS + B_HEADS) // nh, (3 * A_HEADS + 2 * B_HEADS) // nh
    ctx_blk = n_lat // n_ctx

    def kv_spec(off, shift):
        return pl.BlockSpec((blk, hw), lambda h, b: (jnp.clip(b + shift, 0, nblk - 1), off + h))

    return pl.pallas_call(
        _natten_kernel,
        grid=(B_HEADS // nh, nblk),
        in_specs=[pl.BlockSpec((nh, None, blk, 3 * blk),
                               lambda h, b: (h, jnp.where(b == 0, 0, jnp.where(b == nblk - 1, 2, 1)), 0, 0)),
                  pl.BlockSpec((blk, hw), lambda h, b: (b, qo + h)),
                  kv_spec(ko, -1), kv_spec(ko, 0), kv_spec(ko, 1),
                  kv_spec(vo, -1), kv_spec(vo, 0), kv_spec(vo, 1),
                  pl.BlockSpec((n_ctx, hw), lambda h, b: (ctx_blk, ko + h)),
                  pl.BlockSpec((n_ctx, hw), lambda h, b: (ctx_blk, vo + h))],
        out_specs=pl.BlockSpec((blk, hw), lambda h, b: (b, h)),
        out_shape=jax.ShapeDtypeStruct((t, B_HEADS * HEAD_W), BF16),
        compiler_params=_cparams(("arbitrary", "arbitrary")),
        name="natten",
    )(tabs, qkv, qkv, qkv, qkv, qkv, qkv, qkv, qkv, qkv)


def _ctx_attn_kernel(q_ref, k_ref, v_ref, prev_ref, o_ref):
    del prev_ref
    _, l, acc = _softmax_block(q_ref[...], k_ref[...], v_ref[...])
    o_ref[...] = (acc / l).astype(o_ref.dtype)


def _ctx_attn(qkv, prev, n_lat, n_ctx):
    qo, ko, vo = 3 * A_HEADS, 3 * A_HEADS + B_HEADS, 3 * A_HEADS + 2 * B_HEADS
    ctx_blk = n_lat // n_ctx
    return pl.pallas_call(
        _ctx_attn_kernel,
        grid=(B_HEADS,),
        in_specs=[pl.BlockSpec((n_ctx, HEAD_W), lambda h: (ctx_blk, qo + h)),
                  pl.BlockSpec((n_ctx, HEAD_W), lambda h: (ctx_blk, ko + h)),
                  pl.BlockSpec((n_ctx, HEAD_W), lambda h: (ctx_blk, vo + h)),
                  pl.BlockSpec(memory_space=pl.ANY)],
        out_specs=pl.BlockSpec((n_ctx, HEAD_W), lambda h: (ctx_blk, h)),
        out_shape=jax.ShapeDtypeStruct(prev.shape, prev.dtype),
        input_output_aliases={3: 0},
        compiler_params=_cparams(("arbitrary",)),
        name="ctx_attn",
    )(qkv, qkv, qkv, prev)


def _s5_tables(lam_re, lam_im, b_re, b_im, c_re, c_im, log_step, d_skip):
    hp = lax.Precision.HIGHEST
    ln = S5_CHUNK
    g, p, c = S5_GROUPS, S5_STATE, S5_GROUP
    lr, li = lam_re.astype(F32), lam_im.astype(F32)
    dt = jnp.exp(log_step.astype(F32))[:, :, None]
    mag = jnp.exp(lr * dt)
    ar, ai = mag * jnp.cos(li * dt), mag * jnp.sin(li * dt)
    den = lr * lr + li * li
    nr, ni = ar - 1.0, ai
    fr = (nr * lr + ni * li) / den
    fi = (ni * lr - nr * li) / den
    br_, bi_ = b_re.astype(F32), b_im.astype(F32)
    bbr = fr[..., None] * br_ - fi[..., None] * bi_
    bbi = fr[..., None] * bi_ + fi[..., None] * br_
    lag = jnp.arange(ln + 1, dtype=F32)[:, None, None, None]
    magl = jnp.exp(lr * dt * lag)
    pr, pi_ = magl * jnp.cos(li * dt * lag), magl * jnp.sin(li * dt * lag)
    wr = pr[..., None] * bbr - pi_[..., None] * bbi
    wi = pr[..., None] * bbi + pi_[..., None] * bbr
    cr, ci = c_re.astype(F32), c_im.astype(F32)
    kern = (jnp.einsum('dgcp,ldgpe->ldgce', cr, wr[:ln], precision=hp)
            - jnp.einsum('dgcp,ldgpe->ldgce', ci, wi[:ln], precision=hp))
    lag = np.arange(ln)[:, None, None]
    s_idx = np.arange(ln)[None, :, None]
    t_idx = np.arange(ln)[None, None, :]
    sel_f = jnp.asarray(t_idx - s_idx == lag, F32)
    sel_b = jnp.asarray(s_idx - t_idx == lag, F32)
    kfb = (jnp.einsum('lst,lgce->stgce', sel_f, kern[:, 0], precision=hp)
           + jnp.einsum('lst,lgce->stgce', sel_b, kern[:, 1], precision=hp))
    m_tab = kfb.transpose(2, 0, 4, 1, 3).reshape(g, ln * c, ln * c)
    wf_r, wf_i = wr[:ln, 0][::-1], wi[:ln, 0][::-1]
    wb_r, wb_i = wr[:ln, 1], wi[:ln, 1]

    def inj(w):
        return w.transpose(1, 0, 3, 2).reshape(g, ln * c, p)

    b_tab = jnp.concatenate([inj(wf_r), inj(wf_i), inj(wb_r), inj(wb_i)], axis=-1)
    pf_r, pf_i = pr[1:ln + 1, 0], pi_[1:ln + 1, 0]
    pb_r, pb_i = pr[1:ln + 1, 1][::-1], pi_[1:ln + 1, 1][::-1]

    def rd(pw_r, pw_i, cre, cim):
        car = cre[None] * pw_r[:, :, None, :] - cim[None] * pw_i[:, :, None, :]
        cai = cre[None] * pw_i[:, :, None, :] + cim[None] * pw_r[:, :, None, :]
        to_rows = lambda a: a.transpose(1, 3, 0, 2).reshape(g, p, ln * c)
        return to_rows(car), to_rows(-cai)

    c_tab = jnp.concatenate(rd(pf_r, pf_i, cr[0], ci[0]) + rd(pb_r, pb_i, cr[1], ci[1]), axis=1)
    a_chunk = jnp.stack([jnp.stack([pr[ln, 0], pi_[ln, 0]]), jnp.stack([pr[ln, 1], pi_[ln, 1]])])
    d_tab = jnp.tile(d_skip.astype(F32).reshape(g, 1, c), (1, ln, 1)).reshape(g, 1, ln * c)
    return m_tab.astype(BF16), b_tab.astype(BF16), c_tab.astype(BF16), a_chunk, d_tab


def _s5_inject_kernel(u_ref, b_ref, o_ref):
    o_ref[...] = jnp.dot(u_ref[...], b_ref[...], preferred_element_type=F32)


def _s5_scan_kernel(a_ref, e_ref, o_ref, s_scr, *, nb):
    d = pl.program_id(0)

    @pl.when(pl.program_id(1) == 0)
    def _():
        s_scr[...] = jnp.zeros_like(s_scr)

    ar, ai = a_ref[0], a_ref[1]

    def body(i, carry):
        sr, si = carry
        c = jnp.where(d == 0, i, nb - 1 - i)
        o_ref[0, c] = sr
        o_ref[1, c] = si
        return ar * sr - ai * si + e_ref[0, c], ar * si + ai * sr + e_ref[1, c]

    sr, si = lax.fori_loop(0, nb, body, (s_scr[0], s_scr[1]))
    s_scr[0] = sr
    s_scr[1] = si


def _gelu_tanh(x):
    return 0.5 * x * (1.0 + jnp.tanh(math.sqrt(2.0 / math.pi) * (x + 0.044715 * (x * x * x))))


def _s5_readout_kernel(u_ref, m_ref, s_ref, c_ref, d_ref, o_ref):
    u = u_ref[...]
    y = jnp.dot(u, m_ref[...], preferred_element_type=F32) + u.astype(F32) * d_ref[...]
    for k in range(4):
        y = y + jnp.dot(s_ref[k].astype(BF16), c_ref[k * S5_STATE:(k + 1) * S5_STATE, :],
                        preferred_element_type=F32)
    o_ref[...] = _gelu_tanh(y).astype(o_ref.dtype)


def _s5_glu_kernel(z_ref, w_ref, o_ref):
    z = z_ref[...]
    gate = jax.nn.sigmoid(jnp.dot(z.astype(BF16), w_ref[...], preferred_element_type=F32))
    o_ref[...] = (z.astype(F32) * gate).astype(o_ref.dtype)


def _s5_mixer(u, tables, w_glu, n_lat, n_ctx):
    m_tab, b_tab, c_tab, a_chunk, d_tab = tables
    t = u.shape[0]
    g, c, ln, p = S5_GROUPS, S5_GROUP, S5_CHUNK, S5_STATE
    nc = t // ln
    w = ln * c
    ug = u.reshape(nc, ln, g, c).transpose(2, 0, 1, 3).reshape(g, nc, w)

    e = pl.pallas_call(
        _s5_inject_kernel,
        grid=(g,),
        in_specs=[pl.BlockSpec((None, nc, w), lambda i: (i, 0, 0)),
                  pl.BlockSpec((None, w, 4 * p), lambda i: (i, 0, 0))],
        out_specs=pl.BlockSpec((None, nc, 4 * p), lambda i: (i, 0, 0)),
        out_shape=jax.ShapeDtypeStruct((g, nc, 4 * p), F32),
        compiler_params=_cparams(("arbitrary",)),
        name="s5_inject",
    )(ug, b_tab)
    e = e.reshape(g, nc, 2, 2, p).transpose(2, 3, 1, 0, 4)

    nb = S5_BLOCK
    assert (n_lat // ln) % nb == 0 and n_ctx // ln == nb
    lat_blocks = n_lat // ln // nb

    def blk(d, s):
        return jnp.where(s == 0, lat_blocks, jnp.where(d == 0, s - 1, lat_blocks - s))

    s_in = pl.pallas_call(
        functools.partial(_s5_scan_kernel, nb=nb),
        grid=(2, lat_blocks + 1),
        in_specs=[pl.BlockSpec((None, 2, g, p), lambda d, s: (d, 0, 0, 0)),
                  pl.BlockSpec((None, 2, nb, g, p), lambda d, s: (d, 0, blk(d, s), 0, 0))],
        out_specs=pl.BlockSpec((None, 2, nb, g, p), lambda d, s: (d, 0, blk(d, s), 0, 0)),
        out_shape=jax.ShapeDtypeStruct((2, 2, nc, g, p), F32),
        scratch_shapes=[pltpu.VMEM((2, g, p), F32)],
        compiler_params=_cparams(("arbitrary", "arbitrary")),
        name="s5_scan",
    )(a_chunk, e)
    s_in = s_in.transpose(3, 0, 1, 2, 4).reshape(g, 4, nc, p)

    z = pl.pallas_call(
        _s5_readout_kernel,
        grid=(g,),
        in_specs=[pl.BlockSpec((None, nc, w), lambda i: (i, 0, 0)),
                  pl.BlockSpec((None, w, w), lambda i: (i, 0, 0)),
                  pl.BlockSpec((None, 4, nc, p), lambda i: (i, 0, 0, 0)),
                  pl.BlockSpec((None, 4 * p, w), lambda i: (i, 0, 0)),
                  pl.BlockSpec((None, 1, w), lambda i: (i, 0, 0))],
        out_specs=pl.BlockSpec((None, nc, w), lambda i: (i, 0, 0)),
        out_shape=jax.ShapeDtypeStruct((g, nc, w), BF16),
        compiler_params=_cparams(("arbitrary",)),
        name="s5_readout",
    )(ug, m_tab, s_in, c_tab, d_tab)
    z = z.reshape(g, nc, ln, c).transpose(1, 2, 0, 3).reshape(t, g * c)

    tm = _row_tile(t, (1280, 640, 256, 128))
    return pl.pallas_call(
        _s5_glu_kernel,
        grid=(t // tm,),
        in_specs=[pl.BlockSpec((tm, g * c), lambda i: (i, 0)),
                  pl.BlockSpec((g * c, g * c), lambda i: (0, 0))],
        out_specs=pl.BlockSpec((tm, g * c), lambda i: (i, 0)),
        out_shape=jax.ShapeDtypeStruct((t, g * c), BF16),
        compiler_params=_cparams(("arbitrary",)),
        name="s5_glu",
    )(z, w_glu)


def _ret_tables(decay_logit):
    scale = HEAD_W ** -0.5
    lg = jax.nn.log_sigmoid(decay_logit.astype(F32))
    lf, lb = lg[0][:, None, None], lg[1][:, None, None]
    i = jnp.arange(RET_CHUNK, dtype=F32)[None, :, None]
    j = jnp.arange(RET_CHUNK, dtype=F32)[None, None, :]
    diff = i - j
    intra = (jnp.where(diff >= 0, jnp.exp(lf * jnp.maximum(diff, 0.0)), 0.0)
             + jnp.where(diff <= 0, jnp.exp(lb * jnp.maximum(-diff, 0.0)), 0.0)) * scale
    ones = jnp.ones((1, 1, HEAD_W), F32)
    q_f = jnp.exp(lf * (i + 1.0)) * ones
    k_f = jnp.exp(lf * (RET_CHUNK - 1.0 - i)) * scale * ones
    c_f = jnp.exp(lf * RET_CHUNK) * jnp.ones((1, RET_CHUNK, HEAD_W), F32)
    q_b = jnp.exp(lb * (RET_CHUNK - i)) * ones
    k_b = jnp.exp(lb * i) * scale * ones
    c_b = jnp.exp(lb * RET_CHUNK) * jnp.ones((1, RET_CHUNK, HEAD_W), F32)
    return intra, jnp.stack([q_f, k_f, c_f]), jnp.stack([q_b, k_b, c_b])


def _ret_state_step(q, k, v, dec_ref, s_scr, h):
    s = s_scr[h]
    qd = (q.astype(F32) * dec_ref[0, h]).astype(BF16)
    kd = (k.astype(F32) * dec_ref[1, h]).astype(BF16)
    o = jnp.dot(qd, s.astype(BF16), preferred_element_type=F32)
    s_scr[h] = s * dec_ref[2, h] + lax.dot_general(kd, v, (((0,), (0,)), ((), ())),
                                                    preferred_element_type=F32)
    return o


def _ret_fwd_kernel(intra_ref, dec_ref, q_ref, k_ref, v_ref, o_ref, s_scr):
    @pl.when(pl.program_id(0) == 0)
    def _():
        s_scr[...] = jnp.zeros_like(s_scr)

    for c in range(RET_STEP_CHUNKS):
        rows = slice(c * RET_CHUNK, (c + 1) * RET_CHUNK)
        for h in range(RET_HEADS):
            sl = slice(h * HEAD_W, (h + 1) * HEAD_W)
            q, k, v = q_ref[rows, sl], k_ref[rows, sl], v_ref[rows, sl]
            att = lax.dot_general(q, k, (((1,), (1,)), ((), ())), preferred_element_type=F32) * intra_ref[h]
            o = jnp.dot(att.astype(BF16), v, preferred_element_type=F32)
            o_ref[rows, sl] = o + _ret_state_step(q, k, v, dec_ref, s_scr, h)


def _ret_bwd_kernel(dec_ref, q_ref, k_ref, v_ref, g_ref, o1_ref, o_ref, s_scr):
    @pl.when(pl.program_id(0) == 0)
    def _():
        s_scr[...] = jnp.zeros_like(s_scr)

    for c in reversed(range(RET_STEP_CHUNKS)):
        rows = slice(c * RET_CHUNK, (c + 1) * RET_CHUNK)
        for h in range(RET_HEADS):
            sl = slice(h * HEAD_W, (h + 1) * HEAD_W)
            q, k, v = q_ref[rows, sl], k_ref[rows, sl], v_ref[rows, sl]
            o = o1_ref[rows, sl] + _ret_state_step(q, k, v, dec_ref, s_scr, h)
            y = o * lax.rsqrt(jnp.mean(o * o, axis=-1, keepdims=True) + EPS)
            gate = g_ref[rows, sl].astype(F32)
            o_ref[rows, sl] = (y * (gate * jax.nn.sigmoid(gate))).astype(o_ref.dtype)


def _retention(proj, tables, n_lat, n_ctx):
    intra, dec_f, dec_b = tables
    t = proj.shape[0]
    wd = RET_HEADS * HEAD_W
    ck = RET_CHUNK
    bt = RET_STEP_CHUNKS * ck
    assert n_lat % bt == 0 and n_ctx % bt == 0
    n_lat_c, n_ctx_c = n_lat // bt, n_ctx // bt
    steps = n_lat_c + n_ctx_c

    def fwd_blk(s):
        return jnp.where(s < n_ctx_c, n_lat_c + s, s - n_ctx_c)

    def bwd_blk(s):
        return jnp.where(s < n_ctx_c, n_lat_c + n_ctx_c - 1 - s, n_lat_c - 1 - (s - n_ctx_c))

    def col_spec(col, order):
        return pl.BlockSpec((bt, wd), lambda s: (order(s), col))

    tab_spec = pl.BlockSpec((3, RET_HEADS, ck, HEAD_W), lambda s: (0, 0, 0, 0))
    o1 = pl.pallas_call(
        _ret_fwd_kernel,
        grid=(steps,),
        in_specs=[pl.BlockSpec((RET_HEADS, ck, ck), lambda s: (0, 0, 0)), tab_spec,
                  col_spec(0, fwd_blk), col_spec(1, fwd_blk), col_spec(2, fwd_blk)],
        out_specs=col_spec(0, fwd_blk),
        out_shape=jax.ShapeDtypeStruct((t, wd), F32),
        scratch_shapes=[pltpu.VMEM((RET_HEADS, HEAD_W, HEAD_W), F32)],
        compiler_params=_cparams(("arbitrary",)),
        name="retention_fwd",
    )(intra, dec_f, proj, proj, proj)
    return pl.pallas_call(
        _ret_bwd_kernel,
        grid=(steps,),
        in_specs=[tab_spec, col_spec(0, bwd_blk), col_spec(1, bwd_blk), col_spec(2, bwd_blk),
                  col_spec(3, bwd_blk), col_spec(0, bwd_blk)],
        out_specs=col_spec(0, bwd_blk),
        out_shape=jax.ShapeDtypeStruct((t, wd), BF16),
        scratch_shapes=[pltpu.VMEM((RET_HEADS, HEAD_W, HEAD_W), F32)],
        compiler_params=_cparams(("arbitrary",)),
        name="retention_bwd",
    )(dec_b, proj, proj, proj, proj, o1)


def _rope_tables(n_lat, n_ctx):
    n_freq = A_QK_DIM // 4
    freq = ROPE_BASE ** (-jnp.arange(n_freq, dtype=F32) / n_freq)
    rows = n_lat // GRID_W
    ang_row = jnp.arange(rows, dtype=F32)[:, None] * freq
    ang_col = jnp.arange(GRID_W, dtype=F32)[:, None] * freq
    reps = HEAD_W // A_QK_DIM

    def table(fn, ctx_value):
        r = jnp.broadcast_to(jnp.tile(fn(ang_row), (1, 2))[:, None, :], (rows, GRID_W, 2 * n_freq))
        c = jnp.broadcast_to(jnp.tile(fn(ang_col), (1, 2))[None, :, :], (rows, GRID_W, 2 * n_freq))
        lat = jnp.tile(jnp.concatenate([r, c], axis=-1).reshape(n_lat, A_QK_DIM), (1, reps))
        return jnp.concatenate([lat, jnp.full((n_ctx, HEAD_W), ctx_value, F32)], axis=0)

    return table(jnp.cos, 1.0), table(jnp.sin, 0.0)


def kernel(x, c, ctx, c_ctx, ada_w, ada_b, norm1_w, norm2_w, ffn_w13, ffn_w2, e_w_in, e_w_out, diff_lq1, diff_lk1, diff_lq2, diff_lk2, diff_subln_w, na_rpb, o_w_in, o_w_out, s5_lam_re, s5_lam_im, s5_b_re, s5_b_im, s5_c_re, s5_c_im, s5_log_step, s5_d, s5_w_glu, ret_decay_logit, final_norm_w):
    bsz, n_lat, d = x.shape
    n_ctx = ctx.shape[1]
    depth = ada_w.shape[0]
    assert bsz == 1
    xs = jnp.concatenate([x[0], ctx[0]], axis=0)
    mods_all = _mods(c, c_ctx, ada_w, ada_b)
    cos, sin = _rope_tables(n_lat, n_ctx)
    ret_w = RET_HEADS * HEAD_W
    ffn_w13_bf16 = ffn_w13.astype(BF16)
    e_w_in_bf16 = e_w_in.astype(BF16)

    for i in range(depth):
        compute_ctx = i != depth - 1
        mods = mods_all[i]
        j = i // 2
        if i % 2 == 0:
            lambda_init = 0.8 - 0.6 * math.exp(-0.3 * i)
            qkv = _inproj_even(xs, norm1_w[i], mods, n_lat, e_w_in_bf16, j, cos, sin)
            lam_args = (diff_lq1[j], diff_lk1[j], diff_lq2[j], diff_lk2[j], diff_subln_w[j])
            o_a = _diff_attn(qkv, *lam_args, n_lat, n_ctx, lambda_init)
            o_b = _natten(qkv, _na_bias_tables(na_rpb[j], n_lat // GRID_W), n_lat, n_ctx)
            if compute_ctx:
                o_a = _diff_attn(qkv, *lam_args, n_lat, n_ctx, lambda_init, prev=o_a)
                o_b = _ctx_attn(qkv, o_b, n_lat, n_ctx)
            xs = _gated_residual([o_a, o_b], e_w_out[j].astype(BF16), xs, mods, n_lat, 2, "outproj_even")
        else:
            w_in = o_w_in[j]
            w_in = jnp.concatenate([w_in[:, S5_CH:], w_in[:, :S5_CH]], axis=1).astype(BF16)
            proj = _inproj_odd(xs, norm1_w[i], mods, n_lat, w_in)
            s5_tabs = _s5_tables(s5_lam_re[j], s5_lam_im[j], s5_b_re[j], s5_b_im[j], s5_c_re[j], s5_c_im[j],
                                 s5_log_step[j], s5_d[j])
            y_c = _s5_mixer(proj[:, 4 * ret_w:], s5_tabs, s5_w_glu[j].astype(BF16), n_lat, n_ctx)
            y_d = _retention(proj, _ret_tables(ret_decay_logit[j]), n_lat, n_ctx)
            xs = _gated_residual([y_c, y_d], o_w_out[j].astype(BF16), xs, mods, n_lat, 2, "outproj_odd")
        a = _ffn_up(xs, norm2_w[i], mods, n_lat, ffn_w13_bf16, i)
        xs = _gated_residual([a], ffn_w2[i].astype(BF16), xs, mods, n_lat, 5, "ffn_down")
    return _final_norm(xs, final_norm_w, n_lat)[None]
```

```python
import functools
import math

import jax
import jax.numpy as jnp
import numpy as np
from jax import lax
from jax.experimental import pallas as pl
from jax.experimental.pallas import tpu as pltpu

F32 = jnp.float32
BF16 = jnp.bfloat16

GRID_W = 64
A_HEADS = 8
A_QK_DIM = 64
HEAD_W = 128
B_HEADS = 8
WIN_R = 8
WIN_C = 16
NA_ROWS = 4
NA_HEADS_PER_STEP = 8
S5_CH = 512
S5_GROUP = 16
S5_GROUPS = 32
S5_STATE = 64
S5_CHUNK = 16
S5_BLOCK = 16
RET_HEADS = 12
RET_CHUNK = 128
RET_STEP_CHUNKS = 2
ROPE_BASE = 10000.0
EPS = 1e-6
NEG_INF = -1e30
VMEM_LIMIT = 56 * 1024 * 1024
N_MOD = 8
NORM_ROW_CHUNK = 128


def _cparams(sem):
    return pltpu.CompilerParams(dimension_semantics=sem, vmem_limit_bytes=VMEM_LIMIT)


def _row_tile(t, candidates):
    for c in candidates:
        if t % c == 0:
            return c
    raise ValueError(f"no row tile for {t}")


def _pick_mod(mods_ref, idx, is_ctx):
    return jnp.where(is_ctx, mods_ref[1, idx:idx + 1, :], mods_ref[0, idx:idx + 1, :])


def _is_ctx_rows(tm, n_lat, axis):
    row = pl.program_id(axis) * tm + lax.broadcasted_iota(jnp.int32, (tm, 1), 0)
    return row >= n_lat


def _mods_kernel(s_ref, w_ref, b_ref, o_ref):
    s = s_ref[...]
    s = s * jax.nn.sigmoid(s)
    o_ref[...] = jnp.dot(s, w_ref[...], preferred_element_type=F32,
                         precision=lax.Precision.HIGHEST) + b_ref[...]


def _mods(c, c_ctx, ada_w, ada_b):
    depth, d, w6 = ada_w.shape
    s = jnp.zeros((8, d), F32).at[0].set(c[0]).at[1].set(c_ctx)
    tn = 2048
    out = pl.pallas_call(
        _mods_kernel,
        grid=(depth, w6 // tn),
        in_specs=[pl.BlockSpec((8, d), lambda l, j: (0, 0)),
                  pl.BlockSpec((None, d, tn), lambda l, j: (l, 0, j)),
                  pl.BlockSpec((None, 1, tn), lambda l, j: (l, 0, j))],
        out_specs=pl.BlockSpec((None, 8, tn), lambda l, j: (l, 0, j)),
        out_shape=jax.ShapeDtypeStruct((depth, 8, w6), F32),
        compiler_params=_cparams(("arbitrary", "arbitrary")),
        name="ada_mods",
    )(s, ada_w, ada_b.reshape(depth, 1, w6))
    m = out[:, :2].reshape(depth, 2, 6, d)
    return jnp.pad(m, ((0, 0), (0, 0), (0, N_MOD - 6), (0, 0)))


def _normmod_prologue(x_ref, nw_ref, mods_ref, h_scr, *, n_lat, tm, shift_idx, scale_idx):
    @pl.when(pl.program_id(1) == 0)
    def _():
        rc = NORM_ROW_CHUNK
        assert tm % rc == 0 and n_lat % rc == 0

        def chunk(c, carry):
            r0 = pl.multiple_of(c * rc, rc)
            seg = (pl.program_id(0) * tm + r0 >= n_lat).astype(jnp.int32)
            gain = nw_ref[...] * (1.0 + mods_ref[seg, scale_idx:scale_idx + 1, :])
            shift = mods_ref[seg, shift_idx:shift_idx + 1, :]
            x = x_ref[pl.ds(r0, rc), :]
            r = lax.rsqrt(jnp.mean(x * x, axis=-1, keepdims=True) + EPS)
            h_scr[pl.ds(r0, rc), :] = (x * r * gain + shift).astype(h_scr.dtype)
            return carry

        lax.fori_loop(0, tm // rc, chunk, 0)

    return h_scr[...]


def _norm_proj_call(kern, x, norm_w, mods, n_lat, shift_idx, scale_idx, w_args, w_specs, extra_args, extra_specs,
                    n_out, tn, tm, name, **kern_kw):
    t, d = x.shape
    norm = dict(n_lat=n_lat, tm=tm, shift_idx=shift_idx, scale_idx=scale_idx)
    return pl.pallas_call(
        functools.partial(kern, norm=norm, **kern_kw),
        grid=(t // tm, n_out // tn),
        in_specs=[pl.BlockSpec((tm, d), lambda i, j: (i, 0)),
                  pl.BlockSpec((1, d), lambda i, j: (0, 0)),
                  pl.BlockSpec((2, N_MOD, d), lambda i, j: (0, 0, 0))] + w_specs + extra_specs,
        out_specs=pl.BlockSpec((tm, tn), lambda i, j: (i, j)),
        out_shape=jax.ShapeDtypeStruct((t, n_out), BF16),
        scratch_shapes=[pltpu.VMEM((tm, d), BF16)],
        compiler_params=_cparams(("arbitrary", "arbitrary")),
        name=name,
    )(x, norm_w.reshape(1, d), mods, *w_args, *extra_args)


def _final_norm_kernel(x_ref, w_ref, o_ref):
    x = x_ref[...]
    o_ref[...] = x * lax.rsqrt(jnp.mean(x * x, axis=-1, keepdims=True) + EPS) * w_ref[...]


def _final_norm(x, w, n_lat):
    d = x.shape[1]
    tm = _row_tile(n_lat, (512, 256, 128))
    return pl.pallas_call(
        _final_norm_kernel,
        grid=(n_lat // tm,),
        in_specs=[pl.BlockSpec((tm, d), lambda i: (i, 0)),
                  pl.BlockSpec((1, d), lambda i: (0, 0))],
        out_specs=pl.BlockSpec((tm, d), lambda i: (i, 0)),
        out_shape=jax.ShapeDtypeStruct((n_lat, d), F32),
        compiler_params=_cparams(("arbitrary",)),
        name="final_norm",
    )(x, w.reshape(1, d))


def _rope_store(acc, cos, sin, o_ref, scale):
    first_half = (lax.broadcasted_iota(jnp.int32, (1, HEAD_W), 1) % 32) < 16
    for c in range(acc.shape[1] // HEAD_W):
        x = acc[:, c * HEAD_W:(c + 1) * HEAD_W]
        rot = jnp.where(first_half, -pltpu.roll(x, HEAD_W - 16, 1), pltpu.roll(x, 16, 1))
        o_ref[:, c * HEAD_W:(c + 1) * HEAD_W] = ((x * cos + rot * sin) * scale).astype(o_ref.dtype)


def _inproj_even_kernel(x_ref, nw_ref, mods_ref, w_ref, cos_ref, sin_ref, o_ref, h_scr, *, norm, a_scale, b_scale):
    h = _normmod_prologue(x_ref, nw_ref, mods_ref, h_scr, **norm)
    j = pl.program_id(1)
    acc = jnp.dot(h, w_ref[...], preferred_element_type=F32)

    @pl.when(j == 0)
    def _():
        _rope_store(acc, cos_ref[...], sin_ref[...], o_ref, a_scale)

    @pl.when(j == 1)
    def _():
        _rope_store(acc, cos_ref[...], sin_ref[...], o_ref, 1.0)

    @pl.when(j == 3)
    def _():
        o_ref[...] = (acc * b_scale).astype(o_ref.dtype)

    @pl.when((j == 2) | (j > 3))
    def _():
        o_ref[...] = acc.astype(o_ref.dtype)


def _inproj_even(x, norm_w, mods, n_lat, w, layer, cos, sin):
    t, d = x.shape
    n = w.shape[2]
    tn = n // 6
    tm = _row_tile(t, (1280, 640, 256, 128))
    rope_spec = pl.BlockSpec((tm, HEAD_W), lambda i, j: (i, 0))
    return _norm_proj_call(
        _inproj_even_kernel, x, norm_w, mods, n_lat, 0, 1,
        [w], [pl.BlockSpec((None, d, tn), lambda i, j: (layer, 0, j))],
        [cos, sin], [rope_spec, rope_spec], n, tn, tm, "inproj_even",
        a_scale=A_QK_DIM ** -0.5 * math.log2(math.e), b_scale=HEAD_W ** -0.5 * math.log2(math.e))


def _inproj_odd_kernel(x_ref, nw_ref, mods_ref, w_ref, o_ref, h_scr, *, norm):
    h = _normmod_prologue(x_ref, nw_ref, mods_ref, h_scr, **norm)
    o_ref[...] = jnp.dot(h, w_ref[...], preferred_element_type=F32).astype(o_ref.dtype)


def _inproj_odd(x, norm_w, mods, n_lat, w):
    t, d = x.shape
    n = w.shape[1]
    tn = 512
    tm = _row_tile(t, (1280, 640, 256, 128))
    return _norm_proj_call(
        _inproj_odd_kernel, x, norm_w, mods, n_lat, 0, 1,
        [w], [pl.BlockSpec((d, tn), lambda i, j: (0, j))], [], [], n, tn, tm, "inproj_odd")


def _ffn_up_kernel(x_ref, nw_ref, mods_ref, w1_ref, w3_ref, o_ref, h_scr, *, norm):
    h = _normmod_prologue(x_ref, nw_ref, mods_ref, h_scr, **norm)
    a = jnp.dot(h, w1_ref[...], preferred_element_type=F32)
    b = jnp.dot(h, w3_ref[...], preferred_element_type=F32)
    o_ref[...] = (a * jax.nn.sigmoid(a) * b).astype(o_ref.dtype)


def _ffn_up(x, norm_w, mods, n_lat, w13, layer):
    t, d = x.shape
    d_ff = w13.shape[2] // 2
    tn = 512
    nj = d_ff // tn
    tm = _row_tile(t, (1280, 640, 256, 128))
    return _norm_proj_call(
        _ffn_up_kernel, x, norm_w, mods, n_lat, 3, 4,
        [w13, w13], [pl.BlockSpec((None, d, tn), lambda i, j: (layer, 0, j)),
                     pl.BlockSpec((None, d, tn), lambda i, j: (layer, 0, j + nj))],
        [], [], d_ff, tn, tm, "ffn_up")


def _gated_residual_kernel(*refs, n_a, n_lat, tm, gate_idx):
    a_refs = refs[:n_a]
    w_ref, x_ref, mods_ref, o_ref = refs[n_a:]
    k0 = 0
    y = None
    for a_ref in a_refs:
        kk = a_ref.shape[1]
        part = jnp.dot(a_ref[...], w_ref[k0:k0 + kk, :], preferred_element_type=F32)
        y = part if y is None else y + part
        k0 += kk
    gate = _pick_mod(mods_ref, gate_idx, _is_ctx_rows(tm, n_lat, 1))
    o_ref[...] = x_ref[...] + gate * y


def _gated_residual(a_list, w, x, mods, n_lat, gate_idx, name):
    t, d = x.shape
    k = w.shape[0]
    tm = _row_tile(t, (1280, 640, 256, 128) if k <= 2048 else (640, 256, 128))
    n_a = len(a_list)
    tn = 1024 if d % 1024 == 0 else 512
    w_spec = pl.BlockSpec((k, tn), lambda j, i: (0, j), pipeline_mode=pl.Buffered(1))
    in_specs = [pl.BlockSpec((tm, a.shape[1]), lambda j, i: (i, 0)) for a in a_list]
    in_specs += [w_spec,
                 pl.BlockSpec((tm, tn), lambda j, i: (i, j)),
                 pl.BlockSpec((2, N_MOD, tn), lambda j, i: (0, 0, j))]
    return pl.pallas_call(
        functools.partial(_gated_residual_kernel, n_a=n_a, n_lat=n_lat, tm=tm, gate_idx=gate_idx),
        grid=(d // tn, t // tm),
        in_specs=in_specs,
        out_specs=pl.BlockSpec((tm, tn), lambda j, i: (i, j)),
        out_shape=jax.ShapeDtypeStruct((t, d), F32),
        input_output_aliases={n_a + 1: 0},
        compiler_params=_cparams(("arbitrary", "arbitrary")),
        name=name,
    )(*a_list, w, x, mods)


def _softmax_block(qs, k, v):
    s = lax.dot_general(qs, k, (((1,), (1,)), ((), ())), preferred_element_type=F32)
    m = jnp.max(s, axis=-1, keepdims=True)
    p = jnp.exp2(s - m)
    l = jnp.sum(p, axis=-1, keepdims=True)
    acc = jnp.dot(p.astype(v.dtype), v, preferred_element_type=F32)
    return m, l, acc


DIFF_Q_TILE = 512
DIFF_K_CHUNK = 2048
DIFF_PV_ROWS = 512
DIFF_SOFTMAX_ROWS = 64
DIFF_EXP_DTYPE = BF16


def _diff_attn_kernel(*refs, tq, tk, n_ctx, n_chunks, lambda_init):
    if n_chunks > 1:
        (lq1, lk1, lq2, lk2, subw, q_ref, kc_ref, vc_ref, k_ref, v_ref, o_ref,
         qs_scr, sa_scr, mca_scr, p_scr, m_scr, l_scr, alpha_scr, acc_scr, sb_scr, mcb_scr) = refs
        buf_b = (sb_scr, mcb_scr)
    elif n_chunks == 1:
        (lq1, lk1, lq2, lk2, subw, q_ref, kc_ref, vc_ref, k_ref, v_ref, o_ref,
         qs_scr, sa_scr, mca_scr, p_scr, m_scr, l_scr, alpha_scr, acc_scr) = refs
    else:
        (lq1, lk1, lq2, lk2, subw, q_ref, kc_ref, vc_ref, o_ref,
         qs_scr, sa_scr, mca_scr, p_scr, m_scr, l_scr, alpha_scr, acc_scr) = refs
    buf_a = (sa_scr, mca_scr)
    nt = (((1,), (1,)), ((), ()))
    rows_all = 2 * tq
    pv_rows = min(DIFF_PV_ROWS, rows_all)
    sm_rows = min(DIFF_SOFTMAX_ROWS, pv_rows)
    q = q_ref[...]
    comp1 = lax.broadcasted_iota(jnp.int32, (1, HEAD_W), 1) < A_QK_DIM
    zero = jnp.zeros_like(q)
    qs_scr[0:tq, :] = jnp.where(comp1, q, zero)
    qs_scr[tq:2 * tq, :] = jnp.where(comp1, zero, q)

    def scores(buf, rows, keys):
        s_buf, mc_buf = buf
        s = lax.dot_general(qs_scr[rows, :], keys, nt, preferred_element_type=F32)
        width = s.shape[1]
        s_buf[rows, 0:width] = s
        mp = s[:, 0:HEAD_W]
        for c in range(1, width // HEAD_W):
            mp = jnp.maximum(mp, s[:, c * HEAD_W:(c + 1) * HEAD_W])
        mc_buf[rows, :] = jnp.broadcast_to(jnp.max(mp, axis=-1, keepdims=True), mp.shape)

    def softmax_rows(buf, rows, n_tiles, first):
        s_buf, mc_buf = buf
        m_cur = mc_buf[rows, :]
        if first:
            m_new = m_cur
        else:
            m_prev = m_scr[rows, :]
            m_new = jnp.maximum(m_prev, m_cur)
        lp = None
        for c in range(n_tiles):
            pc = jnp.exp2((s_buf[rows, c * HEAD_W:(c + 1) * HEAD_W] - m_new).astype(DIFF_EXP_DTYPE))
            p_scr[rows, c * HEAD_W:(c + 1) * HEAD_W] = pc.astype(BF16)
            lp = pc if lp is None else lp + pc
        l_cur = jnp.sum(lp.astype(F32), axis=-1, keepdims=True)
        if first:
            l_scr[rows, :] = jnp.broadcast_to(l_cur, m_new.shape)
        else:
            alpha = jnp.exp2(m_prev - m_new)
            alpha_scr[rows, :] = alpha
            l_scr[rows, :] = alpha * l_scr[rows, :] + l_cur
        m_scr[rows, :] = m_new

    def softmax_pv(buf, width, load_v, first, after_group=None):
        for r in range(rows_all // pv_rows):
            for r2 in range(pv_rows // sm_rows):
                r0 = r * pv_rows + r2 * sm_rows
                softmax_rows(buf, slice(r0, r0 + sm_rows), width // HEAD_W, first)
            rows = slice(r * pv_rows, (r + 1) * pv_rows)
            pv = jnp.dot(p_scr[rows, 0:width], load_v(), preferred_element_type=F32)
            acc_scr[rows, :] = pv if first else alpha_scr[rows, :] * acc_scr[rows, :] + pv
            if after_group is not None:
                after_group(rows)

    def all_groups(fn):
        for r in range(rows_all // pv_rows):
            fn(slice(r * pv_rows, (r + 1) * pv_rows))

    def qk_rows(j, buf, rows):
        off = pl.multiple_of(j * tk, tk)
        scores(buf, rows, k_ref[pl.ds(off, tk), :])

    all_groups(lambda rows: scores(buf_a, rows, kc_ref[...]))
    softmax_pv(buf_a, n_ctx, lambda: vc_ref[...], True,
               (lambda rows: qk_rows(0, buf_a, rows)) if n_chunks else None)

    def soft_pv(j, buf, j_next=None, buf_next=None):
        off = pl.multiple_of(j * tk, tk)
        after = None if j_next is None else (lambda rows: qk_rows(j_next, buf_next, rows))
        softmax_pv(buf, tk, lambda: v_ref[pl.ds(off, tk), :], False, after)

    if n_chunks == 1:
        soft_pv(0, buf_a)
    elif n_chunks > 1:

        def pair(j2, carry):
            soft_pv(2 * j2, buf_a, 2 * j2 + 1, buf_b)
            soft_pv(2 * j2 + 1, buf_b, 2 * j2 + 2, buf_a)
            return carry

        lax.fori_loop(0, n_chunks // 2 - 1, pair, 0)
        soft_pv(n_chunks - 2, buf_a, n_chunks - 1, buf_b)
        soft_pv(n_chunks - 1, buf_b)

    lam = (jnp.exp(jnp.sum(lq1[...] * lk1[...], axis=-1, keepdims=True))
           - jnp.exp(jnp.sum(lq2[...] * lk2[...], axis=-1, keepdims=True)) + lambda_init)
    o1 = acc_scr[0:tq, :] / l_scr[0:tq, :]
    o2 = acc_scr[tq:2 * tq, :] / l_scr[tq:2 * tq, :]
    o = o1 - lam * o2
    y = o * lax.rsqrt(jnp.mean(o * o, axis=-1, keepdims=True) + EPS) * subw[...]
    o_ref[...] = (y * (1.0 - lambda_init)).astype(o_ref.dtype)


def _diff_attn(qkv, lq1, lk1, lq2, lk2, subw, n_lat, n_ctx, lambda_init, prev=None):
    t = qkv.shape[0]
    use_lat = prev is None
    ctx_blk = n_lat // n_ctx
    small = [lq1.reshape(1, -1), lk1.reshape(1, -1), lq2.reshape(1, -1), lk2.reshape(1, -1), subw.reshape(1, -1)]
    small_specs = [pl.BlockSpec(a.shape, lambda h, qi: (0, 0)) for a in small]
    kc_spec = pl.BlockSpec((n_ctx, HEAD_W), lambda h, qi: (ctx_blk, A_HEADS + h))
    vc_spec = pl.BlockSpec((n_ctx, HEAD_W), lambda h, qi: (ctx_blk, 2 * A_HEADS + h))
    if use_lat:
        tq = _row_tile(n_lat, tuple(c for c in (512, 256, 128) if c <= DIFF_Q_TILE))
        tk = _row_tile(n_lat, tuple(c for c in (2048, 1024, 512, 256, 128) if c <= DIFF_K_CHUNK))
        n_chunks = n_lat // tk
        assert n_chunks == 1 or n_chunks % 2 == 0
        grid = (A_HEADS, n_lat // tq)
        in_specs = small_specs + [
            pl.BlockSpec((tq, HEAD_W), lambda h, qi: (qi, h)), kc_spec, vc_spec,
            pl.BlockSpec((n_lat, HEAD_W), lambda h, qi: (0, A_HEADS + h)),
            pl.BlockSpec((n_lat, HEAD_W), lambda h, qi: (0, 2 * A_HEADS + h))]
        args = small + [qkv, qkv, qkv, qkv, qkv]
        out_spec = pl.BlockSpec((tq, HEAD_W), lambda h, qi: (qi, h))
        aliases = {}
    else:
        tq, tk, n_chunks = n_ctx, 0, 0
        grid = (A_HEADS, 1)
        in_specs = small_specs + [
            pl.BlockSpec((tq, HEAD_W), lambda h, qi: (ctx_blk, h)), kc_spec, vc_spec,
            pl.BlockSpec(memory_space=pl.ANY)]
        args = small + [qkv, qkv, qkv, prev]
        out_spec = pl.BlockSpec((tq, HEAD_W), lambda h, qi: (ctx_blk, h))
        aliases = {len(args) - 1: 0}

    sw = max(tk, n_ctx)
    kern = functools.partial(_diff_attn_kernel, tq=tq, tk=tk, n_ctx=n_ctx, n_chunks=n_chunks,
                             lambda_init=lambda_init)
    if not use_lat:
        inner = kern

        def kern(*refs):
            n_in = len(args)
            inner(*refs[:n_in - 1], *refs[n_in:])

    return pl.pallas_call(
        kern,
        grid=grid,
        in_specs=in_specs,
        out_specs=out_spec,
        out_shape=jax.ShapeDtypeStruct((t, A_HEADS * HEAD_W), BF16),
        scratch_shapes=[pltpu.VMEM((2 * tq, HEAD_W), BF16),
                        pltpu.VMEM((2 * tq, sw), F32),
                        pltpu.VMEM((2 * tq, HEAD_W), F32),
                        pltpu.VMEM((2 * tq, sw), BF16),
                        pltpu.VMEM((2 * tq, HEAD_W), F32),
                        pltpu.VMEM((2 * tq, HEAD_W), F32),
                        pltpu.VMEM((2 * tq, HEAD_W), F32),
                        pltpu.VMEM((2 * tq, HEAD_W), F32)]
        + ([pltpu.VMEM((2 * tq, sw), F32),
            pltpu.VMEM((2 * tq, HEAD_W), F32)] if n_chunks > 1 else []),
        input_output_aliases=aliases,
        compiler_params=_cparams(("arbitrary", "arbitrary")),
        name="diff_attn" if use_lat else "diff_attn_ctx",
    )(*args)


def _na_bias_tables(rpb, rows):
    nblk = rows // NA_ROWS
    wr = min(WIN_R, rows)
    nh = rpb.shape[0]
    rq = jnp.arange(NA_ROWS)[:, None, None]
    slot = jnp.arange(3)[None, :, None]
    rk = jnp.arange(NA_ROWS)[None, None, :]
    row_sel, row_ok = [], []
    for b, dup in ((0, 0), (1, -1), (nblk - 1, 2)):
        r = NA_ROWS * b + rq
        r0 = jnp.clip(r - wr // 2, 0, rows - wr)
        rkey = NA_ROWS * (b - 1 + slot) + rk
        ok = (rkey >= r0) & (rkey < r0 + wr) & (slot != dup)
        drow = jnp.broadcast_to(rkey - r + (WIN_R - 1), ok.shape)
        row_sel.append((drow[..., None] == jnp.arange(2 * WIN_R - 1)) & ok[..., None])
        row_ok.append(ok)
    row_sel = jnp.stack(row_sel).astype(F32)
    row_ok = jnp.stack(row_ok)
    jq = jnp.arange(GRID_W)[:, None]
    jk = jnp.arange(GRID_W)[None, :]
    c0 = jnp.clip(jq - WIN_C // 2, 0, GRID_W - WIN_C)
    col_ok = (jk >= c0) & (jk < c0 + WIN_C)
    dcol = jnp.clip(jk - jq + (WIN_C - 1), 0, 2 * WIN_C - 2)
    col_sel = (dcol[..., None] == jnp.arange(2 * WIN_C - 1)).astype(F32)
    bias = jnp.einsum('vasbr,hrc,qkc->hvaqsbk', row_sel, rpb.astype(F32), col_sel,
                      precision=lax.Precision.HIGHEST)
    valid = row_ok[None, :, :, None, :, :, None] & col_ok[None, None, None, :, None, None, :]
    tab = jnp.where(valid, bias * math.log2(math.e), NEG_INF)
    return tab.reshape(nh, 3, NA_ROWS * GRID_W, 3 * NA_ROWS * GRID_W)


def _natten_kernel(tab_ref, q_ref, k0_ref, k1_ref, k2_ref, v0_ref, v1_ref, v2_ref, kc_ref, vc_ref, o_ref):
    nt = (((1,), (1,)), ((), ()))
    blk = q_ref.shape[0]
    for hh in range(NA_HEADS_PER_STEP):
        hs = slice(hh * HEAD_W, (hh + 1) * HEAD_W)
        q = q_ref[:, hs]
        s_loc = [lax.dot_general(q, k_ref[:, hs], nt, preferred_element_type=F32)
                 + tab_ref[hh, :, i * blk:(i + 1) * blk] for i, k_ref in enumerate((k0_ref, k1_ref, k2_ref))]
        s_ctx = lax.dot_general(q, kc_ref[:, hs], nt, preferred_element_type=F32)
        m = jnp.max(s_ctx, axis=-1, keepdims=True)
        for s in s_loc:
            m = jnp.maximum(m, jnp.max(s, axis=-1, keepdims=True))
        l = None
        acc = None
        for s, v_ref in zip([s_ctx] + s_loc, (vc_ref, v0_ref, v1_ref, v2_ref)):
            p = jnp.exp2((s - m).astype(BF16))
            pt = p[:, 0:HEAD_W]
            for c in range(1, p.shape[1] // HEAD_W):
                pt = pt + p[:, c * HEAD_W:(c + 1) * HEAD_W]
            lp = jnp.sum(pt.astype(F32), axis=-1, keepdims=True)
            pv = jnp.dot(p, v_ref[:, hs], preferred_element_type=F32)
            l = lp if l is None else l + lp
            acc = pv if acc is None else acc + pv
        o_ref[:, hs] = (acc / l).astype(o_ref.dtype)


def _natten(qkv, tabs, n_lat, n_ctx):
    t = qkv.shape[0]
    blk = NA_ROWS * GRID_W
    nblk = n_lat // blk
    assert nblk >= 3 and blk == n_ctx
    nh = NA_HEADS_PER_STEP
    hw = nh * HEAD_W
    qo, ko, vo = (3 * A_HEADS) // nh, (3 * A_HEADS + B_HEADS) // nh, (3 * A_HEADS + 2 * B_HEADS) // nh
    ctx_blk = n_lat // n_ctx

    def kv_spec(off, shift):
        return pl.BlockSpec((blk, hw), lambda h, b: (jnp.clip(b + shift, 0, nblk - 1), off + h))

    return pl.pallas_call(
        _natten_kernel,
        grid=(B_HEADS // nh, nblk),
        in_specs=[pl.BlockSpec((nh, None, blk, 3 * blk),
                               lambda h, b: (h, jnp.where(b == 0, 0, jnp.where(b == nblk - 1, 2, 1)), 0, 0)),
                  pl.BlockSpec((blk, hw), lambda h, b: (b, qo + h)),
                  kv_spec(ko, -1), kv_spec(ko, 0), kv_spec(ko, 1),
                  kv_spec(vo, -1), kv_spec(vo, 0), kv_spec(vo, 1),
                  pl.BlockSpec((n_ctx, hw), lambda h, b: (ctx_blk, ko + h)),
                  pl.BlockSpec((n_ctx, hw), lambda h, b: (ctx_blk, vo + h))],
        out_specs=pl.BlockSpec((blk, hw), lambda h, b: (b, h)),
        out_shape=jax.ShapeDtypeStruct((t, B_HEADS * HEAD_W), BF16),
        compiler_params=_cparams(("arbitrary", "arbitrary")),
        name="natten",
    )(tabs, qkv, qkv, qkv, qkv, qkv, qkv, qkv, qkv, qkv)


def _ctx_attn_kernel(q_ref, k_ref, v_ref, prev_ref, o_ref):
    del prev_ref
    _, l, acc = _softmax_block(q_ref[...], k_ref[...], v_ref[...])
    o_ref[...] = (acc / l).astype(o_ref.dtype)


def _ctx_attn(qkv, prev, n_lat, n_ctx):
    qo, ko, vo = 3 * A_HEADS, 3 * A_HEADS + B_HEADS, 3 * A_HEADS + 2 * B_HEADS
    ctx_blk = n_lat // n_ctx
    return pl.pallas_call(
        _ctx_attn_kernel,
        grid=(B_HEADS,),
        in_specs=[pl.BlockSpec((n_ctx, HEAD_W), lambda h: (ctx_blk, qo + h)),
                  pl.BlockSpec((n_ctx, HEAD_W), lambda h: (ctx_blk, ko + h)),
                  pl.BlockSpec((n_ctx, HEAD_W), lambda h: (ctx_blk, vo + h)),
                  pl.BlockSpec(memory_space=pl.ANY)],
        out_specs=pl.BlockSpec((n_ctx, HEAD_W), lambda h: (ctx_blk, h)),
        out_shape=jax.ShapeDtypeStruct(prev.shape, prev.dtype),
        input_output_aliases={3: 0},
        compiler_params=_cparams(("arbitrary",)),
        name="ctx_attn",
    )(qkv, qkv, qkv, prev)


def _s5_tables(lam_re, lam_im, b_re, b_im, c_re, c_im, log_step, d_skip):
    hp = lax.Precision.HIGHEST
    ln = S5_CHUNK
    g, p, c = S5_GROUPS, S5_STATE, S5_GROUP
    lr, li = lam_re.astype(F32), lam_im.astype(F32)
    dt = jnp.exp(log_step.astype(F32))[:, :, None]
    mag = jnp.exp(lr * dt)
    ar, ai = mag * jnp.cos(li * dt), mag * jnp.sin(li * dt)
    den = lr * lr + li * li
    nr, ni = ar - 1.0, ai
    fr = (nr * lr + ni * li) / den
    fi = (ni * lr - nr * li) / den
    br_, bi_ = b_re.astype(F32), b_im.astype(F32)
    bbr = fr[..., None] * br_ - fi[..., None] * bi_
    bbi = fr[..., None] * bi_ + fi[..., None] * br_
    lag = jnp.arange(ln + 1, dtype=F32)[:, None, None, None]
    magl = jnp.exp(lr * dt * lag)
    pr, pi_ = magl * jnp.cos(li * dt * lag), magl * jnp.sin(li * dt * lag)
    wr = pr[..., None] * bbr - pi_[..., None] * bbi
    wi = pr[..., None] * bbi + pi_[..., None] * bbr
    cr, ci = c_re.astype(F32), c_im.astype(F32)
    kern = (jnp.einsum('dgcp,ldgpe->ldgce', cr, wr[:ln], precision=hp)
            - jnp.einsum('dgcp,ldgpe->ldgce', ci, wi[:ln], precision=hp))
    lag = np.arange(ln)[:, None, None]
    s_idx = np.arange(ln)[None, :, None]
    t_idx = np.arange(ln)[None, None, :]
    sel_f = jnp.asarray(t_idx - s_idx == lag, F32)
    sel_b = jnp.asarray(s_idx - t_idx == lag, F32)
    kfb = (jnp.einsum('lst,lgce->stgce', sel_f, kern[:, 0], precision=hp)
           + jnp.einsum('lst,lgce->stgce', sel_b, kern[:, 1], precision=hp))
    m_tab = kfb.transpose(2, 0, 4, 1, 3).reshape(g, ln * c, ln * c)
    wf_r, wf_i = wr[:ln, 0][::-1], wi[:ln, 0][::-1]
    wb_r, wb_i = wr[:ln, 1], wi[:ln, 1]

    def inj(w):
        return w.transpose(1, 0, 3, 2).reshape(g, ln * c, p)

    b_tab = jnp.concatenate([inj(wf_r), inj(wf_i), inj(wb_r), inj(wb_i)], axis=-1)
    pf_r, pf_i = pr[1:ln + 1, 0], pi_[1:ln + 1, 0]
    pb_r, pb_i = pr[1:ln + 1, 1][::-1], pi_[1:ln + 1, 1][::-1]

    def rd(pw_r, pw_i, cre, cim):
        car = cre[None] * pw_r[:, :, None, :] - cim[None] * pw_i[:, :, None, :]
        cai = cre[None] * pw_i[:, :, None, :] + cim[None] * pw_r[:, :, None, :]
        to_rows = lambda a: a.transpose(1, 3, 0, 2).reshape(g, p, ln * c)
        return to_rows(car), to_rows(-cai)

    c_tab = jnp.concatenate(rd(pf_r, pf_i, cr[0], ci[0]) + rd(pb_r, pb_i, cr[1], ci[1]), axis=1)
    a_chunk = jnp.stack([jnp.stack([pr[ln, 0], pi_[ln, 0]]), jnp.stack([pr[ln, 1], pi_[ln, 1]])])
    d_tab = jnp.tile(d_skip.astype(F32).reshape(g, 1, c), (1, ln, 1)).reshape(g, 1, ln * c)
    return m_tab.astype(BF16), b_tab.astype(BF16), c_tab.astype(BF16), a_chunk, d_tab


def _s5_inject_kernel(u_ref, b_ref, o_ref):
    o_ref[...] = jnp.dot(u_ref[...], b_ref[...], preferred_element_type=F32)


def _s5_scan_kernel(a_ref, e_ref, o_ref, s_scr, *, nb):
    d = pl.program_id(0)

    @pl.when(pl.program_id(1) == 0)
    def _():
        s_scr[...] = jnp.zeros_like(s_scr)

    ar, ai = a_ref[0], a_ref[1]

    def body(i, carry):
        sr, si = carry
        c = jnp.where(d == 0, i, nb - 1 - i)
        o_ref[0, c] = sr
        o_ref[1, c] = si
        return ar * sr - ai * si + e_ref[0, c], ar * si + ai * sr + e_ref[1, c]

    sr, si = lax.fori_loop(0, nb, body, (s_scr[0], s_scr[1]))
    s_scr[0] = sr
    s_scr[1] = si


def _gelu_tanh(x):
    return 0.5 * x * (1.0 + jnp.tanh(math.sqrt(2.0 / math.pi) * (x + 0.044715 * (x * x * x))))


def _s5_readout_kernel(u_ref, m_ref, s_ref, c_ref, d_ref, o_ref):
    u = u_ref[...]
    y = jnp.dot(u, m_ref[...], preferred_element_type=F32) + u.astype(F32) * d_ref[...]
    for k in range(4):
        y = y + jnp.dot(s_ref[k].astype(BF16), c_ref[k * S5_STATE:(k + 1) * S5_STATE, :],
                        preferred_element_type=F32)
    o_ref[...] = _gelu_tanh(y).astype(o_ref.dtype)


def _s5_glu_kernel(z_ref, w_ref, o_ref):
    z = z_ref[...]
    gate = jax.nn.sigmoid(jnp.dot(z.astype(BF16), w_ref[...], preferred_element_type=F32))
    o_ref[...] = (z.astype(F32) * gate).astype(o_ref.dtype)


def _s5_mixer(u, tables, w_glu, n_lat, n_ctx):
    m_tab, b_tab, c_tab, a_chunk, d_tab = tables
    t = u.shape[0]
    g, c, ln, p = S5_GROUPS, S5_GROUP, S5_CHUNK, S5_STATE
    nc = t // ln
    w = ln * c
    ug = u.reshape(nc, ln, g, c).transpose(2, 0, 1, 3).reshape(g, nc, w)

    e = pl.pallas_call(
        _s5_inject_kernel,
        grid=(g,),
        in_specs=[pl.BlockSpec((None, nc, w), lambda i: (i, 0, 0)),
                  pl.BlockSpec((None, w, 4 * p), lambda i: (i, 0, 0))],
        out_specs=pl.BlockSpec((None, nc, 4 * p), lambda i: (i, 0, 0)),
        out_shape=jax.ShapeDtypeStruct((g, nc, 4 * p), F32),
        compiler_params=_cparams(("arbitrary",)),
        name="s5_inject",
    )(ug, b_tab)
    e = e.reshape(g, nc, 2, 2, p).transpose(2, 3, 1, 0, 4)

    nb = S5_BLOCK
    assert (n_lat // ln) % nb == 0 and n_ctx // ln == nb
    lat_blocks = n_lat // ln // nb

    def blk(d, s):
        return jnp.where(s == 0, lat_blocks, jnp.where(d == 0, s - 1, lat_blocks - s))

    s_in = pl.pallas_call(
        functools.partial(_s5_scan_kernel, nb=nb),
        grid=(2, lat_blocks + 1),
        in_specs=[pl.BlockSpec((None, 2, g, p), lambda d, s: (d, 0, 0, 0)),
                  pl.BlockSpec((None, 2, nb, g, p), lambda d, s: (d, 0, blk(d, s), 0, 0))],
        out_specs=pl.BlockSpec((None, 2, nb, g, p), lambda d, s: (d, 0, blk(d, s), 0, 0)),
        out_shape=jax.ShapeDtypeStruct((2, 2, nc, g, p), F32),
        scratch_shapes=[pltpu.VMEM((2, g, p), F32)],
        compiler_params=_cparams(("arbitrary", "arbitrary")),
        name="s5_scan",
    )(a_chunk, e)
    s_in = s_in.transpose(3, 0, 1, 2, 4).reshape(g, 4, nc, p)

    z = pl.pallas_call(
        _s5_readout_kernel,
        grid=(g,),
        in_specs=[pl.BlockSpec((None, nc, w), lambda i: (i, 0, 0)),
                  pl.BlockSpec((None, w, w), lambda i: (i, 0, 0)),
                  pl.BlockSpec((None, 4, nc, p), lambda i: (i, 0, 0, 0)),
                  pl.BlockSpec((None, 4 * p, w), lambda i: (i, 0, 0)),
                  pl.BlockSpec((None, 1, w), lambda i: (i, 0, 0))],
        out_specs=pl.BlockSpec((None, nc, w), lambda i: (i, 0, 0)),
        out_shape=jax.ShapeDtypeStruct((g, nc, w), BF16),
        compiler_params=_cparams(("arbitrary",)),
        name="s5_readout",
    )(ug, m_tab, s_in, c_tab, d_tab)
    z = z.reshape(g, nc, ln, c).transpose(1, 2, 0, 3).reshape(t, g * c)

    tm = _row_tile(t, (1280, 640, 256, 128))
    return pl.pallas_call(
        _s5_glu_kernel,
        grid=(t // tm,),
        in_specs=[pl.BlockSpec((tm, g * c), lambda i: (i, 0)),
                  pl.BlockSpec((g * c, g * c), lambda i: (0, 0))],
        out_specs=pl.BlockSpec((tm, g * c), lambda i: (i, 0)),
        out_shape=jax.ShapeDtypeStruct((t, g * c), BF16),
        compiler_params=_cparams(("arbitrary",)),
        name="s5_glu",
    )(z, w_glu)


def _ret_tables(decay_logit):
    scale = HEAD_W ** -0.5
    lg = jax.nn.log_sigmoid(decay_logit.astype(F32))
    lf, lb = lg[0][:, None, None], lg[1][:, None, None]
    i = jnp.arange(RET_CHUNK, dtype=F32)[None, :, None]
    j = jnp.arange(RET_CHUNK, dtype=F32)[None, None, :]
    diff = i - j
    intra = (jnp.where(diff >= 0, jnp.exp(lf * jnp.maximum(diff, 0.0)), 0.0)
             + jnp.where(diff <= 0, jnp.exp(lb * jnp.maximum(-diff, 0.0)), 0.0)) * scale
    ones = jnp.ones((1, 1, HEAD_W), F32)
    q_f = jnp.exp(lf * (i + 1.0)) * ones
    k_f = jnp.exp(lf * (RET_CHUNK - 1.0 - i)) * scale * ones
    c_f = jnp.exp(lf * RET_CHUNK) * jnp.ones((1, RET_CHUNK, HEAD_W), F32)
    q_b = jnp.exp(lb * (RET_CHUNK - i)) * ones
    k_b = jnp.exp(lb * i) * scale * ones
    c_b = jnp.exp(lb * RET_CHUNK) * jnp.ones((1, RET_CHUNK, HEAD_W), F32)
    return intra, jnp.stack([q_f, k_f, c_f]), jnp.stack([q_b, k_b, c_b])


def _ret_state_step(q, k, v, dec_ref, s_scr, h):
    s = s_scr[h]
    qd = (q.astype(F32) * dec_ref[0, h]).astype(BF16)
    kd = (k.astype(F32) * dec_ref[1, h]).astype(BF16)
    o = jnp.dot(qd, s.astype(BF16), preferred_element_type=F32)
    s_scr[h] = s * dec_ref[2, h] + lax.dot_general(kd, v, (((0,), (0,)), ((), ())),
                                                    preferred_element_type=F32)
    return o


def _ret_fwd_kernel(intra_ref, dec_ref, q_ref, k_ref, v_ref, o_ref, s_scr):
    @pl.when(pl.program_id(0) == 0)
    def _():
        s_scr[...] = jnp.zeros_like(s_scr)

    for c in range(RET_STEP_CHUNKS):
        rows = slice(c * RET_CHUNK, (c + 1) * RET_CHUNK)
        for h in range(RET_HEADS):
            sl = slice(h * HEAD_W, (h + 1) * HEAD_W)
            q, k, v = q_ref[rows, sl], k_ref[rows, sl], v_ref[rows, sl]
            att = lax.dot_general(q, k, (((1,), (1,)), ((), ())), preferred_element_type=F32) * intra_ref[h]
            o = jnp.dot(att.astype(BF16), v, preferred_element_type=F32)
            o_ref[rows, sl] = o + _ret_state_step(q, k, v, dec_ref, s_scr, h)


def _ret_bwd_kernel(dec_ref, q_ref, k_ref, v_ref, g_ref, o1_ref, o_ref, s_scr):
    @pl.when(pl.program_id(0) == 0)
    def _():
        s_scr[...] = jnp.zeros_like(s_scr)

    for c in reversed(range(RET_STEP_CHUNKS)):
        rows = slice(c * RET_CHUNK, (c + 1) * RET_CHUNK)
        for h in range(RET_HEADS):
            sl = slice(h * HEAD_W, (h + 1) * HEAD_W)
            q, k, v = q_ref[rows, sl], k_ref[rows, sl], v_ref[rows, sl]
            o = o1_ref[rows, sl] + _ret_state_step(q, k, v, dec_ref, s_scr, h)
            y = o * lax.rsqrt(jnp.mean(o * o, axis=-1, keepdims=True) + EPS)
            gate = g_ref[rows, sl].astype(F32)
            o_ref[rows, sl] = (y * (gate * jax.nn.sigmoid(gate))).astype(o_ref.dtype)


def _retention(proj, tables, n_lat, n_ctx):
    intra, dec_f, dec_b = tables
    t = proj.shape[0]
    wd = RET_HEADS * HEAD_W
    ck = RET_CHUNK
    bt = RET_STEP_CHUNKS * ck
    assert n_lat % bt == 0 and n_ctx % bt == 0
    n_lat_c, n_ctx_c = n_lat // bt, n_ctx // bt
    steps = n_lat_c + n_ctx_c

    def fwd_blk(s):
        return jnp.where(s < n_ctx_c, n_lat_c + s, s - n_ctx_c)

    def bwd_blk(s):
        return jnp.where(s < n_ctx_c, n_lat_c + n_ctx_c - 1 - s, n_lat_c - 1 - (s - n_ctx_c))

    def col_spec(col, order):
        return pl.BlockSpec((bt, wd), lambda s: (order(s), col))

    tab_spec = pl.BlockSpec((3, RET_HEADS, ck, HEAD_W), lambda s: (0, 0, 0, 0))
    o1 = pl.pallas_call(
        _ret_fwd_kernel,
        grid=(steps,),
        in_specs=[pl.BlockSpec((RET_HEADS, ck, ck), lambda s: (0, 0, 0)), tab_spec,
                  col_spec(0, fwd_blk), col_spec(1, fwd_blk), col_spec(2, fwd_blk)],
        out_specs=col_spec(0, fwd_blk),
        out_shape=jax.ShapeDtypeStruct((t, wd), F32),
        scratch_shapes=[pltpu.VMEM((RET_HEADS, HEAD_W, HEAD_W), F32)],
        compiler_params=_cparams(("arbitrary",)),
        name="retention_fwd",
    )(intra, dec_f, proj, proj, proj)
    return pl.pallas_call(
        _ret_bwd_kernel,
        grid=(steps,),
        in_specs=[tab_spec, col_spec(0, bwd_blk), col_spec(1, bwd_blk), col_spec(2, bwd_blk),
                  col_spec(3, bwd_blk), col_spec(0, bwd_blk)],
        out_specs=col_spec(0, bwd_blk),
        out_shape=jax.ShapeDtypeStruct((t, wd), BF16),
        scratch_shapes=[pltpu.VMEM((RET_HEADS, HEAD_W, HEAD_W), F32)],
        compiler_params=_cparams(("arbitrary",)),
        name="retention_bwd",
    )(dec_b, proj, proj, proj, proj, o1)


def _rope_tables(n_lat, n_ctx):
    n_freq = A_QK_DIM // 4
    freq = ROPE_BASE ** (-jnp.arange(n_freq, dtype=F32) / n_freq)
    rows = n_lat // GRID_W
    ang_row = jnp.arange(rows, dtype=F32)[:, None] * freq
    ang_col = jnp.arange(GRID_W, dtype=F32)[:, None] * freq
    reps = HEAD_W // A_QK_DIM

    def table(fn, ctx_value):
        r = jnp.broadcast_to(jnp.tile(fn(ang_row), (1, 2))[:, None, :], (rows, GRID_W, 2 * n_freq))
        c = jnp.broadcast_to(jnp.tile(fn(ang_col), (1, 2))[None, :, :], (rows, GRID_W, 2 * n_freq))
        lat = jnp.tile(jnp.concatenate([r, c], axis=-1).reshape(n_lat, A_QK_DIM), (1, reps))
        return jnp.concatenate([lat, jnp.full((n_ctx, HEAD_W), ctx_value, F32)], axis=0)

    return table(jnp.cos, 1.0), table(jnp.sin, 0.0)


def kernel(x, c, ctx, c_ctx, ada_w, ada_b, norm1_w, norm2_w, ffn_w13, ffn_w2, e_w_in, e_w_out, diff_lq1, diff_lk1, diff_lq2, diff_lk2, diff_subln_w, na_rpb, o_w_in, o_w_out, s5_lam_re, s5_lam_im, s5_b_re, s5_b_im, s5_c_re, s5_c_im, s5_log_step, s5_d, s5_w_glu, ret_decay_logit, final_norm_w):
    bsz, n_lat, d = x.shape
    n_ctx = ctx.shape[1]
    depth = ada_w.shape[0]
    assert bsz == 1
    xs = jnp.concatenate([x[0], ctx[0]], axis=0)
    mods_all = _mods(c, c_ctx, ada_w, ada_b)
    cos, sin = _rope_tables(n_lat, n_ctx)
    ret_w = RET_HEADS * HEAD_W
    ffn_w13_bf16 = ffn_w13.astype(BF16)
    e_w_in_bf16 = e_w_in.astype(BF16)

    for i in range(depth):
        compute_ctx = i != depth - 1
        mods = mods_all[i]
        j = i // 2
        if i % 2 == 0:
            lambda_init = 0.8 - 0.6 * math.exp(-0.3 * i)
            qkv = _inproj_even(xs, norm1_w[i], mods, n_lat, e_w_in_bf16, j, cos, sin)
            lam_args = (diff_lq1[j], diff_lk1[j], diff_lq2[j], diff_lk2[j], diff_subln_w[j])
            o_a = _diff_attn(qkv, *lam_args, n_lat, n_ctx, lambda_init)
            o_b = _natten(qkv, _na_bias_tables(na_rpb[j], n_lat // GRID_W), n_lat, n_ctx)
            if compute_ctx:
                o_a = _diff_attn(qkv, *lam_args, n_lat, n_ctx, lambda_init, prev=o_a)
                o_b = _ctx_attn(qkv, o_b, n_lat, n_ctx)
            xs = _gated_residual([o_a, o_b], e_w_out[j].astype(BF16), xs, mods, n_lat, 2, "outproj_even")
        else:
            w_in = o_w_in[j]
            w_in = jnp.concatenate([w_in[:, S5_CH:], w_in[:, :S5_CH]], axis=1).astype(BF16)
            proj = _inproj_odd(xs, norm1_w[i], mods, n_lat, w_in)
            s5_tabs = _s5_tables(s5_lam_re[j], s5_lam_im[j], s5_b_re[j], s5_b_im[j], s5_c_re[j], s5_c_im[j],
                                 s5_log_step[j], s5_d[j])
            y_c = _s5_mixer(proj[:, 4 * ret_w:], s5_tabs, s5_w_glu[j].astype(BF16), n_lat, n_ctx)
            y_d = _retention(proj, _ret_tables(ret_decay_logit[j]), n_lat, n_ctx)
            xs = _gated_residual([y_c, y_d], o_w_out[j].astype(BF16), xs, mods, n_lat, 2, "outproj_odd")
        a = _ffn_up(xs, norm2_w[i], mods, n_lat, ffn_w13_bf16, i)
        xs = _gated_residual([a], ffn_w2[i].astype(BF16), xs, mods, n_lat, 5, "ffn_down")
    return _final_norm(xs, final_norm_w, n_lat)[None]
```
